```python
import math
import jax
import jax.numpy as jnp
from jax import lax
import numpy as np

D_MODEL = 1024
BATCH = 2
SEQ = 8192
DEPTH = 1

HEAD_DIM = 64
NSA_HEADS = 8
NSA_KV_GROUPS = 2
NSA_HPG = NSA_HEADS // NSA_KV_GROUPS
RWKV_HEADS = 8
D_NSA = NSA_HEADS * HEAD_DIM
D_RWKV = RWKV_HEADS * HEAD_DIM
D_MIX = D_NSA + D_RWKV
D_KV = NSA_KV_GROUPS * HEAD_DIM
CMP_BLOCK = 32
CMP_STRIDE = 16
CMP_HIDDEN = 128
SEL_BLOCK = 64
SEL_TOPN = 16
WINDOW = 512
Q_BLOCK = 128
N_BUCKETS = 32
MAX_DISTANCE = 128
LORA_W = 64
LORA_A = 64
LORA_G = 128
D_FF = 2816
CONV_W = 3
NORM_EPS = 1e-6
GN_EPS = 64e-5
NEG_INF = -1e30
FORCE_SCORE = 1e9
NSA_SPLITS = (D_NSA, D_KV, D_KV, D_KV, D_KV, D_KV, D_KV, 3 * NSA_HEADS)
RWKV_SPLITS = (D_RWKV, D_RWKV, D_RWKV, LORA_W, LORA_A, LORA_G)
D_NSA_IN = sum(NSA_SPLITS)
D_RWKV_IN = sum(RWKV_SPLITS)
D_IN = D_NSA_IN + D_RWKV_IN

kernel_name = 'hymba_nsa_rwkv7_convffn'


def rmsnorm(x, g, eps=NORM_EPS):
    xf = x.astype(jnp.float32)
    y = xf * lax.rsqrt(jnp.mean(xf * xf, axis=-1, keepdims=True) + eps)
    return (y * g).astype(x.dtype)


def t5_bucket(dist):
    n = jnp.maximum(dist, 0)
    max_exact = N_BUCKETS // 2
    nf = jnp.maximum(n, 1).astype(jnp.float32)
    large = max_exact + (jnp.log(nf / max_exact) / math.log(MAX_DISTANCE / max_exact)
                         * (N_BUCKETS - max_exact)).astype(jnp.int32)
    large = jnp.minimum(large, N_BUCKETS - 1)
    return jnp.where(n < max_exact, n, large)


def token_shift(z):
    return jnp.pad(z, ((0, 0), (1, 0), (0, 0)))[:, :-1]


def causal_dwconv(z, w, b):
    T = z.shape[1]
    zp = jnp.pad(z, ((0, 0), (CONV_W - 1, 0), (0, 0)))
    return b + sum(zp[:, i:i + T] * w[i] for i in range(CONV_W))


def compress(kv, pos, w1, b1, w2, b2):
    B, T, G, dh = kv.shape
    nc = (T - CMP_BLOCK) // CMP_STRIDE + 1
    idx = jnp.arange(nc)[:, None] * CMP_STRIDE + jnp.arange(CMP_BLOCK)[None, :]
    blk = kv[:, idx] + pos[None, None, :, None, :]
    blk = blk.transpose(0, 1, 3, 2, 4).reshape(B, nc, G, CMP_BLOCK * dh)
    return jax.nn.gelu(blk @ w1 + b1) @ w2 + b2


def nsa_attention(q, k_cmp, v_cmp, k_sel, v_sel, k_win, v_win, gates, rel_bias):
    B, T = q.shape[0], q.shape[1]
    G, R, dh = NSA_KV_GROUPS, NSA_HPG, HEAD_DIM
    nc = k_cmp.shape[1]
    ns = T // SEL_BLOCK
    n_top = min(SEL_TOPN, ns)
    scale = dh ** -0.5
    bias_grq = rel_bias.reshape(N_BUCKETS, G, R)
    bias_gbr = bias_grq.transpose(1, 0, 2)
    cmp_start = jnp.arange(nc) * CMP_STRIDE
    cmp_end = cmp_start + CMP_BLOCK - 1
    sel_start = jnp.arange(ns) * SEL_BLOCK
    overlap = ((cmp_start[:, None] < sel_start[None, :] + SEL_BLOCK)
               & (cmp_end[:, None] >= sel_start[None, :])).astype(jnp.float32)
    ks_blk = k_sel.reshape(B, ns, SEL_BLOCK, G, dh).transpose(0, 3, 1, 2, 4)
    vs_blk = v_sel.reshape(B, ns, SEL_BLOCK, G, dh).transpose(0, 3, 1, 2, 4)
    kw_pad = jnp.pad(k_win, ((0, 0), (WINDOW, 0), (0, 0), (0, 0)))
    vw_pad = jnp.pad(v_win, ((0, 0), (WINDOW, 0), (0, 0), (0, 0)))
    b_ix = jnp.arange(B)[:, None, None, None]
    g_ix = jnp.arange(G)[None, :, None, None]
    blk_ix = jnp.arange(SEL_BLOCK)
    win_ix = jnp.arange(Q_BLOCK + WINDOW)
    sel_ix = jnp.arange(ns)[None, :]

    def query_block(qb):
        q0 = qb * Q_BLOCK
        t = q0 + jnp.arange(Q_BLOCK)
        qq = lax.dynamic_slice_in_dim(q, q0, Q_BLOCK, axis=1).reshape(B, Q_BLOCK, G, R, dh)
        gg = lax.dynamic_slice_in_dim(gates, q0, Q_BLOCK, axis=1).reshape(B, Q_BLOCK, G, R, 3)

        d_c = t[:, None] - cmp_end[None, :]
        m_c = d_c >= 0
        l_c = (jnp.einsum('bqgrd,bcgd->bgrqc', qq, k_cmp).astype(jnp.float32) * scale
               + bias_grq[t5_bucket(d_c)].transpose(2, 3, 0, 1))
        p_c = jax.nn.softmax(jnp.where(m_c, l_c, NEG_INF), axis=-1) * m_c
        o_c = jnp.einsum('bgrqc,bcgd->bqgrd', p_c.astype(v_cmp.dtype), v_cmp)

        imp = jnp.einsum('bgrqc,cn->bgqn', p_c, overlap)
        cur = (t // SEL_BLOCK)[:, None]
        forced = (sel_ix == 0) | (sel_ix == cur) | (sel_ix == cur - 1)
        score = jnp.where(sel_start[None, :] <= t[:, None],
                          jnp.where(forced, FORCE_SCORE, imp), NEG_INF)
        _, top_idx = lax.top_k(score, n_top)
        k_s = ks_blk[b_ix, g_ix, top_idx].reshape(B, G, Q_BLOCK, n_top * SEL_BLOCK, dh)
        v_s = vs_blk[b_ix, g_ix, top_idx].reshape(B, G, Q_BLOCK, n_top * SEL_BLOCK, dh)
        s_pos = (top_idx[..., None] * SEL_BLOCK + blk_ix).reshape(B, G, Q_BLOCK, n_top * SEL_BLOCK)
        d_s = t[None, None, :, None] - s_pos
        m_s = (d_s >= 0)[:, :, None]
        l_s = (jnp.einsum('bqgrd,bgqkd->bgrqk', qq, k_s).astype(jnp.float32) * scale
               + bias_gbr[g_ix, t5_bucket(d_s)].transpose(0, 1, 4, 2, 3))
        p_s = jax.nn.softmax(jnp.where(m_s, l_s, NEG_INF), axis=-1)
        o_s = jnp.einsum('bgrqk,bgqkd->bqgrd', p_s.astype(v_s.dtype), v_s)

        k_w = lax.dynamic_slice_in_dim(kw_pad, q0, Q_BLOCK + WINDOW, axis=1)
        v_w = lax.dynamic_slice_in_dim(vw_pad, q0, Q_BLOCK + WINDOW, axis=1)
        s_w = q0 - WINDOW + win_ix
        d_w = t[:, None] - s_w[None, :]
        m_w = (d_w >= 0) & (d_w < WINDOW) & (s_w[None, :] >= 0)
        l_w = (jnp.einsum('bqgrd,bkgd->bgrqk', qq, k_w).astype(jnp.float32) * scale
               + bias_grq[t5_bucket(d_w)].transpose(2, 3, 0, 1))
        p_w = jax.nn.softmax(jnp.where(m_w, l_w, NEG_INF), axis=-1)
        o_w = jnp.einsum('bgrqk,bkgd->bqgrd', p_w.astype(v_w.dtype), v_w)

        o = gg[..., 0:1] * o_c + gg[..., 1:2] * o_s + gg[..., 2:3] * o_w
        return o.reshape(B, Q_BLOCK, D_NSA)

    out = lax.map(query_block, jnp.arange(T // Q_BLOCK))
    return out.transpose(1, 0, 2, 3).reshape(B, T, D_NSA)


def _wkv7_step(state, inp):
    r, w, k, v, a, b = inp
    sa = jnp.einsum('bhvk,bhk->bhv', state, a)
    state = state * w[:, :, None, :] + sa[..., None] * b[:, :, None, :] + v[..., None] * k[:, :, None, :]
    return state, jnp.einsum('bhvk,bhk->bhv', state, r)


def rwkv7_time_mix(feats, w0, w2, a0, a2, g2, k_k, k_a, r_k, ln_w, ln_b):
    B, T, _ = feats.shape
    H, N = RWKV_HEADS, HEAD_DIM
    r, k, v, xw, xa, xg = jnp.split(feats, np.cumsum(RWKV_SPLITS)[:-1].tolist(), axis=-1)
    w = -jax.nn.softplus(-(w0 + jnp.tanh(xw) @ w2)) - 0.5
    decay = jnp.exp(-jnp.exp(w.astype(jnp.float32)))
    a = jax.nn.sigmoid(a0 + xa @ a2)
    g = jax.nn.sigmoid(xg) @ g2

    def heads(z):
        return z.reshape(B, T, H, N).astype(jnp.float32)

    kk = heads(k * k_k)
    kk = kk / jnp.maximum(jnp.sqrt(jnp.sum(kk * kk, axis=-1, keepdims=True)), 1e-12)
    k = k * (1.0 + (a - 1.0) * k_a)
    rh, kh, vh, ah = heads(r), heads(k), heads(v), heads(a)
    xs = tuple(z.transpose(1, 0, 2, 3) for z in (rh, heads(decay), kh, vh, -kk, kk * ah))
    state0 = jnp.zeros((B, H, N, N), jnp.float32)
    _, y = lax.scan(_wkv7_step, state0, xs)
    y = y.transpose(1, 0, 2, 3)
    mu = jnp.mean(y, axis=-1, keepdims=True)
    var = jnp.mean(jnp.square(y - mu), axis=-1, keepdims=True)
    y = ((y - mu) * lax.rsqrt(var + GN_EPS)).reshape(B, T, D_RWKV) * ln_w + ln_b
    bonus = jnp.sum(rh * kh * r_k, axis=-1, keepdims=True) * vh
    y = (y + bonus.reshape(B, T, D_RWKV)) * g
    return y.astype(feats.dtype)


def setup_inputs(seed: int = 0) -> dict:
    key = jax.random.key(seed)
    ks = list(jax.random.split(key, 32))
    L = DEPTH

    def nrm(shape, scale):
        return scale * jax.random.normal(ks.pop(), shape, jnp.float32)

    def unif(shape, lo, hi):
        return jax.random.uniform(ks.pop(), shape, jnp.float32, minval=lo, maxval=hi)

    return {
        'x': nrm((BATCH, SEQ, D_MODEL), 1.0),
        'norm1_g': 1.0 + nrm((L, D_MODEL), 0.02),
        'w_in': nrm((L, D_MODEL, D_IN), D_MODEL ** -0.5),
        'q_norm_g': 1.0 + nrm((L, HEAD_DIM), 0.02),
        'k_norm_g': 1.0 + nrm((L, 3, HEAD_DIM), 0.02),
        'cmp_pos': nrm((L, 2, CMP_BLOCK, HEAD_DIM), 0.02),
        'cmp_w1': nrm((L, 2, CMP_BLOCK * HEAD_DIM, CMP_HIDDEN), (CMP_BLOCK * HEAD_DIM) ** -0.5),
        'cmp_b1': nrm((L, 2, CMP_HIDDEN), 0.02),
        'cmp_w2': nrm((L, 2, CMP_HIDDEN, HEAD_DIM), CMP_HIDDEN ** -0.5),
        'cmp_b2': nrm((L, 2, HEAD_DIM), 0.02),
        'rel_bias': nrm((N_BUCKETS, NSA_HEADS), 0.5),
        'rwkv_mu': unif((L, D_RWKV_IN), 0.0, 1.0),
        'w0': unif((L, D_RWKV), -6.0, -1.0),
        'w2': nrm((L, LORA_W, D_RWKV), 0.1 * LORA_W ** -0.5),
        'a0': nrm((L, D_RWKV), 0.1),
        'a2': nrm((L, LORA_A, D_RWKV), 0.5 * LORA_A ** -0.5),
        'g2': nrm((L, LORA_G, D_RWKV), LORA_G ** -0.5),
        'k_k': 0.85 + nrm((L, D_RWKV), 0.02),
        'k_a': 1.0 + nrm((L, D_RWKV), 0.02),
        'r_k': nrm((L, RWKV_HEADS, HEAD_DIM), 0.1),
        'ln_x_w': 1.0 + nrm((L, D_RWKV), 0.02),
        'ln_x_b': nrm((L, D_RWKV), 0.02),
        'w_out': nrm((L, D_MIX, D_MODEL), D_MIX ** -0.5),
        'norm2_g': 1.0 + nrm((L, D_MODEL), 0.02),
        'ffn_up': nrm((L, D_MODEL, 2 * D_FF), D_MODEL ** -0.5),
        'conv_w': nrm((L, CONV_W, 2 * D_FF), CONV_W ** -0.5),
        'conv_b': nrm((L, 2 * D_FF), 0.02),
        'ffn_down': nrm((L, D_FF, D_MODEL), D_FF ** -0.5),
    }


def reference(x, norm1_g, w_in, q_norm_g, k_norm_g, cmp_pos, cmp_w1, cmp_b1, cmp_w2, cmp_b2,
              rel_bias, rwkv_mu, w0, w2, a0, a2, g2, k_k, k_a, r_k, ln_x_w, ln_x_b,
              w_out, norm2_g, ffn_up, conv_w, conv_b, ffn_down):
    B, T, _ = x.shape
    G, H, dh = NSA_KV_GROUPS, NSA_HEADS, HEAD_DIM
    nsa_cuts = np.cumsum(NSA_SPLITS)[:-1].tolist()

    def kv_heads(z):
        return z.reshape(B, T, G, dh)

    for l in range(DEPTH):
        h = rmsnorm(x, norm1_g[l])
        proj = h @ w_in[l]
        q, kc, vc, ksl, vsl, kwn, vwn, gl = jnp.split(proj[..., :D_NSA_IN], nsa_cuts, axis=-1)
        q = rmsnorm(q.reshape(B, T, H, dh), q_norm_g[l])
        k_cmp = rmsnorm(compress(kv_heads(kc), cmp_pos[l, 0], cmp_w1[l, 0], cmp_b1[l, 0],
                                 cmp_w2[l, 0], cmp_b2[l, 0]), k_norm_g[l, 0])
        v_cmp = compress(kv_heads(vc), cmp_pos[l, 1], cmp_w1[l, 1], cmp_b1[l, 1],
                         cmp_w2[l, 1], cmp_b2[l, 1])
        k_sel = rmsnorm(kv_heads(ksl), k_norm_g[l, 1])
        k_win = rmsnorm(kv_heads(kwn), k_norm_g[l, 2])
        gates = jax.nn.sigmoid(gl).reshape(B, T, H, 3)
        o_nsa = nsa_attention(q, k_cmp, v_cmp, k_sel, kv_heads(vsl), k_win, kv_heads(vwn),
                              gates, rel_bias)

        rw = proj[..., D_NSA_IN:]
        rw = rw + (token_shift(rw) - rw) * rwkv_mu[l]
        o_rwkv = rwkv7_time_mix(rw, w0[l], w2[l], a0[l], a2[l], g2[l], k_k[l], k_a[l],
                                r_k[l], ln_x_w[l], ln_x_b[l])
        x = x + jnp.concatenate([o_nsa, o_rwkv], axis=-1) @ w_out[l]

        h = rmsnorm(x, norm2_g[l])
        u = causal_dwconv(h @ ffn_up[l], conv_w[l], conv_b[l])
        u_val, u_gate = jnp.split(u, 2, axis=-1)
        x = x + (jax.nn.silu(u_gate) * u_val) @ ffn_down[l]
    return x
```

```python
import functools
import math

import jax
import jax.numpy as jnp
import numpy as np
from jax import lax
from jax.experimental import pallas as pl
from jax.experimental.pallas import tpu as pltpu

F32 = jnp.float32
BF16 = jnp.bfloat16

V7X_LANES = 128
V7X_SUBLANES = 8
V7X_VMEM_LIMIT_BYTES = 56 * 1024 * 1024

HEAD_DIM = 64
NSA_HEADS = 8
NSA_GROUPS = 2
NSA_HPG = NSA_HEADS // NSA_GROUPS
RWKV_HEADS = 8
D_NSA = NSA_HEADS * HEAD_DIM
D_RWKV = RWKV_HEADS * HEAD_DIM
D_KV = NSA_GROUPS * HEAD_DIM
CMP_BLOCK = 32
CMP_STRIDE = 16
CMP_HIDDEN = 128
SEL_BLOCK = 64
SEL_TOPN = 16
WINDOW = 512
N_BUCKETS = 32
MAX_DISTANCE = 128
LORA_W = 64
LORA_A = 64
LORA_G = 128
D_RWKV_IN = 3 * D_RWKV + LORA_W + LORA_A + LORA_G
CONV_W = 3
NORM_EPS = 1e-6
GN_EPS = 64e-5
NEG_INF = -1e30
FORCE_SCORE = 1e9

QT = 128
CMP_PAD = 8
CMP_BAND = 16
WKV_CHUNK = 64
GATE_ROWS = 16
N_T_ROWS = D_NSA + 2 * D_KV + NSA_GROUPS * GATE_ROWS
N_STD_COLS = 4 * D_KV + D_RWKV_IN


def _cparams(sem):
    return pltpu.CompilerParams(dimension_semantics=sem, vmem_limit_bytes=V7X_VMEM_LIMIT_BYTES)


def _bdot(a, b):
    return jnp.dot(a.astype(BF16), b.astype(BF16), preferred_element_type=F32)


def _bdot_nt(a, b):
    return lax.dot_general(a.astype(BF16), b.astype(BF16), (((1,), (1,)), ((), ())),
                           preferred_element_type=F32)


def _split3(x):
    hi = x.astype(BF16)
    r1 = x - hi.astype(F32)
    mid = r1.astype(BF16)
    lo = (r1 - mid.astype(F32)).astype(BF16)
    return hi, mid, lo


def _dot_exact_rhs(x, m_bf16):
    hi, mid, lo = _split3(x)
    return (jnp.dot(hi, m_bf16, preferred_element_type=F32)
            + jnp.dot(mid, m_bf16, preferred_element_type=F32)
            + jnp.dot(lo, m_bf16, preferred_element_type=F32))


def _proj_kernel(x_ref, g1_ref, wstd_ref, wt_ref, qg_ref, kg_ref,
                 qT_ref, kc_ref, vc_ref, ksel_ref, kwin_ref, vT_ref, gT_ref, rw_ref):
    tm = x_ref.shape[1]
    ns = ksel_ref.shape[3] - 2 * HEAD_DIM
    x = x_ref[0]
    ms = jnp.mean(x * x, axis=-1, keepdims=True)
    h = (x * lax.rsqrt(ms + NORM_EPS) * g1_ref[...]).astype(BF16)

    std = jnp.dot(h, wstd_ref[...], preferred_element_type=F32)
    kc_ref[0] = std[:, 0:D_KV].astype(BF16)
    vc_ref[0] = std[:, D_KV:2 * D_KV].astype(BF16)
    rw_ref[0] = std[:, 4 * D_KV:]

    lane = lax.broadcasted_iota(jnp.int32, (tm, D_KV), 1)
    first = lane < HEAD_DIM

    def group_rmsnorm(k, gain):
        k2 = k * k
        s0 = jnp.sum(jnp.where(first, k2, 0.0), axis=-1, keepdims=True)
        s1 = jnp.sum(jnp.where(first, 0.0, k2), axis=-1, keepdims=True)
        ms_g = jnp.where(first, s0, s1) * (1.0 / HEAD_DIM)
        return k * lax.rsqrt(ms_g + NORM_EPS) * gain

    ksl = group_rmsnorm(std[:, 2 * D_KV:3 * D_KV], kg_ref[0:1, :]).astype(BF16)
    kwn = group_rmsnorm(std[:, 3 * D_KV:4 * D_KV], kg_ref[1:2, :]).astype(BF16)

    tok = pl.program_id(1) * tm + lax.broadcasted_iota(jnp.int32, (tm, ns), 0)
    blk = lax.broadcasted_iota(jnp.int32, (tm, ns), 1)
    onehot = jnp.where((tok // SEL_BLOCK) == blk, 1.0, 0.0).astype(BF16)
    zeros = jnp.zeros((tm, HEAD_DIM), BF16)
    for g in range(NSA_GROUPS):
        ksel_ref[0, g, :, 0:HEAD_DIM] = ksl[:, g * HEAD_DIM:(g + 1) * HEAD_DIM]
        ksel_ref[0, g, :, HEAD_DIM:2 * HEAD_DIM] = zeros
        ksel_ref[0, g, :, 2 * HEAD_DIM:] = onehot
        kwin_ref[0, g] = kwn[:, g * HEAD_DIM:(g + 1) * HEAD_DIM]

    tr = lax.dot_general(wt_ref[...], h, (((1,), (1,)), ((), ())),
                         preferred_element_type=F32)
    q = tr[0:D_NSA].reshape(NSA_HEADS, HEAD_DIM, tm)
    qms = jnp.mean(q * q, axis=1, keepdims=True)
    qn = q * lax.rsqrt(qms + NORM_EPS) * qg_ref[...][None] * (HEAD_DIM ** -0.5)
    qT_ref[0] = qn.reshape(D_NSA, tm).astype(BF16)
    vt = tr[D_NSA:D_NSA + 2 * D_KV].astype(BF16)
    for a in range(2 * NSA_GROUPS):
        for j in range(tm // QT):
            vT_ref[0, a, j] = vt[a * HEAD_DIM:(a + 1) * HEAD_DIM, j * QT:(j + 1) * QT]
    gT_ref[0] = jax.nn.sigmoid(tr[D_NSA + 2 * D_KV:])


def _proj(x, g1, wstd, wt, qg, kg, *, tm):
    B, T, D = x.shape
    ns = T // SEL_BLOCK
    grid = (B, T // tm)
    const2 = lambda b, i: (0, 0)
    out_shape = (
        jax.ShapeDtypeStruct((B, D_NSA, T), BF16),
        jax.ShapeDtypeStruct((B, T, D_KV), BF16),
        jax.ShapeDtypeStruct((B, T, D_KV), BF16),
        jax.ShapeDtypeStruct((B, NSA_GROUPS, T, 2 * HEAD_DIM + ns), BF16),
        jax.ShapeDtypeStruct((B, NSA_GROUPS, T, HEAD_DIM), BF16),
        jax.ShapeDtypeStruct((B, 2 * NSA_GROUPS, T // QT, HEAD_DIM, QT), BF16),
        jax.ShapeDtypeStruct((B, NSA_GROUPS * GATE_ROWS, T), F32),
        jax.ShapeDtypeStruct((B, T, D_RWKV_IN), F32),
    )
    out_specs = (
        pl.BlockSpec((1, D_NSA, tm), lambda b, i: (b, 0, i)),
        pl.BlockSpec((1, tm, D_KV), lambda b, i: (b, i, 0)),
        pl.BlockSpec((1, tm, D_KV), lambda b, i: (b, i, 0)),
        pl.BlockSpec((1, NSA_GROUPS, tm, 2 * HEAD_DIM + ns), lambda b, i: (b, 0, i, 0)),
        pl.BlockSpec((1, NSA_GROUPS, tm, HEAD_DIM), lambda b, i: (b, 0, i, 0)),
        pl.BlockSpec((1, 2 * NSA_GROUPS, tm // QT, HEAD_DIM, QT), lambda b, i: (b, 0, i, 0, 0)),
        pl.BlockSpec((1, NSA_GROUPS * GATE_ROWS, tm), lambda b, i: (b, 0, i)),
        pl.BlockSpec((1, tm, D_RWKV_IN), lambda b, i: (b, i, 0)),
    )
    in_specs = [
        pl.BlockSpec((1, tm, D), lambda b, i: (b, i, 0)),
        pl.BlockSpec(g1.shape, const2),
        pl.BlockSpec(wstd.shape, const2),
        pl.BlockSpec(wt.shape, const2),
        pl.BlockSpec(qg.shape, const2),
        pl.BlockSpec(kg.shape, const2),
    ]
    return pl.pallas_call(
        _proj_kernel, grid=grid, in_specs=in_specs, out_specs=out_specs, out_shape=out_shape,
        compiler_params=_cparams(("parallel", "parallel")), name="proj",
    )(x, g1, wstd, wt, qg, kg)


def _gelu_tanh(x):
    c = math.sqrt(2.0 / math.pi)
    return x * (0.5 * (1.0 + jnp.tanh(c * (x + 0.044715 * (x * x * x)))))


def _compress_kernel(kc_ref, vc_ref, wbig_ref, pos_ref, w1_ref, b1_ref, w2k_ref, w2vT_ref,
                     b2k_ref, b2v_ref, kg_ref, kcmp_ref, vcmpT_ref, act_ref):
    m = kc_ref.shape[1]
    nc = m - 1
    ncp = act_ref.shape[0]
    row_m = lax.broadcasted_iota(jnp.int32, (m, CMP_HIDDEN), 0)
    row_p = lax.broadcasted_iota(jnp.int32, (ncp, HEAD_DIM), 0)
    col_p = lax.broadcasted_iota(jnp.int32, (HEAD_DIM, ncp), 1)
    act_ref[...] = jnp.zeros(act_ref.shape, F32)
    for which, src_ref in enumerate((kc_ref, vc_ref)):
        p = jnp.dot(src_ref[0], wbig_ref[which], preferred_element_type=F32)
        posc = jnp.dot(pos_ref[which], w1_ref[which], preferred_element_type=F32)[0:1]
        for g in range(NSA_GROUPS):
            top = p[:, (2 * g) * CMP_HIDDEN:(2 * g + 1) * CMP_HIDDEN]
            bot = p[:, (2 * g + 1) * CMP_HIDDEN:(2 * g + 2) * CMP_HIDDEN]
            hid = top + pltpu.roll(bot, m - 1, 0) + (b1_ref[which] + posc)
            act = jnp.where(row_m < nc, _gelu_tanh(hid), 0.0)
            act_ref[CMP_PAD:CMP_PAD + m, :] = act
            ap = act_ref[...].astype(BF16)
            if which == 0:
                kc = jnp.dot(ap, w2k_ref[...], preferred_element_type=F32) + b2k_ref[...]
                ms = jnp.mean(kc * kc, axis=-1, keepdims=True)
                kc = kc * lax.rsqrt(ms + NORM_EPS) * kg_ref[...]
                valid = (row_p >= CMP_PAD) & (row_p < CMP_PAD + nc)
                kcmp_ref[0, g] = jnp.where(valid, kc, 0.0).astype(BF16)
            else:
                vt = lax.dot_general(w2vT_ref[...], ap, (((1,), (1,)), ((), ())),
                                     preferred_element_type=F32) + b2v_ref[...]
                valid = (col_p >= CMP_PAD) & (col_p < CMP_PAD + nc)
                vcmpT_ref[0, g] = jnp.where(valid, vt, 0.0).astype(BF16)


def _compress(kc16, vc16, wbig, pos8, w1, b1, w2k, w2vT, b2k, b2v, kg, *, ncp):
    B, m, _ = kc16.shape
    full = lambda a: pl.BlockSpec(a.shape, lambda b: (0,) * a.ndim)
    return pl.pallas_call(
        _compress_kernel, grid=(B,),
        in_specs=[pl.BlockSpec((1, m, kc16.shape[2]), lambda b: (b, 0, 0)),
                  pl.BlockSpec((1, m, vc16.shape[2]), lambda b: (b, 0, 0)),
                  full(wbig), full(pos8), full(w1), full(b1), full(w2k), full(w2vT),
                  full(b2k), full(b2v), full(kg)],
        out_specs=(pl.BlockSpec((1, NSA_GROUPS, ncp, HEAD_DIM), lambda b: (b, 0, 0, 0)),
                   pl.BlockSpec((1, NSA_GROUPS, HEAD_DIM, ncp), lambda b: (b, 0, 0, 0))),
        out_shape=(jax.ShapeDtypeStruct((B, NSA_GROUPS, ncp, HEAD_DIM), BF16),
                   jax.ShapeDtypeStruct((B, NSA_GROUPS, HEAD_DIM, ncp), BF16)),
        scratch_shapes=[pltpu.VMEM((ncp, CMP_HIDDEN), F32)],
        compiler_params=_cparams(("parallel",)), name="compress",
    )(kc16, vc16, wbig, pos8, w1, b1, w2k, w2vT, b2k, b2v, kg)


def _flash_update(s, vT_tile, m_ref, l_ref, acc_ref):
    m_prev = m_ref[...]
    m_new = jnp.maximum(m_prev, jnp.max(s, axis=0, keepdims=True))
    alpha = jnp.exp(m_prev - m_new)
    p = jnp.exp(s - m_new)
    l_ref[...] = alpha * l_ref[...] + jnp.sum(p, axis=0, keepdims=True)
    acc_ref[...] = alpha * acc_ref[...] + jnp.dot(vT_tile, p.astype(BF16), preferred_element_type=F32)
    m_ref[...] = m_new


def _attn_kernel(qT_ref, kcmp_ref, vcmpT_ref, bc_ref, ksel_ref, vselT_ref, kwin_ref, vwinT_ref,
                 btab_ref, gT_ref, o_ref,
                 lc_ref, psum_ref, qaug_ref, oc_ref, ms_ref, ls_ref, accs_ref, mw_ref, lw_ref, accw_ref):
    qt = pl.program_id(2)
    ncp = kcmp_ref.shape[2]
    ns = qaug_ref.shape[0] - 2 * HEAD_DIM
    nq = NSA_HPG * QT
    t_lane = qt * QT + lax.broadcasted_iota(jnp.int32, (1, QT), 1)

    rho = lax.broadcasted_iota(jnp.int32, (ncp, QT), 0)
    valid_c = (CMP_STRIDE * rho - (CMP_STRIDE * CMP_PAD - CMP_BLOCK + 1) <= t_lane) & (rho >= CMP_PAD)
    band0 = pl.multiple_of(qt * (QT // CMP_STRIDE), V7X_SUBLANES)
    psum_ref[...] = jnp.zeros(psum_ref.shape, F32)
    for r in range(NSA_HPG):
        q_r = qT_ref[0, r * HEAD_DIM:(r + 1) * HEAD_DIM, :]
        lc_ref[...] = jnp.dot(kcmp_ref[0, 0], q_r, preferred_element_type=F32)
        lc_ref[pl.ds(band0, CMP_BAND), :] += bc_ref[r]
        lc = jnp.where(valid_c, lc_ref[...], NEG_INF)
        mx = jnp.max(lc, axis=0, keepdims=True)
        e = jnp.exp(lc - mx)
        p = jnp.where(valid_c, e / jnp.sum(e, axis=0, keepdims=True), 0.0)
        oc_ref[r] = jnp.dot(vcmpT_ref[0, 0], p.astype(BF16), preferred_element_type=F32)
        psum_ref[...] += p

    imp = psum_ref[pl.ds(CMP_PAD - 1, ns, stride=4), :]
    for k in range(1, 5):
        imp = imp + psum_ref[pl.ds(CMP_PAD - 1 + k, ns, stride=4), :]
    jrow = lax.broadcasted_iota(jnp.int32, (ns, QT), 0)
    cur = t_lane // SEL_BLOCK
    forced = (jrow == 0) | (jrow == cur) | (jrow == cur - 1)
    score = jnp.where(jrow * SEL_BLOCK <= t_lane, jnp.where(forced, FORCE_SCORE, imp), NEG_INF)
    sel = jnp.zeros((ns, QT), jnp.bool_)
    for _ in range(min(SEL_TOPN, ns)):
        mx = jnp.max(score, axis=0, keepdims=True)
        idx = jnp.min(jnp.where(score == mx, jrow, ns), axis=0, keepdims=True)
        hit = jrow == idx
        sel = sel | hit
        score = jnp.where(hit, -jnp.inf, score)
    negmask = jnp.where(sel, 0.0, NEG_INF).astype(BF16)

    qaug_ref[HEAD_DIM:2 * HEAD_DIM, :] = jnp.zeros((HEAD_DIM, nq), BF16)
    for r in range(NSA_HPG):
        qaug_ref[0:HEAD_DIM, r * QT:(r + 1) * QT] = qT_ref[0, r * HEAD_DIM:(r + 1) * HEAD_DIM, :]
        qaug_ref[2 * HEAD_DIM:, r * QT:(r + 1) * QT] = negmask

    s_loc = lax.broadcasted_iota(jnp.int32, (QT, nq), 0)
    t_loc = lax.broadcasted_iota(jnp.int32, (QT, nq), 1) % QT
    causal = s_loc <= t_loc

    def near_bias(off):
        return jnp.concatenate([btab_ref[r, off] for r in range(NSA_HPG)], axis=1)

    ms_ref[...] = jnp.full(ms_ref.shape, NEG_INF, F32)
    ls_ref[...] = jnp.zeros(ls_ref.shape, F32)
    accs_ref[...] = jnp.zeros(accs_ref.shape, F32)

    def sel_logits(kt):
        k0 = pl.multiple_of(kt * QT, QT)
        return jnp.dot(ksel_ref[0, 0, pl.ds(k0, QT), :], qaug_ref[...], preferred_element_type=F32)

    s = sel_logits(qt) + near_bias(0)
    _flash_update(jnp.where(causal, s, NEG_INF), vselT_ref[0, 0, qt], ms_ref, ls_ref, accs_ref)

    @pl.when(qt >= 1)
    def _():
        s1 = sel_logits(qt - 1) + near_bias(1)
        _flash_update(s1, vselT_ref[0, 0, qt - 1], ms_ref, ls_ref, accs_ref)

    def sel_body(kt, carry):
        _flash_update(sel_logits(kt), vselT_ref[0, 0, kt], ms_ref, ls_ref, accs_ref)
        return carry

    lax.fori_loop(0, jnp.maximum(qt - 1, 0), sel_body, 0)

    mw_ref[...] = jnp.full(mw_ref.shape, NEG_INF, F32)
    lw_ref[...] = jnp.zeros(lw_ref.shape, F32)
    accw_ref[...] = jnp.zeros(accw_ref.shape, F32)
    q4 = qaug_ref[0:HEAD_DIM, :]

    def win_logits(kt):
        k0 = pl.multiple_of(kt * QT, QT)
        return jnp.dot(kwin_ref[0, 0, pl.ds(k0, QT), :], q4, preferred_element_type=F32)

    sw = win_logits(qt) + near_bias(0)
    _flash_update(jnp.where(causal, sw, NEG_INF), vwinT_ref[0, 0, qt], mw_ref, lw_ref, accw_ref)
    n_win_tiles = WINDOW // QT
    for off in range(1, n_win_tiles + 1):
        @pl.when(qt >= off)
        def _(off=off):
            sw = win_logits(qt - off)
            if off == 1:
                sw = sw + near_bias(1)
            if off == n_win_tiles:
                sw = jnp.where(s_loc > t_loc, sw, NEG_INF)
            _flash_update(sw, vwinT_ref[0, 0, qt - off], mw_ref, lw_ref, accw_ref)

    for r in range(NSA_HPG):
        cols = slice(r * QT, (r + 1) * QT)
        o_s = accs_ref[:, cols] / ls_ref[:, cols]
        o_w = accw_ref[:, cols] / lw_ref[:, cols]
        o = (gT_ref[0, 3 * r:3 * r + 1, :] * oc_ref[r] + gT_ref[0, 3 * r + 1:3 * r + 2, :] * o_s
             + gT_ref[0, 3 * r + 2:3 * r + 3, :] * o_w)
        o_ref[0, :, r * HEAD_DIM:(r + 1) * HEAD_DIM] = o.T


def _attn(qT, kcmp, vcmpT, bc, ksel, vT5, kwin, btab, gT):
    B, _, T = qT.shape
    ncp = kcmp.shape[2]
    ns = T // SEL_BLOCK
    nq = NSA_HPG * QT
    grid = (B, NSA_GROUPS, T // QT)
    in_specs = [
        pl.BlockSpec((1, NSA_HPG * HEAD_DIM, QT), lambda b, g, q: (b, g, q)),
        pl.BlockSpec((1, 1, ncp, HEAD_DIM), lambda b, g, q: (b, g, 0, 0)),
        pl.BlockSpec((1, 1, HEAD_DIM, ncp), lambda b, g, q: (b, g, 0, 0)),
        pl.BlockSpec((NSA_HPG, CMP_BAND, QT), lambda b, g, q: (g, 0, 0)),
        pl.BlockSpec((1, 1, T, 2 * HEAD_DIM + ns), lambda b, g, q: (b, g, 0, 0)),
        pl.BlockSpec((1, 1, T // QT, HEAD_DIM, QT), lambda b, g, q: (b, g, 0, 0, 0)),
        pl.BlockSpec((1, 1, T, HEAD_DIM), lambda b, g, q: (b, g, 0, 0)),
        pl.BlockSpec((1, 1, T // QT, HEAD_DIM, QT), lambda b, g, q: (b, NSA_GROUPS + g, 0, 0, 0)),
        pl.BlockSpec((NSA_HPG, 2, QT, QT), lambda b, g, q: (g, 0, 0, 0)),
        pl.BlockSpec((1, GATE_ROWS, QT), lambda b, g, q: (b, g, q)),
    ]
    scratch = [
        pltpu.VMEM((ncp, QT), F32),
        pltpu.VMEM((ncp, QT), F32),
        pltpu.VMEM((2 * HEAD_DIM + ns, nq), BF16),
        pltpu.VMEM((NSA_HPG, HEAD_DIM, QT), F32),
        pltpu.VMEM((1, nq), F32), pltpu.VMEM((1, nq), F32), pltpu.VMEM((HEAD_DIM, nq), F32),
        pltpu.VMEM((1, nq), F32), pltpu.VMEM((1, nq), F32), pltpu.VMEM((HEAD_DIM, nq), F32),
    ]
    return pl.pallas_call(
        _attn_kernel, grid=grid, in_specs=in_specs,
        out_specs=pl.BlockSpec((1, QT, NSA_HPG * HEAD_DIM), lambda b, g, q: (b, q, g)),
        out_shape=jax.ShapeDtypeStruct((B, T, D_NSA), F32),
        scratch_shapes=scratch,
        compiler_params=_cparams(("parallel", "parallel", "arbitrary")), name="attn",
    )(qT, kcmp, vcmpT, bc, ksel, vT5, kwin, vT5, btab, gT)


def _rwkv_prep_kernel(rw_ref, halo_ref, mu_ref, w0_ref, w2_ref, a0_ref, a2_ref, g2_ref, kk_ref, ka_ref,
                      rk_ref, bd_ref, r_out, lw_out, k_out, v_out, kk_out, b_out, g_out, bonus_out, ext_ref):
    tm = rw_ref.shape[1]
    first_tile = pl.program_id(1) == 0
    ext_ref[0:V7X_SUBLANES, :] = jnp.where(first_tile, 0.0, halo_ref[0])
    ext_ref[V7X_SUBLANES:, :] = rw_ref[0]
    cur = rw_ref[0]
    prev = ext_ref[pl.ds(V7X_SUBLANES - 1, tm), :]
    mixed = cur + (prev - cur) * mu_ref[...]
    c = D_RWKV
    r = mixed[:, 0:c]
    k = mixed[:, c:2 * c]
    v = mixed[:, 2 * c:3 * c]
    xw = mixed[:, 3 * c:3 * c + LORA_W]
    xa = mixed[:, 3 * c + LORA_W:3 * c + LORA_W + LORA_A]
    xg = mixed[:, 3 * c + LORA_W + LORA_A:]

    z = -(w0_ref[...] + _bdot(jnp.tanh(xw), w2_ref[...]))
    softplus = jnp.maximum(z, 0.0) + jnp.log1p(jnp.exp(-jnp.abs(z)))
    w = -softplus - 0.5
    a = jax.nn.sigmoid(a0_ref[...] + _bdot(xa, a2_ref[...]))
    g = _bdot(jax.nn.sigmoid(xg), g2_ref[...])

    kk = k * kk_ref[...]
    n2 = _dot_exact_rhs(kk * kk, bd_ref[...])
    kkn = kk / jnp.maximum(jnp.sqrt(n2), 1e-12)
    k2 = k * (1.0 + (a - 1.0) * ka_ref[...])
    bonus = _dot_exact_rhs(r * k2 * rk_ref[...], bd_ref[...]) * v

    r_out[0] = r
    lw_out[0] = -jnp.exp(w)
    k_out[0] = k2
    v_out[0] = v
    kk_out[0] = kkn
    b_out[0] = kkn * a
    g_out[0] = g
    bonus_out[0] = bonus


def _rwkv_prep(rw, mu, w0, w2, a0, a2, g2, k_k, k_a, r_k, bd, *, tm):
    B, T, C = rw.shape
    hb = tm // V7X_SUBLANES
    full = lambda a: pl.BlockSpec(a.shape, lambda b, i: (0,) * a.ndim)
    tok = pl.BlockSpec((1, tm, D_RWKV), lambda b, i: (b, i, 0))
    return pl.pallas_call(
        _rwkv_prep_kernel, grid=(B, T // tm),
        in_specs=[pl.BlockSpec((1, tm, C), lambda b, i: (b, i, 0)),
                  pl.BlockSpec((1, V7X_SUBLANES, C), lambda b, i: (b, jnp.maximum(i * hb - 1, 0), 0)),
                  full(mu), full(w0), full(w2), full(a0), full(a2), full(g2), full(k_k), full(k_a),
                  full(r_k), full(bd)],
        out_specs=(tok,) * 8,
        out_shape=(jax.ShapeDtypeStruct((B, T, D_RWKV), F32),) * 8,
        scratch_shapes=[pltpu.VMEM((tm + V7X_SUBLANES, C), F32)],
        compiler_params=_cparams(("parallel", "parallel")), name="rwkv_prep",
    )(rw, rw, mu, w0, w2, a0, a2, g2, k_k, k_a, r_k, bd)


def _wkv_kernel(r_ref, lw_ref, k_ref, v_ref, kk_ref, b_ref, lnw_ref, lnb_ref, y_ref, s_ref):
    L = r_ref.shape[1]
    n = HEAD_DIM

    @pl.when(pl.program_id(1) == 0)
    def _():
        s_ref[...] = jnp.zeros(s_ref.shape, F32)

    lw = lw_ref[0]
    ti = lax.broadcasted_iota(jnp.int32, (L, L), 0)
    tj = lax.broadcasted_iota(jnp.int32, (L, L), 1)
    tri = jnp.where(ti >= tj, 1.0, 0.0).astype(BF16)
    cs = _dot_exact_rhs_left(tri, lw)
    c_last = cs[L - 1:L, :]
    e_c = jnp.exp(cs)
    e_nc = jnp.exp(-cs)
    a_t = -kk_ref[0] * jnp.exp(cs - lw)
    r_t = r_ref[0] * e_c
    b_h = b_ref[0] * e_nc
    k_h = k_ref[0] * e_nc
    e_lc = jnp.exp(c_last - cs)
    b_e = b_ref[0] * e_lc
    k_e = k_ref[0] * e_lc
    e_last = jnp.exp(c_last)
    v_all = v_ref[0]

    upper_strict = ti < tj
    upper_incl = ti <= tj
    eye = jnp.where(ti == tj, 1.0, 0.0)
    for h in range(RWKV_HEADS):
        sl = slice(h * n, (h + 1) * n)
        lhs = jnp.concatenate([b_h[:, sl], k_h[:, sl]], axis=0)
        rhs = jnp.concatenate([a_t[:, sl], r_t[:, sl]], axis=0)
        tm_t = _bdot_nt(lhs, rhs)
        tab = jnp.where(upper_strict, tm_t[0:L, 0:L], 0.0)
        tak = jnp.where(upper_strict, tm_t[L:, 0:L], 0.0)
        trb = jnp.where(upper_incl, tm_t[0:L, L:], 0.0)
        trk = jnp.where(upper_incl, tm_t[L:, L:], 0.0)
        x = eye + tab
        pw = tab
        for _ in range(int(math.log2(L)) - 1):
            pw = _bdot(pw, pw)
            x = x + _bdot(x, pw)
        a_T = a_t[:, sl].T
        r_T = r_t[:, sl].T
        v_T = v_all[:, sl].T
        s = s_ref[h]
        u_T = _bdot(_bdot(s, a_T) + _bdot(v_T, tak), x)
        y_T = _bdot(s, r_T) + _bdot(u_T, trb) + _bdot(v_T, trk)
        s_ref[h] = s * e_last[:, sl] + _bdot(u_T, b_e[:, sl]) + _bdot(v_T, k_e[:, sl])
        mu = jnp.mean(y_T, axis=0, keepdims=True)
        var = jnp.mean(jnp.square(y_T - mu), axis=0, keepdims=True)
        y_n = (y_T - mu) * lax.rsqrt(var + GN_EPS) * lnw_ref[sl, :] + lnb_ref[sl, :]
        y_ref[0, :, sl] = y_n.T


def _dot_exact_rhs_left(m_bf16, x):
    hi, mid, lo = _split3(x)
    return (jnp.dot(m_bf16, hi, preferred_element_type=F32)
            + jnp.dot(m_bf16, mid, preferred_element_type=F32)
            + jnp.dot(m_bf16, lo, preferred_element_type=F32))


def _wkv(r, lw, k, v, kk, b, lnw, lnb):
    B, T, C = r.shape
    L = WKV_CHUNK
    tok = pl.BlockSpec((1, L, C), lambda bb, c: (bb, c, 0))
    full = lambda a: pl.BlockSpec(a.shape, lambda bb, c: (0,) * a.ndim)
    return pl.pallas_call(
        _wkv_kernel, grid=(B, T // L),
        in_specs=[tok] * 6 + [full(lnw), full(lnb)],
        out_specs=tok,
        out_shape=jax.ShapeDtypeStruct((B, T, C), F32),
        scratch_shapes=[pltpu.VMEM((RWKV_HEADS, HEAD_DIM, HEAD_DIM), F32)],
        compiler_params=_cparams(("parallel", "arbitrary")), name="wkv",
    )(r, lw, k, v, kk, b, lnw, lnb)


def _mix_kernel(x_ref, on_ref, y_ref, bonus_ref, g_ref, wo_ref, g2_ref, x1_ref, h2_ref):
    orw = (y_ref[...] + bonus_ref[...]) * g_ref[...]
    x1 = (x_ref[...] + _bdot(on_ref[...], wo_ref[0:D_NSA, :]) + _bdot(orw, wo_ref[D_NSA:, :]))
    x1_ref[...] = x1
    ms = jnp.mean(x1 * x1, axis=-1, keepdims=True)
    h2_ref[...] = (x1 * lax.rsqrt(ms + NORM_EPS) * g2_ref[...]).astype(BF16)


def _mix(x2d, on2d, y2d, bonus2d, g2d, wo, g2, *, tm):
    n, d = x2d.shape
    tok = lambda w: pl.BlockSpec((tm, w), lambda i: (i, 0))
    full = lambda a: pl.BlockSpec(a.shape, lambda i: (0,) * a.ndim)
    return pl.pallas_call(
        _mix_kernel, grid=(n // tm,),
        in_specs=[tok(d), tok(D_NSA), tok(D_RWKV), tok(D_RWKV), tok(D_RWKV), full(wo), full(g2)],
        out_specs=(tok(d), tok(d)),
        out_shape=(jax.ShapeDtypeStruct((n, d), F32), jax.ShapeDtypeStruct((n, d), BF16)),
        compiler_params=_cparams(("parallel",)), name="mix",
    )(x2d, on2d, y2d, bonus2d, g2d, wo, g2)


FFN_HALO = 16


def _ffn_kernel(x1_ref, h2_ref, halo_ref, wv_ref, wg_ref, cwv_ref, cwg_ref, cbv_ref, cbg_ref, wd_ref,
                o_ref, ext_ref, acc_ref, *, tiles_per_seq):
    tm = h2_ref.shape[0]
    j = pl.program_id(1)
    seq_start = (pl.program_id(0) % tiles_per_seq) == 0

    @pl.when(j == 0)
    def _():
        acc_ref[...] = jnp.zeros(acc_ref.shape, F32)

    h_main = h2_ref[...]
    h_halo = halo_ref[...]

    def conv_branch(w_ref, cw_ref, cb_ref):
        u_halo = jnp.dot(h_halo, w_ref[...], preferred_element_type=F32)
        ext_ref[0:FFN_HALO, :] = jnp.where(seq_start, 0.0, u_halo)
        ext_ref[FFN_HALO:, :] = jnp.dot(h_main, w_ref[...], preferred_element_type=F32)
        out = cb_ref[...] + ext_ref[pl.ds(FFN_HALO, tm), :] * cw_ref[CONV_W - 1:CONV_W, :]
        for i in range(CONV_W - 1):
            back = CONV_W - 1 - i
            out = out + ext_ref[pl.ds(FFN_HALO - back, tm), :] * cw_ref[i:i + 1, :]
        return out

    u_val = conv_branch(wv_ref, cwv_ref, cbv_ref)
    u_gate = conv_branch(wg_ref, cwg_ref, cbg_ref)
    act = (u_gate * jax.nn.sigmoid(u_gate)) * u_val
    acc_ref[...] += jnp.dot(act.astype(BF16), wd_ref[...], preferred_element_type=F32)

    @pl.when(j == pl.num_programs(1) - 1)
    def _():
        o_ref[...] = x1_ref[...] + acc_ref[...]


def _ffn(x1, h2, wv, wg, cwv, cwg, cbv, cbg, wd, *, tm, ft, seq_len):
    n, d = x1.shape
    dff = wv.shape[1]
    hb = tm // FFN_HALO
    grid = (n // tm, dff // ft)
    kern = functools.partial(_ffn_kernel, tiles_per_seq=seq_len // tm)
    return pl.pallas_call(
        kern, grid=grid,
        in_specs=[pl.BlockSpec((tm, d), lambda i, j: (i, 0)),
                  pl.BlockSpec((tm, d), lambda i, j: (i, 0)),
                  pl.BlockSpec((FFN_HALO, d), lambda i, j: (jnp.maximum(i * hb - 1, 0), 0)),
                  pl.BlockSpec((d, ft), lambda i, j: (0, j)),
                  pl.BlockSpec((d, ft), lambda i, j: (0, j)),
                  pl.BlockSpec((CONV_W, ft), lambda i, j: (0, j)),
                  pl.BlockSpec((CONV_W, ft), lambda i, j: (0, j)),
                  pl.BlockSpec((1, ft), lambda i, j: (0, j)),
                  pl.BlockSpec((1, ft), lambda i, j: (0, j)),
                  pl.BlockSpec((ft, d), lambda i, j: (j, 0))],
        out_specs=pl.BlockSpec((tm, d), lambda i, j: (i, 0)),
        out_shape=jax.ShapeDtypeStruct((n, d), F32),
        scratch_shapes=[pltpu.VMEM((tm + FFN_HALO, ft), F32), pltpu.VMEM((tm, d), F32)],
        compiler_params=_cparams(("parallel", "arbitrary")), name="ffn",
    )(x1, h2, h2, wv, wg, cwv, cwg, cbv, cbg, wd)


def _t5_bucket(dist):
    n = jnp.maximum(dist, 0)
    max_exact = N_BUCKETS // 2
    nf = jnp.maximum(n, 1).astype(F32)
    large = max_exact + (jnp.log(nf / max_exact) / math.log(MAX_DISTANCE / max_exact)
                         * (N_BUCKETS - max_exact)).astype(jnp.int32)
    large = jnp.minimum(large, N_BUCKETS - 1)
    return jnp.where(n < max_exact, n, large)


def _bias_tables(rel_bias):
    dmax = 2 * QT
    tab = rel_bias[_t5_bucket(jnp.arange(dmax))] - rel_bias[N_BUCKETS - 1][None, :]

    def lookup(d):
        return jnp.where((d >= 0)[..., None], tab[jnp.clip(d, 0, dmax - 1)], 0.0)

    s_idx = jnp.arange(QT)[:, None]
    t_idx = jnp.arange(QT)[None, :]
    near = jnp.stack([lookup(off * QT + t_idx - s_idx) for off in range(2)])
    btab = near.transpose(3, 0, 1, 2)
    rho = jnp.arange(CMP_BAND)[:, None]
    d_c = t_idx - CMP_STRIDE * rho + (CMP_STRIDE * CMP_PAD - CMP_BLOCK + 1)
    bc = lookup(d_c).transpose(2, 0, 1)
    return btab.astype(F32), bc.astype(F32)


def _compress_weights(w1):
    half = CMP_BLOCK // 2
    w1r = w1.reshape(2, half, HEAD_DIM, CMP_HIDDEN).transpose(1, 2, 0, 3)
    big = jnp.zeros((half, NSA_GROUPS, HEAD_DIM, NSA_GROUPS, 2, CMP_HIDDEN), w1.dtype)
    for g in range(NSA_GROUPS):
        big = big.at[:, g, :, g, :, :].set(w1r)
    return big.reshape(half * NSA_GROUPS * HEAD_DIM, NSA_GROUPS * 2 * CMP_HIDDEN)


def kernel(x, norm1_g, w_in, q_norm_g, k_norm_g, cmp_pos, cmp_w1, cmp_b1, cmp_w2, cmp_b2, rel_bias, rwkv_mu,
           w0, w2, a0, a2, g2, k_k, k_a, r_k, ln_x_w, ln_x_b, w_out, norm2_g, ffn_up, conv_w, conv_b, ffn_down):
    B, T, D = x.shape
    depth = w_in.shape[0]
    d_ff = ffn_down.shape[1]
    assert T % 2048 == 0 and D_NSA + 6 * D_KV + 3 * NSA_HEADS + D_RWKV_IN == w_in.shape[2]
    ncp = T // CMP_STRIDE + QT
    btab, bc = _bias_tables(rel_bias)
    ii = jnp.arange(D_RWKV)
    bd = (ii[:, None] // HEAD_DIM == ii[None, :] // HEAD_DIM).astype(BF16)

    for l in range(depth):
        wi = w_in[l]
        o = D_NSA
        q_w, kc_w, vc_w, ksl_w, vsl_w, kwn_w, vwn_w = (
            wi[:, 0:o], wi[:, o:o + D_KV], wi[:, o + D_KV:o + 2 * D_KV], wi[:, o + 2 * D_KV:o + 3 * D_KV],
            wi[:, o + 3 * D_KV:o + 4 * D_KV], wi[:, o + 4 * D_KV:o + 5 * D_KV], wi[:, o + 5 * D_KV:o + 6 * D_KV])
        gl_w = wi[:, o + 6 * D_KV:o + 6 * D_KV + 3 * NSA_HEADS]
        rw_w = wi[:, o + 6 * D_KV + 3 * NSA_HEADS:]
        wstd = jnp.concatenate([kc_w, vc_w, ksl_w, kwn_w, rw_w], axis=1).astype(BF16)
        gl_rows = gl_w.T.reshape(NSA_GROUPS, 3 * NSA_HPG, D)
        gl_rows = jnp.pad(gl_rows, ((0, 0), (0, GATE_ROWS - 3 * NSA_HPG), (0, 0))).reshape(-1, D)
        wt = jnp.concatenate([q_w.T, vsl_w.T, vwn_w.T, gl_rows], axis=0).astype(BF16)
        qg = q_norm_g[l].reshape(HEAD_DIM, 1)
        kg = jnp.stack([jnp.tile(k_norm_g[l, 1], NSA_GROUPS), jnp.tile(k_norm_g[l, 2], NSA_GROUPS)])

        qT, kc, vc, ksel, kwin, vT5, gT, rw = _proj(x, norm1_g[l].reshape(1, D), wstd, wt, qg, kg, tm=512)

        wbig = jnp.stack([_compress_weights(cmp_w1[l, 0]), _compress_weights(cmp_w1[l, 1])]).astype(BF16)
        pos8 = jnp.pad(cmp_pos[l].reshape(2, 1, CMP_BLOCK * HEAD_DIM),
                       ((0, 0), (0, V7X_SUBLANES - 1), (0, 0))).astype(BF16)
        rows16 = CMP_STRIDE * D_KV
        kcmp, vcmpT = _compress(
            kc.reshape(B, T // CMP_STRIDE, rows16), vc.reshape(B, T // CMP_STRIDE, rows16), wbig, pos8,
            cmp_w1[l].astype(BF16), cmp_b1[l].reshape(2, 1, CMP_HIDDEN), cmp_w2[l, 0].astype(BF16),
            cmp_w2[l, 1].T.astype(BF16), cmp_b2[l, 0].reshape(1, HEAD_DIM), cmp_b2[l, 1].reshape(HEAD_DIM, 1),
            k_norm_g[l, 0].reshape(1, HEAD_DIM), ncp=ncp)

        o_nsa = _attn(qT, kcmp, vcmpT, bc, ksel, vT5, kwin, btab, gT)

        row = lambda a: a.reshape(1, -1)
        r, lw, k2, v, kkn, bvec, g, bonus = _rwkv_prep(
            rw, row(rwkv_mu[l]), row(w0[l]), w2[l].astype(BF16), row(a0[l]), a2[l].astype(BF16),
            g2[l].astype(BF16), row(k_k[l]), row(k_a[l]), row(r_k[l]), bd, tm=256)
        y = _wkv(r, lw, k2, v, kkn, bvec, ln_x_w[l].reshape(-1, 1), ln_x_b[l].reshape(-1, 1))

        n = B * T
        x1, h2 = _mix(x.reshape(n, D), o_nsa.reshape(n, D_NSA), y.reshape(n, D_RWKV),
                      bonus.reshape(n, D_RWKV), g.reshape(n, D_RWKV), w_out[l].astype(BF16),
                      norm2_g[l].reshape(1, D), tm=512)

        up = ffn_up[l].astype(BF16)
        x = _ffn(x1, h2, up[:, :d_ff], up[:, d_ff:], conv_w[l][:, :d_ff], conv_w[l][:, d_ff:],
                 conv_b[l][:d_ff].reshape(1, -1), conv_b[l][d_ff:].reshape(1, -1),
                 ffn_down[l].astype(BF16), tm=512, ft=256, seq_len=T).reshape(B, T, D)
    return x
```

```python
import functools
import math

import jax
import jax.numpy as jnp
import numpy as np
from jax import lax
from jax.experimental import pallas as pl
from jax.experimental.pallas import tpu as pltpu

F32 = jnp.float32
BF16 = jnp.bfloat16

V7X_LANES = 128
V7X_SUBLANES = 8
V7X_VMEM_LIMIT_BYTES = 56 * 1024 * 1024

HEAD_DIM = 64
NSA_HEADS = 8
NSA_GROUPS = 2
NSA_HPG = NSA_HEADS // NSA_GROUPS
RWKV_HEADS = 8
D_NSA = NSA_HEADS * HEAD_DIM
D_RWKV = RWKV_HEADS * HEAD_DIM
D_KV = NSA_GROUPS * HEAD_DIM
CMP_BLOCK = 32
CMP_STRIDE = 16
CMP_HIDDEN = 128
SEL_BLOCK = 64
SEL_TOPN = 16
WINDOW = 512
N_BUCKETS = 32
MAX_DISTANCE = 128
LORA_W = 64
LORA_A = 64
LORA_G = 128
D_RWKV_IN = 3 * D_RWKV + LORA_W + LORA_A + LORA_G
CONV_W = 3
NORM_EPS = 1e-6
GN_EPS = 64e-5
NEG_INF = -1e30
FORCE_SCORE = 1e9

QT = 128
CMP_PAD = 8
CMP_BAND = 16
ATT_PAD = WINDOW
NEAR_KEYS = 2 * QT
FAR_KEYS = 4 * QT
WIN_KEYS = WINDOW + QT
WKV_CHUNK = 64
GATE_ROWS = 16
N_T_ROWS = D_NSA + 2 * D_KV + NSA_GROUPS * GATE_ROWS
N_STD_COLS = 4 * D_KV + D_RWKV_IN


def _cparams(sem):
    return pltpu.CompilerParams(dimension_semantics=sem, vmem_limit_bytes=V7X_VMEM_LIMIT_BYTES)


def _bdot(a, b):
    return jnp.dot(a.astype(BF16), b.astype(BF16), preferred_element_type=F32)


def _bdot_nt(a, b):
    return lax.dot_general(a.astype(BF16), b.astype(BF16), (((1,), (1,)), ((), ())),
                           preferred_element_type=F32)


def _split3(x):
    hi = x.astype(BF16)
    r1 = x - hi.astype(F32)
    mid = r1.astype(BF16)
    lo = (r1 - mid.astype(F32)).astype(BF16)
    return hi, mid, lo


def _dot_exact_rhs(x, m_bf16):
    hi, mid, lo = _split3(x)
    return (jnp.dot(hi, m_bf16, preferred_element_type=F32)
            + jnp.dot(mid, m_bf16, preferred_element_type=F32)
            + jnp.dot(lo, m_bf16, preferred_element_type=F32))


def _proj_kernel(x_ref, g1_ref, wstd_ref, wt_ref, qg_ref, kg_ref,
                 qT_ref, kc_ref, vc_ref, ksel_ref, kwin_ref, vT_ref, gT_ref, rw_ref):
    tm = x_ref.shape[1]
    ns = ksel_ref.shape[3] - 2 * HEAD_DIM
    x = x_ref[0]
    ms = jnp.mean(x * x, axis=-1, keepdims=True)
    h = (x * lax.rsqrt(ms + NORM_EPS) * g1_ref[...]).astype(BF16)

    std = jnp.dot(h, wstd_ref[...], preferred_element_type=F32)
    kc_ref[0] = std[:, 0:D_KV].astype(BF16)
    vc_ref[0] = std[:, D_KV:2 * D_KV].astype(BF16)
    rw_ref[0] = std[:, 4 * D_KV:]

    lane = lax.broadcasted_iota(jnp.int32, (tm, D_KV), 1)
    first = lane < HEAD_DIM

    def group_rmsnorm(k, gain):
        k2 = k * k
        s0 = jnp.sum(jnp.where(first, k2, 0.0), axis=-1, keepdims=True)
        s1 = jnp.sum(jnp.where(first, 0.0, k2), axis=-1, keepdims=True)
        ms_g = jnp.where(first, s0, s1) * (1.0 / HEAD_DIM)
        return k * lax.rsqrt(ms_g + NORM_EPS) * gain

    ksl = group_rmsnorm(std[:, 2 * D_KV:3 * D_KV], kg_ref[0:1, :]).astype(BF16)
    kwn = group_rmsnorm(std[:, 3 * D_KV:4 * D_KV], kg_ref[1:2, :]).astype(BF16)

    tok = pl.program_id(1) * tm + lax.broadcasted_iota(jnp.int32, (tm, ns), 0)
    blk = lax.broadcasted_iota(jnp.int32, (tm, ns), 1)
    onehot = jnp.where((tok // SEL_BLOCK) == blk, 1.0, 0.0).astype(BF16)
    zeros = jnp.zeros((tm, HEAD_DIM), BF16)
    for g in range(NSA_GROUPS):
        ksel_ref[0, g, :, 0:HEAD_DIM] = ksl[:, g * HEAD_DIM:(g + 1) * HEAD_DIM]
        ksel_ref[0, g, :, HEAD_DIM:2 * HEAD_DIM] = zeros
        ksel_ref[0, g, :, 2 * HEAD_DIM:] = onehot
        kwin_ref[0, g, :, 0:HEAD_DIM] = kwn[:, g * HEAD_DIM:(g + 1) * HEAD_DIM]
        kwin_ref[0, g, :, HEAD_DIM:] = zeros

    tr = lax.dot_general(wt_ref[...], h, (((1,), (1,)), ((), ())),
                         preferred_element_type=F32)
    q = tr[0:D_NSA].reshape(NSA_HEADS, HEAD_DIM, tm)
    qms = jnp.mean(q * q, axis=1, keepdims=True)
    qn = q * lax.rsqrt(qms + NORM_EPS) * qg_ref[...][None] * (HEAD_DIM ** -0.5)
    qT_ref[0] = qn.reshape(D_NSA, tm).astype(BF16)
    vt = tr[D_NSA:D_NSA + 2 * D_KV].astype(BF16)
    for a in range(2 * NSA_GROUPS):
        for j in range(tm // QT):
            vT_ref[0, a, j] = vt[a * HEAD_DIM:(a + 1) * HEAD_DIM, j * QT:(j + 1) * QT]
    gT_ref[0] = jax.nn.sigmoid(tr[D_NSA + 2 * D_KV:])


def _proj(x, g1, wstd, wt, qg, kg, *, tm):
    B, T, D = x.shape
    ns = T // SEL_BLOCK
    grid = (B, T // tm)
    const2 = lambda b, i: (0, 0)
    out_shape = (
        jax.ShapeDtypeStruct((B, D_NSA, T), BF16),
        jax.ShapeDtypeStruct((B, T, D_KV), BF16),
        jax.ShapeDtypeStruct((B, T, D_KV), BF16),
        jax.ShapeDtypeStruct((B, NSA_GROUPS, T, 2 * HEAD_DIM + ns), BF16),
        jax.ShapeDtypeStruct((B, NSA_GROUPS, T, 2 * HEAD_DIM), BF16),
        jax.ShapeDtypeStruct((B, 2 * NSA_GROUPS, T // QT, HEAD_DIM, QT), BF16),
        jax.ShapeDtypeStruct((B, NSA_GROUPS * GATE_ROWS, T), F32),
        jax.ShapeDtypeStruct((B, T, D_RWKV_IN), F32),
    )
    out_specs = (
        pl.BlockSpec((1, D_NSA, tm), lambda b, i: (b, 0, i)),
        pl.BlockSpec((1, tm, D_KV), lambda b, i: (b, i, 0)),
        pl.BlockSpec((1, tm, D_KV), lambda b, i: (b, i, 0)),
        pl.BlockSpec((1, NSA_GROUPS, tm, 2 * HEAD_DIM + ns), lambda b, i: (b, 0, i, 0)),
        pl.BlockSpec((1, NSA_GROUPS, tm, 2 * HEAD_DIM), lambda b, i: (b, 0, i, 0)),
        pl.BlockSpec((1, 2 * NSA_GROUPS, tm // QT, HEAD_DIM, QT), lambda b, i: (b, 0, i, 0, 0)),
        pl.BlockSpec((1, NSA_GROUPS * GATE_ROWS, tm), lambda b, i: (b, 0, i)),
        pl.BlockSpec((1, tm, D_RWKV_IN), lambda b, i: (b, i, 0)),
    )
    in_specs = [
        pl.BlockSpec((1, tm, D), lambda b, i: (b, i, 0)),
        pl.BlockSpec(g1.shape, const2),
        pl.BlockSpec(wstd.shape, const2),
        pl.BlockSpec(wt.shape, const2),
        pl.BlockSpec(qg.shape, const2),
        pl.BlockSpec(kg.shape, const2),
    ]
    return pl.pallas_call(
        _proj_kernel, grid=grid, in_specs=in_specs, out_specs=out_specs, out_shape=out_shape,
        compiler_params=_cparams(("parallel", "parallel")), name="proj",
    )(x, g1, wstd, wt, qg, kg)


def _gelu_tanh(x):
    c = math.sqrt(2.0 / math.pi)
    return x * (0.5 * (1.0 + jnp.tanh(c * (x + 0.044715 * (x * x * x)))))


def _compress_kernel(kc_ref, vc_ref, wbig_ref, pos_ref, w1_ref, b1_ref, w2k_ref, w2vT_ref,
                     b2k_ref, b2v_ref, kg_ref, kcmp_ref, vcmpT_ref, act_ref):
    m = kc_ref.shape[1]
    nc = m - 1
    ncp = act_ref.shape[0]
    row_m = lax.broadcasted_iota(jnp.int32, (m, CMP_HIDDEN), 0)
    row_p = lax.broadcasted_iota(jnp.int32, (ncp, HEAD_DIM), 0)
    col_p = lax.broadcasted_iota(jnp.int32, (HEAD_DIM, ncp), 1)
    act_ref[...] = jnp.zeros(act_ref.shape, F32)
    for which, src_ref in enumerate((kc_ref, vc_ref)):
        p = jnp.dot(src_ref[0], wbig_ref[which], preferred_element_type=F32)
        posc = jnp.dot(pos_ref[which], w1_ref[which], preferred_element_type=F32)[0:1]
        for g in range(NSA_GROUPS):
            top = p[:, (2 * g) * CMP_HIDDEN:(2 * g + 1) * CMP_HIDDEN]
            bot = p[:, (2 * g + 1) * CMP_HIDDEN:(2 * g + 2) * CMP_HIDDEN]
            hid = top + pltpu.roll(bot, m - 1, 0) + (b1_ref[which] + posc)
            act = jnp.where(row_m < nc, _gelu_tanh(hid), 0.0)
            act_ref[CMP_PAD:CMP_PAD + m, :] = act
            ap = act_ref[...].astype(BF16)
            if which == 0:
                kc = jnp.dot(ap, w2k_ref[...], preferred_element_type=F32) + b2k_ref[...]
                ms = jnp.mean(kc * kc, axis=-1, keepdims=True)
                kc = kc * lax.rsqrt(ms + NORM_EPS) * kg_ref[...]
                valid = (row_p >= CMP_PAD) & (row_p < CMP_PAD + nc)
                kcmp_ref[0, g] = jnp.where(valid, kc, 0.0).astype(BF16)
            else:
                vt = lax.dot_general(w2vT_ref[...], ap, (((1,), (1,)), ((), ())),
                                     preferred_element_type=F32) + b2v_ref[...]
                valid = (col_p >= CMP_PAD) & (col_p < CMP_PAD + nc)
                vcmpT_ref[0, g] = jnp.where(valid, vt, 0.0).astype(BF16)


def _compress(kc16, vc16, wbig, pos8, w1, b1, w2k, w2vT, b2k, b2v, kg, *, ncp):
    B, m, _ = kc16.shape
    full = lambda a: pl.BlockSpec(a.shape, lambda b: (0,) * a.ndim)
    return pl.pallas_call(
        _compress_kernel, grid=(B,),
        in_specs=[pl.BlockSpec((1, m, kc16.shape[2]), lambda b: (b, 0, 0)),
                  pl.BlockSpec((1, m, vc16.shape[2]), lambda b: (b, 0, 0)),
                  full(wbig), full(pos8), full(w1), full(b1), full(w2k), full(w2vT),
                  full(b2k), full(b2v), full(kg)],
        out_specs=(pl.BlockSpec((1, NSA_GROUPS, ncp, HEAD_DIM), lambda b: (b, 0, 0, 0)),
                   pl.BlockSpec((1, NSA_GROUPS, HEAD_DIM, ncp), lambda b: (b, 0, 0, 0))),
        out_shape=(jax.ShapeDtypeStruct((B, NSA_GROUPS, ncp, HEAD_DIM), BF16),
                   jax.ShapeDtypeStruct((B, NSA_GROUPS, HEAD_DIM, ncp), BF16)),
        scratch_shapes=[pltpu.VMEM((ncp, CMP_HIDDEN), F32)],
        compiler_params=_cparams(("parallel",)), name="compress",
    )(kc16, vc16, wbig, pos8, w1, b1, w2k, w2vT, b2k, b2v, kg)


def _flash_update(s, vT_tile, m_ref, l_ref, acc_ref):
    m_prev = m_ref[...]
    m_new = jnp.maximum(m_prev, jnp.max(s, axis=0, keepdims=True))
    alpha = jnp.exp(m_prev - m_new)
    p = jnp.exp(s - m_new)
    l_ref[...] = alpha * l_ref[...] + jnp.sum(p, axis=0, keepdims=True)
    acc_ref[...] = alpha * acc_ref[...] + jnp.dot(vT_tile, p.astype(BF16), preferred_element_type=F32)
    m_ref[...] = m_new


def _attn_kernel(qT_ref, kcmp_ref, vcmpT_ref, bc_ref, ksel_ref, vselT_ref, kwin_ref, vwinT_ref,
                 stab_ref, wtab_ref, gT_ref, o_ref,
                 lc_ref, psum_ref, qaug_ref, qfar_ref, oc_ref, ow_ref, ms_ref, ls_ref, accs_ref):
    qt = pl.program_id(2)
    ncp = kcmp_ref.shape[2]
    ns = qaug_ref.shape[0] - 2 * HEAD_DIM
    nq = NSA_HPG * QT
    t_lane = qt * QT + lax.broadcasted_iota(jnp.int32, (1, QT), 1)

    for r in range(NSA_HPG):
        qaug_ref[0:HEAD_DIM, r * QT:(r + 1) * QT] = qT_ref[0, r * HEAD_DIM:(r + 1) * HEAD_DIM, :]
    flag_row = lax.broadcasted_iota(jnp.int32, (HEAD_DIM, nq), 0) == 0
    qaug_ref[HEAD_DIM:2 * HEAD_DIM, :] = jnp.where(flag_row, NEG_INF, 0.0).astype(BF16)
    q4 = qaug_ref[0:HEAD_DIM, :]

    rho = lax.broadcasted_iota(jnp.int32, (ncp, nq), 0)
    t_all = qt * QT + lax.broadcasted_iota(jnp.int32, (ncp, nq), 1) % QT
    valid_c = (CMP_STRIDE * rho - (CMP_STRIDE * CMP_PAD - CMP_BLOCK + 1) <= t_all) & (rho >= CMP_PAD)
    band0 = pl.multiple_of(qt * (QT // CMP_STRIDE), V7X_SUBLANES)
    lc_ref[...] = jnp.dot(kcmp_ref[0, 0], q4, preferred_element_type=F32)
    lc_ref[pl.ds(band0, CMP_BAND), :] += bc_ref[0]
    lc = jnp.where(valid_c, lc_ref[...], NEG_INF)
    e = jnp.exp(lc - jnp.max(lc, axis=0, keepdims=True))
    p = jnp.where(valid_c, e * (1.0 / jnp.sum(e, axis=0, keepdims=True)), 0.0)
    oc_ref[...] = jnp.dot(vcmpT_ref[0, 0], p.astype(BF16), preferred_element_type=F32)
    psum_ref[...] = (p[:, 0:QT] + p[:, QT:2 * QT]) + (p[:, 2 * QT:3 * QT] + p[:, 3 * QT:4 * QT])

    imp = psum_ref[pl.ds(CMP_PAD - 1, ns, stride=4), :]
    for k in range(1, 5):
        imp = imp + psum_ref[pl.ds(CMP_PAD - 1 + k, ns, stride=4), :]
    jrow = lax.broadcasted_iota(jnp.int32, (ns, QT), 0)
    cur = t_lane // SEL_BLOCK
    forced = (jrow == 0) | (jrow == cur) | (jrow == cur - 1)
    score = jnp.where(jrow * SEL_BLOCK <= t_lane, jnp.where(forced, FORCE_SCORE, imp), NEG_INF)
    sel = jnp.zeros((ns, QT), jnp.bool_)
    for _ in range(min(SEL_TOPN, ns)):
        mx = jnp.max(score, axis=0, keepdims=True)
        idx = jnp.min(jnp.where(score == mx, jrow, ns), axis=0, keepdims=True)
        hit = jrow == idx
        sel = sel | hit
        score = jnp.where(hit, -jnp.inf, score)
    negmask = jnp.where(sel, 0.0, NEG_INF).astype(BF16)
    negfar = jnp.where(sel & (jrow < (qt - 1) * (QT // SEL_BLOCK)), 0.0, NEG_INF).astype(BF16)
    qfar_ref[0:2 * HEAD_DIM, :] = qaug_ref[0:2 * HEAD_DIM, :]
    for r in range(NSA_HPG):
        qaug_ref[2 * HEAD_DIM:, r * QT:(r + 1) * QT] = negmask
        qfar_ref[2 * HEAD_DIM:, r * QT:(r + 1) * QT] = negfar

    w0 = pl.multiple_of(qt * QT, QT)
    sw = jnp.dot(kwin_ref[0, 0, pl.ds(w0, WIN_KEYS), :], qaug_ref[0:2 * HEAD_DIM, :],
                 preferred_element_type=F32) + wtab_ref[0]
    pw = jnp.exp(sw - jnp.max(sw, axis=0, keepdims=True))
    vw = jnp.concatenate([vwinT_ref[0, 0, qt + j] for j in range(WIN_KEYS // QT)], axis=1)
    ow_ref[...] = (jnp.dot(vw, pw.astype(BF16), preferred_element_type=F32)
                   * (1.0 / jnp.sum(pw, axis=0, keepdims=True)))

    n0 = pl.multiple_of((ATT_PAD // QT - 1 + qt) * QT, QT)
    sn = jnp.dot(ksel_ref[0, 0, pl.ds(n0, NEAR_KEYS), :], qaug_ref[...],
                 preferred_element_type=F32) + stab_ref[0]
    mn = jnp.max(sn, axis=0, keepdims=True)
    pn = jnp.exp(sn - mn)
    vn = jnp.concatenate([vselT_ref[0, 0, ATT_PAD // QT - 1 + qt + j] for j in range(NEAR_KEYS // QT)], axis=1)
    ms_ref[...] = mn
    ls_ref[...] = jnp.sum(pn, axis=0, keepdims=True)
    accs_ref[...] = jnp.dot(vn, pn.astype(BF16), preferred_element_type=F32)

    def far_body(g, carry):
        r0 = pl.multiple_of(ATT_PAD + g * FAR_KEYS, FAR_KEYS)
        s = jnp.dot(ksel_ref[0, 0, pl.ds(r0, FAR_KEYS), :], qfar_ref[...], preferred_element_type=F32)
        t0 = (ATT_PAD + g * FAR_KEYS) // QT
        vt = jnp.concatenate([vselT_ref[0, 0, t0 + j] for j in range(FAR_KEYS // QT)], axis=1)
        _flash_update(s, vt, ms_ref, ls_ref, accs_ref)
        return carry

    tiles_per_far = FAR_KEYS // QT
    lax.fori_loop(0, (qt - 1 + tiles_per_far - 1) // tiles_per_far, far_body, 0)

    o_s = accs_ref[...] * (1.0 / ls_ref[...])
    for r in range(NSA_HPG):
        cols = slice(r * QT, (r + 1) * QT)
        o = (gT_ref[0, 3 * r:3 * r + 1, :] * oc_ref[:, cols] + gT_ref[0, 3 * r + 1:3 * r + 2, :] * o_s[:, cols]
             + gT_ref[0, 3 * r + 2:3 * r + 3, :] * ow_ref[:, cols])
        o_ref[0, :, r * HEAD_DIM:(r + 1) * HEAD_DIM] = o.T


def _attn(qT, kcmp, vcmpT, bc, ksel, vT5, kwin, stab, wtab, gT):
    B, _, T = qT.shape
    ncp = kcmp.shape[2]
    ns = T // SEL_BLOCK
    nq = NSA_HPG * QT
    tp = T + ATT_PAD
    nt = tp // QT
    grid = (B, NSA_GROUPS, T // QT)
    in_specs = [
        pl.BlockSpec((1, NSA_HPG * HEAD_DIM, QT), lambda b, g, q: (b, g, q)),
        pl.BlockSpec((1, 1, ncp, HEAD_DIM), lambda b, g, q: (b, g, 0, 0)),
        pl.BlockSpec((1, 1, HEAD_DIM, ncp), lambda b, g, q: (b, g, 0, 0)),
        pl.BlockSpec((1, CMP_BAND, nq), lambda b, g, q: (g, 0, 0)),
        pl.BlockSpec((1, 1, tp, 2 * HEAD_DIM + ns), lambda b, g, q: (b, g, 0, 0)),
        pl.BlockSpec((1, 1, nt, HEAD_DIM, QT), lambda b, g, q: (b, g, 0, 0, 0)),
        pl.BlockSpec((1, 1, tp, 2 * HEAD_DIM), lambda b, g, q: (b, g, 0, 0)),
        pl.BlockSpec((1, 1, nt, HEAD_DIM, QT), lambda b, g, q: (b, NSA_GROUPS + g, 0, 0, 0)),
        pl.BlockSpec((1, NEAR_KEYS, nq), lambda b, g, q: (g, 0, 0)),
        pl.BlockSpec((1, WIN_KEYS, nq), lambda b, g, q: (g, 0, 0)),
        pl.BlockSpec((1, GATE_ROWS, QT), lambda b, g, q: (b, g, q)),
    ]
    scratch = [
        pltpu.VMEM((ncp, nq), F32),
        pltpu.VMEM((ncp, QT), F32),
        pltpu.VMEM((2 * HEAD_DIM + ns, nq), BF16),
        pltpu.VMEM((2 * HEAD_DIM + ns, nq), BF16),
        pltpu.VMEM((HEAD_DIM, nq), F32),
        pltpu.VMEM((HEAD_DIM, nq), F32),
        pltpu.VMEM((1, nq), F32), pltpu.VMEM((1, nq), F32), pltpu.VMEM((HEAD_DIM, nq), F32),
    ]
    return pl.pallas_call(
        _attn_kernel, grid=grid, in_specs=in_specs,
        out_specs=pl.BlockSpec((1, QT, NSA_HPG * HEAD_DIM), lambda b, g, q: (b, q, g)),
        out_shape=jax.ShapeDtypeStruct((B, T, D_NSA), F32),
        scratch_shapes=scratch,
        compiler_params=_cparams(("parallel", "parallel", "arbitrary")), name="attn",
    )(qT, kcmp, vcmpT, bc, ksel, vT5, kwin, vT5, stab, wtab, gT)


def _rwkv_prep_kernel(rw_ref, halo_ref, mu_ref, w0_ref, w2_ref, a0_ref, a2_ref, g2_ref, kk_ref, ka_ref,
                      rk_ref, bd_ref, r_out, lw_out, k_out, v_out, kk_out, b_out, g_out, bonus_out, ext_ref):
    tm = rw_ref.shape[1]
    first_tile = pl.program_id(1) == 0
    ext_ref[0:V7X_SUBLANES, :] = jnp.where(first_tile, 0.0, halo_ref[0])
    ext_ref[V7X_SUBLANES:, :] = rw_ref[0]
    cur = rw_ref[0]
    prev = ext_ref[pl.ds(V7X_SUBLANES - 1, tm), :]
    mixed = cur + (prev - cur) * mu_ref[...]
    c = D_RWKV
    r = mixed[:, 0:c]
    k = mixed[:, c:2 * c]
    v = mixed[:, 2 * c:3 * c]
    xw = mixed[:, 3 * c:3 * c + LORA_W]
    xa = mixed[:, 3 * c + LORA_W:3 * c + LORA_W + LORA_A]
    xg = mixed[:, 3 * c + LORA_W + LORA_A:]

    z = -(w0_ref[...] + _bdot(jnp.tanh(xw), w2_ref[...]))
    softplus = jnp.maximum(z, 0.0) + jnp.log1p(jnp.exp(-jnp.abs(z)))
    w = -softplus - 0.5
    a = jax.nn.sigmoid(a0_ref[...] + _bdot(xa, a2_ref[...]))
    g = _bdot(jax.nn.sigmoid(xg), g2_ref[...])

    kk = k * kk_ref[...]
    n2 = _dot_exact_rhs(kk * kk, bd_ref[...])
    kkn = kk / jnp.maximum(jnp.sqrt(n2), 1e-12)
    k2 = k * (1.0 + (a - 1.0) * ka_ref[...])
    bonus = _dot_exact_rhs(r * k2 * rk_ref[...], bd_ref[...]) * v

    r_out[0] = r
    lw_out[0] = -jnp.exp(w)
    k_out[0] = k2
    v_out[0] = v
    kk_out[0] = kkn
    b_out[0] = kkn * a
    g_out[0] = g
    bonus_out[0] = bonus


def _rwkv_prep(rw, mu, w0, w2, a0, a2, g2, k_k, k_a, r_k, bd, *, tm):
    B, T, C = rw.shape
    hb = tm // V7X_SUBLANES
    full = lambda a: pl.BlockSpec(a.shape, lambda b, i: (0,) * a.ndim)
    tok = pl.BlockSpec((1, tm, D_RWKV), lambda b, i: (b, i, 0))
    return pl.pallas_call(
        _rwkv_prep_kernel, grid=(B, T // tm),
        in_specs=[pl.BlockSpec((1, tm, C), lambda b, i: (b, i, 0)),
                  pl.BlockSpec((1, V7X_SUBLANES, C), lambda b, i: (b, jnp.maximum(i * hb - 1, 0), 0)),
                  full(mu), full(w0), full(w2), full(a0), full(a2), full(g2), full(k_k), full(k_a),
                  full(r_k), full(bd)],
        out_specs=(tok,) * 8,
        out_shape=(jax.ShapeDtypeStruct((B, T, D_RWKV), F32),) * 8,
        scratch_shapes=[pltpu.VMEM((tm + V7X_SUBLANES, C), F32)],
        compiler_params=_cparams(("parallel", "parallel")), name="rwkv_prep",
    )(rw, rw, mu, w0, w2, a0, a2, g2, k_k, k_a, r_k, bd)


def _wkv_kernel(r_ref, lw_ref, k_ref, v_ref, kk_ref, b_ref, lnw_ref, lnb_ref, y_ref, s_ref):
    L = r_ref.shape[1]
    n = HEAD_DIM

    @pl.when(pl.program_id(1) == 0)
    def _():
        s_ref[...] = jnp.zeros(s_ref.shape, F32)

    lw = lw_ref[0]
    ti = lax.broadcasted_iota(jnp.int32, (L, L), 0)
    tj = lax.broadcasted_iota(jnp.int32, (L, L), 1)
    tri = jnp.where(ti >= tj, 1.0, 0.0).astype(BF16)
    cs = _dot_exact_rhs_left(tri, lw)
    c_last = cs[L - 1:L, :]
    e_c = jnp.exp(cs)
    e_nc = jnp.exp(-cs)
    a_t = -kk_ref[0] * jnp.exp(cs - lw)
    r_t = r_ref[0] * e_c
    b_h = b_ref[0] * e_nc
    k_h = k_ref[0] * e_nc
    e_lc = jnp.exp(c_last - cs)
    b_e = b_ref[0] * e_lc
    k_e = k_ref[0] * e_lc
    e_last = jnp.exp(c_last)
    v_all = v_ref[0]

    upper_strict = ti < tj
    upper_incl = ti <= tj
    eye = jnp.where(ti == tj, 1.0, 0.0)
    for h in range(RWKV_HEADS):
        sl = slice(h * n, (h + 1) * n)
        lhs = jnp.concatenate([b_h[:, sl], k_h[:, sl]], axis=0)
        rhs = jnp.concatenate([a_t[:, sl], r_t[:, sl]], axis=0)
        tm_t = _bdot_nt(lhs, rhs)
        tab = jnp.where(upper_strict, tm_t[0:L, 0:L], 0.0)
        tak = jnp.where(upper_strict, tm_t[L:, 0:L], 0.0)
        trb = jnp.where(upper_incl, tm_t[0:L, L:], 0.0)
        trk = jnp.where(upper_incl, tm_t[L:, L:], 0.0)
        x = eye + tab
        pw = tab
        for _ in range(int(math.log2(L)) - 1):
            pw = _bdot(pw, pw)
            x = x + _bdot(x, pw)
        a_T = a_t[:, sl].T
        r_T = r_t[:, sl].T
        v_T = v_all[:, sl].T
        s = s_ref[h]
        u_T = _bdot(_bdot(s, a_T) + _bdot(v_T, tak), x)
        y_T = _bdot(s, r_T) + _bdot(u_T, trb) + _bdot(v_T, trk)
        s_ref[h] = s * e_last[:, sl] + _bdot(u_T, b_e[:, sl]) + _bdot(v_T, k_e[:, sl])
        mu = jnp.mean(y_T, axis=0, keepdims=True)
        var = jnp.mean(jnp.square(y_T - mu), axis=0, keepdims=True)
        y_n = (y_T - mu) * lax.rsqrt(var + GN_EPS) * lnw_ref[sl, :] + lnb_ref[sl, :]
        y_ref[0, :, sl] = y_n.T


def _dot_exact_rhs_left(m_bf16, x):
    hi, mid, lo = _split3(x)
    return (jnp.dot(m_bf16, hi, preferred_element_type=F32)
            + jnp.dot(m_bf16, mid, preferred_element_type=F32)
            + jnp.dot(m_bf16, lo, preferred_element_type=F32))


def _wkv(r, lw, k, v, kk, b, lnw, lnb):
    B, T, C = r.shape
    L = WKV_CHUNK
    tok = pl.BlockSpec((1, L, C), lambda bb, c: (bb, c, 0))
    full = lambda a: pl.BlockSpec(a.shape, lambda bb, c: (0,) * a.ndim)
    return pl.pallas_call(
        _wkv_kernel, grid=(B, T // L),
        in_specs=[tok] * 6 + [full(lnw), full(lnb)],
        out_specs=tok,
        out_shape=jax.ShapeDtypeStruct((B, T, C), F32),
        scratch_shapes=[pltpu.VMEM((RWKV_HEADS, HEAD_DIM, HEAD_DIM), F32)],
        compiler_params=_cparams(("parallel", "arbitrary")), name="wkv",
    )(r, lw, k, v, kk, b, lnw, lnb)


def _mix_kernel(x_ref, on_ref, y_ref, bonus_ref, g_ref, wo_ref, g2_ref, x1_ref, h2_ref):
    orw = (y_ref[...] + bonus_ref[...]) * g_ref[...]
    x1 = (x_ref[...] + _bdot(on_ref[...], wo_ref[0:D_NSA, :]) + _bdot(orw, wo_ref[D_NSA:, :]))
    x1_ref[...] = x1
    ms = jnp.mean(x1 * x1, axis=-1, keepdims=True)
    h2_ref[...] = (x1 * lax.rsqrt(ms + NORM_EPS) * g2_ref[...]).astype(BF16)


def _mix(x2d, on2d, y2d, bonus2d, g2d, wo, g2, *, tm):
    n, d = x2d.shape
    tok = lambda w: pl.BlockSpec((tm, w), lambda i: (i, 0))
    full = lambda a: pl.BlockSpec(a.shape, lambda i: (0,) * a.ndim)
    return pl.pallas_call(
        _mix_kernel, grid=(n // tm,),
        in_specs=[tok(d), tok(D_NSA), tok(D_RWKV), tok(D_RWKV), tok(D_RWKV), full(wo), full(g2)],
        out_specs=(tok(d), tok(d)),
        out_shape=(jax.ShapeDtypeStruct((n, d), F32), jax.ShapeDtypeStruct((n, d), BF16)),
        compiler_params=_cparams(("parallel",)), name="mix",
    )(x2d, on2d, y2d, bonus2d, g2d, wo, g2)


FFN_HALO = 16


def _ffn_kernel(x1_ref, h2_ref, halo_ref, wv_ref, wg_ref, cwv_ref, cwg_ref, cbv_ref, cbg_ref, wd_ref,
                o_ref, ext_ref, acc_ref, *, tiles_per_seq):
    tm = h2_ref.shape[0]
    j = pl.program_id(1)
    seq_start = (pl.program_id(0) % tiles_per_seq) == 0

    @pl.when(j == 0)
    def _():
        acc_ref[...] = jnp.zeros(acc_ref.shape, F32)

    h_main = h2_ref[...]
    h_halo = halo_ref[...]

    def conv_branch(w_ref, cw_ref, cb_ref):
        u_halo = jnp.dot(h_halo, w_ref[...], preferred_element_type=F32)
        ext_ref[0:FFN_HALO, :] = jnp.where(seq_start, 0.0, u_halo)
        ext_ref[FFN_HALO:, :] = jnp.dot(h_main, w_ref[...], preferred_element_type=F32)
        out = cb_ref[...] + ext_ref[pl.ds(FFN_HALO, tm), :] * cw_ref[CONV_W - 1:CONV_W, :]
        for i in range(CONV_W - 1):
            back = CONV_W - 1 - i
            out = out + ext_ref[pl.ds(FFN_HALO - back, tm), :] * cw_ref[i:i + 1, :]
        return out

    u_val = conv_branch(wv_ref, cwv_ref, cbv_ref)
    u_gate = conv_branch(wg_ref, cwg_ref, cbg_ref)
    act = (u_gate * jax.nn.sigmoid(u_gate)) * u_val
    acc_ref[...] += jnp.dot(act.astype(BF16), wd_ref[...], preferred_element_type=F32)

    @pl.when(j == pl.num_programs(1) - 1)
    def _():
        o_ref[...] = x1_ref[...] + acc_ref[...]


def _ffn(x1, h2, wv, wg, cwv, cwg, cbv, cbg, wd, *, tm, ft, seq_len):
    n, d = x1.shape
    dff = wv.shape[1]
    hb = tm // FFN_HALO
    grid = (n // tm, dff // ft)
    kern = functools.partial(_ffn_kernel, tiles_per_seq=seq_len // tm)
    return pl.pallas_call(
        kern, grid=grid,
        in_specs=[pl.BlockSpec((tm, d), lambda i, j: (i, 0)),
                  pl.BlockSpec((tm, d), lambda i, j: (i, 0)),
                  pl.BlockSpec((FFN_HALO, d), lambda i, j: (jnp.maximum(i * hb - 1, 0), 0)),
                  pl.BlockSpec((d, ft), lambda i, j: (0, j)),
                  pl.BlockSpec((d, ft), lambda i, j: (0, j)),
                  pl.BlockSpec((CONV_W, ft), lambda i, j: (0, j)),
                  pl.BlockSpec((CONV_W, ft), lambda i, j: (0, j)),
                  pl.BlockSpec((1, ft), lambda i, j: (0, j)),
                  pl.BlockSpec((1, ft), lambda i, j: (0, j)),
                  pl.BlockSpec((ft, d), lambda i, j: (j, 0))],
        out_specs=pl.BlockSpec((tm, d), lambda i, j: (i, 0)),
        out_shape=jax.ShapeDtypeStruct((n, d), F32),
        scratch_shapes=[pltpu.VMEM((tm + FFN_HALO, ft), F32), pltpu.VMEM((tm, d), F32)],
        compiler_params=_cparams(("parallel", "arbitrary")), name="ffn",
    )(x1, h2, h2, wv, wg, cwv, cwg, cbv, cbg, wd)


def _t5_bucket(dist):
    n = jnp.maximum(dist, 0)
    max_exact = N_BUCKETS // 2
    nf = jnp.maximum(n, 1).astype(F32)
    large = max_exact + (jnp.log(nf / max_exact) / math.log(MAX_DISTANCE / max_exact)
                         * (N_BUCKETS - max_exact)).astype(jnp.int32)
    large = jnp.minimum(large, N_BUCKETS - 1)
    return jnp.where(n < max_exact, n, large)


def _bias_tables(rel_bias):
    dmax = 2 * QT
    tab = rel_bias[_t5_bucket(jnp.arange(dmax))] - rel_bias[N_BUCKETS - 1][None, :]

    def lookup(d, valid, fill):
        v = jnp.where(valid[..., None], tab[jnp.clip(d, 0, dmax - 1)], fill)
        v = v.transpose(2, 0, 1).reshape(NSA_GROUPS, NSA_HPG, d.shape[0], QT)
        return v.transpose(0, 2, 1, 3).reshape(NSA_GROUPS, d.shape[0], NSA_HPG * QT).astype(F32)

    t_idx = jnp.arange(QT)[None, :]
    d_n = QT + t_idx - jnp.arange(NEAR_KEYS)[:, None]
    stab = lookup(d_n, d_n >= 0, NEG_INF)
    d_w = WINDOW + t_idx - jnp.arange(WIN_KEYS)[:, None]
    wtab = lookup(d_w, (d_w >= 0) & (d_w < WINDOW), NEG_INF)
    d_c = t_idx - CMP_STRIDE * jnp.arange(CMP_BAND)[:, None] + (CMP_STRIDE * CMP_PAD - CMP_BLOCK + 1)
    bc = lookup(d_c, d_c >= 0, 0.0)
    return stab, wtab, bc


def _compress_weights(w1):
    half = CMP_BLOCK // 2
    w1r = w1.reshape(2, half, HEAD_DIM, CMP_HIDDEN).transpose(1, 2, 0, 3)
    big = jnp.zeros((half, NSA_GROUPS, HEAD_DIM, NSA_GROUPS, 2, CMP_HIDDEN), w1.dtype)
    for g in range(NSA_GROUPS):
        big = big.at[:, g, :, g, :, :].set(w1r)
    return big.reshape(half * NSA_GROUPS * HEAD_DIM, NSA_GROUPS * 2 * CMP_HIDDEN)


def kernel(x, norm1_g, w_in, q_norm_g, k_norm_g, cmp_pos, cmp_w1, cmp_b1, cmp_w2, cmp_b2, rel_bias, rwkv_mu,
           w0, w2, a0, a2, g2, k_k, k_a, r_k, ln_x_w, ln_x_b, w_out, norm2_g, ffn_up, conv_w, conv_b, ffn_down):
    B, T, D = x.shape
    depth = w_in.shape[0]
    d_ff = ffn_down.shape[1]
    assert T % 2048 == 0 and D_NSA + 6 * D_KV + 3 * NSA_HEADS + D_RWKV_IN == w_in.shape[2]
    ncp = T // CMP_STRIDE + QT
    stab, wtab, bc = _bias_tables(rel_bias)
    ii = jnp.arange(D_RWKV)
    bd = (ii[:, None] // HEAD_DIM == ii[None, :] // HEAD_DIM).astype(BF16)

    for l in range(depth):
        wi = w_in[l]
        o = D_NSA
        q_w, kc_w, vc_w, ksl_w, vsl_w, kwn_w, vwn_w = (
            wi[:, 0:o], wi[:, o:o + D_KV], wi[:, o + D_KV:o + 2 * D_KV], wi[:, o + 2 * D_KV:o + 3 * D_KV],
            wi[:, o + 3 * D_KV:o + 4 * D_KV], wi[:, o + 4 * D_KV:o + 5 * D_KV], wi[:, o + 5 * D_KV:o + 6 * D_KV])
        gl_w = wi[:, o + 6 * D_KV:o + 6 * D_KV + 3 * NSA_HEADS]
        rw_w = wi[:, o + 6 * D_KV + 3 * NSA_HEADS:]
        wstd = jnp.concatenate([kc_w, vc_w, ksl_w, kwn_w, rw_w], axis=1).astype(BF16)
        gl_rows = gl_w.T.reshape(NSA_GROUPS, 3 * NSA_HPG, D)
        gl_rows = jnp.pad(gl_rows, ((0, 0), (0, GATE_ROWS - 3 * NSA_HPG), (0, 0))).reshape(-1, D)
        wt = jnp.concatenate([q_w.T, vsl_w.T, vwn_w.T, gl_rows], axis=0).astype(BF16)
        qg = q_norm_g[l].reshape(HEAD_DIM, 1)
        kg = jnp.stack([jnp.tile(k_norm_g[l, 1], NSA_GROUPS), jnp.tile(k_norm_g[l, 2], NSA_GROUPS)])

        qT, kc, vc, ksel, kwin, vT5, gT, rw = _proj(x, norm1_g[l].reshape(1, D), wstd, wt, qg, kg, tm=512)

        wbig = jnp.stack([_compress_weights(cmp_w1[l, 0]), _compress_weights(cmp_w1[l, 1])]).astype(BF16)
        pos8 = jnp.pad(cmp_pos[l].reshape(2, 1, CMP_BLOCK * HEAD_DIM),
                       ((0, 0), (0, V7X_SUBLANES - 1), (0, 0))).astype(BF16)
        rows16 = CMP_STRIDE * D_KV
        kcmp, vcmpT = _compress(
            kc.reshape(B, T // CMP_STRIDE, rows16), vc.reshape(B, T // CMP_STRIDE, rows16), wbig, pos8,
            cmp_w1[l].astype(BF16), cmp_b1[l].reshape(2, 1, CMP_HIDDEN), cmp_w2[l, 0].astype(BF16),
            cmp_w2[l, 1].T.astype(BF16), cmp_b2[l, 0].reshape(1, HEAD_DIM), cmp_b2[l, 1].reshape(HEAD_DIM, 1),
            k_norm_g[l, 0].reshape(1, HEAD_DIM), ncp=ncp)

        def front_pad(k):
            flag = (jnp.arange(k.shape[-1]) == HEAD_DIM).astype(BF16)
            return jnp.concatenate([jnp.broadcast_to(flag, k.shape[:2] + (ATT_PAD, k.shape[-1])), k], axis=2)

        vT5p = jnp.pad(vT5, ((0, 0), (0, 0), (ATT_PAD // QT, 0), (0, 0), (0, 0)))
        o_nsa = _attn(qT, kcmp, vcmpT, bc, front_pad(ksel), vT5p, front_pad(kwin), stab, wtab, gT)

        row = lambda a: a.reshape(1, -1)
        r, lw, k2, v, kkn, bvec, g, bonus = _rwkv_prep(
            rw, row(rwkv_mu[l]), row(w0[l]), w2[l].astype(BF16), row(a0[l]), a2[l].astype(BF16),
            g2[l].astype(BF16), row(k_k[l]), row(k_a[l]), row(r_k[l]), bd, tm=256)
        y = _wkv(r, lw, k2, v, kkn, bvec, ln_x_w[l].reshape(-1, 1), ln_x_b[l].reshape(-1, 1))

        n = B * T
        x1, h2 = _mix(x.reshape(n, D), o_nsa.reshape(n, D_NSA), y.reshape(n, D_RWKV),
                      bonus.reshape(n, D_RWKV), g.reshape(n, D_RWKV), w_out[l].astype(BF16),
                      norm2_g[l].reshape(1, D), tm=512)

        up = ffn_up[l].astype(BF16)
        x = _ffn(x1, h2, up[:, :d_ff], up[:, d_ff:], conv_w[l][:, :d_ff], conv_w[l][:, d_ff:],
                 conv_b[l][:d_ff].reshape(1, -1), conv_b[l][d_ff:].reshape(1, -1),
                 ffn_down[l].astype(BF16), tm=512, ft=256, seq_len=T).reshape(B, T, D)
    return x
```

```python
import functools
import math

import jax
import jax.numpy as jnp
import numpy as np
from jax import lax
from jax.experimental import pallas as pl
from jax.experimental.pallas import tpu as pltpu

F32 = jnp.float32
BF16 = jnp.bfloat16

V7X_LANES = 128
V7X_SUBLANES = 8
V7X_VMEM_LIMIT_BYTES = 56 * 1024 * 1024

HEAD_DIM = 64
NSA_HEADS = 8
NSA_GROUPS = 2
NSA_HPG = NSA_HEADS // NSA_GROUPS
RWKV_HEADS = 8
D_NSA = NSA_HEADS * HEAD_DIM
D_RWKV = RWKV_HEADS * HEAD_DIM
D_KV = NSA_GROUPS * HEAD_DIM
CMP_BLOCK = 32
CMP_STRIDE = 16
CMP_HIDDEN = 128
SEL_BLOCK = 64
SEL_TOPN = 16
WINDOW = 512
N_BUCKETS = 32
MAX_DISTANCE = 128
LORA_W = 64
LORA_A = 64
LORA_G = 128
D_RWKV_IN = 3 * D_RWKV + LORA_W + LORA_A + LORA_G
CONV_W = 3
NORM_EPS = 1e-6
GN_EPS = 64e-5
NEG_INF = -1e30
FORCE_SCORE = 1e9

QT = 128
CMP_PAD = 8
CMP_BAND = 16
ATT_PAD = WINDOW
NEAR_KEYS = 2 * QT
FAR_KEYS = 4 * QT
WIN_KEYS = WINDOW + QT
WKV_CHUNK = 64
GATE_ROWS = 16
N_T_ROWS = D_NSA + 2 * D_KV + NSA_GROUPS * GATE_ROWS
N_STD_COLS = 4 * D_KV + D_RWKV_IN


def _cparams(sem):
    return pltpu.CompilerParams(dimension_semantics=sem, vmem_limit_bytes=V7X_VMEM_LIMIT_BYTES)


def _bdot(a, b):
    return jnp.dot(a.astype(BF16), b.astype(BF16), preferred_element_type=F32)


def _bdot_nt(a, b):
    return lax.dot_general(a.astype(BF16), b.astype(BF16), (((1,), (1,)), ((), ())),
                           preferred_element_type=F32)


def _split3(x):
    hi = x.astype(BF16)
    r1 = x - hi.astype(F32)
    mid = r1.astype(BF16)
    lo = (r1 - mid.astype(F32)).astype(BF16)
    return hi, mid, lo


def _dot_exact_rhs(x, m_bf16):
    hi, mid, lo = _split3(x)
    return (jnp.dot(hi, m_bf16, preferred_element_type=F32)
            + jnp.dot(mid, m_bf16, preferred_element_type=F32)
            + jnp.dot(lo, m_bf16, preferred_element_type=F32))


def _proj_kernel(x_ref, g1_ref, wstd_ref, wt_ref, qg_ref, kg_ref,
                 qT_ref, kc_ref, vc_ref, ksel_ref, kwin_ref, vT_ref, gT_ref, rw_ref):
    tm = x_ref.shape[1]
    ns = ksel_ref.shape[3] - 2 * HEAD_DIM
    x = x_ref[0]
    ms = jnp.mean(x * x, axis=-1, keepdims=True)
    h = (x * lax.rsqrt(ms + NORM_EPS) * g1_ref[...]).astype(BF16)

    std = jnp.dot(h, wstd_ref[...], preferred_element_type=F32)
    kc_ref[0] = std[:, 0:D_KV].astype(BF16)
    vc_ref[0] = std[:, D_KV:2 * D_KV].astype(BF16)
    rw_ref[0] = std[:, 4 * D_KV:]

    lane = lax.broadcasted_iota(jnp.int32, (tm, D_KV), 1)
    first = lane < HEAD_DIM

    def group_rmsnorm(k, gain):
        k2 = k * k
        s0 = jnp.sum(jnp.where(first, k2, 0.0), axis=-1, keepdims=True)
        s1 = jnp.sum(jnp.where(first, 0.0, k2), axis=-1, keepdims=True)
        ms_g = jnp.where(first, s0, s1) * (1.0 / HEAD_DIM)
        return k * lax.rsqrt(ms_g + NORM_EPS) * gain

    ksl = group_rmsnorm(std[:, 2 * D_KV:3 * D_KV], kg_ref[0:1, :]).astype(BF16)
    kwn = group_rmsnorm(std[:, 3 * D_KV:4 * D_KV], kg_ref[1:2, :]).astype(BF16)

    tok = pl.program_id(1) * tm + lax.broadcasted_iota(jnp.int32, (tm, ns), 0)
    blk = lax.broadcasted_iota(jnp.int32, (tm, ns), 1)
    onehot = jnp.where((tok // SEL_BLOCK) == blk, 1.0, 0.0).astype(BF16)
    zeros = jnp.zeros((tm, HEAD_DIM), BF16)
    for g in range(NSA_GROUPS):
        ksel_ref[0, g, :, 0:HEAD_DIM] = ksl[:, g * HEAD_DIM:(g + 1) * HEAD_DIM]
        ksel_ref[0, g, :, HEAD_DIM:2 * HEAD_DIM] = zeros
        ksel_ref[0, g, :, 2 * HEAD_DIM:] = onehot
        kwin_ref[0, g, :, 0:HEAD_DIM] = kwn[:, g * HEAD_DIM:(g + 1) * HEAD_DIM]
        kwin_ref[0, g, :, HEAD_DIM:] = zeros

    tr = lax.dot_general(wt_ref[...], h, (((1,), (1,)), ((), ())),
                         preferred_element_type=F32)
    q = tr[0:D_NSA].reshape(NSA_HEADS, HEAD_DIM, tm)
    qms = jnp.mean(q * q, axis=1, keepdims=True)
    qn = q * lax.rsqrt(qms + NORM_EPS) * qg_ref[...][None] * (HEAD_DIM ** -0.5)
    qT_ref[0] = qn.reshape(D_NSA, tm).astype(BF16)
    vt = tr[D_NSA:D_NSA + 2 * D_KV].astype(BF16)
    for a in range(2 * NSA_GROUPS):
        for j in range(tm // QT):
            vT_ref[0, a, j] = vt[a * HEAD_DIM:(a + 1) * HEAD_DIM, j * QT:(j + 1) * QT]
    gT_ref[0] = jax.nn.sigmoid(tr[D_NSA + 2 * D_KV:])


def _proj(x, g1, wstd, wt, qg, kg, *, tm):
    B, T, D = x.shape
    ns = T // SEL_BLOCK
    grid = (B, T // tm)
    const2 = lambda b, i: (0, 0)
    out_shape = (
        jax.ShapeDtypeStruct((B, D_NSA, T), BF16),
        jax.ShapeDtypeStruct((B, T, D_KV), BF16),
        jax.ShapeDtypeStruct((B, T, D_KV), BF16),
        jax.ShapeDtypeStruct((B, NSA_GROUPS, T, 2 * HEAD_DIM + ns), BF16),
        jax.ShapeDtypeStruct((B, NSA_GROUPS, T, 2 * HEAD_DIM), BF16),
        jax.ShapeDtypeStruct((B, 2 * NSA_GROUPS, T // QT, HEAD_DIM, QT), BF16),
        jax.ShapeDtypeStruct((B, NSA_GROUPS * GATE_ROWS, T), F32),
        jax.ShapeDtypeStruct((B, T, D_RWKV_IN), F32),
    )
    out_specs = (
        pl.BlockSpec((1, D_NSA, tm), lambda b, i: (b, 0, i)),
        pl.BlockSpec((1, tm, D_KV), lambda b, i: (b, i, 0)),
        pl.BlockSpec((1, tm, D_KV), lambda b, i: (b, i, 0)),
        pl.BlockSpec((1, NSA_GROUPS, tm, 2 * HEAD_DIM + ns), lambda b, i: (b, 0, i, 0)),
        pl.BlockSpec((1, NSA_GROUPS, tm, 2 * HEAD_DIM), lambda b, i: (b, 0, i, 0)),
        pl.BlockSpec((1, 2 * NSA_GROUPS, tm // QT, HEAD_DIM, QT), lambda b, i: (b, 0, i, 0, 0)),
        pl.BlockSpec((1, NSA_GROUPS * GATE_ROWS, tm), lambda b, i: (b, 0, i)),
        pl.BlockSpec((1, tm, D_RWKV_IN), lambda b, i: (b, i, 0)),
    )
    in_specs = [
        pl.BlockSpec((1, tm, D), lambda b, i: (b, i, 0)),
        pl.BlockSpec(g1.shape, const2),
        pl.BlockSpec(wstd.shape, const2),
        pl.BlockSpec(wt.shape, const2),
        pl.BlockSpec(qg.shape, const2),
        pl.BlockSpec(kg.shape, const2),
    ]
    return pl.pallas_call(
        _proj_kernel, grid=grid, in_specs=in_specs, out_specs=out_specs, out_shape=out_shape,
        compiler_params=_cparams(("parallel", "parallel")), name="proj",
    )(x, g1, wstd, wt, qg, kg)


def _gelu_tanh(x):
    c = math.sqrt(2.0 / math.pi)
    return x * (0.5 * (1.0 + jnp.tanh(c * (x + 0.044715 * (x * x * x)))))


def _compress_kernel(kc_ref, vc_ref, wbig_ref, pos_ref, w1_ref, b1_ref, w2k_ref, w2vT_ref,
                     b2k_ref, b2v_ref, kg_ref, kcmp_ref, vcmpT_ref, act_ref):
    m = kc_ref.shape[1]
    nc = m - 1
    ncp = act_ref.shape[0]
    row_m = lax.broadcasted_iota(jnp.int32, (m, CMP_HIDDEN), 0)
    row_p = lax.broadcasted_iota(jnp.int32, (ncp, HEAD_DIM), 0)
    col_p = lax.broadcasted_iota(jnp.int32, (HEAD_DIM, ncp), 1)
    act_ref[...] = jnp.zeros(act_ref.shape, F32)
    for which, src_ref in enumerate((kc_ref, vc_ref)):
        p = jnp.dot(src_ref[0], wbig_ref[which], preferred_element_type=F32)
        posc = jnp.dot(pos_ref[which], w1_ref[which], preferred_element_type=F32)[0:1]
        for g in range(NSA_GROUPS):
            top = p[:, (2 * g) * CMP_HIDDEN:(2 * g + 1) * CMP_HIDDEN]
            bot = p[:, (2 * g + 1) * CMP_HIDDEN:(2 * g + 2) * CMP_HIDDEN]
            hid = top + pltpu.roll(bot, m - 1, 0) + (b1_ref[which] + posc)
            act = jnp.where(row_m < nc, _gelu_tanh(hid), 0.0)
            act_ref[CMP_PAD:CMP_PAD + m, :] = act
            ap = act_ref[...].astype(BF16)
            if which == 0:
                kc = jnp.dot(ap, w2k_ref[...], preferred_element_type=F32) + b2k_ref[...]
                ms = jnp.mean(kc * kc, axis=-1, keepdims=True)
                kc = kc * lax.rsqrt(ms + NORM_EPS) * kg_ref[...]
                valid = (row_p >= CMP_PAD) & (row_p < CMP_PAD + nc)
                kcmp_ref[0, g] = jnp.where(valid, kc, 0.0).astype(BF16)
            else:
                vt = lax.dot_general(w2vT_ref[...], ap, (((1,), (1,)), ((), ())),
                                     preferred_element_type=F32) + b2v_ref[...]
                valid = (col_p >= CMP_PAD) & (col_p < CMP_PAD + nc)
                vcmpT_ref[0, g] = jnp.where(valid, vt, 0.0).astype(BF16)


def _compress(kc16, vc16, wbig, pos8, w1, b1, w2k, w2vT, b2k, b2v, kg, *, ncp):
    B, m, _ = kc16.shape
    full = lambda a: pl.BlockSpec(a.shape, lambda b: (0,) * a.ndim)
    return pl.pallas_call(
        _compress_kernel, grid=(B,),
        in_specs=[pl.BlockSpec((1, m, kc16.shape[2]), lambda b: (b, 0, 0)),
                  pl.BlockSpec((1, m, vc16.shape[2]), lambda b: (b, 0, 0)),
                  full(wbig), full(pos8), full(w1), full(b1), full(w2k), full(w2vT),
                  full(b2k), full(b2v), full(kg)],
        out_specs=(pl.BlockSpec((1, NSA_GROUPS, ncp, HEAD_DIM), lambda b: (b, 0, 0, 0)),
                   pl.BlockSpec((1, NSA_GROUPS, HEAD_DIM, ncp), lambda b: (b, 0, 0, 0))),
        out_shape=(jax.ShapeDtypeStruct((B, NSA_GROUPS, ncp, HEAD_DIM), BF16),
                   jax.ShapeDtypeStruct((B, NSA_GROUPS, HEAD_DIM, ncp), BF16)),
        scratch_shapes=[pltpu.VMEM((ncp, CMP_HIDDEN), F32)],
        compiler_params=_cparams(("parallel",)), name="compress",
    )(kc16, vc16, wbig, pos8, w1, b1, w2k, w2vT, b2k, b2v, kg)


def _flash_update(s, vT_tile, m_ref, l_ref, acc_ref):
    m_prev = m_ref[...]
    m_new = jnp.maximum(m_prev, jnp.max(s, axis=0, keepdims=True))
    alpha = jnp.exp(m_prev - m_new)
    p = jnp.exp(s - m_new)
    l_ref[...] = alpha * l_ref[...] + jnp.sum(p, axis=0, keepdims=True)
    acc_ref[...] = alpha * acc_ref[...] + jnp.dot(vT_tile, p.astype(BF16), preferred_element_type=F32)
    m_ref[...] = m_new


def _attn_kernel(qT_ref, kcmp_ref, vcmpT_ref, bc_ref, ksel_ref, vselT_ref, kwin_ref, vwinT_ref,
                 stab_ref, wtab_ref, gT_ref, o_ref,
                 lc_ref, psum_ref, qaug_ref, qfar_ref, oc_ref, ow_ref, ms_ref, ls_ref, accs_ref):
    qt = pl.program_id(2)
    ncp = kcmp_ref.shape[2]
    ns = qaug_ref.shape[0] - 2 * HEAD_DIM
    nq = NSA_HPG * QT
    t_lane = qt * QT + lax.broadcasted_iota(jnp.int32, (1, QT), 1)

    for r in range(NSA_HPG):
        qaug_ref[0:HEAD_DIM, r * QT:(r + 1) * QT] = qT_ref[0, r * HEAD_DIM:(r + 1) * HEAD_DIM, :]
    flag_row = lax.broadcasted_iota(jnp.int32, (HEAD_DIM, nq), 0) == 0
    qaug_ref[HEAD_DIM:2 * HEAD_DIM, :] = jnp.where(flag_row, NEG_INF, 0.0).astype(BF16)
    q4 = qaug_ref[0:HEAD_DIM, :]

    rho = lax.broadcasted_iota(jnp.int32, (ncp, nq), 0)
    t_all = qt * QT + lax.broadcasted_iota(jnp.int32, (ncp, nq), 1) % QT
    valid_c = (CMP_STRIDE * rho - (CMP_STRIDE * CMP_PAD - CMP_BLOCK + 1) <= t_all) & (rho >= CMP_PAD)
    band0 = pl.multiple_of(qt * (QT // CMP_STRIDE), V7X_SUBLANES)
    lc_ref[...] = jnp.dot(kcmp_ref[0, 0], q4, preferred_element_type=F32)
    lc_ref[pl.ds(band0, CMP_BAND), :] += bc_ref[0]
    lc = jnp.where(valid_c, lc_ref[...], NEG_INF)
    e = jnp.exp(lc - jnp.max(lc, axis=0, keepdims=True))
    p = jnp.where(valid_c, e * (1.0 / jnp.sum(e, axis=0, keepdims=True)), 0.0)
    oc_ref[...] = jnp.dot(vcmpT_ref[0, 0], p.astype(BF16), preferred_element_type=F32)
    psum_ref[...] = (p[:, 0:QT] + p[:, QT:2 * QT]) + (p[:, 2 * QT:3 * QT] + p[:, 3 * QT:4 * QT])

    imp = psum_ref[pl.ds(CMP_PAD - 1, ns, stride=4), :]
    for k in range(1, 5):
        imp = imp + psum_ref[pl.ds(CMP_PAD - 1 + k, ns, stride=4), :]
    jrow = lax.broadcasted_iota(jnp.int32, (ns, QT), 0)
    cur = t_lane // SEL_BLOCK
    forced = (jrow == 0) | (jrow == cur) | (jrow == cur - 1)
    score = jnp.where(jrow * SEL_BLOCK <= t_lane, jnp.where(forced, FORCE_SCORE, imp), NEG_INF)
    sel = jnp.zeros((ns, QT), jnp.bool_)
    for _ in range(min(SEL_TOPN, ns)):
        mx = jnp.max(score, axis=0, keepdims=True)
        idx = jnp.min(jnp.where(score == mx, jrow, ns), axis=0, keepdims=True)
        hit = jrow == idx
        sel = sel | hit
        score = jnp.where(hit, -jnp.inf, score)
    negmask = jnp.where(sel, 0.0, NEG_INF).astype(BF16)
    negfar = jnp.where(sel & (jrow < (qt - 1) * (QT // SEL_BLOCK)), 0.0, NEG_INF).astype(BF16)
    qfar_ref[0:2 * HEAD_DIM, :] = qaug_ref[0:2 * HEAD_DIM, :]
    for r in range(NSA_HPG):
        qaug_ref[2 * HEAD_DIM:, r * QT:(r + 1) * QT] = negmask
        qfar_ref[2 * HEAD_DIM:, r * QT:(r + 1) * QT] = negfar

    w0 = pl.multiple_of(qt * QT, QT)
    sw = jnp.dot(kwin_ref[0, 0, pl.ds(w0, WIN_KEYS), :], qaug_ref[0:2 * HEAD_DIM, :],
                 preferred_element_type=F32) + wtab_ref[0]
    pw = jnp.exp(sw - jnp.max(sw, axis=0, keepdims=True))
    vw = jnp.concatenate([vwinT_ref[0, 0, qt + j] for j in range(WIN_KEYS // QT)], axis=1)
    ow_ref[...] = (jnp.dot(vw, pw.astype(BF16), preferred_element_type=F32)
                   * (1.0 / jnp.sum(pw, axis=0, keepdims=True)))

    n0 = pl.multiple_of((ATT_PAD // QT - 1 + qt) * QT, QT)
    sn = jnp.dot(ksel_ref[0, 0, pl.ds(n0, NEAR_KEYS), :], qaug_ref[...],
                 preferred_element_type=F32) + stab_ref[0]
    mn = jnp.max(sn, axis=0, keepdims=True)
    pn = jnp.exp(sn - mn)
    vn = jnp.concatenate([vselT_ref[0, 0, ATT_PAD // QT - 1 + qt + j] for j in range(NEAR_KEYS // QT)], axis=1)
    ms_ref[...] = mn
    ls_ref[...] = jnp.sum(pn, axis=0, keepdims=True)
    accs_ref[...] = jnp.dot(vn, pn.astype(BF16), preferred_element_type=F32)

    def far_body(g, carry):
        r0 = pl.multiple_of(ATT_PAD + g * FAR_KEYS, FAR_KEYS)
        s = jnp.dot(ksel_ref[0, 0, pl.ds(r0, FAR_KEYS), :], qfar_ref[...], preferred_element_type=F32)
        t0 = (ATT_PAD + g * FAR_KEYS) // QT
        vt = jnp.concatenate([vselT_ref[0, 0, t0 + j] for j in range(FAR_KEYS // QT)], axis=1)
        _flash_update(s, vt, ms_ref, ls_ref, accs_ref)
        return carry

    tiles_per_far = FAR_KEYS // QT
    lax.fori_loop(0, (qt - 1 + tiles_per_far - 1) // tiles_per_far, far_body, 0)

    o_s = accs_ref[...] * (1.0 / ls_ref[...])
    for r in range(NSA_HPG):
        cols = slice(r * QT, (r + 1) * QT)
        o = (gT_ref[0, 3 * r:3 * r + 1, :] * oc_ref[:, cols] + gT_ref[0, 3 * r + 1:3 * r + 2, :] * o_s[:, cols]
             + gT_ref[0, 3 * r + 2:3 * r + 3, :] * ow_ref[:, cols])
        o_ref[0, :, r * HEAD_DIM:(r + 1) * HEAD_DIM] = o.T


def _attn(qT, kcmp, vcmpT, bc, ksel, vT5, kwin, stab, wtab, gT):
    B, _, T = qT.shape
    ncp = kcmp.shape[2]
    ns = T // SEL_BLOCK
    nq = NSA_HPG * QT
    tp = T + ATT_PAD
    nt = tp // QT
    grid = (B, NSA_GROUPS, T // QT)
    in_specs = [
        pl.BlockSpec((1, NSA_HPG * HEAD_DIM, QT), lambda b, g, q: (b, g, q)),
        pl.BlockSpec((1, 1, ncp, HEAD_DIM), lambda b, g, q: (b, g, 0, 0)),
        pl.BlockSpec((1, 1, HEAD_DIM, ncp), lambda b, g, q: (b, g, 0, 0)),
        pl.BlockSpec((1, CMP_BAND, nq), lambda b, g, q: (g, 0, 0)),
        pl.BlockSpec((1, 1, tp, 2 * HEAD_DIM + ns), lambda b, g, q: (b, g, 0, 0)),
        pl.BlockSpec((1, 1, nt, HEAD_DIM, QT), lambda b, g, q: (b, g, 0, 0, 0)),
        pl.BlockSpec((1, 1, tp, 2 * HEAD_DIM), lambda b, g, q: (b, g, 0, 0)),
        pl.BlockSpec((1, 1, nt, HEAD_DIM, QT), lambda b, g, q: (b, NSA_GROUPS + g, 0, 0, 0)),
        pl.BlockSpec((1, NEAR_KEYS, nq), lambda b, g, q: (g, 0, 0)),
        pl.BlockSpec((1, WIN_KEYS, nq), lambda b, g, q: (g, 0, 0)),
        pl.BlockSpec((1, GATE_ROWS, QT), lambda b, g, q: (b, g, q)),
    ]
    scratch = [
        pltpu.VMEM((ncp, nq), F32),
        pltpu.VMEM((ncp, QT), F32),
        pltpu.VMEM((2 * HEAD_DIM + ns, nq), BF16),
        pltpu.VMEM((2 * HEAD_DIM + ns, nq), BF16),
        pltpu.VMEM((HEAD_DIM, nq), F32),
        pltpu.VMEM((HEAD_DIM, nq), F32),
        pltpu.VMEM((1, nq), F32), pltpu.VMEM((1, nq), F32), pltpu.VMEM((HEAD_DIM, nq), F32),
    ]
    return pl.pallas_call(
        _attn_kernel, grid=grid, in_specs=in_specs,
        out_specs=pl.BlockSpec((1, QT, NSA_HPG * HEAD_DIM), lambda b, g, q: (b, q, g)),
        out_shape=jax.ShapeDtypeStruct((B, T, D_NSA), F32),
        scratch_shapes=scratch,
        compiler_params=_cparams(("parallel", "parallel", "arbitrary")), name="attn",
    )(qT, kcmp, vcmpT, bc, ksel, vT5, kwin, vT5, stab, wtab, gT)


def _rwkv_prep_kernel(rw_ref, halo_ref, mu_ref, w0_ref, w2_ref, a0_ref, a2_ref, g2_ref, kk_ref, ka_ref,
                      rk_ref, bd_ref, r_out, lw_out, k_out, v_out, kk_out, b_out, g_out, bonus_out, ext_ref):
    tm = rw_ref.shape[1]
    first_tile = pl.program_id(1) == 0
    ext_ref[0:V7X_SUBLANES, :] = jnp.where(first_tile, 0.0, halo_ref[0])
    ext_ref[V7X_SUBLANES:, :] = rw_ref[0]
    cur = rw_ref[0]
    prev = ext_ref[pl.ds(V7X_SUBLANES - 1, tm), :]
    mixed = cur + (prev - cur) * mu_ref[...]
    c = D_RWKV
    r = mixed[:, 0:c]
    k = mixed[:, c:2 * c]
    v = mixed[:, 2 * c:3 * c]
    xw = mixed[:, 3 * c:3 * c + LORA_W]
    xa = mixed[:, 3 * c + LORA_W:3 * c + LORA_W + LORA_A]
    xg = mixed[:, 3 * c + LORA_W + LORA_A:]

    z = -(w0_ref[...] + _bdot(jnp.tanh(xw), w2_ref[...]))
    softplus = jnp.maximum(z, 0.0) + jnp.log1p(jnp.exp(-jnp.abs(z)))
    w = -softplus - 0.5
    a = jax.nn.sigmoid(a0_ref[...] + _bdot(xa, a2_ref[...]))
    g = _bdot(jax.nn.sigmoid(xg), g2_ref[...])

    kk = k * kk_ref[...]
    n2 = _dot_exact_rhs(kk * kk, bd_ref[...])
    kkn = kk / jnp.maximum(jnp.sqrt(n2), 1e-12)
    k2 = k * (1.0 + (a - 1.0) * ka_ref[...])
    bonus = _dot_exact_rhs(r * k2 * rk_ref[...], bd_ref[...]) * v

    r_out[0] = r
    lw_out[0] = -jnp.exp(w)
    k_out[0] = k2
    v_out[0] = v
    kk_out[0] = kkn
    b_out[0] = kkn * a
    g_out[0] = g
    bonus_out[0] = bonus


def _rwkv_prep(rw, mu, w0, w2, a0, a2, g2, k_k, k_a, r_k, bd, *, tm):
    B, T, C = rw.shape
    hb = tm // V7X_SUBLANES
    full = lambda a: pl.BlockSpec(a.shape, lambda b, i: (0,) * a.ndim)
    tok = pl.BlockSpec((1, tm, D_RWKV), lambda b, i: (b, i, 0))
    return pl.pallas_call(
        _rwkv_prep_kernel, grid=(B, T // tm),
        in_specs=[pl.BlockSpec((1, tm, C), lambda b, i: (b, i, 0)),
                  pl.BlockSpec((1, V7X_SUBLANES, C), lambda b, i: (b, jnp.maximum(i * hb - 1, 0), 0)),
                  full(mu), full(w0), full(w2), full(a0), full(a2), full(g2), full(k_k), full(k_a),
                  full(r_k), full(bd)],
        out_specs=(tok,) * 8,
        out_shape=(jax.ShapeDtypeStruct((B, T, D_RWKV), F32),) * 8,
        scratch_shapes=[pltpu.VMEM((tm + V7X_SUBLANES, C), F32)],
        compiler_params=_cparams(("parallel", "parallel")), name="rwkv_prep",
    )(rw, rw, mu, w0, w2, a0, a2, g2, k_k, k_a, r_k, bd)


WKV_GROUP = 4
WKV_GW = WKV_GROUP * HEAD_DIM


def _wkv_kernel(r_ref, lw_ref, k_ref, v_ref, kk_ref, b_ref, y_ref, s_ref, *, chunks):
    L = WKV_CHUNK
    assert L == HEAD_DIM
    gw = WKV_GW
    n_groups = RWKV_HEADS // WKV_GROUP

    @pl.when(pl.program_id(1) == 0)
    def _():
        s_ref[...] = jnp.zeros(s_ref.shape, F32)

    ti = lax.broadcasted_iota(jnp.int32, (L, L), 0)
    tj = lax.broadcasted_iota(jnp.int32, (L, L), 1)
    tri = jnp.where(ti >= tj, 1.0, 0.0).astype(BF16)
    row = lax.broadcasted_iota(jnp.int32, (L, gw), 0)
    col = lax.broadcasted_iota(jnp.int32, (L, gw), 1) % L
    low_strict = col < row
    low_incl = col <= row
    eye_sbs = jnp.where(col == row, 1.0, 0.0)
    brow = lax.broadcasted_iota(jnp.int32, (gw, gw), 0) // L
    bcol = lax.broadcasted_iota(jnp.int32, (gw, gw), 1) // HEAD_DIM
    same_head = brow == bcol

    def bd_rows(x):
        xb = x.astype(BF16)
        return jnp.where(same_head, jnp.concatenate([xb] * WKV_GROUP, axis=0), jnp.zeros((), BF16))

    def mm(a, b_bf16):
        return jnp.dot(a.astype(BF16), b_bf16, preferred_element_type=F32)

    def tn(a, b):
        return lax.dot_general(a.astype(BF16), b.astype(BF16), (((0,), (0,)), ((), ())),
                               preferred_element_type=F32)

    inst = [(c, g) for c in range(chunks) for g in range(n_groups)]
    pre = {}
    for c in range(chunks):
        rows = slice(c * L, (c + 1) * L)
        lw = lw_ref[0, rows, :]
        cs = _dot_exact_rhs_left(tri, lw)
        c_last = cs[L - 1:L, :]
        e_nc = jnp.exp(-cs)
        e_lc = jnp.exp(c_last - cs)
        kk = kk_ref[0, rows, :]
        b = b_ref[0, rows, :]
        k = k_ref[0, rows, :]
        pre[c] = dict(a=-kk * jnp.exp(cs - lw), r=r_ref[0, rows, :] * jnp.exp(cs), bh=b * e_nc, kh=k * e_nc,
                      be=b * e_lc, ke=k * e_lc, v=v_ref[0, rows, :], e_last=jnp.exp(c_last))

    def grp(c, g, name):
        return pre[c][name][:, g * gw:(g + 1) * gw]

    t_all = {}
    for (c, g) in inst:
        lhs = jnp.concatenate([grp(c, g, "a"), grp(c, g, "r")], axis=0).astype(BF16)
        rhs = jnp.concatenate([bd_rows(grp(c, g, "bh")), bd_rows(grp(c, g, "kh"))], axis=0)
        t_all[c, g] = lax.dot_general(lhs, rhs, (((1,), (1,)), ((), ())), preferred_element_type=F32)
    n_m = {i: jnp.where(low_strict, t_all[i][0:L, 0:gw], 0.0) for i in inst}
    tak = {i: jnp.where(low_strict, t_all[i][0:L, gw:], 0.0) for i in inst}
    trb = {i: jnp.where(low_incl, t_all[i][L:, 0:gw], 0.0) for i in inst}
    trk = {i: jnp.where(low_incl, t_all[i][L:, gw:], 0.0) for i in inst}
    z = {i: eye_sbs + n_m[i] for i in inst}
    pw = dict(n_m)
    for _ in range(int(math.log2(L)) - 1):
        pw = {i: mm(pw[i], bd_rows(pw[i])) for i in inst}
        z = {i: z[i] + mm(z[i], bd_rows(pw[i])) for i in inst}
    vbd = {(c, g): bd_rows(grp(c, g, "v")) for (c, g) in inst}
    w1 = {(c, g): mm(z[c, g], bd_rows(grp(c, g, "a"))) for (c, g) in inst}
    tv = {i: mm(tak[i], vbd[i]) for i in inst}
    c1 = {i: mm(z[i], bd_rows(tv[i])) for i in inst}
    w2 = {(c, g): grp(c, g, "r") + mm(trb[c, g], bd_rows(w1[c, g])) for (c, g) in inst}
    c2 = {i: mm(trb[i], bd_rows(c1[i])) + mm(trk[i], vbd[i]) for i in inst}
    m_lr = {(c, g): jnp.where(same_head, tn(w1[c, g], grp(c, g, "be")), 0.0).astype(BF16) for (c, g) in inst}
    d_sbs = {}
    for (c, g) in inst:
        full = jnp.where(same_head,
                         tn(jnp.concatenate([c1[c, g], grp(c, g, "v")], axis=0),
                            jnp.concatenate([grp(c, g, "be"), grp(c, g, "ke")], axis=0)), 0.0)
        d_sbs[c, g] = ((full[0:L] + full[L:2 * L]) + (full[2 * L:3 * L] + full[3 * L:4 * L]))

    for g in range(n_groups):
        s = s_ref[:, g * gw:(g + 1) * gw]
        for c in range(chunks):
            g_bd = jnp.where(same_head, jnp.concatenate([s.T.astype(BF16)] * WKV_GROUP, axis=1),
                             jnp.zeros((), BF16))
            y_ref[0, c * L:(c + 1) * L, g * gw:(g + 1) * gw] = mm(w2[c, g], g_bd) + c2[c, g]
            s = s * grp(c, g, "e_last") + mm(s, m_lr[c, g]) + d_sbs[c, g]
        s_ref[:, g * gw:(g + 1) * gw] = s


def _dot_exact_rhs_left(m_bf16, x):
    hi, mid, lo = _split3(x)
    return (jnp.dot(m_bf16, hi, preferred_element_type=F32)
            + jnp.dot(m_bf16, mid, preferred_element_type=F32)
            + jnp.dot(m_bf16, lo, preferred_element_type=F32))


def _wkv(r, lw, k, v, kk, b, *, chunks):
    B, T, C = r.shape
    rows = chunks * WKV_CHUNK
    tok = pl.BlockSpec((1, rows, C), lambda bb, c: (bb, c, 0))
    return pl.pallas_call(
        functools.partial(_wkv_kernel, chunks=chunks), grid=(B, T // rows),
        in_specs=[tok] * 6,
        out_specs=tok,
        out_shape=jax.ShapeDtypeStruct((B, T, C), F32),
        scratch_shapes=[pltpu.VMEM((HEAD_DIM, C), F32)],
        compiler_params=_cparams(("parallel", "arbitrary")), name="wkv",
    )(r, lw, k, v, kk, b)


def _mix_kernel(x_ref, on_ref, y_ref, bonus_ref, g_ref, bd_ref, lnw_ref, lnb_ref, wo_ref, g2_ref, x1_ref, h2_ref):
    y = y_ref[...]
    mu = _dot_exact_rhs(y, bd_ref[...]) * (1.0 / HEAD_DIM)
    yc = y - mu
    var = _dot_exact_rhs(yc * yc, bd_ref[...]) * (1.0 / HEAD_DIM)
    yn = yc * lax.rsqrt(var + GN_EPS) * lnw_ref[...] + lnb_ref[...]
    orw = (yn + bonus_ref[...]) * g_ref[...]
    x1 = (x_ref[...] + _bdot(on_ref[...], wo_ref[0:D_NSA, :]) + _bdot(orw, wo_ref[D_NSA:, :]))
    x1_ref[...] = x1
    ms = jnp.mean(x1 * x1, axis=-1, keepdims=True)
    h2_ref[...] = (x1 * lax.rsqrt(ms + NORM_EPS) * g2_ref[...]).astype(BF16)


def _mix(x2d, on2d, y2d, bonus2d, g2d, bd, lnw, lnb, wo, g2, *, tm):
    n, d = x2d.shape
    tok = lambda w: pl.BlockSpec((tm, w), lambda i: (i, 0))
    full = lambda a: pl.BlockSpec(a.shape, lambda i: (0,) * a.ndim)
    return pl.pallas_call(
        _mix_kernel, grid=(n // tm,),
        in_specs=[tok(d), tok(D_NSA), tok(D_RWKV), tok(D_RWKV), tok(D_RWKV), full(bd), full(lnw), full(lnb),
                  full(wo), full(g2)],
        out_specs=(tok(d), tok(d)),
        out_shape=(jax.ShapeDtypeStruct((n, d), F32), jax.ShapeDtypeStruct((n, d), BF16)),
        compiler_params=_cparams(("parallel",)), name="mix",
    )(x2d, on2d, y2d, bonus2d, g2d, bd, lnw, lnb, wo, g2)


FFN_HALO = 16


def _ffn_kernel(x1_ref, h2_ref, halo_ref, wv_ref, wg_ref, cwv_ref, cwg_ref, cbv_ref, cbg_ref, wd_ref,
                o_ref, ext_ref, acc_ref, *, tiles_per_seq):
    tm = h2_ref.shape[0]
    j = pl.program_id(1)
    seq_start = (pl.program_id(0) % tiles_per_seq) == 0

    @pl.when(j == 0)
    def _():
        acc_ref[...] = jnp.zeros(acc_ref.shape, F32)

    h_main = h2_ref[...]
    h_halo = halo_ref[...]

    def conv_branch(w_ref, cw_ref, cb_ref):
        u_halo = jnp.dot(h_halo, w_ref[...], preferred_element_type=F32)
        ext_ref[0:FFN_HALO, :] = jnp.where(seq_start, 0.0, u_halo)
        ext_ref[FFN_HALO:, :] = jnp.dot(h_main, w_ref[...], preferred_element_type=F32)
        out = cb_ref[...] + ext_ref[pl.ds(FFN_HALO, tm), :] * cw_ref[CONV_W - 1:CONV_W, :]
        for i in range(CONV_W - 1):
            back = CONV_W - 1 - i
            out = out + ext_ref[pl.ds(FFN_HALO - back, tm), :] * cw_ref[i:i + 1, :]
        return out

    u_val = conv_branch(wv_ref, cwv_ref, cbv_ref)
    u_gate = conv_branch(wg_ref, cwg_ref, cbg_ref)
    act = (u_gate * jax.nn.sigmoid(u_gate)) * u_val
    acc_ref[...] += jnp.dot(act.astype(BF16), wd_ref[...], preferred_element_type=F32)

    @pl.when(j == pl.num_programs(1) - 1)
    def _():
        o_ref[...] = x1_ref[...] + acc_ref[...]


def _ffn(x1, h2, wv, wg, cwv, cwg, cbv, cbg, wd, *, tm, ft, seq_len):
    n, d = x1.shape
    dff = wv.shape[1]
    hb = tm // FFN_HALO
    grid = (n // tm, dff // ft)
    kern = functools.partial(_ffn_kernel, tiles_per_seq=seq_len // tm)
    return pl.pallas_call(
        kern, grid=grid,
        in_specs=[pl.BlockSpec((tm, d), lambda i, j: (i, 0)),
                  pl.BlockSpec((tm, d), lambda i, j: (i, 0)),
                  pl.BlockSpec((FFN_HALO, d), lambda i, j: (jnp.maximum(i * hb - 1, 0), 0)),
                  pl.BlockSpec((d, ft), lambda i, j: (0, j)),
                  pl.BlockSpec((d, ft), lambda i, j: (0, j)),
                  pl.BlockSpec((CONV_W, ft), lambda i, j: (0, j)),
                  pl.BlockSpec((CONV_W, ft), lambda i, j: (0, j)),
                  pl.BlockSpec((1, ft), lambda i, j: (0, j)),
                  pl.BlockSpec((1, ft), lambda i, j: (0, j)),
                  pl.BlockSpec((ft, d), lambda i, j: (j, 0))],
        out_specs=pl.BlockSpec((tm, d), lambda i, j: (i, 0)),
        out_shape=jax.ShapeDtypeStruct((n, d), F32),
        scratch_shapes=[pltpu.VMEM((tm + FFN_HALO, ft), F32), pltpu.VMEM((tm, d), F32)],
        compiler_params=_cparams(("parallel", "arbitrary")), name="ffn",
    )(x1, h2, h2, wv, wg, cwv, cwg, cbv, cbg, wd)


def _t5_bucket(dist):
    n = jnp.maximum(dist, 0)
    max_exact = N_BUCKETS // 2
    nf = jnp.maximum(n, 1).astype(F32)
    large = max_exact + (jnp.log(nf / max_exact) / math.log(MAX_DISTANCE / max_exact)
                         * (N_BUCKETS - max_exact)).astype(jnp.int32)
    large = jnp.minimum(large, N_BUCKETS - 1)
    return jnp.where(n < max_exact, n, large)


def _bias_tables(rel_bias):
    rel = (rel_bias - rel_bias[N_BUCKETS - 1][None, :]).reshape(N_BUCKETS, NSA_GROUPS, 1, NSA_HPG, 1)

    def lookup(d, valid, fill):
        bucket = _t5_bucket(d)[None, :, None, :]
        v = jnp.zeros((NSA_GROUPS, d.shape[0], NSA_HPG, QT), F32)
        for bkt in range(N_BUCKETS - 1):
            v = jnp.where(bucket == bkt, rel[bkt], v)
        v = jnp.where(valid[None, :, None, :], v, fill)
        return v.reshape(NSA_GROUPS, d.shape[0], NSA_HPG * QT)

    t_idx = jnp.arange(QT)[None, :]
    d_n = QT + t_idx - jnp.arange(NEAR_KEYS)[:, None]
    stab = lookup(d_n, d_n >= 0, NEG_INF)
    d_w = WINDOW + t_idx - jnp.arange(WIN_KEYS)[:, None]
    wtab = lookup(d_w, (d_w >= 0) & (d_w < WINDOW), NEG_INF)
    d_c = t_idx - CMP_STRIDE * jnp.arange(CMP_BAND)[:, None] + (CMP_STRIDE * CMP_PAD - CMP_BLOCK + 1)
    bc = lookup(d_c, d_c >= 0, 0.0)
    return stab, wtab, bc


def _compress_weights(w1):
    half = CMP_BLOCK // 2
    w1r = w1.reshape(2, half, HEAD_DIM, CMP_HIDDEN).transpose(1, 2, 0, 3)
    big = jnp.zeros((half, NSA_GROUPS, HEAD_DIM, NSA_GROUPS, 2, CMP_HIDDEN), w1.dtype)
    for g in range(NSA_GROUPS):
        big = big.at[:, g, :, g, :, :].set(w1r)
    return big.reshape(half * NSA_GROUPS * HEAD_DIM, NSA_GROUPS * 2 * CMP_HIDDEN)


def kernel(x, norm1_g, w_in, q_norm_g, k_norm_g, cmp_pos, cmp_w1, cmp_b1, cmp_w2, cmp_b2, rel_bias, rwkv_mu,
           w0, w2, a0, a2, g2, k_k, k_a, r_k, ln_x_w, ln_x_b, w_out, norm2_g, ffn_up, conv_w, conv_b, ffn_down):
    B, T, D = x.shape
    depth = w_in.shape[0]
    d_ff = ffn_down.shape[1]
    assert T % 2048 == 0 and D_NSA + 6 * D_KV + 3 * NSA_HEADS + D_RWKV_IN == w_in.shape[2]
    ncp = T // CMP_STRIDE + QT
    stab, wtab, bc = _bias_tables(rel_bias)
    ii = jnp.arange(D_RWKV)
    bd = (ii[:, None] // HEAD_DIM == ii[None, :] // HEAD_DIM).astype(BF16)

    for l in range(depth):
        wi = w_in[l]
        o = D_NSA
        q_w, kc_w, vc_w, ksl_w, vsl_w, kwn_w, vwn_w = (
            wi[:, 0:o], wi[:, o:o + D_KV], wi[:, o + D_KV:o + 2 * D_KV], wi[:, o + 2 * D_KV:o + 3 * D_KV],
            wi[:, o + 3 * D_KV:o + 4 * D_KV], wi[:, o + 4 * D_KV:o + 5 * D_KV], wi[:, o + 5 * D_KV:o + 6 * D_KV])
        gl_w = wi[:, o + 6 * D_KV:o + 6 * D_KV + 3 * NSA_HEADS]
        rw_w = wi[:, o + 6 * D_KV + 3 * NSA_HEADS:]
        wstd = jnp.concatenate([kc_w, vc_w, ksl_w, kwn_w, rw_w], axis=1).astype(BF16)
        gl_rows = gl_w.T.reshape(NSA_GROUPS, 3 * NSA_HPG, D)
        gl_rows = jnp.pad(gl_rows, ((0, 0), (0, GATE_ROWS - 3 * NSA_HPG), (0, 0))).reshape(-1, D)
        wt = jnp.concatenate([q_w.T, vsl_w.T, vwn_w.T, gl_rows], axis=0).astype(BF16)
        qg = q_norm_g[l].reshape(HEAD_DIM, 1)
        kg = jnp.stack([jnp.tile(k_norm_g[l, 1], NSA_GROUPS), jnp.tile(k_norm_g[l, 2], NSA_GROUPS)])

        qT, kc, vc, ksel, kwin, vT5, gT, rw = _proj(x, norm1_g[l].reshape(1, D), wstd, wt, qg, kg, tm=512)

        wbig = jnp.stack([_compress_weights(cmp_w1[l, 0]), _compress_weights(cmp_w1[l, 1])]).astype(BF16)
        pos8 = jnp.pad(cmp_pos[l].reshape(2, 1, CMP_BLOCK * HEAD_DIM),
                       ((0, 0), (0, V7X_SUBLANES - 1), (0, 0))).astype(BF16)
        rows16 = CMP_STRIDE * D_KV
        kcmp, vcmpT = _compress(
            kc.reshape(B, T // CMP_STRIDE, rows16), vc.reshape(B, T // CMP_STRIDE, rows16), wbig, pos8,
            cmp_w1[l].astype(BF16), cmp_b1[l].reshape(2, 1, CMP_HIDDEN), cmp_w2[l, 0].astype(BF16),
            cmp_w2[l, 1].T.astype(BF16), cmp_b2[l, 0].reshape(1, HEAD_DIM), cmp_b2[l, 1].reshape(HEAD_DIM, 1),
            k_norm_g[l, 0].reshape(1, HEAD_DIM), ncp=ncp)

        def front_pad(k):
            flag = (jnp.arange(k.shape[-1]) == HEAD_DIM).astype(BF16)
            return jnp.concatenate([jnp.broadcast_to(flag, k.shape[:2] + (ATT_PAD, k.shape[-1])), k], axis=2)

        vT5p = jnp.pad(vT5, ((0, 0), (0, 0), (ATT_PAD // QT, 0), (0, 0), (0, 0)))
        o_nsa = _attn(qT, kcmp, vcmpT, bc, front_pad(ksel), vT5p, front_pad(kwin), stab, wtab, gT)

        row = lambda a: a.reshape(1, -1)
        r, lw, k2, v, kkn, bvec, g, bonus = _rwkv_prep(
            rw, row(rwkv_mu[l]), row(w0[l]), w2[l].astype(BF16), row(a0[l]), a2[l].astype(BF16),
            g2[l].astype(BF16), row(k_k[l]), row(k_a[l]), row(r_k[l]), bd, tm=256)
        y = _wkv(r, lw, k2, v, kkn, bvec, chunks=2)

        n = B * T
        x1, h2 = _mix(x.reshape(n, D), o_nsa.reshape(n, D_NSA), y.reshape(n, D_RWKV),
                      bonus.reshape(n, D_RWKV), g.reshape(n, D_RWKV), bd, row(ln_x_w[l]), row(ln_x_b[l]),
                      w_out[l].astype(BF16), norm2_g[l].reshape(1, D), tm=512)

        up = ffn_up[l].astype(BF16)
        x = _ffn(x1, h2, up[:, :d_ff], up[:, d_ff:], conv_w[l][:, :d_ff], conv_w[l][:, d_ff:],
                 conv_b[l][:d_ff].reshape(1, -1), conv_b[l][d_ff:].reshape(1, -1),
                 ffn_down[l].astype(BF16), tm=512, ft=256, seq_len=T).reshape(B, T, D)
    return x
```

```python
import functools
import math

import jax
import jax.numpy as jnp
import numpy as np
from jax import lax
from jax.experimental import pallas as pl
from jax.experimental.pallas import tpu as pltpu

F32 = jnp.float32
BF16 = jnp.bfloat16

V7X_LANES = 128
V7X_SUBLANES = 8
V7X_VMEM_LIMIT_BYTES = 56 * 1024 * 1024

HEAD_DIM = 64
NSA_HEADS = 8
NSA_GROUPS = 2
NSA_HPG = NSA_HEADS // NSA_GROUPS
RWKV_HEADS = 8
D_NSA = NSA_HEADS * HEAD_DIM
D_RWKV = RWKV_HEADS * HEAD_DIM
D_KV = NSA_GROUPS * HEAD_DIM
CMP_BLOCK = 32
CMP_STRIDE = 16
CMP_HIDDEN = 128
SEL_BLOCK = 64
SEL_TOPN = 16
WINDOW = 512
N_BUCKETS = 32
MAX_DISTANCE = 128
LORA_W = 64
LORA_A = 64
LORA_G = 128
D_RWKV_IN = 3 * D_RWKV + LORA_W + LORA_A + LORA_G
CONV_W = 3
NORM_EPS = 1e-6
GN_EPS = 64e-5
NEG_INF = -1e30
FORCE_SCORE = 1e9

QT = 128
CMP_PAD = 8
CMP_BAND = 16
LOG2E = math.log2(math.e)
Q_SCALE = HEAD_DIM ** -0.5 * LOG2E
V_ROWS = HEAD_DIM + 16
ATT_PAD = WINDOW
NEAR_KEYS = 2 * QT
FAR_KEYS = 4 * QT
WIN_KEYS = WINDOW + QT
WKV_CHUNK = 64
GATE_ROWS = 16
N_T_ROWS = D_NSA + 2 * D_KV + NSA_GROUPS * GATE_ROWS
N_STD_COLS = 4 * D_KV + D_RWKV_IN


def _cparams(sem):
    return pltpu.CompilerParams(dimension_semantics=sem, vmem_limit_bytes=V7X_VMEM_LIMIT_BYTES)


def _bdot(a, b):
    return jnp.dot(a.astype(BF16), b.astype(BF16), preferred_element_type=F32)


def _bdot_nt(a, b):
    return lax.dot_general(a.astype(BF16), b.astype(BF16), (((1,), (1,)), ((), ())),
                           preferred_element_type=F32)


def _split3(x):
    hi = x.astype(BF16)
    r1 = x - hi.astype(F32)
    mid = r1.astype(BF16)
    lo = (r1 - mid.astype(F32)).astype(BF16)
    return hi, mid, lo


def _dot_exact_rhs(x, m_bf16):
    hi, mid, lo = _split3(x)
    return (jnp.dot(hi, m_bf16, preferred_element_type=F32)
            + jnp.dot(mid, m_bf16, preferred_element_type=F32)
            + jnp.dot(lo, m_bf16, preferred_element_type=F32))


def _proj_kernel(x_ref, g1_ref, wstd_ref, wt_ref, qg_ref, kg_ref,
                 qT_ref, kc_ref, vc_ref, ksel_ref, kwin_ref, vT_ref, gT_ref, rw_ref):
    tm = x_ref.shape[1]
    ns = ksel_ref.shape[3] - 2 * HEAD_DIM
    x = x_ref[0]
    ms = jnp.mean(x * x, axis=-1, keepdims=True)
    h = (x * lax.rsqrt(ms + NORM_EPS) * g1_ref[...]).astype(BF16)

    std = jnp.dot(h, wstd_ref[...], preferred_element_type=F32)
    kc_ref[0] = std[:, 0:D_KV].astype(BF16)
    vc_ref[0] = std[:, D_KV:2 * D_KV].astype(BF16)
    rw_ref[0] = std[:, 4 * D_KV:]

    lane = lax.broadcasted_iota(jnp.int32, (tm, D_KV), 1)
    first = lane < HEAD_DIM

    def group_rmsnorm(k, gain):
        k2 = k * k
        s0 = jnp.sum(jnp.where(first, k2, 0.0), axis=-1, keepdims=True)
        s1 = jnp.sum(jnp.where(first, 0.0, k2), axis=-1, keepdims=True)
        ms_g = jnp.where(first, s0, s1) * (1.0 / HEAD_DIM)
        return k * lax.rsqrt(ms_g + NORM_EPS) * gain

    ksl = group_rmsnorm(std[:, 2 * D_KV:3 * D_KV], kg_ref[0:1, :]).astype(BF16)
    kwn = group_rmsnorm(std[:, 3 * D_KV:4 * D_KV], kg_ref[1:2, :]).astype(BF16)

    tok = pl.program_id(1) * tm + lax.broadcasted_iota(jnp.int32, (tm, ns), 0)
    blk = lax.broadcasted_iota(jnp.int32, (tm, ns), 1)
    onehot = jnp.where((tok // SEL_BLOCK) == blk, 1.0, 0.0).astype(BF16)
    zeros = jnp.zeros((tm, HEAD_DIM), BF16)
    for g in range(NSA_GROUPS):
        ksel_ref[0, g, :, 0:HEAD_DIM] = ksl[:, g * HEAD_DIM:(g + 1) * HEAD_DIM]
        ksel_ref[0, g, :, HEAD_DIM:2 * HEAD_DIM] = zeros
        ksel_ref[0, g, :, 2 * HEAD_DIM:] = onehot
        kwin_ref[0, g, :, 0:HEAD_DIM] = kwn[:, g * HEAD_DIM:(g + 1) * HEAD_DIM]
        kwin_ref[0, g, :, HEAD_DIM:] = zeros

    tr = lax.dot_general(wt_ref[...], h, (((1,), (1,)), ((), ())),
                         preferred_element_type=F32)
    q = tr[0:D_NSA].reshape(NSA_HEADS, HEAD_DIM, tm)
    qms = jnp.mean(q * q, axis=1, keepdims=True)
    qn = q * lax.rsqrt(qms + NORM_EPS) * qg_ref[...][None] * Q_SCALE
    qT_ref[0] = qn.reshape(D_NSA, tm).astype(BF16)
    vt = tr[D_NSA:D_NSA + 2 * D_KV].astype(BF16)
    ones_rows = jnp.where(lax.broadcasted_iota(jnp.int32, (V_ROWS - HEAD_DIM, QT), 0) == 0, 1.0, 0.0).astype(BF16)
    for a in range(2 * NSA_GROUPS):
        for j in range(tm // QT):
            vT_ref[0, a, j, 0:HEAD_DIM, :] = vt[a * HEAD_DIM:(a + 1) * HEAD_DIM, j * QT:(j + 1) * QT]
            vT_ref[0, a, j, HEAD_DIM:, :] = ones_rows
    gT_ref[0] = jax.nn.sigmoid(tr[D_NSA + 2 * D_KV:])


def _proj(x, g1, wstd, wt, qg, kg, *, tm):
    B, T, D = x.shape
    ns = T // SEL_BLOCK
    grid = (B, T // tm)
    const2 = lambda b, i: (0, 0)
    out_shape = (
        jax.ShapeDtypeStruct((B, D_NSA, T), BF16),
        jax.ShapeDtypeStruct((B, T, D_KV), BF16),
        jax.ShapeDtypeStruct((B, T, D_KV), BF16),
        jax.ShapeDtypeStruct((B, NSA_GROUPS, T, 2 * HEAD_DIM + ns), BF16),
        jax.ShapeDtypeStruct((B, NSA_GROUPS, T, 2 * HEAD_DIM), BF16),
        jax.ShapeDtypeStruct((B, 2 * NSA_GROUPS, T // QT, V_ROWS, QT), BF16),
        jax.ShapeDtypeStruct((B, NSA_GROUPS * GATE_ROWS, T), F32),
        jax.ShapeDtypeStruct((B, T, D_RWKV_IN), F32),
    )
    out_specs = (
        pl.BlockSpec((1, D_NSA, tm), lambda b, i: (b, 0, i)),
        pl.BlockSpec((1, tm, D_KV), lambda b, i: (b, i, 0)),
        pl.BlockSpec((1, tm, D_KV), lambda b, i: (b, i, 0)),
        pl.BlockSpec((1, NSA_GROUPS, tm, 2 * HEAD_DIM + ns), lambda b, i: (b, 0, i, 0)),
        pl.BlockSpec((1, NSA_GROUPS, tm, 2 * HEAD_DIM), lambda b, i: (b, 0, i, 0)),
        pl.BlockSpec((1, 2 * NSA_GROUPS, tm // QT, V_ROWS, QT), lambda b, i: (b, 0, i, 0, 0)),
        pl.BlockSpec((1, NSA_GROUPS * GATE_ROWS, tm), lambda b, i: (b, 0, i)),
        pl.BlockSpec((1, tm, D_RWKV_IN), lambda b, i: (b, i, 0)),
    )
    in_specs = [
        pl.BlockSpec((1, tm, D), lambda b, i: (b, i, 0)),
        pl.BlockSpec(g1.shape, const2),
        pl.BlockSpec(wstd.shape, const2),
        pl.BlockSpec(wt.shape, const2),
        pl.BlockSpec(qg.shape, const2),
        pl.BlockSpec(kg.shape, const2),
    ]
    return pl.pallas_call(
        _proj_kernel, grid=grid, in_specs=in_specs, out_specs=out_specs, out_shape=out_shape,
        compiler_params=_cparams(("parallel", "parallel")), name="proj",
    )(x, g1, wstd, wt, qg, kg)


def _gelu_tanh(x):
    c = math.sqrt(2.0 / math.pi)
    return x * (0.5 * (1.0 + jnp.tanh(c * (x + 0.044715 * (x * x * x)))))


def _compress_kernel(kc_ref, vc_ref, wbig_ref, pos_ref, w1_ref, b1_ref, w2k_ref, w2vT_ref,
                     b2k_ref, b2v_ref, kg_ref, kcmp_ref, vcmpT_ref, act_ref):
    m = kc_ref.shape[1]
    nc = m - 1
    ncp = act_ref.shape[0]
    row_m = lax.broadcasted_iota(jnp.int32, (m, CMP_HIDDEN), 0)
    row_p = lax.broadcasted_iota(jnp.int32, (ncp, HEAD_DIM), 0)
    col_p = lax.broadcasted_iota(jnp.int32, (HEAD_DIM, ncp), 1)
    col_f = lax.broadcasted_iota(jnp.int32, (ncp, HEAD_DIM), 1)
    row_o = lax.broadcasted_iota(jnp.int32, (V_ROWS - HEAD_DIM, ncp), 0)
    act_ref[...] = jnp.zeros(act_ref.shape, F32)
    for which, src_ref in enumerate((kc_ref, vc_ref)):
        p = jnp.dot(src_ref[0], wbig_ref[which], preferred_element_type=F32)
        posc = jnp.dot(pos_ref[which], w1_ref[which], preferred_element_type=F32)[0:1]
        for g in range(NSA_GROUPS):
            top = p[:, (2 * g) * CMP_HIDDEN:(2 * g + 1) * CMP_HIDDEN]
            bot = p[:, (2 * g + 1) * CMP_HIDDEN:(2 * g + 2) * CMP_HIDDEN]
            hid = top + pltpu.roll(bot, m - 1, 0) + (b1_ref[which] + posc)
            act = jnp.where(row_m < nc, _gelu_tanh(hid), 0.0)
            act_ref[CMP_PAD:CMP_PAD + m, :] = act
            ap = act_ref[...].astype(BF16)
            if which == 0:
                kc = jnp.dot(ap, w2k_ref[...], preferred_element_type=F32) + b2k_ref[...]
                ms = jnp.mean(kc * kc, axis=-1, keepdims=True)
                kc = kc * lax.rsqrt(ms + NORM_EPS) * kg_ref[...]
                valid = (row_p >= CMP_PAD) & (row_p < CMP_PAD + nc)
                kcmp_ref[0, g, :, 0:HEAD_DIM] = jnp.where(valid, kc, 0.0).astype(BF16)
                flag = (col_f == 0) & jnp.logical_not(valid)
                kcmp_ref[0, g, :, HEAD_DIM:] = jnp.where(flag, 1.0, 0.0).astype(BF16)
            else:
                vt = lax.dot_general(w2vT_ref[...], ap, (((1,), (1,)), ((), ())),
                                     preferred_element_type=F32) + b2v_ref[...]
                valid = (col_p >= CMP_PAD) & (col_p < CMP_PAD + nc)
                vcmpT_ref[0, g, 0:HEAD_DIM, :] = jnp.where(valid, vt, 0.0).astype(BF16)
                vcmpT_ref[0, g, HEAD_DIM:, :] = jnp.where(row_o == 0, 1.0, 0.0).astype(BF16)


def _compress(kc16, vc16, wbig, pos8, w1, b1, w2k, w2vT, b2k, b2v, kg, *, ncp):
    B, m, _ = kc16.shape
    full = lambda a: pl.BlockSpec(a.shape, lambda b: (0,) * a.ndim)
    return pl.pallas_call(
        _compress_kernel, grid=(B,),
        in_specs=[pl.BlockSpec((1, m, kc16.shape[2]), lambda b: (b, 0, 0)),
                  pl.BlockSpec((1, m, vc16.shape[2]), lambda b: (b, 0, 0)),
                  full(wbig), full(pos8), full(w1), full(b1), full(w2k), full(w2vT),
                  full(b2k), full(b2v), full(kg)],
        out_specs=(pl.BlockSpec((1, NSA_GROUPS, ncp, 2 * HEAD_DIM), lambda b: (b, 0, 0, 0)),
                   pl.BlockSpec((1, NSA_GROUPS, V_ROWS, ncp), lambda b: (b, 0, 0, 0))),
        out_shape=(jax.ShapeDtypeStruct((B, NSA_GROUPS, ncp, 2 * HEAD_DIM), BF16),
                   jax.ShapeDtypeStruct((B, NSA_GROUPS, V_ROWS, ncp), BF16)),
        scratch_shapes=[pltpu.VMEM((ncp, CMP_HIDDEN), F32)],
        compiler_params=_cparams(("parallel",)), name="compress",
    )(kc16, vc16, wbig, pos8, w1, b1, w2k, w2vT, b2k, b2v, kg)


def _flash_update(s_ref, vT, m_ref, acc_ref):
    m_prev = m_ref[...]
    m_new = jnp.maximum(m_prev, jnp.max(s_ref[...], axis=0, keepdims=True))
    p = jnp.exp2(s_ref[...] - m_new)
    acc_ref[...] = (jnp.exp2(m_prev - m_new) * acc_ref[...]
                    + jnp.dot(vT, p.astype(BF16), preferred_element_type=F32))
    m_ref[...] = m_new


def _attn_kernel(qT_ref, kcmp_ref, vcmpT_ref, bc_ref, ksel_ref, vselT_ref, kwin_ref, vwinT_ref,
                 stab_ref, wtab_ref, gT_ref, o_ref,
                 lc_ref, psum_ref, qaug_ref, qfar_ref, oc_ref, ow_ref, ms_ref, accs_ref, sa_ref, sb_ref):
    qt = pl.program_id(2)
    ncp = kcmp_ref.shape[2]
    ns = qaug_ref.shape[0] - 2 * HEAD_DIM
    nq = NSA_HPG * QT
    t_lane = qt * QT + lax.broadcasted_iota(jnp.int32, (1, QT), 1)

    for r in range(NSA_HPG):
        qaug_ref[0:HEAD_DIM, r * QT:(r + 1) * QT] = qT_ref[0, r * HEAD_DIM:(r + 1) * HEAD_DIM, :]
    flag_row = lax.broadcasted_iota(jnp.int32, (HEAD_DIM, nq), 0) == 0
    qaug_ref[HEAD_DIM:2 * HEAD_DIM, :] = jnp.where(flag_row, NEG_INF, 0.0).astype(BF16)
    qk = qaug_ref[0:2 * HEAD_DIM, :]

    rho = lax.broadcasted_iota(jnp.int32, (ncp, nq), 0)
    band0 = pl.multiple_of(qt * (QT // CMP_STRIDE), V7X_SUBLANES)
    lc_ref[...] = jnp.dot(kcmp_ref[0, 0], qk, preferred_element_type=F32)
    lc_ref[pl.ds(band0, CMP_BAND), :] += bc_ref[0]
    lc = jnp.where(rho < band0 + CMP_BAND, lc_ref[...], NEG_INF)
    e = jnp.exp2(lc - jnp.max(lc, axis=0, keepdims=True))
    oc_aug = jnp.dot(vcmpT_ref[0, 0], e.astype(BF16), preferred_element_type=F32)
    t_q = qt * QT + lax.broadcasted_iota(jnp.int32, (1, nq), 1) % QT
    inv_c = jnp.where(t_q >= CMP_BLOCK - 1, 1.0 / oc_aug[HEAD_DIM:HEAD_DIM + 1, :], 0.0)
    oc_ref[...] = oc_aug[0:HEAD_DIM] * inv_c
    p = e * inv_c
    psum_ref[...] = (p[:, 0:QT] + p[:, QT:2 * QT]) + (p[:, 2 * QT:3 * QT] + p[:, 3 * QT:4 * QT])

    imp = psum_ref[pl.ds(CMP_PAD - 1, ns, stride=4), :]
    for k in range(1, 5):
        imp = imp + psum_ref[pl.ds(CMP_PAD - 1 + k, ns, stride=4), :]
    jrow = lax.broadcasted_iota(jnp.int32, (ns, QT), 0)
    cur = t_lane // SEL_BLOCK
    forced = (jrow == 0) | (jrow == cur) | (jrow == cur - 1)
    score = jnp.where(jrow * SEL_BLOCK <= t_lane, jnp.where(forced, FORCE_SCORE, imp), NEG_INF)
    sel = jnp.zeros((ns, QT), jnp.bool_)
    for _ in range(min(SEL_TOPN, ns)):
        mx = jnp.max(score, axis=0, keepdims=True)
        idx = jnp.min(jnp.where(score == mx, jrow, ns), axis=0, keepdims=True)
        hit = jrow == idx
        sel = sel | hit
        score = jnp.where(hit, -jnp.inf, score)
    negmask = jnp.where(sel, 0.0, NEG_INF).astype(BF16)
    negfar = jnp.where(sel & (jrow < (qt - 1) * (QT // SEL_BLOCK)), 0.0, NEG_INF).astype(BF16)
    qfar_ref[0:2 * HEAD_DIM, :] = qaug_ref[0:2 * HEAD_DIM, :]
    for r in range(NSA_HPG):
        qaug_ref[2 * HEAD_DIM:, r * QT:(r + 1) * QT] = negmask
        qfar_ref[2 * HEAD_DIM:, r * QT:(r + 1) * QT] = negfar

    w0 = pl.multiple_of(qt * QT, QT)
    sw = jnp.dot(kwin_ref[0, 0, pl.ds(w0, WIN_KEYS), :], qk, preferred_element_type=F32) + wtab_ref[0]
    pw = jnp.exp2(sw - jnp.max(sw, axis=0, keepdims=True))
    vw = jnp.concatenate([vwinT_ref[0, 0, qt + j] for j in range(WIN_KEYS // QT)], axis=1)
    ow_aug = jnp.dot(vw, pw.astype(BF16), preferred_element_type=F32)
    ow_ref[...] = ow_aug[0:HEAD_DIM] * (1.0 / ow_aug[HEAD_DIM:HEAD_DIM + 1, :])

    n0 = pl.multiple_of((ATT_PAD // QT - 1 + qt) * QT, QT)
    sn = jnp.dot(ksel_ref[0, 0, pl.ds(n0, NEAR_KEYS), :], qaug_ref[...],
                 preferred_element_type=F32) + stab_ref[0]
    mn = jnp.max(sn, axis=0, keepdims=True)
    pn = jnp.exp2(sn - mn)
    vn = jnp.concatenate([vselT_ref[0, 0, ATT_PAD // QT - 1 + qt + j] for j in range(NEAR_KEYS // QT)], axis=1)
    ms_ref[...] = mn
    accs_ref[...] = jnp.dot(vn, pn.astype(BF16), preferred_element_type=F32)

    tiles_per_far = FAR_KEYS // QT
    n_far = (qt - 1 + tiles_per_far - 1) // tiles_per_far
    last_far = (ksel_ref.shape[2] - ATT_PAD) // FAR_KEYS - 1

    def far_logits(g, dst_ref):
        r0 = pl.multiple_of(ATT_PAD + jnp.minimum(g, last_far) * FAR_KEYS, FAR_KEYS)
        dst_ref[...] = jnp.dot(ksel_ref[0, 0, pl.ds(r0, FAR_KEYS), :], qfar_ref[...],
                               preferred_element_type=F32)

    def far_values(g):
        t0 = (ATT_PAD + g * FAR_KEYS) // QT
        return jnp.concatenate([vselT_ref[0, 0, t0 + j] for j in range(tiles_per_far)], axis=1)

    far_logits(0, sa_ref)

    def far_body(j, carry):
        far_logits(2 * j + 1, sb_ref)
        _flash_update(sa_ref, far_values(2 * j), ms_ref, accs_ref)
        far_logits(2 * j + 2, sa_ref)
        _flash_update(sb_ref, far_values(jnp.minimum(2 * j + 1, last_far)), ms_ref, accs_ref)
        return carry

    lax.fori_loop(0, (n_far + 1) // 2, far_body, 0)

    o_s = accs_ref[0:HEAD_DIM, :] * (1.0 / accs_ref[HEAD_DIM:HEAD_DIM + 1, :])
    for r in range(NSA_HPG):
        cols = slice(r * QT, (r + 1) * QT)
        o = (gT_ref[0, 3 * r:3 * r + 1, :] * oc_ref[:, cols] + gT_ref[0, 3 * r + 1:3 * r + 2, :] * o_s[:, cols]
             + gT_ref[0, 3 * r + 2:3 * r + 3, :] * ow_ref[:, cols])
        o_ref[0, :, r * HEAD_DIM:(r + 1) * HEAD_DIM] = o.T


def _attn(qT, kcmp, vcmpT, bc, ksel, vT5, kwin, stab, wtab, gT):
    B, _, T = qT.shape
    ncp = kcmp.shape[2]
    ns = T // SEL_BLOCK
    nq = NSA_HPG * QT
    tp = T + ATT_PAD
    nt = tp // QT
    grid = (B, NSA_GROUPS, T // QT)
    in_specs = [
        pl.BlockSpec((1, NSA_HPG * HEAD_DIM, QT), lambda b, g, q: (b, g, q)),
        pl.BlockSpec((1, 1, ncp, 2 * HEAD_DIM), lambda b, g, q: (b, g, 0, 0)),
        pl.BlockSpec((1, 1, V_ROWS, ncp), lambda b, g, q: (b, g, 0, 0)),
        pl.BlockSpec((1, CMP_BAND, nq), lambda b, g, q: (g, 0, 0)),
        pl.BlockSpec((1, 1, tp, 2 * HEAD_DIM + ns), lambda b, g, q: (b, g, 0, 0)),
        pl.BlockSpec((1, 1, nt, V_ROWS, QT), lambda b, g, q: (b, g, 0, 0, 0)),
        pl.BlockSpec((1, 1, tp, 2 * HEAD_DIM), lambda b, g, q: (b, g, 0, 0)),
        pl.BlockSpec((1, 1, nt, V_ROWS, QT), lambda b, g, q: (b, NSA_GROUPS + g, 0, 0, 0)),
        pl.BlockSpec((1, NEAR_KEYS, nq), lambda b, g, q: (g, 0, 0)),
        pl.BlockSpec((1, WIN_KEYS, nq), lambda b, g, q: (g, 0, 0)),
        pl.BlockSpec((1, GATE_ROWS, QT), lambda b, g, q: (b, g, q)),
    ]
    scratch = [
        pltpu.VMEM((ncp, nq), F32),
        pltpu.VMEM((ncp, QT), F32),
        pltpu.VMEM((2 * HEAD_DIM + ns, nq), BF16),
        pltpu.VMEM((2 * HEAD_DIM + ns, nq), BF16),
        pltpu.VMEM((HEAD_DIM, nq), F32),
        pltpu.VMEM((HEAD_DIM, nq), F32),
        pltpu.VMEM((1, nq), F32),
        pltpu.VMEM((V_ROWS, nq), F32),
        pltpu.VMEM((FAR_KEYS, nq), F32),
        pltpu.VMEM((FAR_KEYS, nq), F32),
    ]
    return pl.pallas_call(
        _attn_kernel, grid=grid, in_specs=in_specs,
        out_specs=pl.BlockSpec((1, QT, NSA_HPG * HEAD_DIM), lambda b, g, q: (b, q, g)),
        out_shape=jax.ShapeDtypeStruct((B, T, D_NSA), F32),
        scratch_shapes=scratch,
        compiler_params=_cparams(("parallel", "parallel", "arbitrary")), name="attn",
    )(qT, kcmp, vcmpT, bc, ksel, vT5, kwin, vT5, stab, wtab, gT)


def _rwkv_prep_kernel(rw_ref, halo_ref, mu_ref, w0_ref, w2_ref, a0_ref, a2_ref, g2_ref, kk_ref, ka_ref,
                      rk_ref, bd_ref, r_out, lw_out, k_out, v_out, kk_out, b_out, g_out, bonus_out, ext_ref):
    tm = rw_ref.shape[1]
    first_tile = pl.program_id(1) == 0
    ext_ref[0:V7X_SUBLANES, :] = jnp.where(first_tile, 0.0, halo_ref[0])
    ext_ref[V7X_SUBLANES:, :] = rw_ref[0]
    cur = rw_ref[0]
    prev = ext_ref[pl.ds(V7X_SUBLANES - 1, tm), :]
    mixed = cur + (prev - cur) * mu_ref[...]
    c = D_RWKV
    r = mixed[:, 0:c]
    k = mixed[:, c:2 * c]
    v = mixed[:, 2 * c:3 * c]
    xw = mixed[:, 3 * c:3 * c + LORA_W]
    xa = mixed[:, 3 * c + LORA_W:3 * c + LORA_W + LORA_A]
    xg = mixed[:, 3 * c + LORA_W + LORA_A:]

    z = -(w0_ref[...] + _bdot(jnp.tanh(xw), w2_ref[...]))
    softplus = jnp.maximum(z, 0.0) + jnp.log1p(jnp.exp(-jnp.abs(z)))
    w = -softplus - 0.5
    a = jax.nn.sigmoid(a0_ref[...] + _bdot(xa, a2_ref[...]))
    g = _bdot(jax.nn.sigmoid(xg), g2_ref[...])

    kk = k * kk_ref[...]
    n2 = _dot_exact_rhs(kk * kk, bd_ref[...])
    kkn = kk / jnp.maximum(jnp.sqrt(n2), 1e-12)
    k2 = k * (1.0 + (a - 1.0) * ka_ref[...])
    bonus = _dot_exact_rhs(r * k2 * rk_ref[...], bd_ref[...]) * v

    r_out[0] = r
    lw_out[0] = -jnp.exp(w)
    k_out[0] = k2
    v_out[0] = v
    kk_out[0] = kkn
    b_out[0] = kkn * a
    g_out[0] = g
    bonus_out[0] = bonus


def _rwkv_prep(rw, mu, w0, w2, a0, a2, g2, k_k, k_a, r_k, bd, *, tm):
    B, T, C = rw.shape
    hb = tm // V7X_SUBLANES
    full = lambda a: pl.BlockSpec(a.shape, lambda b, i: (0,) * a.ndim)
    tok = pl.BlockSpec((1, tm, D_RWKV), lambda b, i: (b, i, 0))
    return pl.pallas_call(
        _rwkv_prep_kernel, grid=(B, T // tm),
        in_specs=[pl.BlockSpec((1, tm, C), lambda b, i: (b, i, 0)),
                  pl.BlockSpec((1, V7X_SUBLANES, C), lambda b, i: (b, jnp.maximum(i * hb - 1, 0), 0)),
                  full(mu), full(w0), full(w2), full(a0), full(a2), full(g2), full(k_k), full(k_a),
                  full(r_k), full(bd)],
        out_specs=(tok,) * 8,
        out_shape=(jax.ShapeDtypeStruct((B, T, D_RWKV), F32),) * 8,
        scratch_shapes=[pltpu.VMEM((tm + V7X_SUBLANES, C), F32)],
        compiler_params=_cparams(("parallel", "parallel")), name="rwkv_prep",
    )(rw, rw, mu, w0, w2, a0, a2, g2, k_k, k_a, r_k, bd)


WKV_GROUP = 4
WKV_GW = WKV_GROUP * HEAD_DIM


def _wkv_kernel(r_ref, lw_ref, k_ref, v_ref, kk_ref, b_ref, y_ref, s_ref, *, chunks):
    L = WKV_CHUNK
    assert L == HEAD_DIM
    gw = WKV_GW
    n_groups = RWKV_HEADS // WKV_GROUP

    @pl.when(pl.program_id(1) == 0)
    def _():
        s_ref[...] = jnp.zeros(s_ref.shape, F32)

    ti = lax.broadcasted_iota(jnp.int32, (L, L), 0)
    tj = lax.broadcasted_iota(jnp.int32, (L, L), 1)
    tri = jnp.where(ti >= tj, 1.0, 0.0).astype(BF16)
    row = lax.broadcasted_iota(jnp.int32, (L, gw), 0)
    col = lax.broadcasted_iota(jnp.int32, (L, gw), 1) % L
    low_strict = col < row
    low_incl = col <= row
    eye_sbs = jnp.where(col == row, 1.0, 0.0)
    brow = lax.broadcasted_iota(jnp.int32, (gw, gw), 0) // L
    bcol = lax.broadcasted_iota(jnp.int32, (gw, gw), 1) // HEAD_DIM
    same_head = brow == bcol

    def bd_rows(x):
        xb = x.astype(BF16)
        return jnp.where(same_head, jnp.concatenate([xb] * WKV_GROUP, axis=0), jnp.zeros((), BF16))

    def mm(a, b_bf16):
        return jnp.dot(a.astype(BF16), b_bf16, preferred_element_type=F32)

    def tn(a, b):
        return lax.dot_general(a.astype(BF16), b.astype(BF16), (((0,), (0,)), ((), ())),
                               preferred_element_type=F32)

    inst = [(c, g) for c in range(chunks) for g in range(n_groups)]
    pre = {}
    for c in range(chunks):
        rows = slice(c * L, (c + 1) * L)
        lw = lw_ref[0, rows, :]
        cs = _dot_exact_rhs_left(tri, lw)
        c_last = cs[L - 1:L, :]
        e_nc = jnp.exp(-cs)
        e_lc = jnp.exp(c_last - cs)
        kk = kk_ref[0, rows, :]
        b = b_ref[0, rows, :]
        k = k_ref[0, rows, :]
        pre[c] = dict(a=-kk * jnp.exp(cs - lw), r=r_ref[0, rows, :] * jnp.exp(cs), bh=b * e_nc, kh=k * e_nc,
                      be=b * e_lc, ke=k * e_lc, v=v_ref[0, rows, :], e_last=jnp.exp(c_last))

    def grp(c, g, name):
        return pre[c][name][:, g * gw:(g + 1) * gw]

    t_all = {}
    for (c, g) in inst:
        lhs = jnp.concatenate([grp(c, g, "a"), grp(c, g, "r")], axis=0).astype(BF16)
        rhs = jnp.concatenate([bd_rows(grp(c, g, "bh")), bd_rows(grp(c, g, "kh"))], axis=0)
        t_all[c, g] = lax.dot_general(lhs, rhs, (((1,), (1,)), ((), ())), preferred_element_type=F32)
    n_m = {i: jnp.where(low_strict, t_all[i][0:L, 0:gw], 0.0) for i in inst}
    tak = {i: jnp.where(low_strict, t_all[i][0:L, gw:], 0.0) for i in inst}
    trb = {i: jnp.where(low_incl, t_all[i][L:, 0:gw], 0.0) for i in inst}
    trk = {i: jnp.where(low_incl, t_all[i][L:, gw:], 0.0) for i in inst}
    z = {i: eye_sbs + n_m[i] for i in inst}
    pw = dict(n_m)
    for _ in range(int(math.log2(L)) - 1):
        pw = {i: mm(pw[i], bd_rows(pw[i])) for i in inst}
        z = {i: z[i] + mm(z[i], bd_rows(pw[i])) for i in inst}
    vbd = {(c, g): bd_rows(grp(c, g, "v")) for (c, g) in inst}
    w1 = {(c, g): mm(z[c, g], bd_rows(grp(c, g, "a"))) for (c, g) in inst}
    tv = {i: mm(tak[i], vbd[i]) for i in inst}
    c1 = {i: mm(z[i], bd_rows(tv[i])) for i in inst}
    w2 = {(c, g): grp(c, g, "r") + mm(trb[c, g], bd_rows(w1[c, g])) for (c, g) in inst}
    c2 = {i: mm(trb[i], bd_rows(c1[i])) + mm(trk[i], vbd[i]) for i in inst}
    m_lr = {(c, g): jnp.where(same_head, tn(w1[c, g], grp(c, g, "be")), 0.0).astype(BF16) for (c, g) in inst}
    d_sbs = {}
    for (c, g) in inst:
        full = jnp.where(same_head,
                         tn(jnp.concatenate([c1[c, g], grp(c, g, "v")], axis=0),
                            jnp.concatenate([grp(c, g, "be"), grp(c, g, "ke")], axis=0)), 0.0)
        d_sbs[c, g] = ((full[0:L] + full[L:2 * L]) + (full[2 * L:3 * L] + full[3 * L:4 * L]))

    for g in range(n_groups):
        s = s_ref[:, g * gw:(g + 1) * gw]
        for c in range(chunks):
            g_bd = jnp.where(same_head, jnp.concatenate([s.T.astype(BF16)] * WKV_GROUP, axis=1),
                             jnp.zeros((), BF16))
            y_ref[0, c * L:(c + 1) * L, g * gw:(g + 1) * gw] = mm(w2[c, g], g_bd) + c2[c, g]
            s = s * grp(c, g, "e_last") + mm(s, m_lr[c, g]) + d_sbs[c, g]
        s_ref[:, g * gw:(g + 1) * gw] = s


def _dot_exact_rhs_left(m_bf16, x):
    hi, mid, lo = _split3(x)
    return (jnp.dot(m_bf16, hi, preferred_element_type=F32)
            + jnp.dot(m_bf16, mid, preferred_element_type=F32)
            + jnp.dot(m_bf16, lo, preferred_element_type=F32))


def _wkv(r, lw, k, v, kk, b, *, chunks):
    B, T, C = r.shape
    rows = chunks * WKV_CHUNK
    tok = pl.BlockSpec((1, rows, C), lambda bb, c: (bb, c, 0))
    return pl.pallas_call(
        functools.partial(_wkv_kernel, chunks=chunks), grid=(B, T // rows),
        in_specs=[tok] * 6,
        out_specs=tok,
        out_shape=jax.ShapeDtypeStruct((B, T, C), F32),
        scratch_shapes=[pltpu.VMEM((HEAD_DIM, C), F32)],
        compiler_params=_cparams(("parallel", "arbitrary")), name="wkv",
    )(r, lw, k, v, kk, b)


def _mix_kernel(x_ref, on_ref, y_ref, bonus_ref, g_ref, bd_ref, lnw_ref, lnb_ref, wo_ref, g2_ref, x1_ref, h2_ref):
    y = y_ref[...]
    mu = _dot_exact_rhs(y, bd_ref[...]) * (1.0 / HEAD_DIM)
    yc = y - mu
    var = _dot_exact_rhs(yc * yc, bd_ref[...]) * (1.0 / HEAD_DIM)
    yn = yc * lax.rsqrt(var + GN_EPS) * lnw_ref[...] + lnb_ref[...]
    orw = (yn + bonus_ref[...]) * g_ref[...]
    x1 = (x_ref[...] + _bdot(on_ref[...], wo_ref[0:D_NSA, :]) + _bdot(orw, wo_ref[D_NSA:, :]))
    x1_ref[...] = x1
    ms = jnp.mean(x1 * x1, axis=-1, keepdims=True)
    h2_ref[...] = (x1 * lax.rsqrt(ms + NORM_EPS) * g2_ref[...]).astype(BF16)


def _mix(x2d, on2d, y2d, bonus2d, g2d, bd, lnw, lnb, wo, g2, *, tm):
    n, d = x2d.shape
    tok = lambda w: pl.BlockSpec((tm, w), lambda i: (i, 0))
    full = lambda a: pl.BlockSpec(a.shape, lambda i: (0,) * a.ndim)
    return pl.pallas_call(
        _mix_kernel, grid=(n // tm,),
        in_specs=[tok(d), tok(D_NSA), tok(D_RWKV), tok(D_RWKV), tok(D_RWKV), full(bd), full(lnw), full(lnb),
                  full(wo), full(g2)],
        out_specs=(tok(d), tok(d)),
        out_shape=(jax.ShapeDtypeStruct((n, d), F32), jax.ShapeDtypeStruct((n, d), BF16)),
        compiler_params=_cparams(("parallel",)), name="mix",
    )(x2d, on2d, y2d, bonus2d, g2d, bd, lnw, lnb, wo, g2)


FFN_HALO = 16


def _ffn_kernel(x1_ref, h2_ref, halo_ref, wv_ref, wg_ref, cwv_ref, cwg_ref, cbv_ref, cbg_ref, wd_ref,
                o_ref, hext_ref, extv_ref, extg_ref, act_ref, *, tiles_per_seq, ft):
    tm = h2_ref.shape[0]
    dff = wd_ref.shape[0]
    seq_start = (pl.program_id(0) % tiles_per_seq) == 0
    hext_ref[0:FFN_HALO, :] = jnp.where(seq_start, jnp.zeros((), BF16), halo_ref[...])
    hext_ref[FFN_HALO:, :] = h2_ref[...]

    for j in range(dff // ft):
        cols = slice(j * ft, (j + 1) * ft)

        def conv_branch(w_ref, cw_ref, cb_ref, ext_ref):
            ext_ref[j % 2] = jnp.dot(hext_ref[...], w_ref[:, cols], preferred_element_type=F32)
            out = cb_ref[:, cols] + ext_ref[j % 2, pl.ds(FFN_HALO, tm), :] * cw_ref[CONV_W - 1:CONV_W, cols]
            for i in range(CONV_W - 1):
                back = CONV_W - 1 - i
                out = out + ext_ref[j % 2, pl.ds(FFN_HALO - back, tm), :] * cw_ref[i:i + 1, cols]
            return out

        u_val = conv_branch(wv_ref, cwv_ref, cbv_ref, extv_ref)
        u_gate = conv_branch(wg_ref, cwg_ref, cbg_ref, extg_ref)
        act_ref[:, cols] = ((u_gate * jax.nn.sigmoid(u_gate)) * u_val).astype(BF16)
    o_ref[...] = x1_ref[...] + jnp.dot(act_ref[...], wd_ref[...], preferred_element_type=F32)


def _ffn(x1, h2, wv, wg, cwv, cwg, cbv, cbg, wd, *, tm, ft, seq_len):
    n, d = x1.shape
    dff = wv.shape[1]
    hb = tm // FFN_HALO
    kern = functools.partial(_ffn_kernel, tiles_per_seq=seq_len // tm, ft=ft)
    resident = lambda a: pl.BlockSpec(a.shape, lambda i: (0,) * a.ndim, pipeline_mode=pl.Buffered(1))
    return pl.pallas_call(
        kern, grid=(n // tm,),
        in_specs=[pl.BlockSpec((tm, d), lambda i: (i, 0)),
                  pl.BlockSpec((tm, d), lambda i: (i, 0)),
                  pl.BlockSpec((FFN_HALO, d), lambda i: (jnp.maximum(i * hb - 1, 0), 0)),
                  resident(wv), resident(wg), resident(cwv), resident(cwg), resident(cbv), resident(cbg),
                  resident(wd)],
        out_specs=pl.BlockSpec((tm, d), lambda i: (i, 0)),
        out_shape=jax.ShapeDtypeStruct((n, d), F32),
        scratch_shapes=[pltpu.VMEM((tm + FFN_HALO, d), BF16),
                        pltpu.VMEM((2, tm + FFN_HALO, ft), F32), pltpu.VMEM((2, tm + FFN_HALO, ft), F32),
                        pltpu.VMEM((tm, dff), BF16)],
        compiler_params=_cparams(("parallel",)), name="ffn",
    )(x1, h2, h2, wv, wg, cwv, cwg, cbv, cbg, wd)


def _t5_bucket(dist):
    n = jnp.maximum(dist, 0)
    max_exact = N_BUCKETS // 2
    nf = jnp.maximum(n, 1).astype(F32)
    large = max_exact + (jnp.log(nf / max_exact) / math.log(MAX_DISTANCE / max_exact)
                         * (N_BUCKETS - max_exact)).astype(jnp.int32)
    large = jnp.minimum(large, N_BUCKETS - 1)
    return jnp.where(n < max_exact, n, large)


def _bias_tables(rel_bias):
    rel = ((rel_bias - rel_bias[N_BUCKETS - 1][None, :]) * LOG2E).reshape(N_BUCKETS, NSA_GROUPS, 1, NSA_HPG, 1)

    def lookup(d, valid, fill):
        bucket = _t5_bucket(d)[None, :, None, :]
        v = jnp.zeros((NSA_GROUPS, d.shape[0], NSA_HPG, QT), F32)
        for bkt in range(N_BUCKETS - 1):
            v = jnp.where(bucket == bkt, rel[bkt], v)
        v = jnp.where(valid[None, :, None, :], v, fill)
        return v.reshape(NSA_GROUPS, d.shape[0], NSA_HPG * QT)

    t_idx = jnp.arange(QT)[None, :]
    d_n = QT + t_idx - jnp.arange(NEAR_KEYS)[:, None]
    stab = lookup(d_n, d_n >= 0, NEG_INF)
    d_w = WINDOW + t_idx - jnp.arange(WIN_KEYS)[:, None]
    wtab = lookup(d_w, (d_w >= 0) & (d_w < WINDOW), NEG_INF)
    d_c = t_idx - CMP_STRIDE * jnp.arange(CMP_BAND)[:, None] + (CMP_STRIDE * CMP_PAD - CMP_BLOCK + 1)
    bc = lookup(d_c, d_c >= 0, NEG_INF)
    return stab, wtab, bc


def _compress_weights(w1):
    half = CMP_BLOCK // 2
    w1r = w1.reshape(2, half, HEAD_DIM, CMP_HIDDEN).transpose(1, 2, 0, 3)
    big = jnp.zeros((half, NSA_GROUPS, HEAD_DIM, NSA_GROUPS, 2, CMP_HIDDEN), w1.dtype)
    for g in range(NSA_GROUPS):
        big = big.at[:, g, :, g, :, :].set(w1r)
    return big.reshape(half * NSA_GROUPS * HEAD_DIM, NSA_GROUPS * 2 * CMP_HIDDEN)


def kernel(x, norm1_g, w_in, q_norm_g, k_norm_g, cmp_pos, cmp_w1, cmp_b1, cmp_w2, cmp_b2, rel_bias, rwkv_mu,
           w0, w2, a0, a2, g2, k_k, k_a, r_k, ln_x_w, ln_x_b, w_out, norm2_g, ffn_up, conv_w, conv_b, ffn_down):
    B, T, D = x.shape
    depth = w_in.shape[0]
    d_ff = ffn_down.shape[1]
    assert T % 2048 == 0 and D_NSA + 6 * D_KV + 3 * NSA_HEADS + D_RWKV_IN == w_in.shape[2]
    ncp = T // CMP_STRIDE + QT
    stab, wtab, bc = _bias_tables(rel_bias)
    ii = jnp.arange(D_RWKV)
    bd = (ii[:, None] // HEAD_DIM == ii[None, :] // HEAD_DIM).astype(BF16)

    for l in range(depth):
        wi = w_in[l]
        o = D_NSA
        q_w, kc_w, vc_w, ksl_w, vsl_w, kwn_w, vwn_w = (
            wi[:, 0:o], wi[:, o:o + D_KV], wi[:, o + D_KV:o + 2 * D_KV], wi[:, o + 2 * D_KV:o + 3 * D_KV],
            wi[:, o + 3 * D_KV:o + 4 * D_KV], wi[:, o + 4 * D_KV:o + 5 * D_KV], wi[:, o + 5 * D_KV:o + 6 * D_KV])
        gl_w = wi[:, o + 6 * D_KV:o + 6 * D_KV + 3 * NSA_HEADS]
        rw_w = wi[:, o + 6 * D_KV + 3 * NSA_HEADS:]
        wstd = jnp.concatenate([kc_w, vc_w, ksl_w, kwn_w, rw_w], axis=1).astype(BF16)
        gl_rows = gl_w.T.reshape(NSA_GROUPS, 3 * NSA_HPG, D)
        gl_rows = jnp.pad(gl_rows, ((0, 0), (0, GATE_ROWS - 3 * NSA_HPG), (0, 0))).reshape(-1, D)
        wt = jnp.concatenate([q_w.T, vsl_w.T, vwn_w.T, gl_rows], axis=0).astype(BF16)
        qg = q_norm_g[l].reshape(HEAD_DIM, 1)
        kg = jnp.stack([jnp.tile(k_norm_g[l, 1], NSA_GROUPS), jnp.tile(k_norm_g[l, 2], NSA_GROUPS)])

        qT, kc, vc, ksel, kwin, vT5, gT, rw = _proj(x, norm1_g[l].reshape(1, D), wstd, wt, qg, kg, tm=512)

        wbig = jnp.stack([_compress_weights(cmp_w1[l, 0]), _compress_weights(cmp_w1[l, 1])]).astype(BF16)
        pos8 = jnp.pad(cmp_pos[l].reshape(2, 1, CMP_BLOCK * HEAD_DIM),
                       ((0, 0), (0, V7X_SUBLANES - 1), (0, 0))).astype(BF16)
        rows16 = CMP_STRIDE * D_KV
        kcmp, vcmpT = _compress(
            kc.reshape(B, T // CMP_STRIDE, rows16), vc.reshape(B, T // CMP_STRIDE, rows16), wbig, pos8,
            cmp_w1[l].astype(BF16), cmp_b1[l].reshape(2, 1, CMP_HIDDEN), cmp_w2[l, 0].astype(BF16),
            cmp_w2[l, 1].T.astype(BF16), cmp_b2[l, 0].reshape(1, HEAD_DIM), cmp_b2[l, 1].reshape(HEAD_DIM, 1),
            k_norm_g[l, 0].reshape(1, HEAD_DIM), ncp=ncp)

        def front_pad(k):
            flag = (jnp.arange(k.shape[-1]) == HEAD_DIM).astype(BF16)
            return jnp.concatenate([jnp.broadcast_to(flag, k.shape[:2] + (ATT_PAD, k.shape[-1])), k], axis=2)

        vT5p = jnp.pad(vT5, ((0, 0), (0, 0), (ATT_PAD // QT, 0), (0, 0), (0, 0)))
        o_nsa = _attn(qT, kcmp, vcmpT, bc, front_pad(ksel), vT5p, front_pad(kwin), stab, wtab, gT)

        row = lambda a: a.reshape(1, -1)
        r, lw, k2, v, kkn, bvec, g, bonus = _rwkv_prep(
            rw, row(rwkv_mu[l]), row(w0[l]), w2[l].astype(BF16), row(a0[l]), a2[l].astype(BF16),
            g2[l].astype(BF16), row(k_k[l]), row(k_a[l]), row(r_k[l]), bd, tm=256)
        y = _wkv(r, lw, k2, v, kkn, bvec, chunks=2)

        n = B * T
        x1, h2 = _mix(x.reshape(n, D), o_nsa.reshape(n, D_NSA), y.reshape(n, D_RWKV),
                      bonus.reshape(n, D_RWKV), g.reshape(n, D_RWKV), bd, row(ln_x_w[l]), row(ln_x_b[l]),
                      w_out[l].astype(BF16), norm2_g[l].reshape(1, D), tm=512)

        up = ffn_up[l].astype(BF16)
        x = _ffn(x1, h2, up[:, :d_ff], up[:, d_ff:], conv_w[l][:, :d_ff], conv_w[l][:, d_ff:],
                 conv_b[l][:d_ff].reshape(1, -1), conv_b[l][d_ff:].reshape(1, -1),
                 ffn_down[l].astype(BF16), tm=512, ft=256, seq_len=T).reshape(B, T, D)
    return x
```

```python
import functools
import math

import jax
import jax.numpy as jnp
import numpy as np
from jax import lax
from jax.experimental import pallas as pl
from jax.experimental.pallas import tpu as pltpu

F32 = jnp.float32
BF16 = jnp.bfloat16

V7X_LANES = 128
V7X_SUBLANES = 8
V7X_VMEM_LIMIT_BYTES = 56 * 1024 * 1024

HEAD_DIM = 64
NSA_HEADS = 8
NSA_GROUPS = 2
NSA_HPG = NSA_HEADS // NSA_GROUPS
RWKV_HEADS = 8
D_NSA = NSA_HEADS * HEAD_DIM
D_RWKV = RWKV_HEADS * HEAD_DIM
D_KV = NSA_GROUPS * HEAD_DIM
CMP_BLOCK = 32
CMP_STRIDE = 16
CMP_HIDDEN = 128
SEL_BLOCK = 64
SEL_TOPN = 16
WINDOW = 512
N_BUCKETS = 32
MAX_DISTANCE = 128
LORA_W = 64
LORA_A = 64
LORA_G = 128
D_RWKV_IN = 3 * D_RWKV + LORA_W + LORA_A + LORA_G
CONV_W = 3
NORM_EPS = 1e-6
GN_EPS = 64e-5
NEG_INF = -1e30
FORCE_SCORE = 1e9

QT = 256
CMP_PAD = 8
CMP_BAND = QT // CMP_STRIDE + 8
LOG2E = math.log2(math.e)
Q_SCALE = HEAD_DIM ** -0.5 * LOG2E
V_ROWS = HEAD_DIM + 16
ATT_PAD = WINDOW
NEAR_KEYS = 2 * QT
FAR_KEYS = 2 * QT
WIN_KEYS = WINDOW + QT
WKV_CHUNK = 64
GATE_ROWS = 16
N_T_ROWS = D_NSA + 2 * D_KV + NSA_GROUPS * GATE_ROWS
N_STD_COLS = 4 * D_KV + D_RWKV_IN


def _cparams(sem):
    return pltpu.CompilerParams(dimension_semantics=sem, vmem_limit_bytes=V7X_VMEM_LIMIT_BYTES)


def _bdot(a, b):
    return jnp.dot(a.astype(BF16), b.astype(BF16), preferred_element_type=F32)


def _bdot_nt(a, b):
    return lax.dot_general(a.astype(BF16), b.astype(BF16), (((1,), (1,)), ((), ())),
                           preferred_element_type=F32)


def _split3(x):
    hi = x.astype(BF16)
    r1 = x - hi.astype(F32)
    mid = r1.astype(BF16)
    lo = (r1 - mid.astype(F32)).astype(BF16)
    return hi, mid, lo


def _dot_exact_rhs(x, m_bf16):
    hi, mid, lo = _split3(x)
    return (jnp.dot(hi, m_bf16, preferred_element_type=F32)
            + jnp.dot(mid, m_bf16, preferred_element_type=F32)
            + jnp.dot(lo, m_bf16, preferred_element_type=F32))


def _proj_kernel(x_ref, g1_ref, wstd_ref, wt_ref, qg_ref, kg_ref,
                 qT_ref, kc_ref, vc_ref, ksel_ref, kwin_ref, vT_ref, gT_ref, rw_ref):
    tm = x_ref.shape[1]
    ns = ksel_ref.shape[3] - 2 * HEAD_DIM
    x = x_ref[0]
    ms = jnp.mean(x * x, axis=-1, keepdims=True)
    h = (x * lax.rsqrt(ms + NORM_EPS) * g1_ref[...]).astype(BF16)

    std = jnp.dot(h, wstd_ref[...], preferred_element_type=F32)
    kc_ref[0] = std[:, 0:D_KV].astype(BF16)
    vc_ref[0] = std[:, D_KV:2 * D_KV].astype(BF16)
    rw_ref[0] = std[:, 4 * D_KV:]

    lane = lax.broadcasted_iota(jnp.int32, (tm, D_KV), 1)
    first = lane < HEAD_DIM

    def group_rmsnorm(k, gain):
        k2 = k * k
        s0 = jnp.sum(jnp.where(first, k2, 0.0), axis=-1, keepdims=True)
        s1 = jnp.sum(jnp.where(first, 0.0, k2), axis=-1, keepdims=True)
        ms_g = jnp.where(first, s0, s1) * (1.0 / HEAD_DIM)
        return k * lax.rsqrt(ms_g + NORM_EPS) * gain

    ksl = group_rmsnorm(std[:, 2 * D_KV:3 * D_KV], kg_ref[0:1, :]).astype(BF16)
    kwn = group_rmsnorm(std[:, 3 * D_KV:4 * D_KV], kg_ref[1:2, :]).astype(BF16)

    tok = pl.program_id(1) * tm + lax.broadcasted_iota(jnp.int32, (tm, ns), 0)
    blk = lax.broadcasted_iota(jnp.int32, (tm, ns), 1)
    onehot = jnp.where((tok // SEL_BLOCK) == blk, 1.0, 0.0).astype(BF16)
    zeros = jnp.zeros((tm, HEAD_DIM), BF16)
    for g in range(NSA_GROUPS):
        ksel_ref[0, g, :, 0:HEAD_DIM] = ksl[:, g * HEAD_DIM:(g + 1) * HEAD_DIM]
        ksel_ref[0, g, :, HEAD_DIM:2 * HEAD_DIM] = zeros
        ksel_ref[0, g, :, 2 * HEAD_DIM:] = onehot
        kwin_ref[0, g, :, 0:HEAD_DIM] = kwn[:, g * HEAD_DIM:(g + 1) * HEAD_DIM]
        kwin_ref[0, g, :, HEAD_DIM:] = zeros

    tr = lax.dot_general(wt_ref[...], h, (((1,), (1,)), ((), ())),
                         preferred_element_type=F32)
    q = tr[0:D_NSA].reshape(NSA_HEADS, HEAD_DIM, tm)
    qms = jnp.mean(q * q, axis=1, keepdims=True)
    qn = q * lax.rsqrt(qms + NORM_EPS) * qg_ref[...][None] * Q_SCALE
    qT_ref[0] = qn.reshape(D_NSA, tm).astype(BF16)
    vt = tr[D_NSA:D_NSA + 2 * D_KV].astype(BF16)
    ones_rows = jnp.where(lax.broadcasted_iota(jnp.int32, (V_ROWS - HEAD_DIM, QT), 0) == 0, 1.0, 0.0).astype(BF16)
    for a in range(2 * NSA_GROUPS):
        for j in range(tm // QT):
            vT_ref[0, a, j, 0:HEAD_DIM, :] = vt[a * HEAD_DIM:(a + 1) * HEAD_DIM, j * QT:(j + 1) * QT]
            vT_ref[0, a, j, HEAD_DIM:, :] = ones_rows
    gT_ref[0] = jax.nn.sigmoid(tr[D_NSA + 2 * D_KV:])


def _proj(x, g1, wstd, wt, qg, kg, *, tm):
    B, T, D = x.shape
    ns = T // SEL_BLOCK
    grid = (B, T // tm)
    const2 = lambda b, i: (0, 0)
    out_shape = (
        jax.ShapeDtypeStruct((B, D_NSA, T), BF16),
        jax.ShapeDtypeStruct((B, T, D_KV), BF16),
        jax.ShapeDtypeStruct((B, T, D_KV), BF16),
        jax.ShapeDtypeStruct((B, NSA_GROUPS, T, 2 * HEAD_DIM + ns), BF16),
        jax.ShapeDtypeStruct((B, NSA_GROUPS, T, 2 * HEAD_DIM), BF16),
        jax.ShapeDtypeStruct((B, 2 * NSA_GROUPS, T // QT, V_ROWS, QT), BF16),
        jax.ShapeDtypeStruct((B, NSA_GROUPS * GATE_ROWS, T), F32),
        jax.ShapeDtypeStruct((B, T, D_RWKV_IN), F32),
    )
    out_specs = (
        pl.BlockSpec((1, D_NSA, tm), lambda b, i: (b, 0, i)),
        pl.BlockSpec((1, tm, D_KV), lambda b, i: (b, i, 0)),
        pl.BlockSpec((1, tm, D_KV), lambda b, i: (b, i, 0)),
        pl.BlockSpec((1, NSA_GROUPS, tm, 2 * HEAD_DIM + ns), lambda b, i: (b, 0, i, 0)),
        pl.BlockSpec((1, NSA_GROUPS, tm, 2 * HEAD_DIM), lambda b, i: (b, 0, i, 0)),
        pl.BlockSpec((1, 2 * NSA_GROUPS, tm // QT, V_ROWS, QT), lambda b, i: (b, 0, i, 0, 0)),
        pl.BlockSpec((1, NSA_GROUPS * GATE_ROWS, tm), lambda b, i: (b, 0, i)),
        pl.BlockSpec((1, tm, D_RWKV_IN), lambda b, i: (b, i, 0)),
    )
    in_specs = [
        pl.BlockSpec((1, tm, D), lambda b, i: (b, i, 0)),
        pl.BlockSpec(g1.shape, const2),
        pl.BlockSpec(wstd.shape, const2),
        pl.BlockSpec(wt.shape, const2),
        pl.BlockSpec(qg.shape, const2),
        pl.BlockSpec(kg.shape, const2),
    ]
    return pl.pallas_call(
        _proj_kernel, grid=grid, in_specs=in_specs, out_specs=out_specs, out_shape=out_shape,
        compiler_params=_cparams(("parallel", "parallel")), name="proj",
    )(x, g1, wstd, wt, qg, kg)


def _gelu_tanh(x):
    c = math.sqrt(2.0 / math.pi)
    return x * (0.5 * (1.0 + jnp.tanh(c * (x + 0.044715 * (x * x * x)))))


def _compress_kernel(kc_ref, vc_ref, wbig_ref, pos_ref, w1_ref, b1_ref, w2k_ref, w2vT_ref,
                     b2k_ref, b2v_ref, kg_ref, kcmp_ref, vcmpT_ref, act_ref):
    m = kc_ref.shape[1]
    nc = m - 1
    ncp = act_ref.shape[0]
    row_m = lax.broadcasted_iota(jnp.int32, (m, CMP_HIDDEN), 0)
    row_p = lax.broadcasted_iota(jnp.int32, (ncp, HEAD_DIM), 0)
    col_p = lax.broadcasted_iota(jnp.int32, (HEAD_DIM, ncp), 1)
    col_f = lax.broadcasted_iota(jnp.int32, (ncp, HEAD_DIM), 1)
    row_o = lax.broadcasted_iota(jnp.int32, (V_ROWS - HEAD_DIM, ncp), 0)
    act_ref[...] = jnp.zeros(act_ref.shape, F32)
    for which, src_ref in enumerate((kc_ref, vc_ref)):
        p = jnp.dot(src_ref[0], wbig_ref[which], preferred_element_type=F32)
        posc = jnp.dot(pos_ref[which], w1_ref[which], preferred_element_type=F32)[0:1]
        for g in range(NSA_GROUPS):
            top = p[:, (2 * g) * CMP_HIDDEN:(2 * g + 1) * CMP_HIDDEN]
            bot = p[:, (2 * g + 1) * CMP_HIDDEN:(2 * g + 2) * CMP_HIDDEN]
            hid = top + pltpu.roll(bot, m - 1, 0) + (b1_ref[which] + posc)
            act = jnp.where(row_m < nc, _gelu_tanh(hid), 0.0)
            act_ref[CMP_PAD:CMP_PAD + m, :] = act
            ap = act_ref[...].astype(BF16)
            if which == 0:
                kc = jnp.dot(ap, w2k_ref[...], preferred_element_type=F32) + b2k_ref[...]
                ms = jnp.mean(kc * kc, axis=-1, keepdims=True)
                kc = kc * lax.rsqrt(ms + NORM_EPS) * kg_ref[...]
                valid = (row_p >= CMP_PAD) & (row_p < CMP_PAD + nc)
                kcmp_ref[0, g, :, 0:HEAD_DIM] = jnp.where(valid, kc, 0.0).astype(BF16)
                flag = (col_f == 0) & jnp.logical_not(valid)
                kcmp_ref[0, g, :, HEAD_DIM:] = jnp.where(flag, 1.0, 0.0).astype(BF16)
            else:
                vt = lax.dot_general(w2vT_ref[...], ap, (((1,), (1,)), ((), ())),
                                     preferred_element_type=F32) + b2v_ref[...]
                valid = (col_p >= CMP_PAD) & (col_p < CMP_PAD + nc)
                vcmpT_ref[0, g, 0:HEAD_DIM, :] = jnp.where(valid, vt, 0.0).astype(BF16)
                vcmpT_ref[0, g, HEAD_DIM:, :] = jnp.where(row_o == 0, 1.0, 0.0).astype(BF16)


def _compress(kc16, vc16, wbig, pos8, w1, b1, w2k, w2vT, b2k, b2v, kg, *, ncp):
    B, m, _ = kc16.shape
    full = lambda a: pl.BlockSpec(a.shape, lambda b: (0,) * a.ndim)
    return pl.pallas_call(
        _compress_kernel, grid=(B,),
        in_specs=[pl.BlockSpec((1, m, kc16.shape[2]), lambda b: (b, 0, 0)),
                  pl.BlockSpec((1, m, vc16.shape[2]), lambda b: (b, 0, 0)),
                  full(wbig), full(pos8), full(w1), full(b1), full(w2k), full(w2vT),
                  full(b2k), full(b2v), full(kg)],
        out_specs=(pl.BlockSpec((1, NSA_GROUPS, ncp, 2 * HEAD_DIM), lambda b: (b, 0, 0, 0)),
                   pl.BlockSpec((1, NSA_GROUPS, V_ROWS, ncp), lambda b: (b, 0, 0, 0))),
        out_shape=(jax.ShapeDtypeStruct((B, NSA_GROUPS, ncp, 2 * HEAD_DIM), BF16),
                   jax.ShapeDtypeStruct((B, NSA_GROUPS, V_ROWS, ncp), BF16)),
        scratch_shapes=[pltpu.VMEM((ncp, CMP_HIDDEN), F32)],
        compiler_params=_cparams(("parallel",)), name="compress",
    )(kc16, vc16, wbig, pos8, w1, b1, w2k, w2vT, b2k, b2v, kg)


def _flash_update(s_ref, vT, m_ref, acc_ref):
    m_prev = m_ref[...]
    m_new = jnp.maximum(m_prev, jnp.max(s_ref[...], axis=0, keepdims=True))
    p = jnp.exp2(s_ref[...] - m_new)
    acc_ref[...] = (jnp.exp2(m_prev - m_new) * acc_ref[...]
                    + jnp.dot(vT, p.astype(BF16), preferred_element_type=F32))
    m_ref[...] = m_new


def _attn_kernel(qT_ref, kcmp_ref, vcmpT_ref, bc_ref, ksel_ref, vselT_ref, kwin_ref, vwinT_ref,
                 stab_ref, wtab_ref, gT_ref, o_ref,
                 lc_ref, psum_ref, qaug_ref, qfar_ref, oc_ref, ow_ref, ms_ref, accs_ref, sa_ref, sb_ref):
    qt = pl.program_id(2)
    ncp = kcmp_ref.shape[2]
    ns = qaug_ref.shape[0] - 2 * HEAD_DIM
    nq = NSA_HPG * QT
    t_lane = qt * QT + lax.broadcasted_iota(jnp.int32, (1, QT), 1)

    for r in range(NSA_HPG):
        qaug_ref[0:HEAD_DIM, r * QT:(r + 1) * QT] = qT_ref[0, r * HEAD_DIM:(r + 1) * HEAD_DIM, :]
    flag_row = lax.broadcasted_iota(jnp.int32, (HEAD_DIM, nq), 0) == 0
    qaug_ref[HEAD_DIM:2 * HEAD_DIM, :] = jnp.where(flag_row, NEG_INF, 0.0).astype(BF16)
    qk = qaug_ref[0:2 * HEAD_DIM, :]

    rho = lax.broadcasted_iota(jnp.int32, (ncp, nq), 0)
    band0 = pl.multiple_of(qt * (QT // CMP_STRIDE), V7X_SUBLANES)
    lc_ref[...] = jnp.dot(kcmp_ref[0, 0], qk, preferred_element_type=F32)
    lc_ref[pl.ds(band0, CMP_BAND), :] += bc_ref[0]
    lc = jnp.where(rho < band0 + CMP_BAND, lc_ref[...], NEG_INF)
    e = jnp.exp2(lc - jnp.max(lc, axis=0, keepdims=True))
    oc_aug = jnp.dot(vcmpT_ref[0, 0], e.astype(BF16), preferred_element_type=F32)
    t_q = qt * QT + lax.broadcasted_iota(jnp.int32, (1, nq), 1) % QT
    inv_c = jnp.where(t_q >= CMP_BLOCK - 1, 1.0 / oc_aug[HEAD_DIM:HEAD_DIM + 1, :], 0.0)
    oc_ref[...] = oc_aug[0:HEAD_DIM] * inv_c
    p = e * inv_c
    psum = (p[:, 0:QT] + p[:, QT:2 * QT]) + (p[:, 2 * QT:3 * QT] + p[:, 3 * QT:4 * QT])
    for c in range(QT // V7X_LANES):
        psum_ref[c] = psum[:, c * V7X_LANES:(c + 1) * V7X_LANES]

    def strided_sum(c):
        acc = psum_ref[c, pl.ds(CMP_PAD - 1, ns, stride=4), :]
        for k in range(1, 5):
            acc = acc + psum_ref[c, pl.ds(CMP_PAD - 1 + k, ns, stride=4), :]
        return acc

    imp = jnp.concatenate([strided_sum(c) for c in range(QT // V7X_LANES)], axis=1)
    jrow = lax.broadcasted_iota(jnp.int32, (ns, QT), 0)
    cur = t_lane // SEL_BLOCK
    forced = (jrow == 0) | (jrow == cur) | (jrow == cur - 1)
    score = jnp.where(jrow * SEL_BLOCK <= t_lane, jnp.where(forced, FORCE_SCORE, imp), NEG_INF)
    sel = jnp.zeros((ns, QT), jnp.bool_)
    for _ in range(min(SEL_TOPN, ns)):
        mx = jnp.max(score, axis=0, keepdims=True)
        idx = jnp.min(jnp.where(score == mx, jrow, ns), axis=0, keepdims=True)
        hit = jrow == idx
        sel = sel | hit
        score = jnp.where(hit, -jnp.inf, score)
    negmask = jnp.where(sel, 0.0, NEG_INF).astype(BF16)
    negfar = jnp.where(sel & (jrow < (qt - 1) * (QT // SEL_BLOCK)), 0.0, NEG_INF).astype(BF16)
    qfar_ref[0:2 * HEAD_DIM, :] = qaug_ref[0:2 * HEAD_DIM, :]
    for r in range(NSA_HPG):
        qaug_ref[2 * HEAD_DIM:, r * QT:(r + 1) * QT] = negmask
        qfar_ref[2 * HEAD_DIM:, r * QT:(r + 1) * QT] = negfar

    w0 = pl.multiple_of(qt * QT, QT)
    sw = jnp.dot(kwin_ref[0, 0, pl.ds(w0, WIN_KEYS), :], qk, preferred_element_type=F32) + wtab_ref[0]
    pw = jnp.exp2(sw - jnp.max(sw, axis=0, keepdims=True))
    vw = jnp.concatenate([vwinT_ref[0, 0, qt + j] for j in range(WIN_KEYS // QT)], axis=1)
    ow_aug = jnp.dot(vw, pw.astype(BF16), preferred_element_type=F32)
    ow_ref[...] = ow_aug[0:HEAD_DIM] * (1.0 / ow_aug[HEAD_DIM:HEAD_DIM + 1, :])

    n0 = pl.multiple_of((ATT_PAD // QT - 1 + qt) * QT, QT)
    sn = jnp.dot(ksel_ref[0, 0, pl.ds(n0, NEAR_KEYS), :], qaug_ref[...],
                 preferred_element_type=F32) + stab_ref[0]
    mn = jnp.max(sn, axis=0, keepdims=True)
    pn = jnp.exp2(sn - mn)
    vn = jnp.concatenate([vselT_ref[0, 0, ATT_PAD // QT - 1 + qt + j] for j in range(NEAR_KEYS // QT)], axis=1)
    ms_ref[...] = mn
    accs_ref[...] = jnp.dot(vn, pn.astype(BF16), preferred_element_type=F32)

    tiles_per_far = FAR_KEYS // QT
    n_far = (qt - 1 + tiles_per_far - 1) // tiles_per_far
    last_far = (ksel_ref.shape[2] - ATT_PAD) // FAR_KEYS - 1

    def far_logits(g, dst_ref):
        r0 = pl.multiple_of(ATT_PAD + jnp.minimum(g, last_far) * FAR_KEYS, FAR_KEYS)
        dst_ref[...] = jnp.dot(ksel_ref[0, 0, pl.ds(r0, FAR_KEYS), :], qfar_ref[...],
                               preferred_element_type=F32)

    def far_values(g):
        t0 = (ATT_PAD + g * FAR_KEYS) // QT
        return jnp.concatenate([vselT_ref[0, 0, t0 + j] for j in range(tiles_per_far)], axis=1)

    far_logits(0, sa_ref)

    def far_body(j, carry):
        far_logits(2 * j + 1, sb_ref)
        _flash_update(sa_ref, far_values(2 * j), ms_ref, accs_ref)
        far_logits(2 * j + 2, sa_ref)
        _flash_update(sb_ref, far_values(jnp.minimum(2 * j + 1, last_far)), ms_ref, accs_ref)
        return carry

    lax.fori_loop(0, (n_far + 1) // 2, far_body, 0)

    o_s = accs_ref[0:HEAD_DIM, :] * (1.0 / accs_ref[HEAD_DIM:HEAD_DIM + 1, :])
    for r in range(NSA_HPG):
        cols = slice(r * QT, (r + 1) * QT)
        o = (gT_ref[0, 3 * r:3 * r + 1, :] * oc_ref[:, cols] + gT_ref[0, 3 * r + 1:3 * r + 2, :] * o_s[:, cols]
             + gT_ref[0, 3 * r + 2:3 * r + 3, :] * ow_ref[:, cols])
        o_ref[0, :, r * HEAD_DIM:(r + 1) * HEAD_DIM] = o.T


def _attn(qT, kcmp, vcmpT, bc, ksel, vT5, kwin, stab, wtab, gT):
    B, _, T = qT.shape
    ncp = kcmp.shape[2]
    ns = T // SEL_BLOCK
    nq = NSA_HPG * QT
    tp = T + ATT_PAD
    nt = tp // QT
    grid = (B, NSA_GROUPS, T // QT)

    def held(shape, index_map):
        return pl.BlockSpec(shape, index_map, pipeline_mode=pl.Buffered(1))

    in_specs = [
        pl.BlockSpec((1, NSA_HPG * HEAD_DIM, QT), lambda b, g, q: (b, g, q)),
        held((1, 1, ncp, 2 * HEAD_DIM), lambda b, g, q: (b, g, 0, 0)),
        held((1, 1, V_ROWS, ncp), lambda b, g, q: (b, g, 0, 0)),
        held((1, CMP_BAND, nq), lambda b, g, q: (g, 0, 0)),
        held((1, 1, tp, 2 * HEAD_DIM + ns), lambda b, g, q: (b, g, 0, 0)),
        held((1, 1, nt, V_ROWS, QT), lambda b, g, q: (b, g, 0, 0, 0)),
        held((1, 1, tp, 2 * HEAD_DIM), lambda b, g, q: (b, g, 0, 0)),
        held((1, 1, nt, V_ROWS, QT), lambda b, g, q: (b, NSA_GROUPS + g, 0, 0, 0)),
        held((1, NEAR_KEYS, nq), lambda b, g, q: (g, 0, 0)),
        held((1, WIN_KEYS, nq), lambda b, g, q: (g, 0, 0)),
        pl.BlockSpec((1, GATE_ROWS, QT), lambda b, g, q: (b, g, q)),
    ]
    scratch = [
        pltpu.VMEM((ncp, nq), F32),
        pltpu.VMEM((QT // V7X_LANES, ncp, V7X_LANES), F32),
        pltpu.VMEM((2 * HEAD_DIM + ns, nq), BF16),
        pltpu.VMEM((2 * HEAD_DIM + ns, nq), BF16),
        pltpu.VMEM((HEAD_DIM, nq), F32),
        pltpu.VMEM((HEAD_DIM, nq), F32),
        pltpu.VMEM((1, nq), F32),
        pltpu.VMEM((V_ROWS, nq), F32),
        pltpu.VMEM((FAR_KEYS, nq), F32),
        pltpu.VMEM((FAR_KEYS, nq), F32),
    ]
    return pl.pallas_call(
        _attn_kernel, grid=grid, in_specs=in_specs,
        out_specs=pl.BlockSpec((1, QT, NSA_HPG * HEAD_DIM), lambda b, g, q: (b, q, g)),
        out_shape=jax.ShapeDtypeStruct((B, T, D_NSA), F32),
        scratch_shapes=scratch,
        compiler_params=_cparams(("parallel", "parallel", "arbitrary")), name="attn",
    )(qT, kcmp, vcmpT, bc, ksel, vT5, kwin, vT5, stab, wtab, gT)


def _rwkv_prep_kernel(rw_ref, halo_ref, mu_ref, w0_ref, w2_ref, a0_ref, a2_ref, g2_ref, kk_ref, ka_ref,
                      rk_ref, bd_ref, r_out, lw_out, k_out, v_out, kk_out, b_out, g_out, bonus_out, ext_ref):
    tm = rw_ref.shape[1]
    first_tile = pl.program_id(1) == 0
    ext_ref[0:V7X_SUBLANES, :] = jnp.where(first_tile, 0.0, halo_ref[0])
    ext_ref[V7X_SUBLANES:, :] = rw_ref[0]
    cur = rw_ref[0]
    prev = ext_ref[pl.ds(V7X_SUBLANES - 1, tm), :]
    mixed = cur + (prev - cur) * mu_ref[...]
    c = D_RWKV
    r = mixed[:, 0:c]
    k = mixed[:, c:2 * c]
    v = mixed[:, 2 * c:3 * c]
    xw = mixed[:, 3 * c:3 * c + LORA_W]
    xa = mixed[:, 3 * c + LORA_W:3 * c + LORA_W + LORA_A]
    xg = mixed[:, 3 * c + LORA_W + LORA_A:]

    z = -(w0_ref[...] + _bdot(jnp.tanh(xw), w2_ref[...]))
    softplus = jnp.maximum(z, 0.0) + jnp.log1p(jnp.exp(-jnp.abs(z)))
    w = -softplus - 0.5
    a = jax.nn.sigmoid(a0_ref[...] + _bdot(xa, a2_ref[...]))
    g = _bdot(jax.nn.sigmoid(xg), g2_ref[...])

    kk = k * kk_ref[...]
    n2 = _dot_exact_rhs(kk * kk, bd_ref[...])
    kkn = kk / jnp.maximum(jnp.sqrt(n2), 1e-12)
    k2 = k * (1.0 + (a - 1.0) * ka_ref[...])
    bonus = _dot_exact_rhs(r * k2 * rk_ref[...], bd_ref[...]) * v

    r_out[0] = r
    lw_out[0] = -jnp.exp(w)
    k_out[0] = k2
    v_out[0] = v
    kk_out[0] = kkn
    b_out[0] = kkn * a
    g_out[0] = g
    bonus_out[0] = bonus


def _rwkv_prep(rw, mu, w0, w2, a0, a2, g2, k_k, k_a, r_k, bd, *, tm):
    B, T, C = rw.shape
    hb = tm // V7X_SUBLANES
    full = lambda a: pl.BlockSpec(a.shape, lambda b, i: (0,) * a.ndim)
    tok = pl.BlockSpec((1, tm, D_RWKV), lambda b, i: (b, i, 0))
    return pl.pallas_call(
        _rwkv_prep_kernel, grid=(B, T // tm),
        in_specs=[pl.BlockSpec((1, tm, C), lambda b, i: (b, i, 0)),
                  pl.BlockSpec((1, V7X_SUBLANES, C), lambda b, i: (b, jnp.maximum(i * hb - 1, 0), 0)),
                  full(mu), full(w0), full(w2), full(a0), full(a2), full(g2), full(k_k), full(k_a),
                  full(r_k), full(bd)],
        out_specs=(tok,) * 8,
        out_shape=(jax.ShapeDtypeStruct((B, T, D_RWKV), F32),) * 8,
        scratch_shapes=[pltpu.VMEM((tm + V7X_SUBLANES, C), F32)],
        compiler_params=_cparams(("parallel", "parallel")), name="rwkv_prep",
    )(rw, rw, mu, w0, w2, a0, a2, g2, k_k, k_a, r_k, bd)


WKV_GROUP = 4
WKV_GW = WKV_GROUP * HEAD_DIM


def _wkv_kernel(r_ref, lw_ref, k_ref, v_ref, kk_ref, b_ref, y_ref, s_ref, *, chunks):
    L = WKV_CHUNK
    assert L == HEAD_DIM
    gw = WKV_GW
    n_groups = RWKV_HEADS // WKV_GROUP

    @pl.when(pl.program_id(1) == 0)
    def _():
        s_ref[...] = jnp.zeros(s_ref.shape, F32)

    ti = lax.broadcasted_iota(jnp.int32, (L, L), 0)
    tj = lax.broadcasted_iota(jnp.int32, (L, L), 1)
    tri = jnp.where(ti >= tj, 1.0, 0.0).astype(BF16)
    row = lax.broadcasted_iota(jnp.int32, (L, gw), 0)
    col = lax.broadcasted_iota(jnp.int32, (L, gw), 1) % L
    low_strict = col < row
    low_incl = col <= row
    eye_sbs = jnp.where(col == row, 1.0, 0.0)
    brow = lax.broadcasted_iota(jnp.int32, (gw, gw), 0) // L
    bcol = lax.broadcasted_iota(jnp.int32, (gw, gw), 1) // HEAD_DIM
    same_head = brow == bcol

    def bd_rows(x):
        xb = x.astype(BF16)
        return jnp.where(same_head, jnp.concatenate([xb] * WKV_GROUP, axis=0), jnp.zeros((), BF16))

    def mm(a, b_bf16):
        return jnp.dot(a.astype(BF16), b_bf16, preferred_element_type=F32)

    def tn(a, b):
        return lax.dot_general(a.astype(BF16), b.astype(BF16), (((0,), (0,)), ((), ())),
                               preferred_element_type=F32)

    inst = [(c, g) for c in range(chunks) for g in range(n_groups)]
    pre = {}
    for c in range(chunks):
        rows = slice(c * L, (c + 1) * L)
        lw = lw_ref[0, rows, :]
        cs = _dot_exact_rhs_left(tri, lw)
        c_last = cs[L - 1:L, :]
        e_nc = jnp.exp(-cs)
        e_lc = jnp.exp(c_last - cs)
        kk = kk_ref[0, rows, :]
        b = b_ref[0, rows, :]
        k = k_ref[0, rows, :]
        pre[c] = dict(a=-kk * jnp.exp(cs - lw), r=r_ref[0, rows, :] * jnp.exp(cs), bh=b * e_nc, kh=k * e_nc,
                      be=b * e_lc, ke=k * e_lc, v=v_ref[0, rows, :], e_last=jnp.exp(c_last))

    def grp(c, g, name):
        return pre[c][name][:, g * gw:(g + 1) * gw]

    t_all = {}
    for (c, g) in inst:
        lhs = jnp.concatenate([grp(c, g, "a"), grp(c, g, "r")], axis=0).astype(BF16)
        rhs = jnp.concatenate([bd_rows(grp(c, g, "bh")), bd_rows(grp(c, g, "kh"))], axis=0)
        t_all[c, g] = lax.dot_general(lhs, rhs, (((1,), (1,)), ((), ())), preferred_element_type=F32)
    n_m = {i: jnp.where(low_strict, t_all[i][0:L, 0:gw], 0.0) for i in inst}
    tak = {i: jnp.where(low_strict, t_all[i][0:L, gw:], 0.0) for i in inst}
    trb = {i: jnp.where(low_incl, t_all[i][L:, 0:gw], 0.0) for i in inst}
    trk = {i: jnp.where(low_incl, t_all[i][L:, gw:], 0.0) for i in inst}
    z = {i: eye_sbs + n_m[i] for i in inst}
    pw = dict(n_m)
    for _ in range(int(math.log2(L)) - 1):
        pw = {i: mm(pw[i], bd_rows(pw[i])) for i in inst}
        z = {i: z[i] + mm(z[i], bd_rows(pw[i])) for i in inst}
    vbd = {(c, g): bd_rows(grp(c, g, "v")) for (c, g) in inst}
    w1 = {(c, g): mm(z[c, g], bd_rows(grp(c, g, "a"))) for (c, g) in inst}
    tv = {i: mm(tak[i], vbd[i]) for i in inst}
    c1 = {i: mm(z[i], bd_rows(tv[i])) for i in inst}
    w2 = {(c, g): grp(c, g, "r") + mm(trb[c, g], bd_rows(w1[c, g])) for (c, g) in inst}
    c2 = {i: mm(trb[i], bd_rows(c1[i])) + mm(trk[i], vbd[i]) for i in inst}
    m_lr = {(c, g): jnp.where(same_head, tn(w1[c, g], grp(c, g, "be")), 0.0).astype(BF16) for (c, g) in inst}
    d_sbs = {}
    for (c, g) in inst:
        full = jnp.where(same_head,
                         tn(jnp.concatenate([c1[c, g], grp(c, g, "v")], axis=0),
                            jnp.concatenate([grp(c, g, "be"), grp(c, g, "ke")], axis=0)), 0.0)
        d_sbs[c, g] = ((full[0:L] + full[L:2 * L]) + (full[2 * L:3 * L] + full[3 * L:4 * L]))

    for g in range(n_groups):
        s = s_ref[:, g * gw:(g + 1) * gw]
        for c in range(chunks):
            g_bd = jnp.where(same_head, jnp.concatenate([s.T.astype(BF16)] * WKV_GROUP, axis=1),
                             jnp.zeros((), BF16))
            y_ref[0, c * L:(c + 1) * L, g * gw:(g + 1) * gw] = mm(w2[c, g], g_bd) + c2[c, g]
            s = s * grp(c, g, "e_last") + mm(s, m_lr[c, g]) + d_sbs[c, g]
        s_ref[:, g * gw:(g + 1) * gw] = s


def _dot_exact_rhs_left(m_bf16, x):
    hi, mid, lo = _split3(x)
    return (jnp.dot(m_bf16, hi, preferred_element_type=F32)
            + jnp.dot(m_bf16, mid, preferred_element_type=F32)
            + jnp.dot(m_bf16, lo, preferred_element_type=F32))


def _wkv(r, lw, k, v, kk, b, *, chunks):
    B, T, C = r.shape
    rows = chunks * WKV_CHUNK
    tok = pl.BlockSpec((1, rows, C), lambda bb, c: (bb, c, 0))
    return pl.pallas_call(
        functools.partial(_wkv_kernel, chunks=chunks), grid=(B, T // rows),
        in_specs=[tok] * 6,
        out_specs=tok,
        out_shape=jax.ShapeDtypeStruct((B, T, C), F32),
        scratch_shapes=[pltpu.VMEM((HEAD_DIM, C), F32)],
        compiler_params=_cparams(("parallel", "arbitrary")), name="wkv",
    )(r, lw, k, v, kk, b)


def _mix_kernel(x_ref, on_ref, y_ref, bonus_ref, g_ref, bd_ref, lnw_ref, lnb_ref, wo_ref, g2_ref, x1_ref, h2_ref):
    y = y_ref[...]
    mu = _dot_exact_rhs(y, bd_ref[...]) * (1.0 / HEAD_DIM)
    yc = y - mu
    var = _dot_exact_rhs(yc * yc, bd_ref[...]) * (1.0 / HEAD_DIM)
    yn = yc * lax.rsqrt(var + GN_EPS) * lnw_ref[...] + lnb_ref[...]
    orw = (yn + bonus_ref[...]) * g_ref[...]
    x1 = (x_ref[...] + _bdot(on_ref[...], wo_ref[0:D_NSA, :]) + _bdot(orw, wo_ref[D_NSA:, :]))
    x1_ref[...] = x1
    ms = jnp.mean(x1 * x1, axis=-1, keepdims=True)
    h2_ref[...] = (x1 * lax.rsqrt(ms + NORM_EPS) * g2_ref[...]).astype(BF16)


def _mix(x2d, on2d, y2d, bonus2d, g2d, bd, lnw, lnb, wo, g2, *, tm):
    n, d = x2d.shape
    tok = lambda w: pl.BlockSpec((tm, w), lambda i: (i, 0))
    full = lambda a: pl.BlockSpec(a.shape, lambda i: (0,) * a.ndim)
    return pl.pallas_call(
        _mix_kernel, grid=(n // tm,),
        in_specs=[tok(d), tok(D_NSA), tok(D_RWKV), tok(D_RWKV), tok(D_RWKV), full(bd), full(lnw), full(lnb),
                  full(wo), full(g2)],
        out_specs=(tok(d), tok(d)),
        out_shape=(jax.ShapeDtypeStruct((n, d), F32), jax.ShapeDtypeStruct((n, d), BF16)),
        compiler_params=_cparams(("parallel",)), name="mix",
    )(x2d, on2d, y2d, bonus2d, g2d, bd, lnw, lnb, wo, g2)


FFN_HALO = 16


def _ffn_kernel(x1_ref, h2_ref, halo_ref, wv_ref, wg_ref, cwv_ref, cwg_ref, cbv_ref, cbg_ref, wd_ref,
                o_ref, hext_ref, extv_ref, extg_ref, act_ref, *, tiles_per_seq, ft):
    tm = h2_ref.shape[0]
    dff = wd_ref.shape[0]
    seq_start = (pl.program_id(0) % tiles_per_seq) == 0
    hext_ref[0:FFN_HALO, :] = jnp.where(seq_start, jnp.zeros((), BF16), halo_ref[...])
    hext_ref[FFN_HALO:, :] = h2_ref[...]

    for j in range(dff // ft):
        cols = slice(j * ft, (j + 1) * ft)

        def conv_branch(w_ref, cw_ref, cb_ref, ext_ref):
            ext_ref[j % 2] = jnp.dot(hext_ref[...], w_ref[:, cols], preferred_element_type=F32)
            out = cb_ref[:, cols] + ext_ref[j % 2, pl.ds(FFN_HALO, tm), :] * cw_ref[CONV_W - 1:CONV_W, cols]
            for i in range(CONV_W - 1):
                back = CONV_W - 1 - i
                out = out + ext_ref[j % 2, pl.ds(FFN_HALO - back, tm), :] * cw_ref[i:i + 1, cols]
            return out

        u_val = conv_branch(wv_ref, cwv_ref, cbv_ref, extv_ref)
        u_gate = conv_branch(wg_ref, cwg_ref, cbg_ref, extg_ref)
        act_ref[:, cols] = ((u_gate * jax.nn.sigmoid(u_gate)) * u_val).astype(BF16)
    o_ref[...] = x1_ref[...] + jnp.dot(act_ref[...], wd_ref[...], preferred_element_type=F32)


def _ffn(x1, h2, wv, wg, cwv, cwg, cbv, cbg, wd, *, tm, ft, seq_len):
    n, d = x1.shape
    dff = wv.shape[1]
    hb = tm // FFN_HALO
    kern = functools.partial(_ffn_kernel, tiles_per_seq=seq_len // tm, ft=ft)
    resident = lambda a: pl.BlockSpec(a.shape, lambda i: (0,) * a.ndim, pipeline_mode=pl.Buffered(1))
    return pl.pallas_call(
        kern, grid=(n // tm,),
        in_specs=[pl.BlockSpec((tm, d), lambda i: (i, 0)),
                  pl.BlockSpec((tm, d), lambda i: (i, 0)),
                  pl.BlockSpec((FFN_HALO, d), lambda i: (jnp.maximum(i * hb - 1, 0), 0)),
                  resident(wv), resident(wg), resident(cwv), resident(cwg), resident(cbv), resident(cbg),
                  resident(wd)],
        out_specs=pl.BlockSpec((tm, d), lambda i: (i, 0)),
        out_shape=jax.ShapeDtypeStruct((n, d), F32),
        scratch_shapes=[pltpu.VMEM((tm + FFN_HALO, d), BF16),
                        pltpu.VMEM((2, tm + FFN_HALO, ft), F32), pltpu.VMEM((2, tm + FFN_HALO, ft), F32),
                        pltpu.VMEM((tm, dff), BF16)],
        compiler_params=_cparams(("parallel",)), name="ffn",
    )(x1, h2, h2, wv, wg, cwv, cwg, cbv, cbg, wd)


def _t5_bucket(dist):
    n = jnp.maximum(dist, 0)
    max_exact = N_BUCKETS // 2
    nf = jnp.maximum(n, 1).astype(F32)
    large = max_exact + (jnp.log(nf / max_exact) / math.log(MAX_DISTANCE / max_exact)
                         * (N_BUCKETS - max_exact)).astype(jnp.int32)
    large = jnp.minimum(large, N_BUCKETS - 1)
    return jnp.where(n < max_exact, n, large)


def _bias_tables(rel_bias):
    rel = ((rel_bias - rel_bias[N_BUCKETS - 1][None, :]) * LOG2E).reshape(N_BUCKETS, NSA_GROUPS, 1, NSA_HPG, 1)

    def lookup(d, valid, fill):
        bucket = _t5_bucket(d)[None, :, None, :]
        v = jnp.zeros((NSA_GROUPS, d.shape[0], NSA_HPG, QT), F32)
        for bkt in range(N_BUCKETS - 1):
            v = jnp.where(bucket == bkt, rel[bkt], v)
        v = jnp.where(valid[None, :, None, :], v, fill)
        return v.reshape(NSA_GROUPS, d.shape[0], NSA_HPG * QT)

    t_idx = jnp.arange(QT)[None, :]
    d_n = QT + t_idx - jnp.arange(NEAR_KEYS)[:, None]
    stab = lookup(d_n, d_n >= 0, NEG_INF)
    d_w = WINDOW + t_idx - jnp.arange(WIN_KEYS)[:, None]
    wtab = lookup(d_w, (d_w >= 0) & (d_w < WINDOW), NEG_INF)
    d_c = t_idx - CMP_STRIDE * jnp.arange(CMP_BAND)[:, None] + (CMP_STRIDE * CMP_PAD - CMP_BLOCK + 1)
    bc = lookup(d_c, d_c >= 0, NEG_INF)
    return stab, wtab, bc


def _compress_weights(w1):
    half = CMP_BLOCK // 2
    w1r = w1.reshape(2, half, HEAD_DIM, CMP_HIDDEN).transpose(1, 2, 0, 3)
    big = jnp.zeros((half, NSA_GROUPS, HEAD_DIM, NSA_GROUPS, 2, CMP_HIDDEN), w1.dtype)
    for g in range(NSA_GROUPS):
        big = big.at[:, g, :, g, :, :].set(w1r)
    return big.reshape(half * NSA_GROUPS * HEAD_DIM, NSA_GROUPS * 2 * CMP_HIDDEN)


def kernel(x, norm1_g, w_in, q_norm_g, k_norm_g, cmp_pos, cmp_w1, cmp_b1, cmp_w2, cmp_b2, rel_bias, rwkv_mu,
           w0, w2, a0, a2, g2, k_k, k_a, r_k, ln_x_w, ln_x_b, w_out, norm2_g, ffn_up, conv_w, conv_b, ffn_down):
    B, T, D = x.shape
    depth = w_in.shape[0]
    d_ff = ffn_down.shape[1]
    assert T % 2048 == 0 and D_NSA + 6 * D_KV + 3 * NSA_HEADS + D_RWKV_IN == w_in.shape[2]
    ncp = T // CMP_STRIDE + V7X_LANES
    stab, wtab, bc = _bias_tables(rel_bias)
    ii = jnp.arange(D_RWKV)
    bd = (ii[:, None] // HEAD_DIM == ii[None, :] // HEAD_DIM).astype(BF16)

    for l in range(depth):
        wi = w_in[l]
        o = D_NSA
        q_w, kc_w, vc_w, ksl_w, vsl_w, kwn_w, vwn_w = (
            wi[:, 0:o], wi[:, o:o + D_KV], wi[:, o + D_KV:o + 2 * D_KV], wi[:, o + 2 * D_KV:o + 3 * D_KV],
            wi[:, o + 3 * D_KV:o + 4 * D_KV], wi[:, o + 4 * D_KV:o + 5 * D_KV], wi[:, o + 5 * D_KV:o + 6 * D_KV])
        gl_w = wi[:, o + 6 * D_KV:o + 6 * D_KV + 3 * NSA_HEADS]
        rw_w = wi[:, o + 6 * D_KV + 3 * NSA_HEADS:]
        wstd = jnp.concatenate([kc_w, vc_w, ksl_w, kwn_w, rw_w], axis=1).astype(BF16)
        gl_rows = gl_w.T.reshape(NSA_GROUPS, 3 * NSA_HPG, D)
        gl_rows = jnp.pad(gl_rows, ((0, 0), (0, GATE_ROWS - 3 * NSA_HPG), (0, 0))).reshape(-1, D)
        wt = jnp.concatenate([q_w.T, vsl_w.T, vwn_w.T, gl_rows], axis=0).astype(BF16)
        qg = q_norm_g[l].reshape(HEAD_DIM, 1)
        kg = jnp.stack([jnp.tile(k_norm_g[l, 1], NSA_GROUPS), jnp.tile(k_norm_g[l, 2], NSA_GROUPS)])

        qT, kc, vc, ksel, kwin, vT5, gT, rw = _proj(x, norm1_g[l].reshape(1, D), wstd, wt, qg, kg, tm=512)

        wbig = jnp.stack([_compress_weights(cmp_w1[l, 0]), _compress_weights(cmp_w1[l, 1])]).astype(BF16)
        pos8 = jnp.pad(cmp_pos[l].reshape(2, 1, CMP_BLOCK * HEAD_DIM),
                       ((0, 0), (0, V7X_SUBLANES - 1), (0, 0))).astype(BF16)
        rows16 = CMP_STRIDE * D_KV
        kcmp, vcmpT = _compress(
            kc.reshape(B, T // CMP_STRIDE, rows16), vc.reshape(B, T // CMP_STRIDE, rows16), wbig, pos8,
            cmp_w1[l].astype(BF16), cmp_b1[l].reshape(2, 1, CMP_HIDDEN), cmp_w2[l, 0].astype(BF16),
            cmp_w2[l, 1].T.astype(BF16), cmp_b2[l, 0].reshape(1, HEAD_DIM), cmp_b2[l, 1].reshape(HEAD_DIM, 1),
            k_norm_g[l, 0].reshape(1, HEAD_DIM), ncp=ncp)

        def front_pad(k):
            flag = (jnp.arange(k.shape[-1]) == HEAD_DIM).astype(BF16)
            return jnp.concatenate([jnp.broadcast_to(flag, k.shape[:2] + (ATT_PAD, k.shape[-1])), k], axis=2)

        vT5p = jnp.pad(vT5, ((0, 0), (0, 0), (ATT_PAD // QT, 0), (0, 0), (0, 0)))
        o_nsa = _attn(qT, kcmp, vcmpT, bc, front_pad(ksel), vT5p, front_pad(kwin), stab, wtab, gT)

        row = lambda a: a.reshape(1, -1)
        r, lw, k2, v, kkn, bvec, g, bonus = _rwkv_prep(
            rw, row(rwkv_mu[l]), row(w0[l]), w2[l].astype(BF16), row(a0[l]), a2[l].astype(BF16),
            g2[l].astype(BF16), row(k_k[l]), row(k_a[l]), row(r_k[l]), bd, tm=256)
        y = _wkv(r, lw, k2, v, kkn, bvec, chunks=8)

        n = B * T
        x1, h2 = _mix(x.reshape(n, D), o_nsa.reshape(n, D_NSA), y.reshape(n, D_RWKV),
                      bonus.reshape(n, D_RWKV), g.reshape(n, D_RWKV), bd, row(ln_x_w[l]), row(ln_x_b[l]),
                      w_out[l].astype(BF16), norm2_g[l].reshape(1, D), tm=512)

        up = ffn_up[l].astype(BF16)
        x = _ffn(x1, h2, up[:, :d_ff], up[:, d_ff:], conv_w[l][:, :d_ff], conv_w[l][:, d_ff:],
                 conv_b[l][:d_ff].reshape(1, -1), conv_b[l][d_ff:].reshape(1, -1),
                 ffn_down[l].astype(BF16), tm=512, ft=256, seq_len=T).reshape(B, T, D)
    return x
```

```python
import functools
import math

import jax
import jax.numpy as jnp
import numpy as np
from jax import lax
from jax.experimental import pallas as pl
from jax.experimental.pallas import tpu as pltpu

F32 = jnp.float32
BF16 = jnp.bfloat16

V7X_LANES = 128
V7X_SUBLANES = 8
V7X_VMEM_LIMIT_BYTES = 56 * 1024 * 1024

HEAD_DIM = 64
NSA_HEADS = 8
NSA_GROUPS = 2
NSA_HPG = NSA_HEADS // NSA_GROUPS
RWKV_HEADS = 8
D_NSA = NSA_HEADS * HEAD_DIM
D_RWKV = RWKV_HEADS * HEAD_DIM
D_KV = NSA_GROUPS * HEAD_DIM
CMP_BLOCK = 32
CMP_STRIDE = 16
CMP_HIDDEN = 128
SEL_BLOCK = 64
SEL_TOPN = 16
WINDOW = 512
N_BUCKETS = 32
MAX_DISTANCE = 128
LORA_W = 64
LORA_A = 64
LORA_G = 128
D_RWKV_IN = 3 * D_RWKV + LORA_W + LORA_A + LORA_G
CONV_W = 3
NORM_EPS = 1e-6
GN_EPS = 64e-5
NEG_INF = -1e30
FORCE_SCORE = 1e9

QT = 256
CMP_PAD = 8
CMP_BAND = QT // CMP_STRIDE + 8
LOG2E = math.log2(math.e)
Q_SCALE = HEAD_DIM ** -0.5 * LOG2E
V_ROWS = HEAD_DIM + 16
ATT_PAD = WINDOW
NEAR_KEYS = 2 * QT
FAR_KEYS = 2 * QT
WIN_KEYS = WINDOW + QT
WKV_CHUNK = 64
GATE_ROWS = 16
N_T_ROWS = D_NSA + 2 * D_KV + NSA_GROUPS * GATE_ROWS
N_STD_COLS = 4 * D_KV + D_RWKV_IN


def _cparams(sem):
    return pltpu.CompilerParams(dimension_semantics=sem, vmem_limit_bytes=V7X_VMEM_LIMIT_BYTES)


def _bdot(a, b):
    return jnp.dot(a.astype(BF16), b.astype(BF16), preferred_element_type=F32)


def _bdot_nt(a, b):
    return lax.dot_general(a.astype(BF16), b.astype(BF16), (((1,), (1,)), ((), ())),
                           preferred_element_type=F32)


def _split3(x):
    hi = x.astype(BF16)
    r1 = x - hi.astype(F32)
    mid = r1.astype(BF16)
    lo = (r1 - mid.astype(F32)).astype(BF16)
    return hi, mid, lo


def _dot_exact_rhs(x, m_bf16):
    hi, mid, lo = _split3(x)
    return (jnp.dot(hi, m_bf16, preferred_element_type=F32)
            + jnp.dot(mid, m_bf16, preferred_element_type=F32)
            + jnp.dot(lo, m_bf16, preferred_element_type=F32))


def _proj_kernel(x_ref, g1_ref, wstd_ref, wt_ref, qg_ref, kg_ref,
                 qT_ref, kc_ref, vc_ref, ksel_ref, kwin_ref, vT_ref, gT_ref, rw_ref):
    tm = x_ref.shape[1]
    ns = ksel_ref.shape[3] - 2 * HEAD_DIM
    x = x_ref[0]
    ms = jnp.mean(x * x, axis=-1, keepdims=True)
    h = (x * lax.rsqrt(ms + NORM_EPS) * g1_ref[...]).astype(BF16)

    std = jnp.dot(h, wstd_ref[...], preferred_element_type=F32)
    kc_ref[0] = std[:, 0:D_KV].astype(BF16)
    vc_ref[0] = std[:, D_KV:2 * D_KV].astype(BF16)
    rw_ref[0] = std[:, 4 * D_KV:]

    lane = lax.broadcasted_iota(jnp.int32, (tm, D_KV), 1)
    first = lane < HEAD_DIM

    def group_rmsnorm(k, gain):
        k2 = k * k
        s0 = jnp.sum(jnp.where(first, k2, 0.0), axis=-1, keepdims=True)
        s1 = jnp.sum(jnp.where(first, 0.0, k2), axis=-1, keepdims=True)
        ms_g = jnp.where(first, s0, s1) * (1.0 / HEAD_DIM)
        return k * lax.rsqrt(ms_g + NORM_EPS) * gain

    ksl = group_rmsnorm(std[:, 2 * D_KV:3 * D_KV], kg_ref[0:1, :]).astype(BF16)
    kwn = group_rmsnorm(std[:, 3 * D_KV:4 * D_KV], kg_ref[1:2, :]).astype(BF16)

    tok = pl.program_id(1) * tm + lax.broadcasted_iota(jnp.int32, (tm, ns), 0)
    blk = lax.broadcasted_iota(jnp.int32, (tm, ns), 1)
    onehot = jnp.where((tok // SEL_BLOCK) == blk, 1.0, 0.0).astype(BF16)
    zeros = jnp.zeros((tm, HEAD_DIM), BF16)
    for g in range(NSA_GROUPS):
        ksel_ref[0, g, :, 0:HEAD_DIM] = ksl[:, g * HEAD_DIM:(g + 1) * HEAD_DIM]
        ksel_ref[0, g, :, HEAD_DIM:2 * HEAD_DIM] = zeros
        ksel_ref[0, g, :, 2 * HEAD_DIM:] = onehot
        kwin_ref[0, g, :, 0:HEAD_DIM] = kwn[:, g * HEAD_DIM:(g + 1) * HEAD_DIM]
        kwin_ref[0, g, :, HEAD_DIM:] = zeros

    tr = lax.dot_general(wt_ref[...], h, (((1,), (1,)), ((), ())),
                         preferred_element_type=F32)
    q = tr[0:D_NSA].reshape(NSA_HEADS, HEAD_DIM, tm)
    qms = jnp.mean(q * q, axis=1, keepdims=True)
    qn = q * lax.rsqrt(qms + NORM_EPS) * qg_ref[...][None] * Q_SCALE
    qT_ref[0] = qn.reshape(D_NSA, tm).astype(BF16)
    vt = tr[D_NSA:D_NSA + 2 * D_KV].astype(BF16)
    ones_rows = jnp.where(lax.broadcasted_iota(jnp.int32, (V_ROWS - HEAD_DIM, QT), 0) == 0, 1.0, 0.0).astype(BF16)
    for a in range(2 * NSA_GROUPS):
        for j in range(tm // QT):
            vT_ref[0, a, j, 0:HEAD_DIM, :] = vt[a * HEAD_DIM:(a + 1) * HEAD_DIM, j * QT:(j + 1) * QT]
            vT_ref[0, a, j, HEAD_DIM:, :] = ones_rows
    gT_ref[0] = jax.nn.sigmoid(tr[D_NSA + 2 * D_KV:])


def _proj(x, g1, wstd, wt, qg, kg, *, tm):
    B, T, D = x.shape
    ns = T // SEL_BLOCK
    grid = (B, T // tm)
    const2 = lambda b, i: (0, 0)
    out_shape = (
        jax.ShapeDtypeStruct((B, D_NSA, T), BF16),
        jax.ShapeDtypeStruct((B, T, D_KV), BF16),
        jax.ShapeDtypeStruct((B, T, D_KV), BF16),
        jax.ShapeDtypeStruct((B, NSA_GROUPS, T, 2 * HEAD_DIM + ns), BF16),
        jax.ShapeDtypeStruct((B, NSA_GROUPS, T, 2 * HEAD_DIM), BF16),
        jax.ShapeDtypeStruct((B, 2 * NSA_GROUPS, T // QT, V_ROWS, QT), BF16),
        jax.ShapeDtypeStruct((B, NSA_GROUPS * GATE_ROWS, T), F32),
        jax.ShapeDtypeStruct((B, T, D_RWKV_IN), F32),
    )
    out_specs = (
        pl.BlockSpec((1, D_NSA, tm), lambda b, i: (b, 0, i)),
        pl.BlockSpec((1, tm, D_KV), lambda b, i: (b, i, 0)),
        pl.BlockSpec((1, tm, D_KV), lambda b, i: (b, i, 0)),
        pl.BlockSpec((1, NSA_GROUPS, tm, 2 * HEAD_DIM + ns), lambda b, i: (b, 0, i, 0)),
        pl.BlockSpec((1, NSA_GROUPS, tm, 2 * HEAD_DIM), lambda b, i: (b, 0, i, 0)),
        pl.BlockSpec((1, 2 * NSA_GROUPS, tm // QT, V_ROWS, QT), lambda b, i: (b, 0, i, 0, 0)),
        pl.BlockSpec((1, NSA_GROUPS * GATE_ROWS, tm), lambda b, i: (b, 0, i)),
        pl.BlockSpec((1, tm, D_RWKV_IN), lambda b, i: (b, i, 0)),
    )
    in_specs = [
        pl.BlockSpec((1, tm, D), lambda b, i: (b, i, 0)),
        pl.BlockSpec(g1.shape, const2),
        pl.BlockSpec(wstd.shape, const2),
        pl.BlockSpec(wt.shape, const2),
        pl.BlockSpec(qg.shape, const2),
        pl.BlockSpec(kg.shape, const2),
    ]
    return pl.pallas_call(
        _proj_kernel, grid=grid, in_specs=in_specs, out_specs=out_specs, out_shape=out_shape,
        compiler_params=_cparams(("parallel", "parallel")), name="proj",
    )(x, g1, wstd, wt, qg, kg)


def _gelu_tanh(x):
    c = math.sqrt(2.0 / math.pi)
    return x * (0.5 * (1.0 + jnp.tanh(c * (x + 0.044715 * (x * x * x)))))


def _compress_kernel(kc_ref, vc_ref, wbig_ref, pos_ref, w1_ref, b1_ref, w2k_ref, w2vT_ref,
                     b2k_ref, b2v_ref, kg_ref, kcmp_ref, vcmpT_ref, act_ref):
    m = kc_ref.shape[1]
    nc = m - 1
    ncp = act_ref.shape[0]
    row_m = lax.broadcasted_iota(jnp.int32, (m, CMP_HIDDEN), 0)
    row_p = lax.broadcasted_iota(jnp.int32, (ncp, HEAD_DIM), 0)
    col_p = lax.broadcasted_iota(jnp.int32, (HEAD_DIM, ncp), 1)
    col_f = lax.broadcasted_iota(jnp.int32, (ncp, HEAD_DIM), 1)
    row_o = lax.broadcasted_iota(jnp.int32, (V_ROWS - HEAD_DIM, ncp), 0)
    act_ref[...] = jnp.zeros(act_ref.shape, F32)
    for which, src_ref in enumerate((kc_ref, vc_ref)):
        p = jnp.dot(src_ref[0], wbig_ref[which], preferred_element_type=F32)
        posc = jnp.dot(pos_ref[which], w1_ref[which], preferred_element_type=F32)[0:1]
        for g in range(NSA_GROUPS):
            top = p[:, (2 * g) * CMP_HIDDEN:(2 * g + 1) * CMP_HIDDEN]
            bot = p[:, (2 * g + 1) * CMP_HIDDEN:(2 * g + 2) * CMP_HIDDEN]
            hid = top + pltpu.roll(bot, m - 1, 0) + (b1_ref[which] + posc)
            act = jnp.where(row_m < nc, _gelu_tanh(hid), 0.0)
            act_ref[CMP_PAD:CMP_PAD + m, :] = act
            ap = act_ref[...].astype(BF16)
            if which == 0:
                kc = jnp.dot(ap, w2k_ref[...], preferred_element_type=F32) + b2k_ref[...]
                ms = jnp.mean(kc * kc, axis=-1, keepdims=True)
                kc = kc * lax.rsqrt(ms + NORM_EPS) * kg_ref[...]
                valid = (row_p >= CMP_PAD) & (row_p < CMP_PAD + nc)
                kcmp_ref[0, g, :, 0:HEAD_DIM] = jnp.where(valid, kc, 0.0).astype(BF16)
                flag = (col_f == 0) & jnp.logical_not(valid)
                kcmp_ref[0, g, :, HEAD_DIM:] = jnp.where(flag, 1.0, 0.0).astype(BF16)
            else:
                vt = lax.dot_general(w2vT_ref[...], ap, (((1,), (1,)), ((), ())),
                                     preferred_element_type=F32) + b2v_ref[...]
                valid = (col_p >= CMP_PAD) & (col_p < CMP_PAD + nc)
                vcmpT_ref[0, g, 0:HEAD_DIM, :] = jnp.where(valid, vt, 0.0).astype(BF16)
                vcmpT_ref[0, g, HEAD_DIM:, :] = jnp.where(row_o == 0, 1.0, 0.0).astype(BF16)


def _compress(kc16, vc16, wbig, pos8, w1, b1, w2k, w2vT, b2k, b2v, kg, *, ncp):
    B, m, _ = kc16.shape
    full = lambda a: pl.BlockSpec(a.shape, lambda b: (0,) * a.ndim)
    return pl.pallas_call(
        _compress_kernel, grid=(B,),
        in_specs=[pl.BlockSpec((1, m, kc16.shape[2]), lambda b: (b, 0, 0)),
                  pl.BlockSpec((1, m, vc16.shape[2]), lambda b: (b, 0, 0)),
                  full(wbig), full(pos8), full(w1), full(b1), full(w2k), full(w2vT),
                  full(b2k), full(b2v), full(kg)],
        out_specs=(pl.BlockSpec((1, NSA_GROUPS, ncp, 2 * HEAD_DIM), lambda b: (b, 0, 0, 0)),
                   pl.BlockSpec((1, NSA_GROUPS, V_ROWS, ncp), lambda b: (b, 0, 0, 0))),
        out_shape=(jax.ShapeDtypeStruct((B, NSA_GROUPS, ncp, 2 * HEAD_DIM), BF16),
                   jax.ShapeDtypeStruct((B, NSA_GROUPS, V_ROWS, ncp), BF16)),
        scratch_shapes=[pltpu.VMEM((ncp, CMP_HIDDEN), F32)],
        compiler_params=_cparams(("parallel",)), name="compress",
    )(kc16, vc16, wbig, pos8, w1, b1, w2k, w2vT, b2k, b2v, kg)


def _flash_update(s_ref, vT, m_ref, acc_ref):
    m_prev = m_ref[...]
    m_new = jnp.maximum(m_prev, jnp.max(s_ref[...], axis=0, keepdims=True))
    p = jnp.exp2(s_ref[...] - m_new)
    acc_ref[...] = (jnp.exp2(m_prev - m_new) * acc_ref[...]
                    + jnp.dot(vT, p.astype(BF16), preferred_element_type=F32))
    m_ref[...] = m_new


def _attn_kernel(qT_ref, kcmp_ref, vcmpT_ref, bc_ref, ksel_ref, vselT_ref, kwin_ref, vwinT_ref,
                 stab_ref, wtab_ref, gT_ref, o_ref,
                 lc_ref, psum_ref, qaug_ref, qfar_ref, oc_ref, ow_ref, ms_ref, accs_ref, sa_ref, sb_ref):
    qt = pl.program_id(2)
    ncp = kcmp_ref.shape[2]
    ns = qaug_ref.shape[0] - 2 * HEAD_DIM
    nq = NSA_HPG * QT
    t_lane = qt * QT + lax.broadcasted_iota(jnp.int32, (1, QT), 1)

    for r in range(NSA_HPG):
        qaug_ref[0:HEAD_DIM, r * QT:(r + 1) * QT] = qT_ref[0, r * HEAD_DIM:(r + 1) * HEAD_DIM, :]
    flag_row = lax.broadcasted_iota(jnp.int32, (HEAD_DIM, nq), 0) == 0
    qaug_ref[HEAD_DIM:2 * HEAD_DIM, :] = jnp.where(flag_row, NEG_INF, 0.0).astype(BF16)
    qk = qaug_ref[0:2 * HEAD_DIM, :]

    rho = lax.broadcasted_iota(jnp.int32, (ncp, nq), 0)
    band0 = pl.multiple_of(qt * (QT // CMP_STRIDE), V7X_SUBLANES)
    lc_ref[...] = jnp.dot(kcmp_ref[0, 0], qk, preferred_element_type=F32)
    lc_ref[pl.ds(band0, CMP_BAND), :] += bc_ref[0]
    lc = jnp.where(rho < band0 + CMP_BAND, lc_ref[...], NEG_INF)
    e = jnp.exp2(lc - jnp.max(lc, axis=0, keepdims=True))
    oc_aug = jnp.dot(vcmpT_ref[0, 0], e.astype(BF16), preferred_element_type=F32)
    t_q = qt * QT + lax.broadcasted_iota(jnp.int32, (1, nq), 1) % QT
    inv_c = jnp.where(t_q >= CMP_BLOCK - 1, 1.0 / oc_aug[HEAD_DIM:HEAD_DIM + 1, :], 0.0)
    oc_ref[...] = oc_aug[0:HEAD_DIM] * inv_c
    p = e * inv_c
    psum = (p[:, 0:QT] + p[:, QT:2 * QT]) + (p[:, 2 * QT:3 * QT] + p[:, 3 * QT:4 * QT])
    for c in range(QT // V7X_LANES):
        psum_ref[c] = psum[:, c * V7X_LANES:(c + 1) * V7X_LANES]

    def strided_sum(c):
        acc = psum_ref[c, pl.ds(CMP_PAD - 1, ns, stride=4), :]
        for k in range(1, 5):
            acc = acc + psum_ref[c, pl.ds(CMP_PAD - 1 + k, ns, stride=4), :]
        return acc

    imp = jnp.concatenate([strided_sum(c) for c in range(QT // V7X_LANES)], axis=1)
    jrow = lax.broadcasted_iota(jnp.int32, (ns, QT), 0)
    cur = t_lane // SEL_BLOCK
    forced = (jrow == 0) | (jrow == cur) | (jrow == cur - 1)
    score = jnp.where(jrow * SEL_BLOCK <= t_lane, jnp.where(forced, FORCE_SCORE, imp), NEG_INF)
    sel = jnp.zeros((ns, QT), jnp.bool_)
    for _ in range(min(SEL_TOPN, ns)):
        mx = jnp.max(score, axis=0, keepdims=True)
        idx = jnp.min(jnp.where(score == mx, jrow, ns), axis=0, keepdims=True)
        hit = jrow == idx
        sel = sel | hit
        score = jnp.where(hit, -jnp.inf, score)
    negmask = jnp.where(sel, 0.0, NEG_INF).astype(BF16)
    negfar = jnp.where(sel & (jrow < (qt - 1) * (QT // SEL_BLOCK)), 0.0, NEG_INF).astype(BF16)
    qfar_ref[0:2 * HEAD_DIM, :] = qaug_ref[0:2 * HEAD_DIM, :]
    for r in range(NSA_HPG):
        qaug_ref[2 * HEAD_DIM:, r * QT:(r + 1) * QT] = negmask
        qfar_ref[2 * HEAD_DIM:, r * QT:(r + 1) * QT] = negfar

    w0 = pl.multiple_of(qt * QT, QT)
    sw = jnp.dot(kwin_ref[0, 0, pl.ds(w0, WIN_KEYS), :], qk, preferred_element_type=F32) + wtab_ref[0]
    pw = jnp.exp2(sw - jnp.max(sw, axis=0, keepdims=True))
    vw = jnp.concatenate([vwinT_ref[0, 0, qt + j] for j in range(WIN_KEYS // QT)], axis=1)
    ow_aug = jnp.dot(vw, pw.astype(BF16), preferred_element_type=F32)
    ow_ref[...] = ow_aug[0:HEAD_DIM] * (1.0 / ow_aug[HEAD_DIM:HEAD_DIM + 1, :])

    n0 = pl.multiple_of((ATT_PAD // QT - 1 + qt) * QT, QT)
    sn = jnp.dot(ksel_ref[0, 0, pl.ds(n0, NEAR_KEYS), :], qaug_ref[...],
                 preferred_element_type=F32) + stab_ref[0]
    mn = jnp.max(sn, axis=0, keepdims=True)
    pn = jnp.exp2(sn - mn)
    vn = jnp.concatenate([vselT_ref[0, 0, ATT_PAD // QT - 1 + qt + j] for j in range(NEAR_KEYS // QT)], axis=1)
    ms_ref[...] = mn
    accs_ref[...] = jnp.dot(vn, pn.astype(BF16), preferred_element_type=F32)

    tiles_per_far = FAR_KEYS // QT
    n_far = (qt - 1 + tiles_per_far - 1) // tiles_per_far
    last_far = (ksel_ref.shape[2] - ATT_PAD) // FAR_KEYS - 1

    def far_logits(g, dst_ref):
        r0 = pl.multiple_of(ATT_PAD + jnp.minimum(g, last_far) * FAR_KEYS, FAR_KEYS)
        dst_ref[...] = jnp.dot(ksel_ref[0, 0, pl.ds(r0, FAR_KEYS), :], qfar_ref[...],
                               preferred_element_type=F32)

    def far_values(g):
        t0 = (ATT_PAD + g * FAR_KEYS) // QT
        return jnp.concatenate([vselT_ref[0, 0, t0 + j] for j in range(tiles_per_far)], axis=1)

    far_logits(0, sa_ref)

    def far_body(j, carry):
        far_logits(2 * j + 1, sb_ref)
        _flash_update(sa_ref, far_values(2 * j), ms_ref, accs_ref)
        far_logits(2 * j + 2, sa_ref)
        _flash_update(sb_ref, far_values(jnp.minimum(2 * j + 1, last_far)), ms_ref, accs_ref)
        return carry

    lax.fori_loop(0, (n_far + 1) // 2, far_body, 0)

    o_s = accs_ref[0:HEAD_DIM, :] * (1.0 / accs_ref[HEAD_DIM:HEAD_DIM + 1, :])
    for r in range(NSA_HPG):
        cols = slice(r * QT, (r + 1) * QT)
        o = (gT_ref[0, 3 * r:3 * r + 1, :] * oc_ref[:, cols] + gT_ref[0, 3 * r + 1:3 * r + 2, :] * o_s[:, cols]
             + gT_ref[0, 3 * r + 2:3 * r + 3, :] * ow_ref[:, cols])
        o_ref[0, :, r * HEAD_DIM:(r + 1) * HEAD_DIM] = o.T


def _attn(qT, kcmp, vcmpT, bc, ksel, vT5, kwin, stab, wtab, gT):
    B, _, T = qT.shape
    ncp = kcmp.shape[2]
    ns = T // SEL_BLOCK
    nq = NSA_HPG * QT
    tp = T + ATT_PAD
    nt = tp // QT
    grid = (B, NSA_GROUPS, T // QT)

    def held(shape, index_map):
        return pl.BlockSpec(shape, index_map, pipeline_mode=pl.Buffered(1))

    in_specs = [
        pl.BlockSpec((1, NSA_HPG * HEAD_DIM, QT), lambda b, g, q: (b, g, q)),
        held((1, 1, ncp, 2 * HEAD_DIM), lambda b, g, q: (b, g, 0, 0)),
        held((1, 1, V_ROWS, ncp), lambda b, g, q: (b, g, 0, 0)),
        held((1, CMP_BAND, nq), lambda b, g, q: (g, 0, 0)),
        held((1, 1, tp, 2 * HEAD_DIM + ns), lambda b, g, q: (b, g, 0, 0)),
        held((1, 1, nt, V_ROWS, QT), lambda b, g, q: (b, g, 0, 0, 0)),
        held((1, 1, tp, 2 * HEAD_DIM), lambda b, g, q: (b, g, 0, 0)),
        held((1, 1, nt, V_ROWS, QT), lambda b, g, q: (b, NSA_GROUPS + g, 0, 0, 0)),
        held((1, NEAR_KEYS, nq), lambda b, g, q: (g, 0, 0)),
        held((1, WIN_KEYS, nq), lambda b, g, q: (g, 0, 0)),
        pl.BlockSpec((1, GATE_ROWS, QT), lambda b, g, q: (b, g, q)),
    ]
    scratch = [
        pltpu.VMEM((ncp, nq), F32),
        pltpu.VMEM((QT // V7X_LANES, ncp, V7X_LANES), F32),
        pltpu.VMEM((2 * HEAD_DIM + ns, nq), BF16),
        pltpu.VMEM((2 * HEAD_DIM + ns, nq), BF16),
        pltpu.VMEM((HEAD_DIM, nq), F32),
        pltpu.VMEM((HEAD_DIM, nq), F32),
        pltpu.VMEM((1, nq), F32),
        pltpu.VMEM((V_ROWS, nq), F32),
        pltpu.VMEM((FAR_KEYS, nq), F32),
        pltpu.VMEM((FAR_KEYS, nq), F32),
    ]
    return pl.pallas_call(
        _attn_kernel, grid=grid, in_specs=in_specs,
        out_specs=pl.BlockSpec((1, QT, NSA_HPG * HEAD_DIM), lambda b, g, q: (b, q, g)),
        out_shape=jax.ShapeDtypeStruct((B, T, D_NSA), F32),
        scratch_shapes=scratch,
        compiler_params=_cparams(("parallel", "parallel", "arbitrary")), name="attn",
    )(qT, kcmp, vcmpT, bc, ksel, vT5, kwin, vT5, stab, wtab, gT)


def _rwkv_tokens(rw_ref, halo_ref, mu_ref, w0_ref, w2_ref, a0_ref, a2_ref, g2_ref, kk_ref, ka_ref,
                 rk_ref, bd_ref, ext_ref):
    tm = rw_ref.shape[1]
    first_tile = pl.program_id(1) == 0
    ext_ref[0:V7X_SUBLANES, :] = jnp.where(first_tile, 0.0, halo_ref[0])
    ext_ref[V7X_SUBLANES:, :] = rw_ref[0]
    cur = rw_ref[0]
    prev = ext_ref[pl.ds(V7X_SUBLANES - 1, tm), :]
    mixed = cur + (prev - cur) * mu_ref[...]
    c = D_RWKV
    r = mixed[:, 0:c]
    k = mixed[:, c:2 * c]
    v = mixed[:, 2 * c:3 * c]
    xw = mixed[:, 3 * c:3 * c + LORA_W]
    xa = mixed[:, 3 * c + LORA_W:3 * c + LORA_W + LORA_A]
    xg = mixed[:, 3 * c + LORA_W + LORA_A:]

    z = -(w0_ref[...] + _bdot(jnp.tanh(xw), w2_ref[...]))
    softplus = jnp.maximum(z, 0.0) + jnp.log1p(jnp.exp(-jnp.abs(z)))
    w = -softplus - 0.5
    a = jax.nn.sigmoid(a0_ref[...] + _bdot(xa, a2_ref[...]))
    g = _bdot(jax.nn.sigmoid(xg), g2_ref[...])

    kk = k * kk_ref[...]
    n2 = _dot_exact_rhs(kk * kk, bd_ref[...])
    kkn = kk / jnp.maximum(jnp.sqrt(n2), 1e-12)
    k2 = k * (1.0 + (a - 1.0) * ka_ref[...])
    bonus = _dot_exact_rhs(r * k2 * rk_ref[...], bd_ref[...]) * v

    lw = -jnp.exp(w)
    return dict(r=r, lw=lw, k=k2, v=v, kk=kkn, b=kkn * a, g=g, bonus=bonus)


WKV_GROUP = 4
WKV_GW = WKV_GROUP * HEAD_DIM


def _wkv_kernel(rw_ref, halo_ref, mu_ref, w0_ref, w2_ref, a0_ref, a2_ref, g2_ref, kk_ref, ka_ref, rk_ref, bd_ref,
                y_ref, g_ref, bonus_ref, s_ref, ext_ref, *, chunks):
    L = WKV_CHUNK
    tok = _rwkv_tokens(rw_ref, halo_ref, mu_ref, w0_ref, w2_ref, a0_ref, a2_ref, g2_ref, kk_ref, ka_ref,
                       rk_ref, bd_ref, ext_ref)
    g_ref[0] = tok["g"]
    bonus_ref[0] = tok["bonus"]
    assert L == HEAD_DIM
    gw = WKV_GW
    n_groups = RWKV_HEADS // WKV_GROUP

    @pl.when(pl.program_id(1) == 0)
    def _():
        s_ref[...] = jnp.zeros(s_ref.shape, F32)

    ti = lax.broadcasted_iota(jnp.int32, (L, L), 0)
    tj = lax.broadcasted_iota(jnp.int32, (L, L), 1)
    tri = jnp.where(ti >= tj, 1.0, 0.0).astype(BF16)
    row = lax.broadcasted_iota(jnp.int32, (L, gw), 0)
    col = lax.broadcasted_iota(jnp.int32, (L, gw), 1) % L
    low_strict = col < row
    low_incl = col <= row
    eye_sbs = jnp.where(col == row, 1.0, 0.0)
    brow = lax.broadcasted_iota(jnp.int32, (gw, gw), 0) // L
    bcol = lax.broadcasted_iota(jnp.int32, (gw, gw), 1) // HEAD_DIM
    same_head = brow == bcol

    def bd_rows(x):
        xb = x.astype(BF16)
        return jnp.where(same_head, jnp.concatenate([xb] * WKV_GROUP, axis=0), jnp.zeros((), BF16))

    def mm(a, b_bf16):
        return jnp.dot(a.astype(BF16), b_bf16, preferred_element_type=F32)

    def tn(a, b):
        return lax.dot_general(a.astype(BF16), b.astype(BF16), (((0,), (0,)), ((), ())),
                               preferred_element_type=F32)

    inst = [(c, g) for c in range(chunks) for g in range(n_groups)]
    pre = {}
    for c in range(chunks):
        rows = slice(c * L, (c + 1) * L)
        lw = tok["lw"][rows]
        cs = _dot_exact_rhs_left(tri, lw)
        c_last = cs[L - 1:L, :]
        e_nc = jnp.exp(-cs)
        e_lc = jnp.exp(c_last - cs)
        kk = tok["kk"][rows]
        b = tok["b"][rows]
        k = tok["k"][rows]
        pre[c] = dict(a=-kk * jnp.exp(cs - lw), r=tok["r"][rows] * jnp.exp(cs), bh=b * e_nc, kh=k * e_nc,
                      be=b * e_lc, ke=k * e_lc, v=tok["v"][rows], e_last=jnp.exp(c_last))

    def grp(c, g, name):
        return pre[c][name][:, g * gw:(g + 1) * gw]

    t_all = {}
    for (c, g) in inst:
        lhs = jnp.concatenate([grp(c, g, "a"), grp(c, g, "r")], axis=0).astype(BF16)
        rhs = jnp.concatenate([bd_rows(grp(c, g, "bh")), bd_rows(grp(c, g, "kh"))], axis=0)
        t_all[c, g] = lax.dot_general(lhs, rhs, (((1,), (1,)), ((), ())), preferred_element_type=F32)
    n_m = {i: jnp.where(low_strict, t_all[i][0:L, 0:gw], 0.0) for i in inst}
    tak = {i: jnp.where(low_strict, t_all[i][0:L, gw:], 0.0) for i in inst}
    trb = {i: jnp.where(low_incl, t_all[i][L:, 0:gw], 0.0) for i in inst}
    trk = {i: jnp.where(low_incl, t_all[i][L:, gw:], 0.0) for i in inst}
    z = {i: eye_sbs + n_m[i] for i in inst}
    pw = dict(n_m)
    for _ in range(int(math.log2(L)) - 1):
        pw = {i: mm(pw[i], bd_rows(pw[i])) for i in inst}
        z = {i: z[i] + mm(z[i], bd_rows(pw[i])) for i in inst}
    vbd = {(c, g): bd_rows(grp(c, g, "v")) for (c, g) in inst}
    w1 = {(c, g): mm(z[c, g], bd_rows(grp(c, g, "a"))) for (c, g) in inst}
    tv = {i: mm(tak[i], vbd[i]) for i in inst}
    c1 = {i: mm(z[i], bd_rows(tv[i])) for i in inst}
    w2 = {(c, g): grp(c, g, "r") + mm(trb[c, g], bd_rows(w1[c, g])) for (c, g) in inst}
    c2 = {i: mm(trb[i], bd_rows(c1[i])) + mm(trk[i], vbd[i]) for i in inst}
    m_lr = {(c, g): jnp.where(same_head, tn(w1[c, g], grp(c, g, "be")), 0.0).astype(BF16) for (c, g) in inst}
    d_sbs = {}
    for (c, g) in inst:
        full = jnp.where(same_head,
                         tn(jnp.concatenate([c1[c, g], grp(c, g, "v")], axis=0),
                            jnp.concatenate([grp(c, g, "be"), grp(c, g, "ke")], axis=0)), 0.0)
        d_sbs[c, g] = ((full[0:L] + full[L:2 * L]) + (full[2 * L:3 * L] + full[3 * L:4 * L]))

    for g in range(n_groups):
        s = s_ref[:, g * gw:(g + 1) * gw]
        for c in range(chunks):
            g_bd = jnp.where(same_head, jnp.concatenate([s.T.astype(BF16)] * WKV_GROUP, axis=1),
                             jnp.zeros((), BF16))
            y_ref[0, c * L:(c + 1) * L, g * gw:(g + 1) * gw] = mm(w2[c, g], g_bd) + c2[c, g]
            s = s * grp(c, g, "e_last") + mm(s, m_lr[c, g]) + d_sbs[c, g]
        s_ref[:, g * gw:(g + 1) * gw] = s


def _dot_exact_rhs_left(m_bf16, x):
    hi, mid, lo = _split3(x)
    return (jnp.dot(m_bf16, hi, preferred_element_type=F32)
            + jnp.dot(m_bf16, mid, preferred_element_type=F32)
            + jnp.dot(m_bf16, lo, preferred_element_type=F32))


def _wkv(rw, mu, w0, w2, a0, a2, g2, k_k, k_a, r_k, bd, *, chunks):
    B, T, C = rw.shape
    rows = chunks * WKV_CHUNK
    hb = rows // V7X_SUBLANES
    full = lambda a: pl.BlockSpec(a.shape, lambda bb, c: (0,) * a.ndim)
    tok = pl.BlockSpec((1, rows, D_RWKV), lambda bb, c: (bb, c, 0))
    return pl.pallas_call(
        functools.partial(_wkv_kernel, chunks=chunks), grid=(B, T // rows),
        in_specs=[pl.BlockSpec((1, rows, C), lambda bb, c: (bb, c, 0)),
                  pl.BlockSpec((1, V7X_SUBLANES, C), lambda bb, c: (bb, jnp.maximum(c * hb - 1, 0), 0)),
                  full(mu), full(w0), full(w2), full(a0), full(a2), full(g2), full(k_k), full(k_a),
                  full(r_k), full(bd)],
        out_specs=(tok,) * 3,
        out_shape=(jax.ShapeDtypeStruct((B, T, D_RWKV), F32),) * 3,
        scratch_shapes=[pltpu.VMEM((HEAD_DIM, D_RWKV), F32), pltpu.VMEM((rows + V7X_SUBLANES, C), F32)],
        compiler_params=_cparams(("parallel", "arbitrary")), name="wkv",
    )(rw, rw, mu, w0, w2, a0, a2, g2, k_k, k_a, r_k, bd)


def _mix_kernel(x_ref, on_ref, y_ref, bonus_ref, g_ref, bd_ref, lnw_ref, lnb_ref, wo_ref, g2_ref, x1_ref, h2_ref):
    y = y_ref[...]
    mu = _dot_exact_rhs(y, bd_ref[...]) * (1.0 / HEAD_DIM)
    yc = y - mu
    var = _dot_exact_rhs(yc * yc, bd_ref[...]) * (1.0 / HEAD_DIM)
    yn = yc * lax.rsqrt(var + GN_EPS) * lnw_ref[...] + lnb_ref[...]
    orw = (yn + bonus_ref[...]) * g_ref[...]
    x1 = (x_ref[...] + _bdot(on_ref[...], wo_ref[0:D_NSA, :]) + _bdot(orw, wo_ref[D_NSA:, :]))
    x1_ref[...] = x1
    ms = jnp.mean(x1 * x1, axis=-1, keepdims=True)
    h2_ref[...] = (x1 * lax.rsqrt(ms + NORM_EPS) * g2_ref[...]).astype(BF16)


def _mix(x2d, on2d, y2d, bonus2d, g2d, bd, lnw, lnb, wo, g2, *, tm):
    n, d = x2d.shape
    tok = lambda w: pl.BlockSpec((tm, w), lambda i: (i, 0))
    full = lambda a: pl.BlockSpec(a.shape, lambda i: (0,) * a.ndim)
    return pl.pallas_call(
        _mix_kernel, grid=(n // tm,),
        in_specs=[tok(d), tok(D_NSA), tok(D_RWKV), tok(D_RWKV), tok(D_RWKV), full(bd), full(lnw), full(lnb),
                  full(wo), full(g2)],
        out_specs=(tok(d), tok(d)),
        out_shape=(jax.ShapeDtypeStruct((n, d), F32), jax.ShapeDtypeStruct((n, d), BF16)),
        compiler_params=_cparams(("parallel",)), name="mix",
    )(x2d, on2d, y2d, bonus2d, g2d, bd, lnw, lnb, wo, g2)


FFN_HALO = 16


def _ffn_kernel(x1_ref, h2_ref, halo_ref, wv_ref, wg_ref, cwv_ref, cwg_ref, cbv_ref, cbg_ref, wd_ref,
                o_ref, hext_ref, extv_ref, extg_ref, act_ref, *, tiles_per_seq, ft):
    tm = h2_ref.shape[0]
    dff = wd_ref.shape[0]
    seq_start = (pl.program_id(0) % tiles_per_seq) == 0
    hext_ref[0:FFN_HALO, :] = jnp.where(seq_start, jnp.zeros((), BF16), halo_ref[...])
    hext_ref[FFN_HALO:, :] = h2_ref[...]

    for j in range(dff // ft):
        cols = slice(j * ft, (j + 1) * ft)

        def conv_branch(w_ref, cw_ref, cb_ref, ext_ref):
            ext_ref[j % 2] = jnp.dot(hext_ref[...], w_ref[:, cols], preferred_element_type=F32)
            out = cb_ref[:, cols] + ext_ref[j % 2, pl.ds(FFN_HALO, tm), :] * cw_ref[CONV_W - 1:CONV_W, cols]
            for i in range(CONV_W - 1):
                back = CONV_W - 1 - i
                out = out + ext_ref[j % 2, pl.ds(FFN_HALO - back, tm), :] * cw_ref[i:i + 1, cols]
            return out

        u_val = conv_branch(wv_ref, cwv_ref, cbv_ref, extv_ref)
        u_gate = conv_branch(wg_ref, cwg_ref, cbg_ref, extg_ref)
        act_ref[:, cols] = ((u_gate * jax.nn.sigmoid(u_gate)) * u_val).astype(BF16)
    o_ref[...] = x1_ref[...] + jnp.dot(act_ref[...], wd_ref[...], preferred_element_type=F32)


def _ffn(x1, h2, wv, wg, cwv, cwg, cbv, cbg, wd, *, tm, ft, seq_len):
    n, d = x1.shape
    dff = wv.shape[1]
    hb = tm // FFN_HALO
    kern = functools.partial(_ffn_kernel, tiles_per_seq=seq_len // tm, ft=ft)
    resident = lambda a: pl.BlockSpec(a.shape, lambda i: (0,) * a.ndim, pipeline_mode=pl.Buffered(1))
    return pl.pallas_call(
        kern, grid=(n // tm,),
        in_specs=[pl.BlockSpec((tm, d), lambda i: (i, 0)),
                  pl.BlockSpec((tm, d), lambda i: (i, 0)),
                  pl.BlockSpec((FFN_HALO, d), lambda i: (jnp.maximum(i * hb - 1, 0), 0)),
                  resident(wv), resident(wg), resident(cwv), resident(cwg), resident(cbv), resident(cbg),
                  resident(wd)],
        out_specs=pl.BlockSpec((tm, d), lambda i: (i, 0)),
        out_shape=jax.ShapeDtypeStruct((n, d), F32),
        scratch_shapes=[pltpu.VMEM((tm + FFN_HALO, d), BF16),
                        pltpu.VMEM((2, tm + FFN_HALO, ft), F32), pltpu.VMEM((2, tm + FFN_HALO, ft), F32),
                        pltpu.VMEM((tm, dff), BF16)],
        compiler_params=_cparams(("parallel",)), name="ffn",
    )(x1, h2, h2, wv, wg, cwv, cwg, cbv, cbg, wd)


def _t5_bucket(dist):
    n = np.maximum(dist, 0)
    max_exact = N_BUCKETS // 2
    nf = np.maximum(n, 1).astype(np.float32)
    large = max_exact + (np.log(nf / np.float32(max_exact)) / np.float32(math.log(MAX_DISTANCE / max_exact))
                         * np.float32(N_BUCKETS - max_exact)).astype(np.int32)
    large = np.minimum(large, N_BUCKETS - 1)
    return np.where(n < max_exact, n, large)


def _toeplitz_kernel(v_ref, o_ref, *, step):
    rows, width = o_ref.shape[1], o_ref.shape[2]
    x = jnp.broadcast_to(v_ref[0], (rows, v_ref.shape[2]))
    o_ref[0] = pltpu.roll(x, 0, 1, stride=step, stride_axis=0)[:, :width]


def _toeplitz(v, rows, step):
    period = v.shape[2]
    return pl.pallas_call(
        functools.partial(_toeplitz_kernel, step=step), grid=(NSA_GROUPS, NSA_HPG),
        in_specs=[pl.BlockSpec((1, 1, period), lambda g, r: (g * NSA_HPG + r, 0, 0))],
        out_specs=pl.BlockSpec((1, rows, QT), lambda g, r: (g, 0, r)),
        out_shape=jax.ShapeDtypeStruct((NSA_GROUPS, rows, NSA_HPG * QT), F32),
        compiler_params=_cparams(("parallel", "parallel")), name="toeplitz",
    )(v)


def _bias_tables(rel_bias):
    rel = (rel_bias - rel_bias[N_BUCKETS - 1][None, :]) * LOG2E

    def table(rows, step, d0, d_hi):
        period = step * rows + QT
        i = np.arange(period)
        d = d0 + np.where(i < QT, i, i - period)
        onehot = (_t5_bucket(d)[:, None] == np.arange(N_BUCKETS)[None, :]).astype(np.float32)
        v = jnp.dot(jnp.asarray(onehot), rel, precision=lax.Precision.HIGHEST)
        v = jnp.where(jnp.asarray((d >= 0) & (d < d_hi))[:, None], v, NEG_INF).T
        return _toeplitz(v.reshape(NSA_HEADS, 1, period), rows, step)

    no_limit = 1 << 30
    stab = table(NEAR_KEYS, 1, QT, no_limit)
    wtab = table(WIN_KEYS, 1, WINDOW, WINDOW)
    bc = table(CMP_BAND, CMP_STRIDE, CMP_STRIDE * CMP_PAD - CMP_BLOCK + 1, no_limit)
    return stab, wtab, bc


def _compress_weights(w1):
    half = CMP_BLOCK // 2
    w1r = w1.reshape(2, half, HEAD_DIM, CMP_HIDDEN).transpose(1, 2, 0, 3)
    big = jnp.zeros((half, NSA_GROUPS, HEAD_DIM, NSA_GROUPS, 2, CMP_HIDDEN), w1.dtype)
    for g in range(NSA_GROUPS):
        big = big.at[:, g, :, g, :, :].set(w1r)
    return big.reshape(half * NSA_GROUPS * HEAD_DIM, NSA_GROUPS * 2 * CMP_HIDDEN)


def kernel(x, norm1_g, w_in, q_norm_g, k_norm_g, cmp_pos, cmp_w1, cmp_b1, cmp_w2, cmp_b2, rel_bias, rwkv_mu,
           w0, w2, a0, a2, g2, k_k, k_a, r_k, ln_x_w, ln_x_b, w_out, norm2_g, ffn_up, conv_w, conv_b, ffn_down):
    B, T, D = x.shape
    depth = w_in.shape[0]
    d_ff = ffn_down.shape[1]
    assert T % 2048 == 0 and D_NSA + 6 * D_KV + 3 * NSA_HEADS + D_RWKV_IN == w_in.shape[2]
    ncp = T // CMP_STRIDE + V7X_LANES
    stab, wtab, bc = _bias_tables(rel_bias)
    ii = jnp.arange(D_RWKV)
    bd = (ii[:, None] // HEAD_DIM == ii[None, :] // HEAD_DIM).astype(BF16)

    for l in range(depth):
        wi = w_in[l]
        o = D_NSA
        q_w, kc_w, vc_w, ksl_w, vsl_w, kwn_w, vwn_w = (
            wi[:, 0:o], wi[:, o:o + D_KV], wi[:, o + D_KV:o + 2 * D_KV], wi[:, o + 2 * D_KV:o + 3 * D_KV],
            wi[:, o + 3 * D_KV:o + 4 * D_KV], wi[:, o + 4 * D_KV:o + 5 * D_KV], wi[:, o + 5 * D_KV:o + 6 * D_KV])
        gl_w = wi[:, o + 6 * D_KV:o + 6 * D_KV + 3 * NSA_HEADS]
        rw_w = wi[:, o + 6 * D_KV + 3 * NSA_HEADS:]
        wstd = jnp.concatenate([kc_w, vc_w, ksl_w, kwn_w, rw_w], axis=1).astype(BF16)
        gl_rows = gl_w.T.reshape(NSA_GROUPS, 3 * NSA_HPG, D)
        gl_rows = jnp.pad(gl_rows, ((0, 0), (0, GATE_ROWS - 3 * NSA_HPG), (0, 0))).reshape(-1, D)
        wt = jnp.concatenate([q_w.T, vsl_w.T, vwn_w.T, gl_rows], axis=0).astype(BF16)
        qg = q_norm_g[l].reshape(HEAD_DIM, 1)
        kg = jnp.stack([jnp.tile(k_norm_g[l, 1], NSA_GROUPS), jnp.tile(k_norm_g[l, 2], NSA_GROUPS)])

        qT, kc, vc, ksel, kwin, vT5, gT, rw = _proj(x, norm1_g[l].reshape(1, D), wstd, wt, qg, kg, tm=512)

        wbig = jnp.stack([_compress_weights(cmp_w1[l, 0]), _compress_weights(cmp_w1[l, 1])]).astype(BF16)
        pos8 = jnp.pad(cmp_pos[l].reshape(2, 1, CMP_BLOCK * HEAD_DIM),
                       ((0, 0), (0, V7X_SUBLANES - 1), (0, 0))).astype(BF16)
        rows16 = CMP_STRIDE * D_KV
        kcmp, vcmpT = _compress(
            kc.reshape(B, T // CMP_STRIDE, rows16), vc.reshape(B, T // CMP_STRIDE, rows16), wbig, pos8,
            cmp_w1[l].astype(BF16), cmp_b1[l].reshape(2, 1, CMP_HIDDEN), cmp_w2[l, 0].astype(BF16),
            cmp_w2[l, 1].T.astype(BF16), cmp_b2[l, 0].reshape(1, HEAD_DIM), cmp_b2[l, 1].reshape(HEAD_DIM, 1),
            k_norm_g[l, 0].reshape(1, HEAD_DIM), ncp=ncp)

        def front_pad(k):
            flag = (jnp.arange(k.shape[-1]) == HEAD_DIM).astype(BF16)
            return jnp.concatenate([jnp.broadcast_to(flag, k.shape[:2] + (ATT_PAD, k.shape[-1])), k], axis=2)

        vT5p = jnp.pad(vT5, ((0, 0), (0, 0), (ATT_PAD // QT, 0), (0, 0), (0, 0)))
        o_nsa = _attn(qT, kcmp, vcmpT, bc, front_pad(ksel), vT5p, front_pad(kwin), stab, wtab, gT)

        row = lambda a: a.reshape(1, -1)
        y, g, bonus = _wkv(
            rw, row(rwkv_mu[l]), row(w0[l]), w2[l].astype(BF16), row(a0[l]), a2[l].astype(BF16),
            g2[l].astype(BF16), row(k_k[l]), row(k_a[l]), row(r_k[l]), bd, chunks=8)

        n = B * T
        x1, h2 = _mix(x.reshape(n, D), o_nsa.reshape(n, D_NSA), y.reshape(n, D_RWKV),
                      bonus.reshape(n, D_RWKV), g.reshape(n, D_RWKV), bd, row(ln_x_w[l]), row(ln_x_b[l]),
                      w_out[l].astype(BF16), norm2_g[l].reshape(1, D), tm=512)

        up = ffn_up[l].astype(BF16)
        x = _ffn(x1, h2, up[:, :d_ff], up[:, d_ff:], conv_w[l][:, :d_ff], conv_w[l][:, d_ff:],
                 conv_b[l][:d_ff].reshape(1, -1), conv_b[l][d_ff:].reshape(1, -1),
                 ffn_down[l].astype(BF16), tm=512, ft=256, seq_len=T).reshape(B, T, D)
    return x
```

```python
import functools
import math

import jax
import jax.numpy as jnp
import numpy as np
from jax import lax
from jax.experimental import pallas as pl
from jax.experimental.pallas import tpu as pltpu

F32 = jnp.float32
BF16 = jnp.bfloat16

V7X_LANES = 128
V7X_SUBLANES = 8
V7X_VMEM_LIMIT_BYTES = 56 * 1024 * 1024

HEAD_DIM = 64
NSA_HEADS = 8
NSA_GROUPS = 2
NSA_HPG = NSA_HEADS // NSA_GROUPS
RWKV_HEADS = 8
D_NSA = NSA_HEADS * HEAD_DIM
D_RWKV = RWKV_HEADS * HEAD_DIM
D_KV = NSA_GROUPS * HEAD_DIM
CMP_BLOCK = 32
CMP_STRIDE = 16
CMP_HIDDEN = 128
SEL_BLOCK = 64
SEL_TOPN = 16
WINDOW = 512
N_BUCKETS = 32
MAX_DISTANCE = 128
LORA_W = 64
LORA_A = 64
LORA_G = 128
D_RWKV_IN = 3 * D_RWKV + LORA_W + LORA_A + LORA_G
CONV_W = 3
NORM_EPS = 1e-6
GN_EPS = 64e-5
NEG_INF = -1e30
FORCE_SCORE = 1e9

QT = 256
CMP_PAD = 8
CMP_BAND = QT // CMP_STRIDE + 8
LOG2E = math.log2(math.e)
Q_SCALE = HEAD_DIM ** -0.5 * LOG2E
V_ROWS = HEAD_DIM + 16
ATT_PAD = WINDOW
NEAR_KEYS = 2 * QT
FAR_KEYS = 2 * QT
WIN_KEYS = WINDOW + QT
WKV_CHUNK = 64
GATE_ROWS = 16
N_T_ROWS = D_NSA + 2 * D_KV + NSA_GROUPS * GATE_ROWS
N_STD_COLS = 4 * D_KV + D_RWKV_IN


def _cparams(sem):
    return pltpu.CompilerParams(dimension_semantics=sem, vmem_limit_bytes=V7X_VMEM_LIMIT_BYTES)


def _bdot(a, b):
    return jnp.dot(a.astype(BF16), b.astype(BF16), preferred_element_type=F32)


def _bdot_nt(a, b):
    return lax.dot_general(a.astype(BF16), b.astype(BF16), (((1,), (1,)), ((), ())),
                           preferred_element_type=F32)


def _split3(x):
    hi = x.astype(BF16)
    r1 = x - hi.astype(F32)
    mid = r1.astype(BF16)
    lo = (r1 - mid.astype(F32)).astype(BF16)
    return hi, mid, lo


def _dot_exact_rhs(x, m_bf16):
    hi = x.astype(BF16)
    lo = (x - hi.astype(F32)).astype(BF16)
    return jnp.dot(hi, m_bf16, preferred_element_type=F32) + jnp.dot(lo, m_bf16, preferred_element_type=F32)


def _proj_kernel(x_ref, g1_ref, wstd_ref, wt_ref, qg_ref, kg_ref,
                 qT_ref, kc_ref, vc_ref, ksel_ref, kwin_ref, vT_ref, gT_ref, rw_ref):
    tm = x_ref.shape[1]
    ns = ksel_ref.shape[3] - 2 * HEAD_DIM
    x = x_ref[0]
    ms = jnp.mean(x * x, axis=-1, keepdims=True)
    h = (x * lax.rsqrt(ms + NORM_EPS) * g1_ref[...]).astype(BF16)

    std = jnp.dot(h, wstd_ref[...], preferred_element_type=F32)
    kc_ref[0] = std[:, 0:D_KV].astype(BF16)
    vc_ref[0] = std[:, D_KV:2 * D_KV].astype(BF16)
    rw_ref[0] = std[:, 4 * D_KV:]

    lane = lax.broadcasted_iota(jnp.int32, (tm, D_KV), 1)
    first = lane < HEAD_DIM

    def group_rmsnorm(k, gain):
        k2 = k * k
        s0 = jnp.sum(jnp.where(first, k2, 0.0), axis=-1, keepdims=True)
        s1 = jnp.sum(jnp.where(first, 0.0, k2), axis=-1, keepdims=True)
        ms_g = jnp.where(first, s0, s1) * (1.0 / HEAD_DIM)
        return k * lax.rsqrt(ms_g + NORM_EPS) * gain

    ksl = group_rmsnorm(std[:, 2 * D_KV:3 * D_KV], kg_ref[0:1, :]).astype(BF16)
    kwn = group_rmsnorm(std[:, 3 * D_KV:4 * D_KV], kg_ref[1:2, :]).astype(BF16)

    tok = pl.program_id(1) * tm + lax.broadcasted_iota(jnp.int32, (tm, ns), 0)
    blk = lax.broadcasted_iota(jnp.int32, (tm, ns), 1)
    onehot = jnp.where((tok // SEL_BLOCK) == blk, 1.0, 0.0).astype(BF16)
    zeros = jnp.zeros((tm, HEAD_DIM), BF16)
    for g in range(NSA_GROUPS):
        ksel_ref[0, g, :, 0:HEAD_DIM] = ksl[:, g * HEAD_DIM:(g + 1) * HEAD_DIM]
        ksel_ref[0, g, :, HEAD_DIM:2 * HEAD_DIM] = zeros
        ksel_ref[0, g, :, 2 * HEAD_DIM:] = onehot
        kwin_ref[0, g, :, 0:HEAD_DIM] = kwn[:, g * HEAD_DIM:(g + 1) * HEAD_DIM]
        kwin_ref[0, g, :, HEAD_DIM:] = zeros

    tr = lax.dot_general(wt_ref[...], h, (((1,), (1,)), ((), ())),
                         preferred_element_type=F32)
    q = tr[0:D_NSA].reshape(NSA_HEADS, HEAD_DIM, tm)
    qms = jnp.mean(q * q, axis=1, keepdims=True)
    qn = q * lax.rsqrt(qms + NORM_EPS) * qg_ref[...][None] * Q_SCALE
    qT_ref[0] = qn.reshape(D_NSA, tm).astype(BF16)
    vt = tr[D_NSA:D_NSA + 2 * D_KV].astype(BF16)
    ones_rows = jnp.where(lax.broadcasted_iota(jnp.int32, (V_ROWS - HEAD_DIM, QT), 0) == 0, 1.0, 0.0).astype(BF16)
    for a in range(2 * NSA_GROUPS):
        for j in range(tm // QT):
            vT_ref[0, a, j, 0:HEAD_DIM, :] = vt[a * HEAD_DIM:(a + 1) * HEAD_DIM, j * QT:(j + 1) * QT]
            vT_ref[0, a, j, HEAD_DIM:, :] = ones_rows
    gT_ref[0] = jax.nn.sigmoid(tr[D_NSA + 2 * D_KV:])


def _proj(x, g1, wstd, wt, qg, kg, *, tm):
    B, T, D = x.shape
    ns = T // SEL_BLOCK
    grid = (B, T // tm)
    const2 = lambda b, i: (0, 0)
    out_shape = (
        jax.ShapeDtypeStruct((B, D_NSA, T), BF16),
        jax.ShapeDtypeStruct((B, T, D_KV), BF16),
        jax.ShapeDtypeStruct((B, T, D_KV), BF16),
        jax.ShapeDtypeStruct((B, NSA_GROUPS, T, 2 * HEAD_DIM + ns), BF16),
        jax.ShapeDtypeStruct((B, NSA_GROUPS, T, 2 * HEAD_DIM), BF16),
        jax.ShapeDtypeStruct((B, 2 * NSA_GROUPS, T // QT, V_ROWS, QT), BF16),
        jax.ShapeDtypeStruct((B, NSA_GROUPS * GATE_ROWS, T), F32),
        jax.ShapeDtypeStruct((B, T, D_RWKV_IN), F32),
    )
    out_specs = (
        pl.BlockSpec((1, D_NSA, tm), lambda b, i: (b, 0, i)),
        pl.BlockSpec((1, tm, D_KV), lambda b, i: (b, i, 0)),
        pl.BlockSpec((1, tm, D_KV), lambda b, i: (b, i, 0)),
        pl.BlockSpec((1, NSA_GROUPS, tm, 2 * HEAD_DIM + ns), lambda b, i: (b, 0, i, 0)),
        pl.BlockSpec((1, NSA_GROUPS, tm, 2 * HEAD_DIM), lambda b, i: (b, 0, i, 0)),
        pl.BlockSpec((1, 2 * NSA_GROUPS, tm // QT, V_ROWS, QT), lambda b, i: (b, 0, i, 0, 0)),
        pl.BlockSpec((1, NSA_GROUPS * GATE_ROWS, tm), lambda b, i: (b, 0, i)),
        pl.BlockSpec((1, tm, D_RWKV_IN), lambda b, i: (b, i, 0)),
    )
    in_specs = [
        pl.BlockSpec((1, tm, D), lambda b, i: (b, i, 0)),
        pl.BlockSpec(g1.shape, const2),
        pl.BlockSpec(wstd.shape, const2),
        pl.BlockSpec(wt.shape, const2),
        pl.BlockSpec(qg.shape, const2),
        pl.BlockSpec(kg.shape, const2),
    ]
    return pl.pallas_call(
        _proj_kernel, grid=grid, in_specs=in_specs, out_specs=out_specs, out_shape=out_shape,
        compiler_params=_cparams(("parallel", "parallel")), name="proj",
    )(x, g1, wstd, wt, qg, kg)


def _gelu_tanh(x):
    c = math.sqrt(2.0 / math.pi)
    return x * (0.5 * (1.0 + jnp.tanh(c * (x + 0.044715 * (x * x * x)))))


def _compress_kernel(kc_ref, vc_ref, wbig_ref, pos_ref, w1_ref, b1_ref, w2k_ref, w2vT_ref,
                     b2k_ref, b2v_ref, kg_ref, kcmp_ref, vcmpT_ref, act_ref):
    m = kc_ref.shape[1]
    nc = m - 1
    ncp = act_ref.shape[0]
    row_m = lax.broadcasted_iota(jnp.int32, (m, CMP_HIDDEN), 0)
    row_p = lax.broadcasted_iota(jnp.int32, (ncp, HEAD_DIM), 0)
    col_p = lax.broadcasted_iota(jnp.int32, (HEAD_DIM, ncp), 1)
    col_f = lax.broadcasted_iota(jnp.int32, (ncp, HEAD_DIM), 1)
    row_o = lax.broadcasted_iota(jnp.int32, (V_ROWS - HEAD_DIM, ncp), 0)
    act_ref[...] = jnp.zeros(act_ref.shape, F32)
    for which, src_ref in enumerate((kc_ref, vc_ref)):
        p = jnp.dot(src_ref[0], wbig_ref[which], preferred_element_type=F32)
        posc = jnp.dot(pos_ref[which], w1_ref[which], preferred_element_type=F32)[0:1]
        for g in range(NSA_GROUPS):
            top = p[:, (2 * g) * CMP_HIDDEN:(2 * g + 1) * CMP_HIDDEN]
            bot = p[:, (2 * g + 1) * CMP_HIDDEN:(2 * g + 2) * CMP_HIDDEN]
            hid = top + pltpu.roll(bot, m - 1, 0) + (b1_ref[which] + posc)
            act = jnp.where(row_m < nc, _gelu_tanh(hid), 0.0)
            act_ref[CMP_PAD:CMP_PAD + m, :] = act
            ap = act_ref[...].astype(BF16)
            if which == 0:
                kc = jnp.dot(ap, w2k_ref[...], preferred_element_type=F32) + b2k_ref[...]
                ms = jnp.mean(kc * kc, axis=-1, keepdims=True)
                kc = kc * lax.rsqrt(ms + NORM_EPS) * kg_ref[...]
                valid = (row_p >= CMP_PAD) & (row_p < CMP_PAD + nc)
                kcmp_ref[0, g, :, 0:HEAD_DIM] = jnp.where(valid, kc, 0.0).astype(BF16)
                flag = (col_f == 0) & jnp.logical_not(valid)
                kcmp_ref[0, g, :, HEAD_DIM:] = jnp.where(flag, 1.0, 0.0).astype(BF16)
            else:
                vt = lax.dot_general(w2vT_ref[...], ap, (((1,), (1,)), ((), ())),
                                     preferred_element_type=F32) + b2v_ref[...]
                valid = (col_p >= CMP_PAD) & (col_p < CMP_PAD + nc)
                vcmpT_ref[0, g, 0:HEAD_DIM, :] = jnp.where(valid, vt, 0.0).astype(BF16)
                vcmpT_ref[0, g, HEAD_DIM:, :] = jnp.where(row_o == 0, 1.0, 0.0).astype(BF16)


def _compress(kc16, vc16, wbig, pos8, w1, b1, w2k, w2vT, b2k, b2v, kg, *, ncp):
    B, m, _ = kc16.shape
    full = lambda a: pl.BlockSpec(a.shape, lambda b: (0,) * a.ndim)
    return pl.pallas_call(
        _compress_kernel, grid=(B,),
        in_specs=[pl.BlockSpec((1, m, kc16.shape[2]), lambda b: (b, 0, 0)),
                  pl.BlockSpec((1, m, vc16.shape[2]), lambda b: (b, 0, 0)),
                  full(wbig), full(pos8), full(w1), full(b1), full(w2k), full(w2vT),
                  full(b2k), full(b2v), full(kg)],
        out_specs=(pl.BlockSpec((1, NSA_GROUPS, ncp, 2 * HEAD_DIM), lambda b: (b, 0, 0, 0)),
                   pl.BlockSpec((1, NSA_GROUPS, V_ROWS, ncp), lambda b: (b, 0, 0, 0))),
        out_shape=(jax.ShapeDtypeStruct((B, NSA_GROUPS, ncp, 2 * HEAD_DIM), BF16),
                   jax.ShapeDtypeStruct((B, NSA_GROUPS, V_ROWS, ncp), BF16)),
        scratch_shapes=[pltpu.VMEM((ncp, CMP_HIDDEN), F32)],
        compiler_params=_cparams(("parallel",)), name="compress",
    )(kc16, vc16, wbig, pos8, w1, b1, w2k, w2vT, b2k, b2v, kg)


def _flash_update(s_ref, smax_ref, vT, m_ref, acc_ref):
    m_prev = m_ref[...]
    m_new = jnp.maximum(m_prev, smax_ref[...])
    p = jnp.exp2(s_ref[...] - m_new)
    acc_ref[...] = (jnp.exp2(m_prev - m_new) * acc_ref[...]
                    + jnp.dot(vT, p.astype(BF16), preferred_element_type=F32))
    m_ref[...] = m_new


def _attn_kernel(qT_ref, kcmp_ref, vcmpT_ref, bc_ref, ksel_ref, vselT_ref, kwin_ref, vwinT_ref,
                 stab_ref, wtab_ref, gT_ref, o_ref,
                 lc_ref, psum_ref, qaug_ref, qfar_ref, oc_ref, ow_ref, ms_ref, accs_ref, sa_ref, sb_ref, ma_ref, mb_ref):
    qt = pl.program_id(2)
    ncp = kcmp_ref.shape[2]
    ns = qaug_ref.shape[0] - 2 * HEAD_DIM
    nq = NSA_HPG * QT
    t_lane = qt * QT + lax.broadcasted_iota(jnp.int32, (1, QT), 1)

    for r in range(NSA_HPG):
        qaug_ref[0:HEAD_DIM, r * QT:(r + 1) * QT] = qT_ref[0, r * HEAD_DIM:(r + 1) * HEAD_DIM, :]
    flag_row = lax.broadcasted_iota(jnp.int32, (HEAD_DIM, nq), 0) == 0
    qaug_ref[HEAD_DIM:2 * HEAD_DIM, :] = jnp.where(flag_row, NEG_INF, 0.0).astype(BF16)
    qk = qaug_ref[0:2 * HEAD_DIM, :]

    rho = lax.broadcasted_iota(jnp.int32, (ncp, nq), 0)
    band0 = pl.multiple_of(qt * (QT // CMP_STRIDE), V7X_SUBLANES)
    lc_ref[...] = jnp.dot(kcmp_ref[0, 0], qk, preferred_element_type=F32)
    lc_ref[pl.ds(band0, CMP_BAND), :] += bc_ref[0]
    lc = jnp.where(rho < band0 + CMP_BAND, lc_ref[...], NEG_INF)
    e = jnp.exp2(lc - jnp.max(lc, axis=0, keepdims=True))
    oc_aug = jnp.dot(vcmpT_ref[0, 0], e.astype(BF16), preferred_element_type=F32)
    t_q = qt * QT + lax.broadcasted_iota(jnp.int32, (1, nq), 1) % QT
    inv_c = jnp.where(t_q >= CMP_BLOCK - 1, 1.0 / oc_aug[HEAD_DIM:HEAD_DIM + 1, :], 0.0)
    oc_ref[...] = oc_aug[0:HEAD_DIM] * inv_c
    p = e * inv_c
    psum = (p[:, 0:QT] + p[:, QT:2 * QT]) + (p[:, 2 * QT:3 * QT] + p[:, 3 * QT:4 * QT])
    for c in range(QT // V7X_LANES):
        psum_ref[c] = psum[:, c * V7X_LANES:(c + 1) * V7X_LANES]

    def strided_sum(c):
        acc = psum_ref[c, pl.ds(CMP_PAD - 1, ns, stride=4), :]
        for k in range(1, 5):
            acc = acc + psum_ref[c, pl.ds(CMP_PAD - 1 + k, ns, stride=4), :]
        return acc

    imp = jnp.concatenate([strided_sum(c) for c in range(QT // V7X_LANES)], axis=1)
    jrow = lax.broadcasted_iota(jnp.int32, (ns, QT), 0)
    cur = t_lane // SEL_BLOCK
    forced = (jrow == 0) | (jrow == cur) | (jrow == cur - 1)
    live = jrow * SEL_BLOCK <= t_lane
    sel = forced & live
    score = jnp.where(live, jnp.where(forced, -jnp.inf, imp), NEG_INF)
    jrow_f = jrow.astype(F32)

    def col_reduce(x, pair, reduce):
        parts = [x[i * V7X_SUBLANES:(i + 1) * V7X_SUBLANES, :] for i in range(x.shape[0] // V7X_SUBLANES)]
        while len(parts) > 1:
            parts = [pair(parts[i], parts[i + 1]) for i in range(0, len(parts) - 1, 2)] + parts[len(parts) & ~1:]
        return reduce(parts[0], axis=0, keepdims=True)

    for _ in range(min(SEL_TOPN, ns) - 3):
        mx = col_reduce(score, jnp.maximum, jnp.max)
        idx = col_reduce(jnp.where(score == mx, jrow_f, float(ns)), jnp.minimum, jnp.min)
        hit = jrow_f == idx
        sel = sel | hit
        score = jnp.where(hit, -jnp.inf, score)
    negmask = jnp.where(sel, 0.0, NEG_INF).astype(BF16)
    negfar = jnp.where(sel & (jrow < (qt - 1) * (QT // SEL_BLOCK)), 0.0, NEG_INF).astype(BF16)
    qfar_ref[0:2 * HEAD_DIM, :] = qaug_ref[0:2 * HEAD_DIM, :]
    for r in range(NSA_HPG):
        qaug_ref[2 * HEAD_DIM:, r * QT:(r + 1) * QT] = negmask
        qfar_ref[2 * HEAD_DIM:, r * QT:(r + 1) * QT] = negfar

    w0 = pl.multiple_of(qt * QT, QT)
    sw = jnp.dot(kwin_ref[0, 0, pl.ds(w0, WIN_KEYS), :], qk, preferred_element_type=F32) + wtab_ref[0]
    pw = jnp.exp2(sw - jnp.max(sw, axis=0, keepdims=True))
    vw = jnp.concatenate([vwinT_ref[0, 0, qt + j] for j in range(WIN_KEYS // QT)], axis=1)
    ow_aug = jnp.dot(vw, pw.astype(BF16), preferred_element_type=F32)
    ow_ref[...] = ow_aug[0:HEAD_DIM] * (1.0 / ow_aug[HEAD_DIM:HEAD_DIM + 1, :])

    n0 = pl.multiple_of((ATT_PAD // QT - 1 + qt) * QT, QT)
    sn = jnp.dot(ksel_ref[0, 0, pl.ds(n0, NEAR_KEYS), :], qaug_ref[...],
                 preferred_element_type=F32) + stab_ref[0]
    mn = jnp.max(sn, axis=0, keepdims=True)
    pn = jnp.exp2(sn - mn)
    vn = jnp.concatenate([vselT_ref[0, 0, ATT_PAD // QT - 1 + qt + j] for j in range(NEAR_KEYS // QT)], axis=1)
    ms_ref[...] = mn
    accs_ref[...] = jnp.dot(vn, pn.astype(BF16), preferred_element_type=F32)

    tiles_per_far = FAR_KEYS // QT
    n_far = (qt - 1 + tiles_per_far - 1) // tiles_per_far
    last_far = (ksel_ref.shape[2] - ATT_PAD) // FAR_KEYS - 1

    def far_logits(g, dst_ref, dmax_ref):
        r0 = pl.multiple_of(ATT_PAD + jnp.minimum(g, last_far) * FAR_KEYS, FAR_KEYS)
        s = jnp.dot(ksel_ref[0, 0, pl.ds(r0, FAR_KEYS), :], qfar_ref[...], preferred_element_type=F32)
        dst_ref[...] = s
        dmax_ref[...] = jnp.max(s, axis=0, keepdims=True)

    def far_values(g):
        t0 = (ATT_PAD + g * FAR_KEYS) // QT
        return jnp.concatenate([vselT_ref[0, 0, t0 + j] for j in range(tiles_per_far)], axis=1)

    far_logits(0, sa_ref, ma_ref)

    def far_body(j, carry):
        far_logits(2 * j + 1, sb_ref, mb_ref)
        _flash_update(sa_ref, ma_ref, far_values(2 * j), ms_ref, accs_ref)
        far_logits(2 * j + 2, sa_ref, ma_ref)
        _flash_update(sb_ref, mb_ref, far_values(jnp.minimum(2 * j + 1, last_far)), ms_ref, accs_ref)
        return carry

    lax.fori_loop(0, (n_far + 1) // 2, far_body, 0)

    o_s = accs_ref[0:HEAD_DIM, :] * (1.0 / accs_ref[HEAD_DIM:HEAD_DIM + 1, :])
    for r in range(NSA_HPG):
        cols = slice(r * QT, (r + 1) * QT)
        o = (gT_ref[0, 3 * r:3 * r + 1, :] * oc_ref[:, cols] + gT_ref[0, 3 * r + 1:3 * r + 2, :] * o_s[:, cols]
             + gT_ref[0, 3 * r + 2:3 * r + 3, :] * ow_ref[:, cols])
        o_ref[0, :, r * HEAD_DIM:(r + 1) * HEAD_DIM] = o.T


def _attn(qT, kcmp, vcmpT, bc, ksel, vT5, kwin, stab, wtab, gT):
    B, _, T = qT.shape
    ncp = kcmp.shape[2]
    ns = T // SEL_BLOCK
    nq = NSA_HPG * QT
    tp = T + ATT_PAD
    nt = tp // QT
    grid = (B, NSA_GROUPS, T // QT)

    def held(shape, index_map):
        return pl.BlockSpec(shape, index_map, pipeline_mode=pl.Buffered(1))

    in_specs = [
        pl.BlockSpec((1, NSA_HPG * HEAD_DIM, QT), lambda b, g, q: (b, g, q)),
        held((1, 1, ncp, 2 * HEAD_DIM), lambda b, g, q: (b, g, 0, 0)),
        held((1, 1, V_ROWS, ncp), lambda b, g, q: (b, g, 0, 0)),
        held((1, CMP_BAND, nq), lambda b, g, q: (g, 0, 0)),
        held((1, 1, tp, 2 * HEAD_DIM + ns), lambda b, g, q: (b, g, 0, 0)),
        held((1, 1, nt, V_ROWS, QT), lambda b, g, q: (b, g, 0, 0, 0)),
        held((1, 1, tp, 2 * HEAD_DIM), lambda b, g, q: (b, g, 0, 0)),
        held((1, 1, nt, V_ROWS, QT), lambda b, g, q: (b, NSA_GROUPS + g, 0, 0, 0)),
        held((1, NEAR_KEYS, nq), lambda b, g, q: (g, 0, 0)),
        held((1, WIN_KEYS, nq), lambda b, g, q: (g, 0, 0)),
        pl.BlockSpec((1, GATE_ROWS, QT), lambda b, g, q: (b, g, q)),
    ]
    scratch = [
        pltpu.VMEM((ncp, nq), F32),
        pltpu.VMEM((QT // V7X_LANES, ncp, V7X_LANES), F32),
        pltpu.VMEM((2 * HEAD_DIM + ns, nq), BF16),
        pltpu.VMEM((2 * HEAD_DIM + ns, nq), BF16),
        pltpu.VMEM((HEAD_DIM, nq), F32),
        pltpu.VMEM((HEAD_DIM, nq), F32),
        pltpu.VMEM((1, nq), F32),
        pltpu.VMEM((V_ROWS, nq), F32),
        pltpu.VMEM((FAR_KEYS, nq), F32),
        pltpu.VMEM((FAR_KEYS, nq), F32),
        pltpu.VMEM((1, nq), F32), pltpu.VMEM((1, nq), F32),
    ]
    return pl.pallas_call(
        _attn_kernel, grid=grid, in_specs=in_specs,
        out_specs=pl.BlockSpec((1, QT, NSA_HPG * HEAD_DIM), lambda b, g, q: (b, q, g)),
        out_shape=jax.ShapeDtypeStruct((B, T, D_NSA), F32),
        scratch_shapes=scratch,
        compiler_params=_cparams(("parallel", "parallel", "arbitrary")), name="attn",
    )(qT, kcmp, vcmpT, bc, ksel, vT5, kwin, vT5, stab, wtab, gT)


def _rwkv_tokens(rw_ref, halo_ref, mu_ref, w0_ref, w2_ref, a0_ref, a2_ref, g2_ref, kk_ref, ka_ref,
                 rk_ref, bd_ref, ext_ref):
    tm = rw_ref.shape[1]
    first_tile = pl.program_id(1) == 0
    ext_ref[0:V7X_SUBLANES, :] = jnp.where(first_tile, 0.0, halo_ref[0])
    ext_ref[V7X_SUBLANES:, :] = rw_ref[0]
    cur = rw_ref[0]
    prev = ext_ref[pl.ds(V7X_SUBLANES - 1, tm), :]
    mixed = cur + (prev - cur) * mu_ref[...]
    c = D_RWKV
    r = mixed[:, 0:c]
    k = mixed[:, c:2 * c]
    v = mixed[:, 2 * c:3 * c]
    xw = mixed[:, 3 * c:3 * c + LORA_W]
    xa = mixed[:, 3 * c + LORA_W:3 * c + LORA_W + LORA_A]
    xg = mixed[:, 3 * c + LORA_W + LORA_A:]

    z = -(w0_ref[...] + _bdot(jnp.tanh(xw), w2_ref[...]))
    softplus = jnp.maximum(z, 0.0) + jnp.log1p(jnp.exp(-jnp.abs(z)))
    w = -softplus - 0.5
    a = jax.nn.sigmoid(a0_ref[...] + _bdot(xa, a2_ref[...]))
    g = _bdot(jax.nn.sigmoid(xg), g2_ref[...])

    kk = k * kk_ref[...]
    n2 = _dot_exact_rhs(kk * kk, bd_ref[...])
    kkn = kk / jnp.maximum(jnp.sqrt(n2), 1e-12)
    k2 = k * (1.0 + (a - 1.0) * ka_ref[...])
    bonus = _dot_exact_rhs(r * k2 * rk_ref[...], bd_ref[...]) * v

    lw = -jnp.exp(w)
    return dict(r=r, lw=lw, k=k2, v=v, kk=kkn, b=kkn * a, g=g, bonus=bonus)


WKV_GROUP = 4
WKV_GW = WKV_GROUP * HEAD_DIM


def _wkv_kernel(rw_ref, halo_ref, mu_ref, w0_ref, w2_ref, a0_ref, a2_ref, g2_ref, kk_ref, ka_ref, rk_ref, bd_ref,
                y_ref, g_ref, bonus_ref, s_ref, ext_ref, *, chunks):
    L = WKV_CHUNK
    tok = _rwkv_tokens(rw_ref, halo_ref, mu_ref, w0_ref, w2_ref, a0_ref, a2_ref, g2_ref, kk_ref, ka_ref,
                       rk_ref, bd_ref, ext_ref)
    g_ref[0] = tok["g"]
    bonus_ref[0] = tok["bonus"]
    assert L == HEAD_DIM
    gw = WKV_GW
    n_groups = RWKV_HEADS // WKV_GROUP

    @pl.when(pl.program_id(1) == 0)
    def _():
        s_ref[...] = jnp.zeros(s_ref.shape, F32)

    ti = lax.broadcasted_iota(jnp.int32, (L, L), 0)
    tj = lax.broadcasted_iota(jnp.int32, (L, L), 1)
    tri = jnp.where(ti >= tj, 1.0, 0.0).astype(BF16)
    row = lax.broadcasted_iota(jnp.int32, (L, gw), 0)
    col = lax.broadcasted_iota(jnp.int32, (L, gw), 1) % L
    low_strict = col < row
    low_incl = col <= row
    eye_sbs = jnp.where(col == row, 1.0, 0.0)
    brow = lax.broadcasted_iota(jnp.int32, (gw, gw), 0) // L
    bcol = lax.broadcasted_iota(jnp.int32, (gw, gw), 1) // HEAD_DIM
    same_head = brow == bcol

    def bd_rows(x):
        xb = x.astype(BF16)
        return jnp.where(same_head, jnp.concatenate([xb] * WKV_GROUP, axis=0), jnp.zeros((), BF16))

    def mm(a, b_bf16):
        return jnp.dot(a.astype(BF16), b_bf16, preferred_element_type=F32)

    def tn(a, b):
        return lax.dot_general(a.astype(BF16), b.astype(BF16), (((0,), (0,)), ((), ())),
                               preferred_element_type=F32)

    inst = [(c, g) for c in range(chunks) for g in range(n_groups)]
    pre = {}
    for c in range(chunks):
        rows = slice(c * L, (c + 1) * L)
        lw = tok["lw"][rows]
        cs = _dot_exact_rhs_left(tri, lw)
        c_last = cs[L - 1:L, :]
        e_nc = jnp.exp(-cs)
        e_lc = jnp.exp(c_last - cs)
        kk = tok["kk"][rows]
        b = tok["b"][rows]
        k = tok["k"][rows]
        pre[c] = dict(a=-kk * jnp.exp(cs - lw), r=tok["r"][rows] * jnp.exp(cs), bh=b * e_nc, kh=k * e_nc,
                      be=b * e_lc, ke=k * e_lc, v=tok["v"][rows], e_last=jnp.exp(c_last))

    def grp(c, g, name):
        return pre[c][name][:, g * gw:(g + 1) * gw]

    t_all = {}
    for (c, g) in inst:
        lhs = jnp.concatenate([grp(c, g, "a"), grp(c, g, "r")], axis=0).astype(BF16)
        rhs = jnp.concatenate([bd_rows(grp(c, g, "bh")), bd_rows(grp(c, g, "kh"))], axis=0)
        t_all[c, g] = lax.dot_general(lhs, rhs, (((1,), (1,)), ((), ())), preferred_element_type=F32)
    n_m = {i: jnp.where(low_strict, t_all[i][0:L, 0:gw], 0.0) for i in inst}
    tak = {i: jnp.where(low_strict, t_all[i][0:L, gw:], 0.0) for i in inst}
    trb = {i: jnp.where(low_incl, t_all[i][L:, 0:gw], 0.0) for i in inst}
    trk = {i: jnp.where(low_incl, t_all[i][L:, gw:], 0.0) for i in inst}
    z = {i: eye_sbs + n_m[i] for i in inst}
    pw = dict(n_m)
    for _ in range(int(math.log2(L)) - 1):
        pw = {i: mm(pw[i], bd_rows(pw[i])) for i in inst}
        z = {i: z[i] + mm(z[i], bd_rows(pw[i])) for i in inst}
    vbd = {(c, g): bd_rows(grp(c, g, "v")) for (c, g) in inst}
    w1 = {(c, g): mm(z[c, g], bd_rows(grp(c, g, "a"))) for (c, g) in inst}
    tv = {i: mm(tak[i], vbd[i]) for i in inst}
    c1 = {i: mm(z[i], bd_rows(tv[i])) for i in inst}
    w2 = {(c, g): grp(c, g, "r") + mm(trb[c, g], bd_rows(w1[c, g])) for (c, g) in inst}
    c2 = {i: mm(trb[i], bd_rows(c1[i])) + mm(trk[i], vbd[i]) for i in inst}
    m_lr = {(c, g): jnp.where(same_head, tn(w1[c, g], grp(c, g, "be")), 0.0).astype(BF16) for (c, g) in inst}
    d_sbs = {}
    for (c, g) in inst:
        full = jnp.where(same_head,
                         tn(jnp.concatenate([c1[c, g], grp(c, g, "v")], axis=0),
                            jnp.concatenate([grp(c, g, "be"), grp(c, g, "ke")], axis=0)), 0.0)
        d_sbs[c, g] = ((full[0:L] + full[L:2 * L]) + (full[2 * L:3 * L] + full[3 * L:4 * L]))

    for g in range(n_groups):
        s = s_ref[:, g * gw:(g + 1) * gw]
        for c in range(chunks):
            g_bd = jnp.where(same_head, jnp.concatenate([s.T.astype(BF16)] * WKV_GROUP, axis=1),
                             jnp.zeros((), BF16))
            y_ref[0, c * L:(c + 1) * L, g * gw:(g + 1) * gw] = mm(w2[c, g], g_bd) + c2[c, g]
            s = s * grp(c, g, "e_last") + mm(s, m_lr[c, g]) + d_sbs[c, g]
        s_ref[:, g * gw:(g + 1) * gw] = s


def _dot_exact_rhs_left(m_bf16, x):
    hi, mid, lo = _split3(x)
    return (jnp.dot(m_bf16, hi, preferred_element_type=F32)
            + jnp.dot(m_bf16, mid, preferred_element_type=F32)
            + jnp.dot(m_bf16, lo, preferred_element_type=F32))


def _wkv(rw, mu, w0, w2, a0, a2, g2, k_k, k_a, r_k, bd, *, chunks):
    B, T, C = rw.shape
    rows = chunks * WKV_CHUNK
    hb = rows // V7X_SUBLANES
    full = lambda a: pl.BlockSpec(a.shape, lambda bb, c: (0,) * a.ndim)
    tok = pl.BlockSpec((1, rows, D_RWKV), lambda bb, c: (bb, c, 0))
    return pl.pallas_call(
        functools.partial(_wkv_kernel, chunks=chunks), grid=(B, T // rows),
        in_specs=[pl.BlockSpec((1, rows, C), lambda bb, c: (bb, c, 0)),
                  pl.BlockSpec((1, V7X_SUBLANES, C), lambda bb, c: (bb, jnp.maximum(c * hb - 1, 0), 0)),
                  full(mu), full(w0), full(w2), full(a0), full(a2), full(g2), full(k_k), full(k_a),
                  full(r_k), full(bd)],
        out_specs=(tok,) * 3,
        out_shape=(jax.ShapeDtypeStruct((B, T, D_RWKV), F32),) * 3,
        scratch_shapes=[pltpu.VMEM((HEAD_DIM, D_RWKV), F32), pltpu.VMEM((rows + V7X_SUBLANES, C), F32)],
        compiler_params=_cparams(("parallel", "arbitrary")), name="wkv",
    )(rw, rw, mu, w0, w2, a0, a2, g2, k_k, k_a, r_k, bd)


def _mix_kernel(x_ref, on_ref, y_ref, bonus_ref, g_ref, bd_ref, lnw_ref, lnb_ref, wo_ref, g2_ref, x1_ref, h2_ref):
    y = y_ref[...]
    mu = _dot_exact_rhs(y, bd_ref[...]) * (1.0 / HEAD_DIM)
    yc = y - mu
    var = _dot_exact_rhs(yc * yc, bd_ref[...]) * (1.0 / HEAD_DIM)
    yn = yc * lax.rsqrt(var + GN_EPS) * lnw_ref[...] + lnb_ref[...]
    orw = (yn + bonus_ref[...]) * g_ref[...]
    x1 = (x_ref[...] + _bdot(on_ref[...], wo_ref[0:D_NSA, :]) + _bdot(orw, wo_ref[D_NSA:, :]))
    x1_ref[...] = x1
    ms = jnp.mean(x1 * x1, axis=-1, keepdims=True)
    h2_ref[...] = (x1 * lax.rsqrt(ms + NORM_EPS) * g2_ref[...]).astype(BF16)


def _mix(x2d, on2d, y2d, bonus2d, g2d, bd, lnw, lnb, wo, g2, *, tm):
    n, d = x2d.shape
    tok = lambda w: pl.BlockSpec((tm, w), lambda i: (i, 0))
    full = lambda a: pl.BlockSpec(a.shape, lambda i: (0,) * a.ndim)
    return pl.pallas_call(
        _mix_kernel, grid=(n // tm,),
        in_specs=[tok(d), tok(D_NSA), tok(D_RWKV), tok(D_RWKV), tok(D_RWKV), full(bd), full(lnw), full(lnb),
                  full(wo), full(g2)],
        out_specs=(tok(d), tok(d)),
        out_shape=(jax.ShapeDtypeStruct((n, d), F32), jax.ShapeDtypeStruct((n, d), BF16)),
        compiler_params=_cparams(("parallel",)), name="mix",
    )(x2d, on2d, y2d, bonus2d, g2d, bd, lnw, lnb, wo, g2)


FFN_HALO = 16


def _ffn_kernel(x1_ref, h2_ref, halo_ref, wv_ref, wg_ref, cwv_ref, cwg_ref, cbv_ref, cbg_ref, wd_ref,
                o_ref, hext_ref, extv_ref, extg_ref, act_ref, *, tiles_per_seq, ft):
    tm = h2_ref.shape[0]
    dff = wd_ref.shape[0]
    seq_start = (pl.program_id(0) % tiles_per_seq) == 0
    hext_ref[0:FFN_HALO, :] = jnp.where(seq_start, jnp.zeros((), BF16), halo_ref[...])
    hext_ref[FFN_HALO:, :] = h2_ref[...]

    for j in range(dff // ft):
        cols = slice(j * ft, (j + 1) * ft)

        def conv_branch(w_ref, cw_ref, cb_ref, ext_ref):
            ext_ref[j % 2] = jnp.dot(hext_ref[...], w_ref[:, cols], preferred_element_type=F32)
            out = cb_ref[:, cols] + ext_ref[j % 2, pl.ds(FFN_HALO, tm), :] * cw_ref[CONV_W - 1:CONV_W, cols]
            for i in range(CONV_W - 1):
                back = CONV_W - 1 - i
                out = out + ext_ref[j % 2, pl.ds(FFN_HALO - back, tm), :] * cw_ref[i:i + 1, cols]
            return out

        u_val = conv_branch(wv_ref, cwv_ref, cbv_ref, extv_ref)
        u_gate = conv_branch(wg_ref, cwg_ref, cbg_ref, extg_ref)
        act_ref[:, cols] = ((u_gate * jax.nn.sigmoid(u_gate)) * u_val).astype(BF16)
    o_ref[...] = x1_ref[...] + jnp.dot(act_ref[...], wd_ref[...], preferred_element_type=F32)


def _ffn(x1, h2, wv, wg, cwv, cwg, cbv, cbg, wd, *, tm, ft, seq_len):
    n, d = x1.shape
    dff = wv.shape[1]
    hb = tm // FFN_HALO
    kern = functools.partial(_ffn_kernel, tiles_per_seq=seq_len // tm, ft=ft)
    resident = lambda a: pl.BlockSpec(a.shape, lambda i: (0,) * a.ndim, pipeline_mode=pl.Buffered(1))
    return pl.pallas_call(
        kern, grid=(n // tm,),
        in_specs=[pl.BlockSpec((tm, d), lambda i: (i, 0)),
                  pl.BlockSpec((tm, d), lambda i: (i, 0)),
                  pl.BlockSpec((FFN_HALO, d), lambda i: (jnp.maximum(i * hb - 1, 0), 0)),
                  resident(wv), resident(wg), resident(cwv), resident(cwg), resident(cbv), resident(cbg),
                  resident(wd)],
        out_specs=pl.BlockSpec((tm, d), lambda i: (i, 0)),
        out_shape=jax.ShapeDtypeStruct((n, d), F32),
        scratch_shapes=[pltpu.VMEM((tm + FFN_HALO, d), BF16),
                        pltpu.VMEM((2, tm + FFN_HALO, ft), F32), pltpu.VMEM((2, tm + FFN_HALO, ft), F32),
                        pltpu.VMEM((tm, dff), BF16)],
        compiler_params=_cparams(("parallel",)), name="ffn",
    )(x1, h2, h2, wv, wg, cwv, cwg, cbv, cbg, wd)


def _t5_bucket(dist):
    n = np.maximum(dist, 0)
    max_exact = N_BUCKETS // 2
    nf = np.maximum(n, 1).astype(np.float32)
    large = max_exact + (np.log(nf / np.float32(max_exact)) / np.float32(math.log(MAX_DISTANCE / max_exact))
                         * np.float32(N_BUCKETS - max_exact)).astype(np.int32)
    large = np.minimum(large, N_BUCKETS - 1)
    return np.where(n < max_exact, n, large)


def _toeplitz_kernel(v_ref, o_ref, *, step):
    rows, width = o_ref.shape[1], o_ref.shape[2]
    x = jnp.broadcast_to(v_ref[0], (rows, v_ref.shape[2]))
    o_ref[0] = pltpu.roll(x, 0, 1, stride=step, stride_axis=0)[:, :width]


def _toeplitz(v, rows, step):
    period = v.shape[2]
    return pl.pallas_call(
        functools.partial(_toeplitz_kernel, step=step), grid=(NSA_GROUPS, NSA_HPG),
        in_specs=[pl.BlockSpec((1, 1, period), lambda g, r: (g * NSA_HPG + r, 0, 0))],
        out_specs=pl.BlockSpec((1, rows, QT), lambda g, r: (g, 0, r)),
        out_shape=jax.ShapeDtypeStruct((NSA_GROUPS, rows, NSA_HPG * QT), F32),
        compiler_params=_cparams(("parallel", "parallel")), name="toeplitz",
    )(v)


def _bias_tables(rel_bias):
    rel = (rel_bias - rel_bias[N_BUCKETS - 1][None, :]) * LOG2E

    def table(rows, step, d0, d_hi):
        period = step * rows + QT
        i = np.arange(period)
        d = d0 + np.where(i < QT, i, i - period)
        onehot = (_t5_bucket(d)[:, None] == np.arange(N_BUCKETS)[None, :]).astype(np.float32)
        v = jnp.dot(jnp.asarray(onehot), rel, precision=lax.Precision.HIGHEST)
        v = jnp.where(jnp.asarray((d >= 0) & (d < d_hi))[:, None], v, NEG_INF).T
        return _toeplitz(v.reshape(NSA_HEADS, 1, period), rows, step)

    no_limit = 1 << 30
    stab = table(NEAR_KEYS, 1, QT, no_limit)
    wtab = table(WIN_KEYS, 1, WINDOW, WINDOW)
    bc = table(CMP_BAND, CMP_STRIDE, CMP_STRIDE * CMP_PAD - CMP_BLOCK + 1, no_limit)
    return stab, wtab, bc


def _compress_weights(w1):
    half = CMP_BLOCK // 2
    w1r = w1.reshape(2, half, HEAD_DIM, CMP_HIDDEN).transpose(1, 2, 0, 3)
    big = jnp.zeros((half, NSA_GROUPS, HEAD_DIM, NSA_GROUPS, 2, CMP_HIDDEN), w1.dtype)
    for g in range(NSA_GROUPS):
        big = big.at[:, g, :, g, :, :].set(w1r)
    return big.reshape(half * NSA_GROUPS * HEAD_DIM, NSA_GROUPS * 2 * CMP_HIDDEN)


def kernel(x, norm1_g, w_in, q_norm_g, k_norm_g, cmp_pos, cmp_w1, cmp_b1, cmp_w2, cmp_b2, rel_bias, rwkv_mu,
           w0, w2, a0, a2, g2, k_k, k_a, r_k, ln_x_w, ln_x_b, w_out, norm2_g, ffn_up, conv_w, conv_b, ffn_down):
    B, T, D = x.shape
    depth = w_in.shape[0]
    d_ff = ffn_down.shape[1]
    assert T % 2048 == 0 and D_NSA + 6 * D_KV + 3 * NSA_HEADS + D_RWKV_IN == w_in.shape[2]
    ncp = T // CMP_STRIDE + V7X_LANES
    stab, wtab, bc = _bias_tables(rel_bias)
    ii = jnp.arange(D_RWKV)
    bd = (ii[:, None] // HEAD_DIM == ii[None, :] // HEAD_DIM).astype(BF16)

    for l in range(depth):
        wi = w_in[l]
        o = D_NSA
        q_w, kc_w, vc_w, ksl_w, vsl_w, kwn_w, vwn_w = (
            wi[:, 0:o], wi[:, o:o + D_KV], wi[:, o + D_KV:o + 2 * D_KV], wi[:, o + 2 * D_KV:o + 3 * D_KV],
            wi[:, o + 3 * D_KV:o + 4 * D_KV], wi[:, o + 4 * D_KV:o + 5 * D_KV], wi[:, o + 5 * D_KV:o + 6 * D_KV])
        gl_w = wi[:, o + 6 * D_KV:o + 6 * D_KV + 3 * NSA_HEADS]
        rw_w = wi[:, o + 6 * D_KV + 3 * NSA_HEADS:]
        wstd = jnp.concatenate([kc_w, vc_w, ksl_w, kwn_w, rw_w], axis=1).astype(BF16)
        gl_rows = gl_w.T.reshape(NSA_GROUPS, 3 * NSA_HPG, D)
        gl_rows = jnp.pad(gl_rows, ((0, 0), (0, GATE_ROWS - 3 * NSA_HPG), (0, 0))).reshape(-1, D)
        wt = jnp.concatenate([q_w.T, vsl_w.T, vwn_w.T, gl_rows], axis=0).astype(BF16)
        qg = q_norm_g[l].reshape(HEAD_DIM, 1)
        kg = jnp.stack([jnp.tile(k_norm_g[l, 1], NSA_GROUPS), jnp.tile(k_norm_g[l, 2], NSA_GROUPS)])

        qT, kc, vc, ksel, kwin, vT5, gT, rw = _proj(x, norm1_g[l].reshape(1, D), wstd, wt, qg, kg, tm=512)

        wbig = jnp.stack([_compress_weights(cmp_w1[l, 0]), _compress_weights(cmp_w1[l, 1])]).astype(BF16)
        pos8 = jnp.pad(cmp_pos[l].reshape(2, 1, CMP_BLOCK * HEAD_DIM),
                       ((0, 0), (0, V7X_SUBLANES - 1), (0, 0))).astype(BF16)
        rows16 = CMP_STRIDE * D_KV
        kcmp, vcmpT = _compress(
            kc.reshape(B, T // CMP_STRIDE, rows16), vc.reshape(B, T // CMP_STRIDE, rows16), wbig, pos8,
            cmp_w1[l].astype(BF16), cmp_b1[l].reshape(2, 1, CMP_HIDDEN), cmp_w2[l, 0].astype(BF16),
            cmp_w2[l, 1].T.astype(BF16), cmp_b2[l, 0].reshape(1, HEAD_DIM), cmp_b2[l, 1].reshape(HEAD_DIM, 1),
            k_norm_g[l, 0].reshape(1, HEAD_DIM), ncp=ncp)

        def front_pad(k):
            flag = (jnp.arange(k.shape[-1]) == HEAD_DIM).astype(BF16)
            return jnp.concatenate([jnp.broadcast_to(flag, k.shape[:2] + (ATT_PAD, k.shape[-1])), k], axis=2)

        vT5p = jnp.pad(vT5, ((0, 0), (0, 0), (ATT_PAD // QT, 0), (0, 0), (0, 0)))
        o_nsa = _attn(qT, kcmp, vcmpT, bc, front_pad(ksel), vT5p, front_pad(kwin), stab, wtab, gT)

        row = lambda a: a.reshape(1, -1)
        y, g, bonus = _wkv(
            rw, row(rwkv_mu[l]), row(w0[l]), w2[l].astype(BF16), row(a0[l]), a2[l].astype(BF16),
            g2[l].astype(BF16), row(k_k[l]), row(k_a[l]), row(r_k[l]), bd, chunks=8)

        n = B * T
        x1, h2 = _mix(x.reshape(n, D), o_nsa.reshape(n, D_NSA), y.reshape(n, D_RWKV),
                      bonus.reshape(n, D_RWKV), g.reshape(n, D_RWKV), bd, row(ln_x_w[l]), row(ln_x_b[l]),
                      w_out[l].astype(BF16), norm2_g[l].reshape(1, D), tm=512)

        up = ffn_up[l].astype(BF16)
        x = _ffn(x1, h2, up[:, :d_ff], up[:, d_ff:], conv_w[l][:, :d_ff], conv_w[l][:, d_ff:],
                 conv_b[l][:d_ff].reshape(1, -1), conv_b[l][d_ff:].reshape(1, -1),
                 ffn_down[l].astype(BF16), tm=512, ft=256, seq_len=T).reshape(B, T, D)
    return x
```

```python
import functools
import math

import jax
import jax.numpy as jnp
import numpy as np
from jax import lax
from jax.experimental import pallas as pl
from jax.experimental.pallas import tpu as pltpu

F32 = jnp.float32
BF16 = jnp.bfloat16

V7X_LANES = 128
V7X_SUBLANES = 8
V7X_VMEM_LIMIT_BYTES = 56 * 1024 * 1024

HEAD_DIM = 64
NSA_HEADS = 8
NSA_GROUPS = 2
NSA_HPG = NSA_HEADS // NSA_GROUPS
RWKV_HEADS = 8
D_NSA = NSA_HEADS * HEAD_DIM
D_RWKV = RWKV_HEADS * HEAD_DIM
D_KV = NSA_GROUPS * HEAD_DIM
CMP_BLOCK = 32
CMP_STRIDE = 16
CMP_HIDDEN = 128
SEL_BLOCK = 64
SEL_TOPN = 16
WINDOW = 512
N_BUCKETS = 32
MAX_DISTANCE = 128
LORA_W = 64
LORA_A = 64
LORA_G = 128
D_RWKV_IN = 3 * D_RWKV + LORA_W + LORA_A + LORA_G
CONV_W = 3
NORM_EPS = 1e-6
GN_EPS = 64e-5
NEG_INF = -1e30
FORCE_SCORE = 1e9

QT = 256
CMP_PAD = 8
CMP_BAND = QT // CMP_STRIDE + 8
LOG2E = math.log2(math.e)
Q_SCALE = HEAD_DIM ** -0.5 * LOG2E
V_ROWS = HEAD_DIM + 16
ATT_PAD = WINDOW
NEAR_KEYS = 2 * QT
FAR_KEYS = 2 * QT
WIN_KEYS = WINDOW + QT
WKV_CHUNK = 64
GATE_ROWS = 16
N_T_ROWS = D_NSA + 2 * D_KV + NSA_GROUPS * GATE_ROWS
N_STD_COLS = 4 * D_KV + D_RWKV_IN


def _cparams(sem):
    return pltpu.CompilerParams(dimension_semantics=sem, vmem_limit_bytes=V7X_VMEM_LIMIT_BYTES)


def _bdot(a, b):
    return jnp.dot(a.astype(BF16), b.astype(BF16), preferred_element_type=F32)


def _bdot_nt(a, b):
    return lax.dot_general(a.astype(BF16), b.astype(BF16), (((1,), (1,)), ((), ())),
                           preferred_element_type=F32)


def _split3(x):
    hi = x.astype(BF16)
    r1 = x - hi.astype(F32)
    mid = r1.astype(BF16)
    lo = (r1 - mid.astype(F32)).astype(BF16)
    return hi, mid, lo


def _dot_exact_rhs(x, m_bf16):
    hi = x.astype(BF16)
    lo = (x - hi.astype(F32)).astype(BF16)
    return jnp.dot(hi, m_bf16, preferred_element_type=F32) + jnp.dot(lo, m_bf16, preferred_element_type=F32)


def _proj_kernel(x_ref, g1_ref, wstd_ref, wt_ref, qg_ref, kg_ref,
                 qT_ref, kc_ref, vc_ref, ksel_ref, kwin_ref, vT_ref, gT_ref, rw_ref):
    tm = x_ref.shape[1]
    ns = ksel_ref.shape[3] - 2 * HEAD_DIM
    x = x_ref[0]
    ms = jnp.mean(x * x, axis=-1, keepdims=True)
    h = (x * lax.rsqrt(ms + NORM_EPS) * g1_ref[...]).astype(BF16)

    std = jnp.dot(h, wstd_ref[...], preferred_element_type=F32)
    kc_ref[0] = std[:, 0:D_KV].astype(BF16)
    vc_ref[0] = std[:, D_KV:2 * D_KV].astype(BF16)
    rw_ref[0] = std[:, 4 * D_KV:]

    lane = lax.broadcasted_iota(jnp.int32, (tm, D_KV), 1)
    first = lane < HEAD_DIM

    def group_rmsnorm(k, gain):
        k2 = k * k
        s0 = jnp.sum(jnp.where(first, k2, 0.0), axis=-1, keepdims=True)
        s1 = jnp.sum(jnp.where(first, 0.0, k2), axis=-1, keepdims=True)
        ms_g = jnp.where(first, s0, s1) * (1.0 / HEAD_DIM)
        return k * lax.rsqrt(ms_g + NORM_EPS) * gain

    ksl = group_rmsnorm(std[:, 2 * D_KV:3 * D_KV], kg_ref[0:1, :]).astype(BF16)
    kwn = group_rmsnorm(std[:, 3 * D_KV:4 * D_KV], kg_ref[1:2, :]).astype(BF16)

    tok = pl.program_id(1) * tm + lax.broadcasted_iota(jnp.int32, (tm, ns), 0)
    blk = lax.broadcasted_iota(jnp.int32, (tm, ns), 1)
    onehot = jnp.where((tok // SEL_BLOCK) == blk, 1.0, 0.0).astype(BF16)
    zeros = jnp.zeros((tm, HEAD_DIM), BF16)
    for g in range(NSA_GROUPS):
        ksel_ref[0, g, :, 0:HEAD_DIM] = ksl[:, g * HEAD_DIM:(g + 1) * HEAD_DIM]
        ksel_ref[0, g, :, HEAD_DIM:2 * HEAD_DIM] = zeros
        ksel_ref[0, g, :, 2 * HEAD_DIM:] = onehot
        kwin_ref[0, g, :, 0:HEAD_DIM] = kwn[:, g * HEAD_DIM:(g + 1) * HEAD_DIM]
        kwin_ref[0, g, :, HEAD_DIM:] = zeros

    tr = lax.dot_general(wt_ref[...], h, (((1,), (1,)), ((), ())),
                         preferred_element_type=F32)
    q = tr[0:D_NSA].reshape(NSA_HEADS, HEAD_DIM, tm)
    qms = jnp.mean(q * q, axis=1, keepdims=True)
    qn = q * lax.rsqrt(qms + NORM_EPS) * qg_ref[...][None] * Q_SCALE
    qT_ref[0] = qn.reshape(D_NSA, tm).astype(BF16)
    vt = tr[D_NSA:D_NSA + 2 * D_KV].astype(BF16)
    ones_rows = jnp.where(lax.broadcasted_iota(jnp.int32, (V_ROWS - HEAD_DIM, QT), 0) == 0, 1.0, 0.0).astype(BF16)
    for a in range(2 * NSA_GROUPS):
        for j in range(tm // QT):
            vT_ref[0, a, j, 0:HEAD_DIM, :] = vt[a * HEAD_DIM:(a + 1) * HEAD_DIM, j * QT:(j + 1) * QT]
            vT_ref[0, a, j, HEAD_DIM:, :] = ones_rows
    gT_ref[0] = jax.nn.sigmoid(tr[D_NSA + 2 * D_KV:])


def _proj(x, g1, wstd, wt, qg, kg, *, tm):
    B, T, D = x.shape
    ns = T // SEL_BLOCK
    grid = (B, T // tm)
    const2 = lambda b, i: (0, 0)
    out_shape = (
        jax.ShapeDtypeStruct((B, D_NSA, T), BF16),
        jax.ShapeDtypeStruct((B, T, D_KV), BF16),
        jax.ShapeDtypeStruct((B, T, D_KV), BF16),
        jax.ShapeDtypeStruct((B, NSA_GROUPS, T, 2 * HEAD_DIM + ns), BF16),
        jax.ShapeDtypeStruct((B, NSA_GROUPS, T, 2 * HEAD_DIM), BF16),
        jax.ShapeDtypeStruct((B, 2 * NSA_GROUPS, T // QT, V_ROWS, QT), BF16),
        jax.ShapeDtypeStruct((B, NSA_GROUPS * GATE_ROWS, T), F32),
        jax.ShapeDtypeStruct((B, T, D_RWKV_IN), F32),
    )
    out_specs = (
        pl.BlockSpec((1, D_NSA, tm), lambda b, i: (b, 0, i)),
        pl.BlockSpec((1, tm, D_KV), lambda b, i: (b, i, 0)),
        pl.BlockSpec((1, tm, D_KV), lambda b, i: (b, i, 0)),
        pl.BlockSpec((1, NSA_GROUPS, tm, 2 * HEAD_DIM + ns), lambda b, i: (b, 0, i, 0)),
        pl.BlockSpec((1, NSA_GROUPS, tm, 2 * HEAD_DIM), lambda b, i: (b, 0, i, 0)),
        pl.BlockSpec((1, 2 * NSA_GROUPS, tm // QT, V_ROWS, QT), lambda b, i: (b, 0, i, 0, 0)),
        pl.BlockSpec((1, NSA_GROUPS * GATE_ROWS, tm), lambda b, i: (b, 0, i)),
        pl.BlockSpec((1, tm, D_RWKV_IN), lambda b, i: (b, i, 0)),
    )
    in_specs = [
        pl.BlockSpec((1, tm, D), lambda b, i: (b, i, 0)),
        pl.BlockSpec(g1.shape, const2),
        pl.BlockSpec(wstd.shape, const2),
        pl.BlockSpec(wt.shape, const2),
        pl.BlockSpec(qg.shape, const2),
        pl.BlockSpec(kg.shape, const2),
    ]
    return pl.pallas_call(
        _proj_kernel, grid=grid, in_specs=in_specs, out_specs=out_specs, out_shape=out_shape,
        compiler_params=_cparams(("parallel", "parallel")), name="proj",
    )(x, g1, wstd, wt, qg, kg)


def _gelu_tanh(x):
    c = math.sqrt(2.0 / math.pi)
    return x * (0.5 * (1.0 + jnp.tanh(c * (x + 0.044715 * (x * x * x)))))


def _compress_kernel(kc_ref, vc_ref, wbig_ref, pos_ref, w1_ref, b1_ref, w2k_ref, w2vT_ref,
                     b2k_ref, b2v_ref, kg_ref, kcmp_ref, vcmpT_ref, act_ref):
    m = kc_ref.shape[1]
    nc = m - 1
    ncp = act_ref.shape[0]
    row_m = lax.broadcasted_iota(jnp.int32, (m, CMP_HIDDEN), 0)
    row_p = lax.broadcasted_iota(jnp.int32, (ncp, HEAD_DIM), 0)
    col_p = lax.broadcasted_iota(jnp.int32, (HEAD_DIM, ncp), 1)
    col_f = lax.broadcasted_iota(jnp.int32, (ncp, HEAD_DIM), 1)
    row_o = lax.broadcasted_iota(jnp.int32, (V_ROWS - HEAD_DIM, ncp), 0)
    act_ref[...] = jnp.zeros(act_ref.shape, F32)
    for which, src_ref in enumerate((kc_ref, vc_ref)):
        p = jnp.dot(src_ref[0], wbig_ref[which], preferred_element_type=F32)
        posc = jnp.dot(pos_ref[which], w1_ref[which], preferred_element_type=F32)[0:1]
        for g in range(NSA_GROUPS):
            top = p[:, (2 * g) * CMP_HIDDEN:(2 * g + 1) * CMP_HIDDEN]
            bot = p[:, (2 * g + 1) * CMP_HIDDEN:(2 * g + 2) * CMP_HIDDEN]
            hid = top + pltpu.roll(bot, m - 1, 0) + (b1_ref[which] + posc)
            act = jnp.where(row_m < nc, _gelu_tanh(hid), 0.0)
            act_ref[CMP_PAD:CMP_PAD + m, :] = act
            ap = act_ref[...].astype(BF16)
            if which == 0:
                kc = jnp.dot(ap, w2k_ref[...], preferred_element_type=F32) + b2k_ref[...]
                ms = jnp.mean(kc * kc, axis=-1, keepdims=True)
                kc = kc * lax.rsqrt(ms + NORM_EPS) * kg_ref[...]
                valid = (row_p >= CMP_PAD) & (row_p < CMP_PAD + nc)
                kcmp_ref[0, g, :, 0:HEAD_DIM] = jnp.where(valid, kc, 0.0).astype(BF16)
                flag = (col_f == 0) & jnp.logical_not(valid)
                kcmp_ref[0, g, :, HEAD_DIM:] = jnp.where(flag, 1.0, 0.0).astype(BF16)
            else:
                vt = lax.dot_general(w2vT_ref[...], ap, (((1,), (1,)), ((), ())),
                                     preferred_element_type=F32) + b2v_ref[...]
                valid = (col_p >= CMP_PAD) & (col_p < CMP_PAD + nc)
                vcmpT_ref[0, g, 0:HEAD_DIM, :] = jnp.where(valid, vt, 0.0).astype(BF16)
                vcmpT_ref[0, g, HEAD_DIM:, :] = jnp.where(row_o == 0, 1.0, 0.0).astype(BF16)


def _compress(kc16, vc16, wbig, pos8, w1, b1, w2k, w2vT, b2k, b2v, kg, *, ncp):
    B, m, _ = kc16.shape
    full = lambda a: pl.BlockSpec(a.shape, lambda b: (0,) * a.ndim)
    return pl.pallas_call(
        _compress_kernel, grid=(B,),
        in_specs=[pl.BlockSpec((1, m, kc16.shape[2]), lambda b: (b, 0, 0)),
                  pl.BlockSpec((1, m, vc16.shape[2]), lambda b: (b, 0, 0)),
                  full(wbig), full(pos8), full(w1), full(b1), full(w2k), full(w2vT),
                  full(b2k), full(b2v), full(kg)],
        out_specs=(pl.BlockSpec((1, NSA_GROUPS, ncp, 2 * HEAD_DIM), lambda b: (b, 0, 0, 0)),
                   pl.BlockSpec((1, NSA_GROUPS, V_ROWS, ncp), lambda b: (b, 0, 0, 0))),
        out_shape=(jax.ShapeDtypeStruct((B, NSA_GROUPS, ncp, 2 * HEAD_DIM), BF16),
                   jax.ShapeDtypeStruct((B, NSA_GROUPS, V_ROWS, ncp), BF16)),
        scratch_shapes=[pltpu.VMEM((ncp, CMP_HIDDEN), F32)],
        compiler_params=_cparams(("parallel",)), name="compress",
    )(kc16, vc16, wbig, pos8, w1, b1, w2k, w2vT, b2k, b2v, kg)


def _flash_update(s_ref, smax_ref, vT, m_ref, acc_ref):
    m_prev = m_ref[...]
    m_new = jnp.maximum(m_prev, smax_ref[...])
    p = jnp.exp2(s_ref[...] - m_new)
    acc_ref[...] = (jnp.exp2(m_prev - m_new) * acc_ref[...]
                    + jnp.dot(vT, p.astype(BF16), preferred_element_type=F32))
    m_ref[...] = m_new


def _attn_kernel(qT_ref, kcmp_ref, vcmpT_ref, bc_ref, ksel_ref, vselT_ref, kwin_ref, vwinT_ref,
                 stab_ref, wtab_ref, gT_ref, o_ref,
                 lc_ref, psum_ref, qaug_ref, qfar_ref, oc_ref, ow_ref, ms_ref, accs_ref, sa_ref, sb_ref, ma_ref, mb_ref):
    qt = pl.program_id(2)
    ncp = kcmp_ref.shape[2]
    ns = qaug_ref.shape[0] - 2 * HEAD_DIM
    nq = NSA_HPG * QT
    t_lane = qt * QT + lax.broadcasted_iota(jnp.int32, (1, QT), 1)

    for r in range(NSA_HPG):
        qaug_ref[0:HEAD_DIM, r * QT:(r + 1) * QT] = qT_ref[0, r * HEAD_DIM:(r + 1) * HEAD_DIM, :]
    flag_row = lax.broadcasted_iota(jnp.int32, (HEAD_DIM, nq), 0) == 0
    qaug_ref[HEAD_DIM:2 * HEAD_DIM, :] = jnp.where(flag_row, NEG_INF, 0.0).astype(BF16)
    qk = qaug_ref[0:2 * HEAD_DIM, :]

    rho = lax.broadcasted_iota(jnp.int32, (ncp, nq), 0)
    band0 = pl.multiple_of(qt * (QT // CMP_STRIDE), V7X_SUBLANES)
    lc_ref[...] = jnp.dot(kcmp_ref[0, 0], qk, preferred_element_type=F32)
    lc_ref[pl.ds(band0, CMP_BAND), :] += bc_ref[0]
    lc = jnp.where(rho < band0 + CMP_BAND, lc_ref[...], NEG_INF)
    e = jnp.exp2(lc - jnp.max(lc, axis=0, keepdims=True))
    oc_aug = jnp.dot(vcmpT_ref[0, 0], e.astype(BF16), preferred_element_type=F32)
    t_q = qt * QT + lax.broadcasted_iota(jnp.int32, (1, nq), 1) % QT
    inv_c = jnp.where(t_q >= CMP_BLOCK - 1, 1.0 / oc_aug[HEAD_DIM:HEAD_DIM + 1, :], 0.0)
    oc_ref[...] = oc_aug[0:HEAD_DIM] * inv_c
    p = e * inv_c
    psum = (p[:, 0:QT] + p[:, QT:2 * QT]) + (p[:, 2 * QT:3 * QT] + p[:, 3 * QT:4 * QT])
    for c in range(QT // V7X_LANES):
        psum_ref[c] = psum[:, c * V7X_LANES:(c + 1) * V7X_LANES]

    def strided_sum(c):
        acc = psum_ref[c, pl.ds(CMP_PAD - 1, ns, stride=4), :]
        for k in range(1, 5):
            acc = acc + psum_ref[c, pl.ds(CMP_PAD - 1 + k, ns, stride=4), :]
        return acc

    imp = jnp.concatenate([strided_sum(c) for c in range(QT // V7X_LANES)], axis=1)
    jrow = lax.broadcasted_iota(jnp.int32, (ns, QT), 0)
    cur = t_lane // SEL_BLOCK
    forced = (jrow == 0) | (jrow == cur) | (jrow == cur - 1)
    live = jrow * SEL_BLOCK <= t_lane
    sel = forced & live
    score = jnp.where(live, jnp.where(forced, -jnp.inf, imp), NEG_INF)
    jrow_f = jrow.astype(F32)

    def col_reduce(x, pair, reduce):
        parts = [x[i * V7X_SUBLANES:(i + 1) * V7X_SUBLANES, :] for i in range(x.shape[0] // V7X_SUBLANES)]
        while len(parts) > 1:
            parts = [pair(parts[i], parts[i + 1]) for i in range(0, len(parts) - 1, 2)] + parts[len(parts) & ~1:]
        return reduce(parts[0], axis=0, keepdims=True)

    for _ in range(min(SEL_TOPN, ns) - 3):
        mx = col_reduce(score, jnp.maximum, jnp.max)
        idx = col_reduce(jnp.where(score == mx, jrow_f, float(ns)), jnp.minimum, jnp.min)
        hit = jrow_f == idx
        sel = sel | hit
        score = jnp.where(hit, -jnp.inf, score)
    negmask = jnp.where(sel, 0.0, NEG_INF).astype(BF16)
    negfar = jnp.where(sel & (jrow < (qt - 1) * (QT // SEL_BLOCK)), 0.0, NEG_INF).astype(BF16)
    qfar_ref[0:2 * HEAD_DIM, :] = qaug_ref[0:2 * HEAD_DIM, :]
    for r in range(NSA_HPG):
        qaug_ref[2 * HEAD_DIM:, r * QT:(r + 1) * QT] = negmask
        qfar_ref[2 * HEAD_DIM:, r * QT:(r + 1) * QT] = negfar

    w0 = pl.multiple_of(qt * QT, QT)
    sw = jnp.dot(kwin_ref[0, 0, pl.ds(w0, WIN_KEYS), :], qk, preferred_element_type=F32) + wtab_ref[0]
    pw = jnp.exp2(sw - jnp.max(sw, axis=0, keepdims=True))
    vw = jnp.concatenate([vwinT_ref[0, 0, qt + j] for j in range(WIN_KEYS // QT)], axis=1)
    ow_aug = jnp.dot(vw, pw.astype(BF16), preferred_element_type=F32)
    ow_ref[...] = ow_aug[0:HEAD_DIM] * (1.0 / ow_aug[HEAD_DIM:HEAD_DIM + 1, :])

    n0 = pl.multiple_of((ATT_PAD // QT - 1 + qt) * QT, QT)
    sn = jnp.dot(ksel_ref[0, 0, pl.ds(n0, NEAR_KEYS), :], qaug_ref[...],
                 preferred_element_type=F32) + stab_ref[0]
    mn = jnp.max(sn, axis=0, keepdims=True)
    pn = jnp.exp2(sn - mn)
    vn = jnp.concatenate([vselT_ref[0, 0, ATT_PAD // QT - 1 + qt + j] for j in range(NEAR_KEYS // QT)], axis=1)
    ms_ref[...] = mn
    accs_ref[...] = jnp.dot(vn, pn.astype(BF16), preferred_element_type=F32)

    tiles_per_far = FAR_KEYS // QT
    n_far = (qt - 1 + tiles_per_far - 1) // tiles_per_far
    last_far = (ksel_ref.shape[2] - ATT_PAD) // FAR_KEYS - 1

    def far_logits(g, dst_ref, dmax_ref):
        r0 = pl.multiple_of(ATT_PAD + jnp.minimum(g, last_far) * FAR_KEYS, FAR_KEYS)
        s = jnp.dot(ksel_ref[0, 0, pl.ds(r0, FAR_KEYS), :], qfar_ref[...], preferred_element_type=F32)
        dst_ref[...] = s
        dmax_ref[...] = jnp.max(s, axis=0, keepdims=True)

    def far_values(g):
        t0 = (ATT_PAD + g * FAR_KEYS) // QT
        return jnp.concatenate([vselT_ref[0, 0, t0 + j] for j in range(tiles_per_far)], axis=1)

    far_logits(0, sa_ref, ma_ref)

    def far_body(j, carry):
        far_logits(2 * j + 1, sb_ref, mb_ref)
        _flash_update(sa_ref, ma_ref, far_values(2 * j), ms_ref, accs_ref)
        far_logits(2 * j + 2, sa_ref, ma_ref)
        _flash_update(sb_ref, mb_ref, far_values(jnp.minimum(2 * j + 1, last_far)), ms_ref, accs_ref)
        return carry

    lax.fori_loop(0, (n_far + 1) // 2, far_body, 0)

    o_s = accs_ref[0:HEAD_DIM, :] * (1.0 / accs_ref[HEAD_DIM:HEAD_DIM + 1, :])
    for r in range(NSA_HPG):
        cols = slice(r * QT, (r + 1) * QT)
        o = (gT_ref[0, 3 * r:3 * r + 1, :] * oc_ref[:, cols] + gT_ref[0, 3 * r + 1:3 * r + 2, :] * o_s[:, cols]
             + gT_ref[0, 3 * r + 2:3 * r + 3, :] * ow_ref[:, cols])
        o_ref[0, :, r * HEAD_DIM:(r + 1) * HEAD_DIM] = o.T


def _attn(qT, kcmp, vcmpT, bc, ksel, vT5, kwin, stab, wtab, gT):
    B, _, T = qT.shape
    ncp = kcmp.shape[2]
    ns = T // SEL_BLOCK
    nq = NSA_HPG * QT
    tp = T + ATT_PAD
    nt = tp // QT
    grid = (B, NSA_GROUPS, T // QT)

    def held(shape, index_map):
        return pl.BlockSpec(shape, index_map, pipeline_mode=pl.Buffered(1))

    in_specs = [
        pl.BlockSpec((1, NSA_HPG * HEAD_DIM, QT), lambda b, g, q: (b, g, q)),
        held((1, 1, ncp, 2 * HEAD_DIM), lambda b, g, q: (b, g, 0, 0)),
        held((1, 1, V_ROWS, ncp), lambda b, g, q: (b, g, 0, 0)),
        held((1, CMP_BAND, nq), lambda b, g, q: (g, 0, 0)),
        pl.BlockSpec((1, 1, tp, 2 * HEAD_DIM + ns), lambda b, g, q: (b, g, 0, 0)),
        pl.BlockSpec((1, 1, nt, V_ROWS, QT), lambda b, g, q: (b, g, 0, 0, 0)),
        pl.BlockSpec((1, 1, tp, 2 * HEAD_DIM), lambda b, g, q: (b, g, 0, 0)),
        pl.BlockSpec((1, 1, nt, V_ROWS, QT), lambda b, g, q: (b, NSA_GROUPS + g, 0, 0, 0)),
        held((1, NEAR_KEYS, nq), lambda b, g, q: (g, 0, 0)),
        held((1, WIN_KEYS, nq), lambda b, g, q: (g, 0, 0)),
        pl.BlockSpec((1, GATE_ROWS, QT), lambda b, g, q: (b, g, q)),
    ]
    scratch = [
        pltpu.VMEM((ncp, nq), F32),
        pltpu.VMEM((QT // V7X_LANES, ncp, V7X_LANES), F32),
        pltpu.VMEM((2 * HEAD_DIM + ns, nq), BF16),
        pltpu.VMEM((2 * HEAD_DIM + ns, nq), BF16),
        pltpu.VMEM((HEAD_DIM, nq), F32),
        pltpu.VMEM((HEAD_DIM, nq), F32),
        pltpu.VMEM((1, nq), F32),
        pltpu.VMEM((V_ROWS, nq), F32),
        pltpu.VMEM((FAR_KEYS, nq), F32),
        pltpu.VMEM((FAR_KEYS, nq), F32),
        pltpu.VMEM((1, nq), F32), pltpu.VMEM((1, nq), F32),
    ]
    return pl.pallas_call(
        _attn_kernel, grid=grid, in_specs=in_specs,
        out_specs=pl.BlockSpec((1, QT, NSA_HPG * HEAD_DIM), lambda b, g, q: (b, q, g)),
        out_shape=jax.ShapeDtypeStruct((B, T, D_NSA), F32),
        scratch_shapes=scratch,
        compiler_params=_cparams(("parallel", "parallel", "arbitrary")), name="attn",
    )(qT, kcmp, vcmpT, bc, ksel, vT5, kwin, vT5, stab, wtab, gT)


def _rwkv_tokens(rw_ref, halo_ref, mu_ref, w0_ref, w2_ref, a0_ref, a2_ref, g2_ref, kk_ref, ka_ref,
                 rk_ref, bd_ref, ext_ref):
    tm = rw_ref.shape[1]
    first_tile = pl.program_id(1) == 0
    ext_ref[0:V7X_SUBLANES, :] = jnp.where(first_tile, 0.0, halo_ref[0])
    ext_ref[V7X_SUBLANES:, :] = rw_ref[0]
    cur = rw_ref[0]
    prev = ext_ref[pl.ds(V7X_SUBLANES - 1, tm), :]
    mixed = cur + (prev - cur) * mu_ref[...]
    c = D_RWKV
    r = mixed[:, 0:c]
    k = mixed[:, c:2 * c]
    v = mixed[:, 2 * c:3 * c]
    xw = mixed[:, 3 * c:3 * c + LORA_W]
    xa = mixed[:, 3 * c + LORA_W:3 * c + LORA_W + LORA_A]
    xg = mixed[:, 3 * c + LORA_W + LORA_A:]

    z = -(w0_ref[...] + _bdot(jnp.tanh(xw), w2_ref[...]))
    softplus = jnp.maximum(z, 0.0) + jnp.log1p(jnp.exp(-jnp.abs(z)))
    w = -softplus - 0.5
    a = jax.nn.sigmoid(a0_ref[...] + _bdot(xa, a2_ref[...]))
    g = _bdot(jax.nn.sigmoid(xg), g2_ref[...])

    kk = k * kk_ref[...]
    n2 = _dot_exact_rhs(kk * kk, bd_ref[...])
    kkn = kk / jnp.maximum(jnp.sqrt(n2), 1e-12)
    k2 = k * (1.0 + (a - 1.0) * ka_ref[...])
    bonus = _dot_exact_rhs(r * k2 * rk_ref[...], bd_ref[...]) * v

    lw = -jnp.exp(w)
    return dict(r=r, lw=lw, k=k2, v=v, kk=kkn, b=kkn * a, g=g, bonus=bonus)


WKV_BATCH = 8
WKV_GROUP = 4
WKV_GW = WKV_GROUP * HEAD_DIM


def _wkv_kernel(rw_ref, halo_ref, mu_ref, w0_ref, w2_ref, a0_ref, a2_ref, g2_ref, kk_ref, ka_ref, rk_ref, bd_ref,
                y_ref, g_ref, bonus_ref, s_ref, ext_ref, *, chunks):
    L = WKV_CHUNK
    tok = _rwkv_tokens(rw_ref, halo_ref, mu_ref, w0_ref, w2_ref, a0_ref, a2_ref, g2_ref, kk_ref, ka_ref,
                       rk_ref, bd_ref, ext_ref)
    g_ref[0] = tok["g"]
    bonus_ref[0] = tok["bonus"]
    assert L == HEAD_DIM
    gw = WKV_GW
    n_groups = RWKV_HEADS // WKV_GROUP

    @pl.when(pl.program_id(1) == 0)
    def _():
        s_ref[...] = jnp.zeros(s_ref.shape, F32)

    ti = lax.broadcasted_iota(jnp.int32, (L, L), 0)
    tj = lax.broadcasted_iota(jnp.int32, (L, L), 1)
    tri = jnp.where(ti >= tj, 1.0, 0.0).astype(BF16)
    row = lax.broadcasted_iota(jnp.int32, (L, gw), 0)
    col = lax.broadcasted_iota(jnp.int32, (L, gw), 1) % L
    low_strict = col < row
    low_incl = col <= row
    eye_sbs = jnp.where(col == row, 1.0, 0.0)
    brow = lax.broadcasted_iota(jnp.int32, (gw, gw), 0) // L
    bcol = lax.broadcasted_iota(jnp.int32, (gw, gw), 1) // HEAD_DIM
    same_head = brow == bcol

    def bd_rows(x):
        xb = x.astype(BF16)
        return jnp.where(same_head, jnp.concatenate([xb] * WKV_GROUP, axis=0), jnp.zeros((), BF16))

    def mm(a, b_bf16):
        return jnp.dot(a.astype(BF16), b_bf16, preferred_element_type=F32)

    def tn(a, b):
        return lax.dot_general(a.astype(BF16), b.astype(BF16), (((0,), (0,)), ((), ())),
                               preferred_element_type=F32)

    pre, w2, c2, m_lr, d_sbs = {}, {}, {}, {}, {}

    def grp(c, g, name):
        return pre[c][name][:, g * gw:(g + 1) * gw]

    for c0 in range(0, chunks, WKV_BATCH):
        batch = range(c0, min(c0 + WKV_BATCH, chunks))
        inst = [(c, g) for c in batch for g in range(n_groups)]
        for c in batch:
            rows = slice(c * L, (c + 1) * L)
            lw = tok["lw"][rows]
            cs = _dot_exact_rhs_left(tri, lw)
            c_last = cs[L - 1:L, :]
            e_nc = jnp.exp(-cs)
            e_lc = jnp.exp(c_last - cs)
            kk = tok["kk"][rows]
            b = tok["b"][rows]
            k = tok["k"][rows]
            pre[c] = dict(a=-kk * jnp.exp(cs - lw), r=tok["r"][rows] * jnp.exp(cs), bh=b * e_nc, kh=k * e_nc,
                          be=b * e_lc, ke=k * e_lc, v=tok["v"][rows], e_last=jnp.exp(c_last))

        t_all = {}
        for (c, g) in inst:
            lhs = jnp.concatenate([grp(c, g, "a"), grp(c, g, "r")], axis=0).astype(BF16)
            rhs = jnp.concatenate([bd_rows(grp(c, g, "bh")), bd_rows(grp(c, g, "kh"))], axis=0)
            t_all[c, g] = lax.dot_general(lhs, rhs, (((1,), (1,)), ((), ())), preferred_element_type=F32)
        n_m = {i: jnp.where(low_strict, t_all[i][0:L, 0:gw], 0.0) for i in inst}
        tak = {i: jnp.where(low_strict, t_all[i][0:L, gw:], 0.0) for i in inst}
        trb = {i: jnp.where(low_incl, t_all[i][L:, 0:gw], 0.0) for i in inst}
        trk = {i: jnp.where(low_incl, t_all[i][L:, gw:], 0.0) for i in inst}
        z = {i: eye_sbs + n_m[i] for i in inst}
        pw = dict(n_m)
        for _ in range(int(math.log2(L)) - 1):
            pw = {i: mm(pw[i], bd_rows(pw[i])) for i in inst}
            z = {i: z[i] + mm(z[i], bd_rows(pw[i])) for i in inst}
        vbd = {(c, g): bd_rows(grp(c, g, "v")) for (c, g) in inst}
        w1 = {(c, g): mm(z[c, g], bd_rows(grp(c, g, "a"))) for (c, g) in inst}
        tv = {i: mm(tak[i], vbd[i]) for i in inst}
        c1 = {i: mm(z[i], bd_rows(tv[i])) for i in inst}
        w2.update({(c, g): grp(c, g, "r") + mm(trb[c, g], bd_rows(w1[c, g])) for (c, g) in inst})
        c2.update({i: mm(trb[i], bd_rows(c1[i])) + mm(trk[i], vbd[i]) for i in inst})
        m_lr.update({(c, g): jnp.where(same_head, tn(w1[c, g], grp(c, g, "be")), 0.0).astype(BF16)
                     for (c, g) in inst})
        for (c, g) in inst:
            full = jnp.where(same_head,
                             tn(jnp.concatenate([c1[c, g], grp(c, g, "v")], axis=0),
                                jnp.concatenate([grp(c, g, "be"), grp(c, g, "ke")], axis=0)), 0.0)
            d_sbs[c, g] = ((full[0:L] + full[L:2 * L]) + (full[2 * L:3 * L] + full[3 * L:4 * L]))

    for g in range(n_groups):
        s = s_ref[:, g * gw:(g + 1) * gw]
        for c in range(chunks):
            g_bd = jnp.where(same_head, jnp.concatenate([s.T.astype(BF16)] * WKV_GROUP, axis=1),
                             jnp.zeros((), BF16))
            y_ref[0, c * L:(c + 1) * L, g * gw:(g + 1) * gw] = mm(w2[c, g], g_bd) + c2[c, g]
            s = s * grp(c, g, "e_last") + mm(s, m_lr[c, g]) + d_sbs[c, g]
        s_ref[:, g * gw:(g + 1) * gw] = s


def _dot_exact_rhs_left(m_bf16, x):
    hi, mid, lo = _split3(x)
    return (jnp.dot(m_bf16, hi, preferred_element_type=F32)
            + jnp.dot(m_bf16, mid, preferred_element_type=F32)
            + jnp.dot(m_bf16, lo, preferred_element_type=F32))


def _wkv(rw, mu, w0, w2, a0, a2, g2, k_k, k_a, r_k, bd, *, chunks):
    B, T, C = rw.shape
    rows = chunks * WKV_CHUNK
    hb = rows // V7X_SUBLANES
    full = lambda a: pl.BlockSpec(a.shape, lambda bb, c: (0,) * a.ndim)
    tok = pl.BlockSpec((1, rows, D_RWKV), lambda bb, c: (bb, c, 0))
    return pl.pallas_call(
        functools.partial(_wkv_kernel, chunks=chunks), grid=(B, T // rows),
        in_specs=[pl.BlockSpec((1, rows, C), lambda bb, c: (bb, c, 0)),
                  pl.BlockSpec((1, V7X_SUBLANES, C), lambda bb, c: (bb, jnp.maximum(c * hb - 1, 0), 0)),
                  full(mu), full(w0), full(w2), full(a0), full(a2), full(g2), full(k_k), full(k_a),
                  full(r_k), full(bd)],
        out_specs=(tok,) * 3,
        out_shape=(jax.ShapeDtypeStruct((B, T, D_RWKV), F32),) * 3,
        scratch_shapes=[pltpu.VMEM((HEAD_DIM, D_RWKV), F32), pltpu.VMEM((rows + V7X_SUBLANES, C), F32)],
        compiler_params=_cparams(("parallel", "arbitrary")), name="wkv",
    )(rw, rw, mu, w0, w2, a0, a2, g2, k_k, k_a, r_k, bd)


def _mix_kernel(x_ref, on_ref, y_ref, bonus_ref, g_ref, bd_ref, lnw_ref, lnb_ref, wo_ref, g2_ref, x1_ref, h2_ref):
    y = y_ref[...]
    mu = _dot_exact_rhs(y, bd_ref[...]) * (1.0 / HEAD_DIM)
    yc = y - mu
    var = _dot_exact_rhs(yc * yc, bd_ref[...]) * (1.0 / HEAD_DIM)
    yn = yc * lax.rsqrt(var + GN_EPS) * lnw_ref[...] + lnb_ref[...]
    orw = (yn + bonus_ref[...]) * g_ref[...]
    x1 = (x_ref[...] + _bdot(on_ref[...], wo_ref[0:D_NSA, :]) + _bdot(orw, wo_ref[D_NSA:, :]))
    x1_ref[...] = x1
    ms = jnp.mean(x1 * x1, axis=-1, keepdims=True)
    h2_ref[...] = (x1 * lax.rsqrt(ms + NORM_EPS) * g2_ref[...]).astype(BF16)


def _mix(x2d, on2d, y2d, bonus2d, g2d, bd, lnw, lnb, wo, g2, *, tm):
    n, d = x2d.shape
    tok = lambda w: pl.BlockSpec((tm, w), lambda i: (i, 0))
    full = lambda a: pl.BlockSpec(a.shape, lambda i: (0,) * a.ndim)
    return pl.pallas_call(
        _mix_kernel, grid=(n // tm,),
        in_specs=[tok(d), tok(D_NSA), tok(D_RWKV), tok(D_RWKV), tok(D_RWKV), full(bd), full(lnw), full(lnb),
                  full(wo), full(g2)],
        out_specs=(tok(d), tok(d)),
        out_shape=(jax.ShapeDtypeStruct((n, d), F32), jax.ShapeDtypeStruct((n, d), BF16)),
        compiler_params=_cparams(("parallel",)), name="mix",
    )(x2d, on2d, y2d, bonus2d, g2d, bd, lnw, lnb, wo, g2)


FFN_HALO = 16


def _ffn_kernel(x1_ref, h2_ref, halo_ref, wv_ref, wg_ref, cwv_ref, cwg_ref, cbv_ref, cbg_ref, wd_ref,
                o_ref, hext_ref, extv_ref, extg_ref, act_ref, *, tiles_per_seq, ft):
    tm = h2_ref.shape[0]
    dff = wd_ref.shape[0]
    seq_start = (pl.program_id(0) % tiles_per_seq) == 0
    hext_ref[0:FFN_HALO, :] = jnp.where(seq_start, jnp.zeros((), BF16), halo_ref[...])
    hext_ref[FFN_HALO:, :] = h2_ref[...]

    for j in range(dff // ft):
        cols = slice(j * ft, (j + 1) * ft)

        def conv_branch(w_ref, cw_ref, cb_ref, ext_ref):
            ext_ref[j % 2] = jnp.dot(hext_ref[...], w_ref[:, cols], preferred_element_type=F32)
            out = cb_ref[:, cols] + ext_ref[j % 2, pl.ds(FFN_HALO, tm), :] * cw_ref[CONV_W - 1:CONV_W, cols]
            for i in range(CONV_W - 1):
                back = CONV_W - 1 - i
                out = out + ext_ref[j % 2, pl.ds(FFN_HALO - back, tm), :] * cw_ref[i:i + 1, cols]
            return out

        u_val = conv_branch(wv_ref, cwv_ref, cbv_ref, extv_ref)
        u_gate = conv_branch(wg_ref, cwg_ref, cbg_ref, extg_ref)
        act_ref[:, cols] = ((u_gate * jax.nn.sigmoid(u_gate)) * u_val).astype(BF16)
    o_ref[...] = x1_ref[...] + jnp.dot(act_ref[...], wd_ref[...], preferred_element_type=F32)


def _ffn(x1, h2, wv, wg, cwv, cwg, cbv, cbg, wd, *, tm, ft, seq_len):
    n, d = x1.shape
    dff = wv.shape[1]
    hb = tm // FFN_HALO
    kern = functools.partial(_ffn_kernel, tiles_per_seq=seq_len // tm, ft=ft)
    resident = lambda a: pl.BlockSpec(a.shape, lambda i: (0,) * a.ndim, pipeline_mode=pl.Buffered(1))
    return pl.pallas_call(
        kern, grid=(n // tm,),
        in_specs=[pl.BlockSpec((tm, d), lambda i: (i, 0)),
                  pl.BlockSpec((tm, d), lambda i: (i, 0)),
                  pl.BlockSpec((FFN_HALO, d), lambda i: (jnp.maximum(i * hb - 1, 0), 0)),
                  resident(wv), resident(wg), resident(cwv), resident(cwg), resident(cbv), resident(cbg),
                  resident(wd)],
        out_specs=pl.BlockSpec((tm, d), lambda i: (i, 0)),
        out_shape=jax.ShapeDtypeStruct((n, d), F32),
        scratch_shapes=[pltpu.VMEM((tm + FFN_HALO, d), BF16),
                        pltpu.VMEM((2, tm + FFN_HALO, ft), F32), pltpu.VMEM((2, tm + FFN_HALO, ft), F32),
                        pltpu.VMEM((tm, dff), BF16)],
        compiler_params=_cparams(("parallel",)), name="ffn",
    )(x1, h2, h2, wv, wg, cwv, cwg, cbv, cbg, wd)


def _t5_bucket(dist):
    n = np.maximum(dist, 0)
    max_exact = N_BUCKETS // 2
    nf = np.maximum(n, 1).astype(np.float32)
    large = max_exact + (np.log(nf / np.float32(max_exact)) / np.float32(math.log(MAX_DISTANCE / max_exact))
                         * np.float32(N_BUCKETS - max_exact)).astype(np.int32)
    large = np.minimum(large, N_BUCKETS - 1)
    return np.where(n < max_exact, n, large)


def _toeplitz_kernel(v_ref, o_ref, *, step):
    rows, width = o_ref.shape[1], o_ref.shape[2]
    x = jnp.broadcast_to(v_ref[0], (rows, v_ref.shape[2]))
    o_ref[0] = pltpu.roll(x, 0, 1, stride=step, stride_axis=0)[:, :width]


def _toeplitz(v, rows, step):
    period = v.shape[2]
    return pl.pallas_call(
        functools.partial(_toeplitz_kernel, step=step), grid=(NSA_GROUPS, NSA_HPG),
        in_specs=[pl.BlockSpec((1, 1, period), lambda g, r: (g * NSA_HPG + r, 0, 0))],
        out_specs=pl.BlockSpec((1, rows, QT), lambda g, r: (g, 0, r)),
        out_shape=jax.ShapeDtypeStruct((NSA_GROUPS, rows, NSA_HPG * QT), F32),
        compiler_params=_cparams(("parallel", "parallel")), name="toeplitz",
    )(v)


def _bias_tables(rel_bias):
    rel = (rel_bias - rel_bias[N_BUCKETS - 1][None, :]) * LOG2E

    def table(rows, step, d0, d_hi):
        period = step * rows + QT
        i = np.arange(period)
        d = d0 + np.where(i < QT, i, i - period)
        onehot = (_t5_bucket(d)[:, None] == np.arange(N_BUCKETS)[None, :]).astype(np.float32)
        v = jnp.dot(jnp.asarray(onehot), rel, precision=lax.Precision.HIGHEST)
        v = jnp.where(jnp.asarray((d >= 0) & (d < d_hi))[:, None], v, NEG_INF).T
        return _toeplitz(v.reshape(NSA_HEADS, 1, period), rows, step)

    no_limit = 1 << 30
    stab = table(NEAR_KEYS, 1, QT, no_limit)
    wtab = table(WIN_KEYS, 1, WINDOW, WINDOW)
    bc = table(CMP_BAND, CMP_STRIDE, CMP_STRIDE * CMP_PAD - CMP_BLOCK + 1, no_limit)
    return stab, wtab, bc


def _compress_weights(w1):
    half = CMP_BLOCK // 2
    w1r = w1.reshape(2, half, HEAD_DIM, CMP_HIDDEN).transpose(1, 2, 0, 3)
    big = jnp.zeros((half, NSA_GROUPS, HEAD_DIM, NSA_GROUPS, 2, CMP_HIDDEN), w1.dtype)
    for g in range(NSA_GROUPS):
        big = big.at[:, g, :, g, :, :].set(w1r)
    return big.reshape(half * NSA_GROUPS * HEAD_DIM, NSA_GROUPS * 2 * CMP_HIDDEN)


def kernel(x, norm1_g, w_in, q_norm_g, k_norm_g, cmp_pos, cmp_w1, cmp_b1, cmp_w2, cmp_b2, rel_bias, rwkv_mu,
           w0, w2, a0, a2, g2, k_k, k_a, r_k, ln_x_w, ln_x_b, w_out, norm2_g, ffn_up, conv_w, conv_b, ffn_down):
    B, T, D = x.shape
    depth = w_in.shape[0]
    d_ff = ffn_down.shape[1]
    assert T % 2048 == 0 and D_NSA + 6 * D_KV + 3 * NSA_HEADS + D_RWKV_IN == w_in.shape[2]
    ncp = T // CMP_STRIDE + V7X_LANES
    stab, wtab, bc = _bias_tables(rel_bias)
    ii = jnp.arange(D_RWKV)
    bd = (ii[:, None] // HEAD_DIM == ii[None, :] // HEAD_DIM).astype(BF16)

    for l in range(depth):
        wi = w_in[l]
        o = D_NSA
        q_w, kc_w, vc_w, ksl_w, vsl_w, kwn_w, vwn_w = (
            wi[:, 0:o], wi[:, o:o + D_KV], wi[:, o + D_KV:o + 2 * D_KV], wi[:, o + 2 * D_KV:o + 3 * D_KV],
            wi[:, o + 3 * D_KV:o + 4 * D_KV], wi[:, o + 4 * D_KV:o + 5 * D_KV], wi[:, o + 5 * D_KV:o + 6 * D_KV])
        gl_w = wi[:, o + 6 * D_KV:o + 6 * D_KV + 3 * NSA_HEADS]
        rw_w = wi[:, o + 6 * D_KV + 3 * NSA_HEADS:]
        wstd = jnp.concatenate([kc_w, vc_w, ksl_w, kwn_w, rw_w], axis=1).astype(BF16)
        gl_rows = gl_w.T.reshape(NSA_GROUPS, 3 * NSA_HPG, D)
        gl_rows = jnp.pad(gl_rows, ((0, 0), (0, GATE_ROWS - 3 * NSA_HPG), (0, 0))).reshape(-1, D)
        wt = jnp.concatenate([q_w.T, vsl_w.T, vwn_w.T, gl_rows], axis=0).astype(BF16)
        qg = q_norm_g[l].reshape(HEAD_DIM, 1)
        kg = jnp.stack([jnp.tile(k_norm_g[l, 1], NSA_GROUPS), jnp.tile(k_norm_g[l, 2], NSA_GROUPS)])

        qT, kc, vc, ksel, kwin, vT5, gT, rw = _proj(x, norm1_g[l].reshape(1, D), wstd, wt, qg, kg, tm=512)

        wbig = jnp.stack([_compress_weights(cmp_w1[l, 0]), _compress_weights(cmp_w1[l, 1])]).astype(BF16)
        pos8 = jnp.pad(cmp_pos[l].reshape(2, 1, CMP_BLOCK * HEAD_DIM),
                       ((0, 0), (0, V7X_SUBLANES - 1), (0, 0))).astype(BF16)
        rows16 = CMP_STRIDE * D_KV
        kcmp, vcmpT = _compress(
            kc.reshape(B, T // CMP_STRIDE, rows16), vc.reshape(B, T // CMP_STRIDE, rows16), wbig, pos8,
            cmp_w1[l].astype(BF16), cmp_b1[l].reshape(2, 1, CMP_HIDDEN), cmp_w2[l, 0].astype(BF16),
            cmp_w2[l, 1].T.astype(BF16), cmp_b2[l, 0].reshape(1, HEAD_DIM), cmp_b2[l, 1].reshape(HEAD_DIM, 1),
            k_norm_g[l, 0].reshape(1, HEAD_DIM), ncp=ncp)

        def front_pad(k):
            flag = (jnp.arange(k.shape[-1]) == HEAD_DIM).astype(BF16)
            return jnp.concatenate([jnp.broadcast_to(flag, k.shape[:2] + (ATT_PAD, k.shape[-1])), k], axis=2)

        vT5p = jnp.pad(vT5, ((0, 0), (0, 0), (ATT_PAD // QT, 0), (0, 0), (0, 0)))
        o_nsa = _attn(qT, kcmp, vcmpT, bc, front_pad(ksel), vT5p, front_pad(kwin), stab, wtab, gT)

        row = lambda a: a.reshape(1, -1)
        y, g, bonus = _wkv(
            rw, row(rwkv_mu[l]), row(w0[l]), w2[l].astype(BF16), row(a0[l]), a2[l].astype(BF16),
            g2[l].astype(BF16), row(k_k[l]), row(k_a[l]), row(r_k[l]), bd, chunks=8)

        n = B * T
        x1, h2 = _mix(x.reshape(n, D), o_nsa.reshape(n, D_NSA), y.reshape(n, D_RWKV),
                      bonus.reshape(n, D_RWKV), g.reshape(n, D_RWKV), bd, row(ln_x_w[l]), row(ln_x_b[l]),
                      w_out[l].astype(BF16), norm2_g[l].reshape(1, D), tm=512)

        up = ffn_up[l].astype(BF16)
        x = _ffn(x1, h2, up[:, :d_ff], up[:, d_ff:], conv_w[l][:, :d_ff], conv_w[l][:, d_ff:],
                 conv_b[l][:d_ff].reshape(1, -1), conv_b[l][d_ff:].reshape(1, -1),
                 ffn_down[l].astype(BF16), tm=512, ft=256, seq_len=T).reshape(B, T, D)
    return x
```

```python
import functools
import math

import jax
import jax.numpy as jnp
import numpy as np
from jax import lax
from jax.experimental import pallas as pl
from jax.experimental.pallas import tpu as pltpu

F32 = jnp.float32
BF16 = jnp.bfloat16

V7X_LANES = 128
V7X_SUBLANES = 8
V7X_VMEM_LIMIT_BYTES = 56 * 1024 * 1024

HEAD_DIM = 64
NSA_HEADS = 8
NSA_GROUPS = 2
NSA_HPG = NSA_HEADS // NSA_GROUPS
RWKV_HEADS = 8
D_NSA = NSA_HEADS * HEAD_DIM
D_RWKV = RWKV_HEADS * HEAD_DIM
D_KV = NSA_GROUPS * HEAD_DIM
CMP_BLOCK = 32
CMP_STRIDE = 16
CMP_HIDDEN = 128
SEL_BLOCK = 64
SEL_TOPN = 16
WINDOW = 512
N_BUCKETS = 32
MAX_DISTANCE = 128
LORA_W = 64
LORA_A = 64
LORA_G = 128
D_RWKV_IN = 3 * D_RWKV + LORA_W + LORA_A + LORA_G
CONV_W = 3
NORM_EPS = 1e-6
GN_EPS = 64e-5
NEG_INF = -1e30
FORCE_SCORE = 1e9

QT = 256
CMP_PAD = 8
CMP_BAND = QT // CMP_STRIDE + 8
LOG2E = math.log2(math.e)
Q_SCALE = HEAD_DIM ** -0.5 * LOG2E
V_ROWS = HEAD_DIM + 16
ATT_PAD = WINDOW
NEAR_KEYS = 2 * QT
FAR_KEYS = QT
FAR_UNROLL = 4
WIN_KEYS = WINDOW + QT
WKV_CHUNK = 64
GATE_ROWS = 16
N_T_ROWS = D_NSA + 2 * D_KV + NSA_GROUPS * GATE_ROWS
N_STD_COLS = 4 * D_KV + D_RWKV_IN


def _cparams(sem):
    return pltpu.CompilerParams(dimension_semantics=sem, vmem_limit_bytes=V7X_VMEM_LIMIT_BYTES)


def _bdot(a, b):
    return jnp.dot(a.astype(BF16), b.astype(BF16), preferred_element_type=F32)


def _bdot_nt(a, b):
    return lax.dot_general(a.astype(BF16), b.astype(BF16), (((1,), (1,)), ((), ())),
                           preferred_element_type=F32)


def _split3(x):
    hi = x.astype(BF16)
    r1 = x - hi.astype(F32)
    mid = r1.astype(BF16)
    lo = (r1 - mid.astype(F32)).astype(BF16)
    return hi, mid, lo


def _dot_exact_rhs(x, m_bf16):
    hi = x.astype(BF16)
    lo = (x - hi.astype(F32)).astype(BF16)
    return jnp.dot(hi, m_bf16, preferred_element_type=F32) + jnp.dot(lo, m_bf16, preferred_element_type=F32)


def _proj_kernel(x_ref, g1_ref, wstd_ref, wt_ref, qg_ref, kg_ref,
                 qT_ref, kc_ref, vc_ref, ksel_ref, kwin_ref, vT_ref, gT_ref, rw_ref):
    tm = x_ref.shape[1]
    ns = ksel_ref.shape[3] - 2 * HEAD_DIM
    x = x_ref[0]
    ms = jnp.mean(x * x, axis=-1, keepdims=True)
    h = (x * lax.rsqrt(ms + NORM_EPS) * g1_ref[...]).astype(BF16)

    std = jnp.dot(h, wstd_ref[...], preferred_element_type=F32)
    kc_ref[0] = std[:, 0:D_KV].astype(BF16)
    vc_ref[0] = std[:, D_KV:2 * D_KV].astype(BF16)
    rw_ref[0] = std[:, 4 * D_KV:]

    lane = lax.broadcasted_iota(jnp.int32, (tm, D_KV), 1)
    first = lane < HEAD_DIM

    def group_rmsnorm(k, gain):
        k2 = k * k
        s0 = jnp.sum(jnp.where(first, k2, 0.0), axis=-1, keepdims=True)
        s1 = jnp.sum(jnp.where(first, 0.0, k2), axis=-1, keepdims=True)
        ms_g = jnp.where(first, s0, s1) * (1.0 / HEAD_DIM)
        return k * lax.rsqrt(ms_g + NORM_EPS) * gain

    ksl = group_rmsnorm(std[:, 2 * D_KV:3 * D_KV], kg_ref[0:1, :]).astype(BF16)
    kwn = group_rmsnorm(std[:, 3 * D_KV:4 * D_KV], kg_ref[1:2, :]).astype(BF16)

    tok = pl.program_id(1) * tm + lax.broadcasted_iota(jnp.int32, (tm, ns), 0)
    blk = lax.broadcasted_iota(jnp.int32, (tm, ns), 1)
    onehot = jnp.where((tok // SEL_BLOCK) == blk, 1.0, 0.0).astype(BF16)
    zeros = jnp.zeros((tm, HEAD_DIM), BF16)
    for g in range(NSA_GROUPS):
        ksel_ref[0, g, :, 0:HEAD_DIM] = ksl[:, g * HEAD_DIM:(g + 1) * HEAD_DIM]
        ksel_ref[0, g, :, HEAD_DIM:2 * HEAD_DIM] = zeros
        ksel_ref[0, g, :, 2 * HEAD_DIM:] = onehot
        kwin_ref[0, g, :, 0:HEAD_DIM] = kwn[:, g * HEAD_DIM:(g + 1) * HEAD_DIM]
        kwin_ref[0, g, :, HEAD_DIM:] = zeros

    tr = lax.dot_general(wt_ref[...], h, (((1,), (1,)), ((), ())),
                         preferred_element_type=F32)
    q = tr[0:D_NSA].reshape(NSA_HEADS, HEAD_DIM, tm)
    qms = jnp.mean(q * q, axis=1, keepdims=True)
    qn = q * lax.rsqrt(qms + NORM_EPS) * qg_ref[...][None] * Q_SCALE
    qT_ref[0] = qn.reshape(D_NSA, tm).astype(BF16)
    vt = tr[D_NSA:D_NSA + 2 * D_KV].astype(BF16)
    ones_rows = jnp.where(lax.broadcasted_iota(jnp.int32, (V_ROWS - HEAD_DIM, QT), 0) == 0, 1.0, 0.0).astype(BF16)
    for a in range(2 * NSA_GROUPS):
        for j in range(tm // QT):
            vT_ref[0, a, j, 0:HEAD_DIM, :] = vt[a * HEAD_DIM:(a + 1) * HEAD_DIM, j * QT:(j + 1) * QT]
            vT_ref[0, a, j, HEAD_DIM:, :] = ones_rows
    gT_ref[0] = jax.nn.sigmoid(tr[D_NSA + 2 * D_KV:])


def _proj(x, g1, wstd, wt, qg, kg, *, tm):
    B, T, D = x.shape
    ns = T // SEL_BLOCK
    grid = (B, T // tm)
    const2 = lambda b, i: (0, 0)
    out_shape = (
        jax.ShapeDtypeStruct((B, D_NSA, T), BF16),
        jax.ShapeDtypeStruct((B, T, D_KV), BF16),
        jax.ShapeDtypeStruct((B, T, D_KV), BF16),
        jax.ShapeDtypeStruct((B, NSA_GROUPS, T, 2 * HEAD_DIM + ns), BF16),
        jax.ShapeDtypeStruct((B, NSA_GROUPS, T, 2 * HEAD_DIM), BF16),
        jax.ShapeDtypeStruct((B, 2 * NSA_GROUPS, T // QT, V_ROWS, QT), BF16),
        jax.ShapeDtypeStruct((B, NSA_GROUPS * GATE_ROWS, T), F32),
        jax.ShapeDtypeStruct((B, T, D_RWKV_IN), F32),
    )
    out_specs = (
        pl.BlockSpec((1, D_NSA, tm), lambda b, i: (b, 0, i)),
        pl.BlockSpec((1, tm, D_KV), lambda b, i: (b, i, 0)),
        pl.BlockSpec((1, tm, D_KV), lambda b, i: (b, i, 0)),
        pl.BlockSpec((1, NSA_GROUPS, tm, 2 * HEAD_DIM + ns), lambda b, i: (b, 0, i, 0)),
        pl.BlockSpec((1, NSA_GROUPS, tm, 2 * HEAD_DIM), lambda b, i: (b, 0, i, 0)),
        pl.BlockSpec((1, 2 * NSA_GROUPS, tm // QT, V_ROWS, QT), lambda b, i: (b, 0, i, 0, 0)),
        pl.BlockSpec((1, NSA_GROUPS * GATE_ROWS, tm), lambda b, i: (b, 0, i)),
        pl.BlockSpec((1, tm, D_RWKV_IN), lambda b, i: (b, i, 0)),
    )
    in_specs = [
        pl.BlockSpec((1, tm, D), lambda b, i: (b, i, 0)),
        pl.BlockSpec(g1.shape, const2),
        pl.BlockSpec(wstd.shape, const2),
        pl.BlockSpec(wt.shape, const2),
        pl.BlockSpec(qg.shape, const2),
        pl.BlockSpec(kg.shape, const2),
    ]
    return pl.pallas_call(
        _proj_kernel, grid=grid, in_specs=in_specs, out_specs=out_specs, out_shape=out_shape,
        compiler_params=_cparams(("parallel", "parallel")), name="proj",
    )(x, g1, wstd, wt, qg, kg)


def _gelu_tanh(x):
    c = math.sqrt(2.0 / math.pi)
    return x * (0.5 * (1.0 + jnp.tanh(c * (x + 0.044715 * (x * x * x)))))


def _compress_kernel(kc_ref, vc_ref, wbig_ref, pos_ref, w1_ref, b1_ref, w2k_ref, w2vT_ref,
                     b2k_ref, b2v_ref, kg_ref, kcmp_ref, vcmpT_ref, act_ref):
    m = kc_ref.shape[1]
    nc = m - 1
    ncp = act_ref.shape[0]
    row_m = lax.broadcasted_iota(jnp.int32, (m, CMP_HIDDEN), 0)
    row_p = lax.broadcasted_iota(jnp.int32, (ncp, HEAD_DIM), 0)
    col_p = lax.broadcasted_iota(jnp.int32, (HEAD_DIM, ncp), 1)
    col_f = lax.broadcasted_iota(jnp.int32, (ncp, HEAD_DIM), 1)
    row_o = lax.broadcasted_iota(jnp.int32, (V_ROWS - HEAD_DIM, ncp), 0)
    act_ref[...] = jnp.zeros(act_ref.shape, F32)
    for which, src_ref in enumerate((kc_ref, vc_ref)):
        p = jnp.dot(src_ref[0], wbig_ref[which], preferred_element_type=F32)
        posc = jnp.dot(pos_ref[which], w1_ref[which], preferred_element_type=F32)[0:1]
        for g in range(NSA_GROUPS):
            top = p[:, (2 * g) * CMP_HIDDEN:(2 * g + 1) * CMP_HIDDEN]
            bot = p[:, (2 * g + 1) * CMP_HIDDEN:(2 * g + 2) * CMP_HIDDEN]
            hid = top + pltpu.roll(bot, m - 1, 0) + (b1_ref[which] + posc)
            act = jnp.where(row_m < nc, _gelu_tanh(hid), 0.0)
            act_ref[CMP_PAD:CMP_PAD + m, :] = act
            ap = act_ref[...].astype(BF16)
            if which == 0:
                kc = jnp.dot(ap, w2k_ref[...], preferred_element_type=F32) + b2k_ref[...]
                ms = jnp.mean(kc * kc, axis=-1, keepdims=True)
                kc = kc * lax.rsqrt(ms + NORM_EPS) * kg_ref[...]
                valid = (row_p >= CMP_PAD) & (row_p < CMP_PAD + nc)
                kcmp_ref[0, g, :, 0:HEAD_DIM] = jnp.where(valid, kc, 0.0).astype(BF16)
                flag = (col_f == 0) & jnp.logical_not(valid)
                kcmp_ref[0, g, :, HEAD_DIM:] = jnp.where(flag, 1.0, 0.0).astype(BF16)
            else:
                vt = lax.dot_general(w2vT_ref[...], ap, (((1,), (1,)), ((), ())),
                                     preferred_element_type=F32) + b2v_ref[...]
                valid = (col_p >= CMP_PAD) & (col_p < CMP_PAD + nc)
                vcmpT_ref[0, g, 0:HEAD_DIM, :] = jnp.where(valid, vt, 0.0).astype(BF16)
                vcmpT_ref[0, g, HEAD_DIM:, :] = jnp.where(row_o == 0, 1.0, 0.0).astype(BF16)


def _compress(kc16, vc16, wbig, pos8, w1, b1, w2k, w2vT, b2k, b2v, kg, *, ncp):
    B, m, _ = kc16.shape
    full = lambda a: pl.BlockSpec(a.shape, lambda b: (0,) * a.ndim)
    return pl.pallas_call(
        _compress_kernel, grid=(B,),
        in_specs=[pl.BlockSpec((1, m, kc16.shape[2]), lambda b: (b, 0, 0)),
                  pl.BlockSpec((1, m, vc16.shape[2]), lambda b: (b, 0, 0)),
                  full(wbig), full(pos8), full(w1), full(b1), full(w2k), full(w2vT),
                  full(b2k), full(b2v), full(kg)],
        out_specs=(pl.BlockSpec((1, NSA_GROUPS, ncp, 2 * HEAD_DIM), lambda b: (b, 0, 0, 0)),
                   pl.BlockSpec((1, NSA_GROUPS, V_ROWS, ncp), lambda b: (b, 0, 0, 0))),
        out_shape=(jax.ShapeDtypeStruct((B, NSA_GROUPS, ncp, 2 * HEAD_DIM), BF16),
                   jax.ShapeDtypeStruct((B, NSA_GROUPS, V_ROWS, ncp), BF16)),
        scratch_shapes=[pltpu.VMEM((ncp, CMP_HIDDEN), F32)],
        compiler_params=_cparams(("parallel",)), name="compress",
    )(kc16, vc16, wbig, pos8, w1, b1, w2k, w2vT, b2k, b2v, kg)


def _flash_update(s_ref, smax_ref, vT, m_ref, acc_ref):
    m_prev = m_ref[...]
    m_new = jnp.maximum(m_prev, smax_ref[...])
    p = jnp.exp2(s_ref[...] - m_new)
    acc_ref[...] = (jnp.exp2(m_prev - m_new) * acc_ref[...]
                    + jnp.dot(vT, p.astype(BF16), preferred_element_type=F32))
    m_ref[...] = m_new


def _attn_kernel(qT_ref, kcmp_ref, vcmpT_ref, bc_ref, ksel_ref, vselT_ref, kwin_ref, vwinT_ref,
                 stab_ref, wtab_ref, gT_ref, o_ref,
                 lc_ref, psum_ref, qaug_ref, qfar_ref, oc_ref, ow_ref, ms_ref, accs_ref, sa_ref, sb_ref, ma_ref, mb_ref):
    qt = pl.program_id(2)
    ncp = kcmp_ref.shape[2]
    ns = qaug_ref.shape[0] - 2 * HEAD_DIM
    nq = NSA_HPG * QT
    t_lane = qt * QT + lax.broadcasted_iota(jnp.int32, (1, QT), 1)

    for r in range(NSA_HPG):
        qaug_ref[0:HEAD_DIM, r * QT:(r + 1) * QT] = qT_ref[0, r * HEAD_DIM:(r + 1) * HEAD_DIM, :]
    flag_row = lax.broadcasted_iota(jnp.int32, (HEAD_DIM, nq), 0) == 0
    qaug_ref[HEAD_DIM:2 * HEAD_DIM, :] = jnp.where(flag_row, NEG_INF, 0.0).astype(BF16)
    qk = qaug_ref[0:2 * HEAD_DIM, :]

    rho = lax.broadcasted_iota(jnp.int32, (ncp, nq), 0)
    band0 = pl.multiple_of(qt * (QT // CMP_STRIDE), V7X_SUBLANES)
    lc_ref[...] = jnp.dot(kcmp_ref[0, 0], qk, preferred_element_type=F32)
    lc_ref[pl.ds(band0, CMP_BAND), :] += bc_ref[0]
    lc = jnp.where(rho < band0 + CMP_BAND, lc_ref[...], NEG_INF)
    e = jnp.exp2(lc - jnp.max(lc, axis=0, keepdims=True))
    oc_aug = jnp.dot(vcmpT_ref[0, 0], e.astype(BF16), preferred_element_type=F32)
    t_q = qt * QT + lax.broadcasted_iota(jnp.int32, (1, nq), 1) % QT
    inv_c = jnp.where(t_q >= CMP_BLOCK - 1, 1.0 / oc_aug[HEAD_DIM:HEAD_DIM + 1, :], 0.0)
    oc_ref[...] = oc_aug[0:HEAD_DIM] * inv_c
    p = e * inv_c
    psum = (p[:, 0:QT] + p[:, QT:2 * QT]) + (p[:, 2 * QT:3 * QT] + p[:, 3 * QT:4 * QT])
    for c in range(QT // V7X_LANES):
        psum_ref[c] = psum[:, c * V7X_LANES:(c + 1) * V7X_LANES]

    def strided_sum(c):
        acc = psum_ref[c, pl.ds(CMP_PAD - 1, ns, stride=4), :]
        for k in range(1, 5):
            acc = acc + psum_ref[c, pl.ds(CMP_PAD - 1 + k, ns, stride=4), :]
        return acc

    imp = jnp.concatenate([strided_sum(c) for c in range(QT // V7X_LANES)], axis=1)
    jrow = lax.broadcasted_iota(jnp.int32, (ns, QT), 0)
    cur = t_lane // SEL_BLOCK
    forced = (jrow == 0) | (jrow == cur) | (jrow == cur - 1)
    live = jrow * SEL_BLOCK <= t_lane
    sel = forced & live
    score = jnp.where(live, jnp.where(forced, -jnp.inf, imp), NEG_INF)
    jrow_f = jrow.astype(F32)

    def col_reduce(x, pair, reduce):
        parts = [x[i * V7X_SUBLANES:(i + 1) * V7X_SUBLANES, :] for i in range(x.shape[0] // V7X_SUBLANES)]
        while len(parts) > 1:
            parts = [pair(parts[i], parts[i + 1]) for i in range(0, len(parts) - 1, 2)] + parts[len(parts) & ~1:]
        return reduce(parts[0], axis=0, keepdims=True)

    for _ in range(min(SEL_TOPN, ns) - 3):
        mx = col_reduce(score, jnp.maximum, jnp.max)
        idx = col_reduce(jnp.where(score == mx, jrow_f, float(ns)), jnp.minimum, jnp.min)
        hit = jrow_f == idx
        sel = sel | hit
        score = jnp.where(hit, -jnp.inf, score)
    negmask = jnp.where(sel, 0.0, NEG_INF).astype(BF16)
    negfar = jnp.where(sel & (jrow < (qt - 1) * (QT // SEL_BLOCK)), 0.0, NEG_INF).astype(BF16)
    qfar_ref[0:2 * HEAD_DIM, :] = qaug_ref[0:2 * HEAD_DIM, :]
    for r in range(NSA_HPG):
        qaug_ref[2 * HEAD_DIM:, r * QT:(r + 1) * QT] = negmask
        qfar_ref[2 * HEAD_DIM:, r * QT:(r + 1) * QT] = negfar

    w0 = pl.multiple_of(qt * QT, QT)
    sw = jnp.dot(kwin_ref[0, 0, pl.ds(w0, WIN_KEYS), :], qk, preferred_element_type=F32) + wtab_ref[0]
    pw = jnp.exp2(sw - jnp.max(sw, axis=0, keepdims=True))
    vw = jnp.concatenate([vwinT_ref[0, 0, qt + j] for j in range(WIN_KEYS // QT)], axis=1)
    ow_aug = jnp.dot(vw, pw.astype(BF16), preferred_element_type=F32)
    ow_ref[...] = ow_aug[0:HEAD_DIM] * (1.0 / ow_aug[HEAD_DIM:HEAD_DIM + 1, :])

    n0 = pl.multiple_of((ATT_PAD // QT - 1 + qt) * QT, QT)
    sn = jnp.dot(ksel_ref[0, 0, pl.ds(n0, NEAR_KEYS), :], qaug_ref[...],
                 preferred_element_type=F32) + stab_ref[0]
    mn = jnp.max(sn, axis=0, keepdims=True)
    pn = jnp.exp2(sn - mn)
    vn = jnp.concatenate([vselT_ref[0, 0, ATT_PAD // QT - 1 + qt + j] for j in range(NEAR_KEYS // QT)], axis=1)
    ms_ref[...] = mn
    accs_ref[...] = jnp.dot(vn, pn.astype(BF16), preferred_element_type=F32)

    tiles_per_far = FAR_KEYS // QT
    n_far = (qt - 1 + tiles_per_far - 1) // tiles_per_far
    last_far = (ksel_ref.shape[2] - ATT_PAD) // FAR_KEYS - 1

    def far_logits(g, dst_ref, dmax_ref):
        r0 = pl.multiple_of(ATT_PAD + jnp.minimum(g, last_far) * FAR_KEYS, FAR_KEYS)
        s = jnp.dot(ksel_ref[0, 0, pl.ds(r0, FAR_KEYS), :], qfar_ref[...], preferred_element_type=F32)
        dst_ref[...] = s
        dmax_ref[...] = jnp.max(s, axis=0, keepdims=True)

    def far_values(g):
        t0 = (ATT_PAD + g * FAR_KEYS) // QT
        return jnp.concatenate([vselT_ref[0, 0, t0 + j] for j in range(tiles_per_far)], axis=1)

    far_logits(0, sa_ref, ma_ref)

    def far_body(j, carry):
        bufs = ((sa_ref, ma_ref), (sb_ref, mb_ref))
        for u in range(FAR_UNROLL):
            g = FAR_UNROLL * j + u
            far_logits(g + 1, *bufs[(u + 1) % 2])
            _flash_update(*bufs[u % 2], far_values(jnp.minimum(g, last_far)), ms_ref, accs_ref)
        return carry

    lax.fori_loop(0, (n_far + FAR_UNROLL - 1) // FAR_UNROLL, far_body, 0)

    o_s = accs_ref[0:HEAD_DIM, :] * (1.0 / accs_ref[HEAD_DIM:HEAD_DIM + 1, :])
    for r in range(NSA_HPG):
        cols = slice(r * QT, (r + 1) * QT)
        o = (gT_ref[0, 3 * r:3 * r + 1, :] * oc_ref[:, cols] + gT_ref[0, 3 * r + 1:3 * r + 2, :] * o_s[:, cols]
             + gT_ref[0, 3 * r + 2:3 * r + 3, :] * ow_ref[:, cols])
        o_ref[0, :, r * HEAD_DIM:(r + 1) * HEAD_DIM] = o.T


def _attn(qT, kcmp, vcmpT, bc, ksel, vT5, kwin, stab, wtab, gT):
    B, _, T = qT.shape
    ncp = kcmp.shape[2]
    ns = T // SEL_BLOCK
    nq = NSA_HPG * QT
    tp = T + ATT_PAD
    nt = tp // QT
    grid = (B, NSA_GROUPS, T // QT)

    def held(shape, index_map):
        return pl.BlockSpec(shape, index_map, pipeline_mode=pl.Buffered(1))

    in_specs = [
        pl.BlockSpec((1, NSA_HPG * HEAD_DIM, QT), lambda b, g, q: (b, g, q)),
        held((1, 1, ncp, 2 * HEAD_DIM), lambda b, g, q: (b, g, 0, 0)),
        held((1, 1, V_ROWS, ncp), lambda b, g, q: (b, g, 0, 0)),
        held((1, CMP_BAND, nq), lambda b, g, q: (g, 0, 0)),
        pl.BlockSpec((1, 1, tp, 2 * HEAD_DIM + ns), lambda b, g, q: (b, g, 0, 0)),
        pl.BlockSpec((1, 1, nt, V_ROWS, QT), lambda b, g, q: (b, g, 0, 0, 0)),
        pl.BlockSpec((1, 1, tp, 2 * HEAD_DIM), lambda b, g, q: (b, g, 0, 0)),
        pl.BlockSpec((1, 1, nt, V_ROWS, QT), lambda b, g, q: (b, NSA_GROUPS + g, 0, 0, 0)),
        held((1, NEAR_KEYS, nq), lambda b, g, q: (g, 0, 0)),
        held((1, WIN_KEYS, nq), lambda b, g, q: (g, 0, 0)),
        pl.BlockSpec((1, GATE_ROWS, QT), lambda b, g, q: (b, g, q)),
    ]
    scratch = [
        pltpu.VMEM((ncp, nq), F32),
        pltpu.VMEM((QT // V7X_LANES, ncp, V7X_LANES), F32),
        pltpu.VMEM((2 * HEAD_DIM + ns, nq), BF16),
        pltpu.VMEM((2 * HEAD_DIM + ns, nq), BF16),
        pltpu.VMEM((HEAD_DIM, nq), F32),
        pltpu.VMEM((HEAD_DIM, nq), F32),
        pltpu.VMEM((1, nq), F32),
        pltpu.VMEM((V_ROWS, nq), F32),
        pltpu.VMEM((FAR_KEYS, nq), F32),
        pltpu.VMEM((FAR_KEYS, nq), F32),
        pltpu.VMEM((1, nq), F32), pltpu.VMEM((1, nq), F32),
    ]
    return pl.pallas_call(
        _attn_kernel, grid=grid, in_specs=in_specs,
        out_specs=pl.BlockSpec((1, QT, NSA_HPG * HEAD_DIM), lambda b, g, q: (b, q, g)),
        out_shape=jax.ShapeDtypeStruct((B, T, D_NSA), F32),
        scratch_shapes=scratch,
        compiler_params=_cparams(("parallel", "parallel", "arbitrary")), name="attn",
    )(qT, kcmp, vcmpT, bc, ksel, vT5, kwin, vT5, stab, wtab, gT)


def _rwkv_tokens(rw_ref, halo_ref, mu_ref, w0_ref, w2_ref, a0_ref, a2_ref, g2_ref, kk_ref, ka_ref,
                 rk_ref, bd_ref, ext_ref):
    tm = rw_ref.shape[1]
    first_tile = pl.program_id(1) == 0
    ext_ref[0:V7X_SUBLANES, :] = jnp.where(first_tile, 0.0, halo_ref[0])
    ext_ref[V7X_SUBLANES:, :] = rw_ref[0]
    cur = rw_ref[0]
    prev = ext_ref[pl.ds(V7X_SUBLANES - 1, tm), :]
    mixed = cur + (prev - cur) * mu_ref[...]
    c = D_RWKV
    r = mixed[:, 0:c]
    k = mixed[:, c:2 * c]
    v = mixed[:, 2 * c:3 * c]
    xw = mixed[:, 3 * c:3 * c + LORA_W]
    xa = mixed[:, 3 * c + LORA_W:3 * c + LORA_W + LORA_A]
    xg = mixed[:, 3 * c + LORA_W + LORA_A:]

    z = -(w0_ref[...] + _bdot(jnp.tanh(xw), w2_ref[...]))
    softplus = jnp.maximum(z, 0.0) + jnp.log1p(jnp.exp(-jnp.abs(z)))
    w = -softplus - 0.5
    a = jax.nn.sigmoid(a0_ref[...] + _bdot(xa, a2_ref[...]))
    g = _bdot(jax.nn.sigmoid(xg), g2_ref[...])

    kk = k * kk_ref[...]
    n2 = _dot_exact_rhs(kk * kk, bd_ref[...])
    kkn = kk / jnp.maximum(jnp.sqrt(n2), 1e-12)
    k2 = k * (1.0 + (a - 1.0) * ka_ref[...])
    bonus = _dot_exact_rhs(r * k2 * rk_ref[...], bd_ref[...]) * v

    lw = -jnp.exp(w)
    return dict(r=r, lw=lw, k=k2, v=v, kk=kkn, b=kkn * a, g=g, bonus=bonus)


WKV_BATCH = 8
WKV_GROUP = 4
WKV_GW = WKV_GROUP * HEAD_DIM


def _wkv_kernel(rw_ref, halo_ref, mu_ref, w0_ref, w2_ref, a0_ref, a2_ref, g2_ref, kk_ref, ka_ref, rk_ref, bd_ref,
                y_ref, g_ref, bonus_ref, s_ref, ext_ref, *, chunks):
    L = WKV_CHUNK
    tok = _rwkv_tokens(rw_ref, halo_ref, mu_ref, w0_ref, w2_ref, a0_ref, a2_ref, g2_ref, kk_ref, ka_ref,
                       rk_ref, bd_ref, ext_ref)
    g_ref[0] = tok["g"]
    bonus_ref[0] = tok["bonus"]
    assert L == HEAD_DIM
    gw = WKV_GW
    n_groups = RWKV_HEADS // WKV_GROUP

    @pl.when(pl.program_id(1) == 0)
    def _():
        s_ref[...] = jnp.zeros(s_ref.shape, F32)

    ti = lax.broadcasted_iota(jnp.int32, (L, L), 0)
    tj = lax.broadcasted_iota(jnp.int32, (L, L), 1)
    tri = jnp.where(ti >= tj, 1.0, 0.0).astype(BF16)
    row = lax.broadcasted_iota(jnp.int32, (L, gw), 0)
    col = lax.broadcasted_iota(jnp.int32, (L, gw), 1) % L
    low_strict = col < row
    low_incl = col <= row
    eye_sbs = jnp.where(col == row, 1.0, 0.0)
    brow = lax.broadcasted_iota(jnp.int32, (gw, gw), 0) // L
    bcol = lax.broadcasted_iota(jnp.int32, (gw, gw), 1) // HEAD_DIM
    same_head = brow == bcol

    def bd_rows(x):
        xb = x.astype(BF16)
        return jnp.where(same_head, jnp.concatenate([xb] * WKV_GROUP, axis=0), jnp.zeros((), BF16))

    def mm(a, b_bf16):
        return jnp.dot(a.astype(BF16), b_bf16, preferred_element_type=F32)

    def tn(a, b):
        return lax.dot_general(a.astype(BF16), b.astype(BF16), (((0,), (0,)), ((), ())),
                               preferred_element_type=F32)

    pre, w2, c2, m_lr, d_sbs = {}, {}, {}, {}, {}

    def grp(c, g, name):
        return pre[c][name][:, g * gw:(g + 1) * gw]

    for c0 in range(0, chunks, WKV_BATCH):
        batch = range(c0, min(c0 + WKV_BATCH, chunks))
        inst = [(c, g) for c in batch for g in range(n_groups)]
        for c in batch:
            rows = slice(c * L, (c + 1) * L)
            lw = tok["lw"][rows]
            cs = _dot_exact_rhs_left(tri, lw)
            c_last = cs[L - 1:L, :]
            e_nc = jnp.exp(-cs)
            e_lc = jnp.exp(c_last - cs)
            kk = tok["kk"][rows]
            b = tok["b"][rows]
            k = tok["k"][rows]
            pre[c] = dict(a=-kk * jnp.exp(cs - lw), r=tok["r"][rows] * jnp.exp(cs), bh=b * e_nc, kh=k * e_nc,
                          be=b * e_lc, ke=k * e_lc, v=tok["v"][rows], e_last=jnp.exp(c_last))

        t_all = {}
        for (c, g) in inst:
            lhs = jnp.concatenate([grp(c, g, "a"), grp(c, g, "r")], axis=0).astype(BF16)
            rhs = jnp.concatenate([bd_rows(grp(c, g, "bh")), bd_rows(grp(c, g, "kh"))], axis=0)
            t_all[c, g] = lax.dot_general(lhs, rhs, (((1,), (1,)), ((), ())), preferred_element_type=F32)
        n_m = {i: jnp.where(low_strict, t_all[i][0:L, 0:gw], 0.0) for i in inst}
        tak = {i: jnp.where(low_strict, t_all[i][0:L, gw:], 0.0) for i in inst}
        trb = {i: jnp.where(low_incl, t_all[i][L:, 0:gw], 0.0) for i in inst}
        trk = {i: jnp.where(low_incl, t_all[i][L:, gw:], 0.0) for i in inst}
        z = {i: eye_sbs + n_m[i] for i in inst}
        pw = dict(n_m)
        for _ in range(int(math.log2(L)) - 1):
            pw = {i: mm(pw[i], bd_rows(pw[i])) for i in inst}
            z = {i: z[i] + mm(z[i], bd_rows(pw[i])) for i in inst}
        vbd = {(c, g): bd_rows(grp(c, g, "v")) for (c, g) in inst}
        w1 = {(c, g): mm(z[c, g], bd_rows(grp(c, g, "a"))) for (c, g) in inst}
        tv = {i: mm(tak[i], vbd[i]) for i in inst}
        c1 = {i: mm(z[i], bd_rows(tv[i])) for i in inst}
        w2.update({(c, g): grp(c, g, "r") + mm(trb[c, g], bd_rows(w1[c, g])) for (c, g) in inst})
        c2.update({i: mm(trb[i], bd_rows(c1[i])) + mm(trk[i], vbd[i]) for i in inst})
        m_lr.update({(c, g): jnp.where(same_head, tn(w1[c, g], grp(c, g, "be")), 0.0).astype(BF16)
                     for (c, g) in inst})
        for (c, g) in inst:
            full = jnp.where(same_head,
                             tn(jnp.concatenate([c1[c, g], grp(c, g, "v")], axis=0),
                                jnp.concatenate([grp(c, g, "be"), grp(c, g, "ke")], axis=0)), 0.0)
            d_sbs[c, g] = ((full[0:L] + full[L:2 * L]) + (full[2 * L:3 * L] + full[3 * L:4 * L]))

    for g in range(n_groups):
        s = s_ref[:, g * gw:(g + 1) * gw]
        for c in range(chunks):
            g_bd = jnp.where(same_head, jnp.concatenate([s.T.astype(BF16)] * WKV_GROUP, axis=1),
                             jnp.zeros((), BF16))
            y_ref[0, c * L:(c + 1) * L, g * gw:(g + 1) * gw] = mm(w2[c, g], g_bd) + c2[c, g]
            s = s * grp(c, g, "e_last") + mm(s, m_lr[c, g]) + d_sbs[c, g]
        s_ref[:, g * gw:(g + 1) * gw] = s


def _dot_exact_rhs_left(m_bf16, x):
    hi, mid, lo = _split3(x)
    return (jnp.dot(m_bf16, hi, preferred_element_type=F32)
            + jnp.dot(m_bf16, mid, preferred_element_type=F32)
            + jnp.dot(m_bf16, lo, preferred_element_type=F32))


def _wkv(rw, mu, w0, w2, a0, a2, g2, k_k, k_a, r_k, bd, *, chunks):
    B, T, C = rw.shape
    rows = chunks * WKV_CHUNK
    hb = rows // V7X_SUBLANES
    full = lambda a: pl.BlockSpec(a.shape, lambda bb, c: (0,) * a.ndim)
    tok = pl.BlockSpec((1, rows, D_RWKV), lambda bb, c: (bb, c, 0))
    return pl.pallas_call(
        functools.partial(_wkv_kernel, chunks=chunks), grid=(B, T // rows),
        in_specs=[pl.BlockSpec((1, rows, C), lambda bb, c: (bb, c, 0)),
                  pl.BlockSpec((1, V7X_SUBLANES, C), lambda bb, c: (bb, jnp.maximum(c * hb - 1, 0), 0)),
                  full(mu), full(w0), full(w2), full(a0), full(a2), full(g2), full(k_k), full(k_a),
                  full(r_k), full(bd)],
        out_specs=(tok,) * 3,
        out_shape=(jax.ShapeDtypeStruct((B, T, D_RWKV), F32),) * 3,
        scratch_shapes=[pltpu.VMEM((HEAD_DIM, D_RWKV), F32), pltpu.VMEM((rows + V7X_SUBLANES, C), F32)],
        compiler_params=_cparams(("parallel", "arbitrary")), name="wkv",
    )(rw, rw, mu, w0, w2, a0, a2, g2, k_k, k_a, r_k, bd)


def _mix_kernel(x_ref, on_ref, y_ref, bonus_ref, g_ref, bd_ref, lnw_ref, lnb_ref, wo_ref, g2_ref, x1_ref, h2_ref):
    y = y_ref[...]
    mu = _dot_exact_rhs(y, bd_ref[...]) * (1.0 / HEAD_DIM)
    yc = y - mu
    var = _dot_exact_rhs(yc * yc, bd_ref[...]) * (1.0 / HEAD_DIM)
    yn = yc * lax.rsqrt(var + GN_EPS) * lnw_ref[...] + lnb_ref[...]
    orw = (yn + bonus_ref[...]) * g_ref[...]
    x1 = (x_ref[...] + _bdot(on_ref[...], wo_ref[0:D_NSA, :]) + _bdot(orw, wo_ref[D_NSA:, :]))
    x1_ref[...] = x1
    ms = jnp.mean(x1 * x1, axis=-1, keepdims=True)
    h2_ref[...] = (x1 * lax.rsqrt(ms + NORM_EPS) * g2_ref[...]).astype(BF16)


def _mix(x2d, on2d, y2d, bonus2d, g2d, bd, lnw, lnb, wo, g2, *, tm):
    n, d = x2d.shape
    tok = lambda w: pl.BlockSpec((tm, w), lambda i: (i, 0))
    full = lambda a: pl.BlockSpec(a.shape, lambda i: (0,) * a.ndim)
    return pl.pallas_call(
        _mix_kernel, grid=(n // tm,),
        in_specs=[tok(d), tok(D_NSA), tok(D_RWKV), tok(D_RWKV), tok(D_RWKV), full(bd), full(lnw), full(lnb),
                  full(wo), full(g2)],
        out_specs=(tok(d), tok(d)),
        out_shape=(jax.ShapeDtypeStruct((n, d), F32), jax.ShapeDtypeStruct((n, d), BF16)),
        compiler_params=_cparams(("parallel",)), name="mix",
    )(x2d, on2d, y2d, bonus2d, g2d, bd, lnw, lnb, wo, g2)


FFN_HALO = 16


def _ffn_kernel(x1_ref, h2_ref, halo_ref, wv_ref, wg_ref, cwv_ref, cwg_ref, cbv_ref, cbg_ref, wd_ref,
                o_ref, hext_ref, extv_ref, extg_ref, act_ref, *, tiles_per_seq, ft):
    tm = h2_ref.shape[0]
    dff = wd_ref.shape[0]
    seq_start = (pl.program_id(0) % tiles_per_seq) == 0
    hext_ref[0:FFN_HALO, :] = jnp.where(seq_start, jnp.zeros((), BF16), halo_ref[...])
    hext_ref[FFN_HALO:, :] = h2_ref[...]

    for j in range(dff // ft):
        cols = slice(j * ft, (j + 1) * ft)

        def conv_branch(w_ref, cw_ref, cb_ref, ext_ref):
            ext_ref[j % 2] = jnp.dot(hext_ref[...], w_ref[:, cols], preferred_element_type=F32)
            out = cb_ref[:, cols] + ext_ref[j % 2, pl.ds(FFN_HALO, tm), :] * cw_ref[CONV_W - 1:CONV_W, cols]
            for i in range(CONV_W - 1):
                back = CONV_W - 1 - i
                out = out + ext_ref[j % 2, pl.ds(FFN_HALO - back, tm), :] * cw_ref[i:i + 1, cols]
            return out

        u_val = conv_branch(wv_ref, cwv_ref, cbv_ref, extv_ref)
        u_gate = conv_branch(wg_ref, cwg_ref, cbg_ref, extg_ref)
        act_ref[:, cols] = ((u_gate * jax.nn.sigmoid(u_gate)) * u_val).astype(BF16)
    o_ref[...] = x1_ref[...] + jnp.dot(act_ref[...], wd_ref[...], preferred_element_type=F32)


def _ffn(x1, h2, wv, wg, cwv, cwg, cbv, cbg, wd, *, tm, ft, seq_len):
    n, d = x1.shape
    dff = wv.shape[1]
    hb = tm // FFN_HALO
    kern = functools.partial(_ffn_kernel, tiles_per_seq=seq_len // tm, ft=ft)
    resident = lambda a: pl.BlockSpec(a.shape, lambda i: (0,) * a.ndim, pipeline_mode=pl.Buffered(1))
    return pl.pallas_call(
        kern, grid=(n // tm,),
        in_specs=[pl.BlockSpec((tm, d), lambda i: (i, 0)),
                  pl.BlockSpec((tm, d), lambda i: (i, 0)),
                  pl.BlockSpec((FFN_HALO, d), lambda i: (jnp.maximum(i * hb - 1, 0), 0)),
                  resident(wv), resident(wg), resident(cwv), resident(cwg), resident(cbv), resident(cbg),
                  resident(wd)],
        out_specs=pl.BlockSpec((tm, d), lambda i: (i, 0)),
        out_shape=jax.ShapeDtypeStruct((n, d), F32),
        scratch_shapes=[pltpu.VMEM((tm + FFN_HALO, d), BF16),
                        pltpu.VMEM((2, tm + FFN_HALO, ft), F32), pltpu.VMEM((2, tm + FFN_HALO, ft), F32),
                        pltpu.VMEM((tm, dff), BF16)],
        compiler_params=_cparams(("parallel",)), name="ffn",
    )(x1, h2, h2, wv, wg, cwv, cwg, cbv, cbg, wd)


def _t5_bucket(dist):
    n = np.maximum(dist, 0)
    max_exact = N_BUCKETS // 2
    nf = np.maximum(n, 1).astype(np.float32)
    large = max_exact + (np.log(nf / np.float32(max_exact)) / np.float32(math.log(MAX_DISTANCE / max_exact))
                         * np.float32(N_BUCKETS - max_exact)).astype(np.int32)
    large = np.minimum(large, N_BUCKETS - 1)
    return np.where(n < max_exact, n, large)


def _toeplitz_kernel(v_ref, o_ref, *, step):
    rows, width = o_ref.shape[1], o_ref.shape[2]
    x = jnp.broadcast_to(v_ref[0], (rows, v_ref.shape[2]))
    o_ref[0] = pltpu.roll(x, 0, 1, stride=step, stride_axis=0)[:, :width]


def _toeplitz(v, rows, step):
    period = v.shape[2]
    return pl.pallas_call(
        functools.partial(_toeplitz_kernel, step=step), grid=(NSA_GROUPS, NSA_HPG),
        in_specs=[pl.BlockSpec((1, 1, period), lambda g, r: (g * NSA_HPG + r, 0, 0))],
        out_specs=pl.BlockSpec((1, rows, QT), lambda g, r: (g, 0, r)),
        out_shape=jax.ShapeDtypeStruct((NSA_GROUPS, rows, NSA_HPG * QT), F32),
        compiler_params=_cparams(("parallel", "parallel")), name="toeplitz",
    )(v)


def _bias_tables(rel_bias):
    rel = (rel_bias - rel_bias[N_BUCKETS - 1][None, :]) * LOG2E

    def table(rows, step, d0, d_hi):
        period = step * rows + QT
        i = np.arange(period)
        d = d0 + np.where(i < QT, i, i - period)
        onehot = (_t5_bucket(d)[:, None] == np.arange(N_BUCKETS)[None, :]).astype(np.float32)
        v = jnp.dot(jnp.asarray(onehot), rel, precision=lax.Precision.HIGHEST)
        v = jnp.where(jnp.asarray((d >= 0) & (d < d_hi))[:, None], v, NEG_INF).T
        return _toeplitz(v.reshape(NSA_HEADS, 1, period), rows, step)

    no_limit = 1 << 30
    stab = table(NEAR_KEYS, 1, QT, no_limit)
    wtab = table(WIN_KEYS, 1, WINDOW, WINDOW)
    bc = table(CMP_BAND, CMP_STRIDE, CMP_STRIDE * CMP_PAD - CMP_BLOCK + 1, no_limit)
    return stab, wtab, bc


def _compress_weights(w1):
    half = CMP_BLOCK // 2
    w1r = w1.reshape(2, half, HEAD_DIM, CMP_HIDDEN).transpose(1, 2, 0, 3)
    big = jnp.zeros((half, NSA_GROUPS, HEAD_DIM, NSA_GROUPS, 2, CMP_HIDDEN), w1.dtype)
    for g in range(NSA_GROUPS):
        big = big.at[:, g, :, g, :, :].set(w1r)
    return big.reshape(half * NSA_GROUPS * HEAD_DIM, NSA_GROUPS * 2 * CMP_HIDDEN)


def kernel(x, norm1_g, w_in, q_norm_g, k_norm_g, cmp_pos, cmp_w1, cmp_b1, cmp_w2, cmp_b2, rel_bias, rwkv_mu,
           w0, w2, a0, a2, g2, k_k, k_a, r_k, ln_x_w, ln_x_b, w_out, norm2_g, ffn_up, conv_w, conv_b, ffn_down):
    B, T, D = x.shape
    depth = w_in.shape[0]
    d_ff = ffn_down.shape[1]
    assert T % 2048 == 0 and D_NSA + 6 * D_KV + 3 * NSA_HEADS + D_RWKV_IN == w_in.shape[2]
    ncp = T // CMP_STRIDE + V7X_LANES
    stab, wtab, bc = _bias_tables(rel_bias)
    ii = jnp.arange(D_RWKV)
    bd = (ii[:, None] // HEAD_DIM == ii[None, :] // HEAD_DIM).astype(BF16)

    for l in range(depth):
        wi = w_in[l]
        o = D_NSA
        q_w, kc_w, vc_w, ksl_w, vsl_w, kwn_w, vwn_w = (
            wi[:, 0:o], wi[:, o:o + D_KV], wi[:, o + D_KV:o + 2 * D_KV], wi[:, o + 2 * D_KV:o + 3 * D_KV],
            wi[:, o + 3 * D_KV:o + 4 * D_KV], wi[:, o + 4 * D_KV:o + 5 * D_KV], wi[:, o + 5 * D_KV:o + 6 * D_KV])
        gl_w = wi[:, o + 6 * D_KV:o + 6 * D_KV + 3 * NSA_HEADS]
        rw_w = wi[:, o + 6 * D_KV + 3 * NSA_HEADS:]
        wstd = jnp.concatenate([kc_w, vc_w, ksl_w, kwn_w, rw_w], axis=1).astype(BF16)
        gl_rows = gl_w.T.reshape(NSA_GROUPS, 3 * NSA_HPG, D)
        gl_rows = jnp.pad(gl_rows, ((0, 0), (0, GATE_ROWS - 3 * NSA_HPG), (0, 0))).reshape(-1, D)
        wt = jnp.concatenate([q_w.T, vsl_w.T, vwn_w.T, gl_rows], axis=0).astype(BF16)
        qg = q_norm_g[l].reshape(HEAD_DIM, 1)
        kg = jnp.stack([jnp.tile(k_norm_g[l, 1], NSA_GROUPS), jnp.tile(k_norm_g[l, 2], NSA_GROUPS)])

        qT, kc, vc, ksel, kwin, vT5, gT, rw = _proj(x, norm1_g[l].reshape(1, D), wstd, wt, qg, kg, tm=512)

        wbig = jnp.stack([_compress_weights(cmp_w1[l, 0]), _compress_weights(cmp_w1[l, 1])]).astype(BF16)
        pos8 = jnp.pad(cmp_pos[l].reshape(2, 1, CMP_BLOCK * HEAD_DIM),
                       ((0, 0), (0, V7X_SUBLANES - 1), (0, 0))).astype(BF16)
        rows16 = CMP_STRIDE * D_KV
        kcmp, vcmpT = _compress(
            kc.reshape(B, T // CMP_STRIDE, rows16), vc.reshape(B, T // CMP_STRIDE, rows16), wbig, pos8,
            cmp_w1[l].astype(BF16), cmp_b1[l].reshape(2, 1, CMP_HIDDEN), cmp_w2[l, 0].astype(BF16),
            cmp_w2[l, 1].T.astype(BF16), cmp_b2[l, 0].reshape(1, HEAD_DIM), cmp_b2[l, 1].reshape(HEAD_DIM, 1),
            k_norm_g[l, 0].reshape(1, HEAD_DIM), ncp=ncp)

        def front_pad(k):
            flag = (jnp.arange(k.shape[-1]) == HEAD_DIM).astype(BF16)
            return jnp.concatenate([jnp.broadcast_to(flag, k.shape[:2] + (ATT_PAD, k.shape[-1])), k], axis=2)

        vT5p = jnp.pad(vT5, ((0, 0), (0, 0), (ATT_PAD // QT, 0), (0, 0), (0, 0)))
        o_nsa = _attn(qT, kcmp, vcmpT, bc, front_pad(ksel), vT5p, front_pad(kwin), stab, wtab, gT)

        row = lambda a: a.reshape(1, -1)
        y, g, bonus = _wkv(
            rw, row(rwkv_mu[l]), row(w0[l]), w2[l].astype(BF16), row(a0[l]), a2[l].astype(BF16),
            g2[l].astype(BF16), row(k_k[l]), row(k_a[l]), row(r_k[l]), bd, chunks=8)

        n = B * T
        x1, h2 = _mix(x.reshape(n, D), o_nsa.reshape(n, D_NSA), y.reshape(n, D_RWKV),
                      bonus.reshape(n, D_RWKV), g.reshape(n, D_RWKV), bd, row(ln_x_w[l]), row(ln_x_b[l]),
                      w_out[l].astype(BF16), norm2_g[l].reshape(1, D), tm=512)

        up = ffn_up[l].astype(BF16)
        x = _ffn(x1, h2, up[:, :d_ff], up[:, d_ff:], conv_w[l][:, :d_ff], conv_w[l][:, d_ff:],
                 conv_b[l][:d_ff].reshape(1, -1), conv_b[l][d_ff:].reshape(1, -1),
                 ffn_down[l].astype(BF16), tm=512, ft=256, seq_len=T).reshape(B, T, D)
    return x
```

```python
import functools
import math

import jax
import jax.numpy as jnp
import numpy as np
from jax import lax
from jax.experimental import pallas as pl
from jax.experimental.pallas import tpu as pltpu

F32 = jnp.float32
BF16 = jnp.bfloat16

V7X_LANES = 128
V7X_SUBLANES = 8
V7X_VMEM_LIMIT_BYTES = 56 * 1024 * 1024

HEAD_DIM = 64
NSA_HEADS = 8
NSA_GROUPS = 2
NSA_HPG = NSA_HEADS // NSA_GROUPS
RWKV_HEADS = 8
D_NSA = NSA_HEADS * HEAD_DIM
D_RWKV = RWKV_HEADS * HEAD_DIM
D_KV = NSA_GROUPS * HEAD_DIM
CMP_BLOCK = 32
CMP_STRIDE = 16
CMP_HIDDEN = 128
SEL_BLOCK = 64
SEL_TOPN = 16
WINDOW = 512
N_BUCKETS = 32
MAX_DISTANCE = 128
LORA_W = 64
LORA_A = 64
LORA_G = 128
D_RWKV_IN = 3 * D_RWKV + LORA_W + LORA_A + LORA_G
CONV_W = 3
NORM_EPS = 1e-6
GN_EPS = 64e-5
NEG_INF = -1e30
FORCE_SCORE = 1e9

QT = 256
CMP_PAD = 8
CMP_BAND = QT // CMP_STRIDE + 8
LOG2E = math.log2(math.e)
Q_SCALE = HEAD_DIM ** -0.5 * LOG2E
V_ROWS = HEAD_DIM + 16
ATT_PAD = WINDOW
NEAR_KEYS = 2 * QT
FAR_KEYS = 2 * QT
FAR_UNROLL = 2
WIN_KEYS = WINDOW + QT
WKV_CHUNK = 64
GATE_ROWS = 16
N_T_ROWS = D_NSA + 2 * D_KV + NSA_GROUPS * GATE_ROWS
N_STD_COLS = 4 * D_KV + D_RWKV_IN


def _cparams(sem):
    return pltpu.CompilerParams(dimension_semantics=sem, vmem_limit_bytes=V7X_VMEM_LIMIT_BYTES)


def _bdot(a, b):
    return jnp.dot(a.astype(BF16), b.astype(BF16), preferred_element_type=F32)


def _bdot_nt(a, b):
    return lax.dot_general(a.astype(BF16), b.astype(BF16), (((1,), (1,)), ((), ())),
                           preferred_element_type=F32)


def _split3(x):
    hi = x.astype(BF16)
    r1 = x - hi.astype(F32)
    mid = r1.astype(BF16)
    lo = (r1 - mid.astype(F32)).astype(BF16)
    return hi, mid, lo


def _dot_exact_rhs(x, m_bf16):
    hi = x.astype(BF16)
    lo = (x - hi.astype(F32)).astype(BF16)
    return jnp.dot(hi, m_bf16, preferred_element_type=F32) + jnp.dot(lo, m_bf16, preferred_element_type=F32)


def _head_sum(x, m_bf16):
    return jnp.dot(x.astype(BF16), m_bf16, preferred_element_type=F32)


def _proj_kernel(x_ref, g1_ref, wstd_ref, wt_ref, qg_ref, kg_ref,
                 qT_ref, kc_ref, vc_ref, ksel_ref, kwin_ref, vT_ref, gT_ref, rw_ref):
    tm = x_ref.shape[1]
    ns = ksel_ref.shape[3] - 2 * HEAD_DIM
    x = x_ref[0]
    ms = jnp.mean(x * x, axis=-1, keepdims=True)
    h = (x * lax.rsqrt(ms + NORM_EPS) * g1_ref[...]).astype(BF16)

    std = jnp.dot(h, wstd_ref[...], preferred_element_type=F32)
    kc_ref[0] = std[:, 0:D_KV].astype(BF16)
    vc_ref[0] = std[:, D_KV:2 * D_KV].astype(BF16)
    rw_ref[0] = std[:, 4 * D_KV:]

    lane = lax.broadcasted_iota(jnp.int32, (tm, D_KV), 1)
    first = lane < HEAD_DIM

    def group_rmsnorm(k, gain):
        k2 = k * k
        s0 = jnp.sum(jnp.where(first, k2, 0.0), axis=-1, keepdims=True)
        s1 = jnp.sum(jnp.where(first, 0.0, k2), axis=-1, keepdims=True)
        ms_g = jnp.where(first, s0, s1) * (1.0 / HEAD_DIM)
        return k * lax.rsqrt(ms_g + NORM_EPS) * gain

    ksl = group_rmsnorm(std[:, 2 * D_KV:3 * D_KV], kg_ref[0:1, :]).astype(BF16)
    kwn = group_rmsnorm(std[:, 3 * D_KV:4 * D_KV], kg_ref[1:2, :]).astype(BF16)

    tok = pl.program_id(1) * tm + lax.broadcasted_iota(jnp.int32, (tm, ns), 0)
    blk = lax.broadcasted_iota(jnp.int32, (tm, ns), 1)
    onehot = jnp.where((tok // SEL_BLOCK) == blk, 1.0, 0.0).astype(BF16)
    zeros = jnp.zeros((tm, HEAD_DIM), BF16)
    for g in range(NSA_GROUPS):
        ksel_ref[0, g, :, 0:HEAD_DIM] = ksl[:, g * HEAD_DIM:(g + 1) * HEAD_DIM]
        ksel_ref[0, g, :, HEAD_DIM:2 * HEAD_DIM] = zeros
        ksel_ref[0, g, :, 2 * HEAD_DIM:] = onehot
        kwin_ref[0, g, :, 0:HEAD_DIM] = kwn[:, g * HEAD_DIM:(g + 1) * HEAD_DIM]
        kwin_ref[0, g, :, HEAD_DIM:] = zeros

    tr = lax.dot_general(wt_ref[...], h, (((1,), (1,)), ((), ())),
                         preferred_element_type=F32)
    q = tr[0:D_NSA].reshape(NSA_HEADS, HEAD_DIM, tm)
    qms = jnp.mean(q * q, axis=1, keepdims=True)
    qn = q * lax.rsqrt(qms + NORM_EPS) * qg_ref[...][None] * Q_SCALE
    qT_ref[0] = qn.reshape(D_NSA, tm).astype(BF16)
    vt = tr[D_NSA:D_NSA + 2 * D_KV].astype(BF16)
    ones_rows = jnp.where(lax.broadcasted_iota(jnp.int32, (V_ROWS - HEAD_DIM, QT), 0) == 0, 1.0, 0.0).astype(BF16)
    for a in range(2 * NSA_GROUPS):
        for j in range(tm // QT):
            vT_ref[0, a, j, 0:HEAD_DIM, :] = vt[a * HEAD_DIM:(a + 1) * HEAD_DIM, j * QT:(j + 1) * QT]
            vT_ref[0, a, j, HEAD_DIM:, :] = ones_rows
    gT_ref[0] = jax.nn.sigmoid(tr[D_NSA + 2 * D_KV:])


def _proj(x, g1, wstd, wt, qg, kg, *, tm):
    B, T, D = x.shape
    ns = T // SEL_BLOCK
    grid = (B, T // tm)
    const2 = lambda b, i: (0, 0)
    out_shape = (
        jax.ShapeDtypeStruct((B, D_NSA, T), BF16),
        jax.ShapeDtypeStruct((B, T, D_KV), BF16),
        jax.ShapeDtypeStruct((B, T, D_KV), BF16),
        jax.ShapeDtypeStruct((B, NSA_GROUPS, T, 2 * HEAD_DIM + ns), BF16),
        jax.ShapeDtypeStruct((B, NSA_GROUPS, T, 2 * HEAD_DIM), BF16),
        jax.ShapeDtypeStruct((B, 2 * NSA_GROUPS, T // QT, V_ROWS, QT), BF16),
        jax.ShapeDtypeStruct((B, NSA_GROUPS * GATE_ROWS, T), F32),
        jax.ShapeDtypeStruct((B, T, D_RWKV_IN), F32),
    )
    out_specs = (
        pl.BlockSpec((1, D_NSA, tm), lambda b, i: (b, 0, i)),
        pl.BlockSpec((1, tm, D_KV), lambda b, i: (b, i, 0)),
        pl.BlockSpec((1, tm, D_KV), lambda b, i: (b, i, 0)),
        pl.BlockSpec((1, NSA_GROUPS, tm, 2 * HEAD_DIM + ns), lambda b, i: (b, 0, i, 0)),
        pl.BlockSpec((1, NSA_GROUPS, tm, 2 * HEAD_DIM), lambda b, i: (b, 0, i, 0)),
        pl.BlockSpec((1, 2 * NSA_GROUPS, tm // QT, V_ROWS, QT), lambda b, i: (b, 0, i, 0, 0)),
        pl.BlockSpec((1, NSA_GROUPS * GATE_ROWS, tm), lambda b, i: (b, 0, i)),
        pl.BlockSpec((1, tm, D_RWKV_IN), lambda b, i: (b, i, 0)),
    )
    in_specs = [
        pl.BlockSpec((1, tm, D), lambda b, i: (b, i, 0)),
        pl.BlockSpec(g1.shape, const2),
        pl.BlockSpec(wstd.shape, const2),
        pl.BlockSpec(wt.shape, const2),
        pl.BlockSpec(qg.shape, const2),
        pl.BlockSpec(kg.shape, const2),
    ]
    return pl.pallas_call(
        _proj_kernel, grid=grid, in_specs=in_specs, out_specs=out_specs, out_shape=out_shape,
        compiler_params=_cparams(("parallel", "parallel")), name="proj",
    )(x, g1, wstd, wt, qg, kg)


def _gelu_tanh(x):
    c = math.sqrt(2.0 / math.pi)
    return x * (0.5 * (1.0 + jnp.tanh(c * (x + 0.044715 * (x * x * x)))))


def _compress_kernel(kc_ref, vc_ref, wbig_ref, pos_ref, w1_ref, b1_ref, w2k_ref, w2vT_ref,
                     b2k_ref, b2v_ref, kg_ref, kcmp_ref, vcmpT_ref, act_ref):
    m = kc_ref.shape[1]
    nc = m - 1
    ncp = act_ref.shape[0]
    row_m = lax.broadcasted_iota(jnp.int32, (m, CMP_HIDDEN), 0)
    row_p = lax.broadcasted_iota(jnp.int32, (ncp, HEAD_DIM), 0)
    col_p = lax.broadcasted_iota(jnp.int32, (HEAD_DIM, ncp), 1)
    col_f = lax.broadcasted_iota(jnp.int32, (ncp, HEAD_DIM), 1)
    row_o = lax.broadcasted_iota(jnp.int32, (V_ROWS - HEAD_DIM, ncp), 0)
    act_ref[...] = jnp.zeros(act_ref.shape, F32)
    for which, src_ref in enumerate((kc_ref, vc_ref)):
        p = jnp.dot(src_ref[0], wbig_ref[which], preferred_element_type=F32)
        posc = jnp.dot(pos_ref[which], w1_ref[which], preferred_element_type=F32)[0:1]
        for g in range(NSA_GROUPS):
            top = p[:, (2 * g) * CMP_HIDDEN:(2 * g + 1) * CMP_HIDDEN]
            bot = p[:, (2 * g + 1) * CMP_HIDDEN:(2 * g + 2) * CMP_HIDDEN]
            hid = top + pltpu.roll(bot, m - 1, 0) + (b1_ref[which] + posc)
            act = jnp.where(row_m < nc, _gelu_tanh(hid), 0.0)
            act_ref[CMP_PAD:CMP_PAD + m, :] = act
            ap = act_ref[...].astype(BF16)
            if which == 0:
                kc = jnp.dot(ap, w2k_ref[...], preferred_element_type=F32) + b2k_ref[...]
                ms = jnp.mean(kc * kc, axis=-1, keepdims=True)
                kc = kc * lax.rsqrt(ms + NORM_EPS) * kg_ref[...]
                valid = (row_p >= CMP_PAD) & (row_p < CMP_PAD + nc)
                kcmp_ref[0, g, :, 0:HEAD_DIM] = jnp.where(valid, kc, 0.0).astype(BF16)
                flag = (col_f == 0) & jnp.logical_not(valid)
                kcmp_ref[0, g, :, HEAD_DIM:] = jnp.where(flag, 1.0, 0.0).astype(BF16)
            else:
                vt = lax.dot_general(w2vT_ref[...], ap, (((1,), (1,)), ((), ())),
                                     preferred_element_type=F32) + b2v_ref[...]
                valid = (col_p >= CMP_PAD) & (col_p < CMP_PAD + nc)
                vcmpT_ref[0, g, 0:HEAD_DIM, :] = jnp.where(valid, vt, 0.0).astype(BF16)
                vcmpT_ref[0, g, HEAD_DIM:, :] = jnp.where(row_o == 0, 1.0, 0.0).astype(BF16)


def _compress(kc16, vc16, wbig, pos8, w1, b1, w2k, w2vT, b2k, b2v, kg, *, ncp):
    B, m, _ = kc16.shape
    full = lambda a: pl.BlockSpec(a.shape, lambda b: (0,) * a.ndim)
    return pl.pallas_call(
        _compress_kernel, grid=(B,),
        in_specs=[pl.BlockSpec((1, m, kc16.shape[2]), lambda b: (b, 0, 0)),
                  pl.BlockSpec((1, m, vc16.shape[2]), lambda b: (b, 0, 0)),
                  full(wbig), full(pos8), full(w1), full(b1), full(w2k), full(w2vT),
                  full(b2k), full(b2v), full(kg)],
        out_specs=(pl.BlockSpec((1, NSA_GROUPS, ncp, 2 * HEAD_DIM), lambda b: (b, 0, 0, 0)),
                   pl.BlockSpec((1, NSA_GROUPS, V_ROWS, ncp), lambda b: (b, 0, 0, 0))),
        out_shape=(jax.ShapeDtypeStruct((B, NSA_GROUPS, ncp, 2 * HEAD_DIM), BF16),
                   jax.ShapeDtypeStruct((B, NSA_GROUPS, V_ROWS, ncp), BF16)),
        scratch_shapes=[pltpu.VMEM((ncp, CMP_HIDDEN), F32)],
        compiler_params=_cparams(("parallel",)), name="compress",
    )(kc16, vc16, wbig, pos8, w1, b1, w2k, w2vT, b2k, b2v, kg)


def _flash_update(s_ref, smax_ref, vT, m_ref, acc_ref):
    m_prev = m_ref[...]
    m_new = jnp.maximum(m_prev, smax_ref[...])
    p = jnp.exp2(s_ref[...] - m_new)
    acc_ref[...] = (jnp.exp2(m_prev - m_new) * acc_ref[...]
                    + jnp.dot(vT, p.astype(BF16), preferred_element_type=F32))
    m_ref[...] = m_new


def _attn_kernel(qT_ref, kcmp_ref, vcmpT_ref, bc_ref, ksel_ref, vselT_ref, kwin_ref, vwinT_ref,
                 stab_ref, wtab_ref, gT_ref, o_ref,
                 lc_ref, psum_ref, qaug_ref, qfar_ref, oc_ref, ow_ref, ms_ref, accs_ref, sa_ref, sb_ref, ma_ref, mb_ref):
    qt = pl.program_id(2)
    ncp = kcmp_ref.shape[2]
    ns = qaug_ref.shape[0] - 2 * HEAD_DIM
    nq = NSA_HPG * QT
    t_lane = qt * QT + lax.broadcasted_iota(jnp.int32, (1, QT), 1)

    for r in range(NSA_HPG):
        qaug_ref[0:HEAD_DIM, r * QT:(r + 1) * QT] = qT_ref[0, r * HEAD_DIM:(r + 1) * HEAD_DIM, :]
    flag_row = lax.broadcasted_iota(jnp.int32, (HEAD_DIM, nq), 0) == 0
    qaug_ref[HEAD_DIM:2 * HEAD_DIM, :] = jnp.where(flag_row, NEG_INF, 0.0).astype(BF16)
    qk = qaug_ref[0:2 * HEAD_DIM, :]

    rho = lax.broadcasted_iota(jnp.int32, (ncp, nq), 0)
    band0 = pl.multiple_of(qt * (QT // CMP_STRIDE), V7X_SUBLANES)
    lc_ref[...] = jnp.dot(kcmp_ref[0, 0], qk, preferred_element_type=F32)
    lc_ref[pl.ds(band0, CMP_BAND), :] += bc_ref[0]
    lc = jnp.where(rho < band0 + CMP_BAND, lc_ref[...], NEG_INF)
    e = jnp.exp2(lc - jnp.max(lc, axis=0, keepdims=True))
    oc_aug = jnp.dot(vcmpT_ref[0, 0], e.astype(BF16), preferred_element_type=F32)
    t_q = qt * QT + lax.broadcasted_iota(jnp.int32, (1, nq), 1) % QT
    inv_c = jnp.where(t_q >= CMP_BLOCK - 1, 1.0 / oc_aug[HEAD_DIM:HEAD_DIM + 1, :], 0.0)
    oc_ref[...] = oc_aug[0:HEAD_DIM] * inv_c
    p = e * inv_c
    psum = (p[:, 0:QT] + p[:, QT:2 * QT]) + (p[:, 2 * QT:3 * QT] + p[:, 3 * QT:4 * QT])
    for c in range(QT // V7X_LANES):
        psum_ref[c] = psum[:, c * V7X_LANES:(c + 1) * V7X_LANES]

    def strided_sum(c):
        acc = psum_ref[c, pl.ds(CMP_PAD - 1, ns, stride=4), :]
        for k in range(1, 5):
            acc = acc + psum_ref[c, pl.ds(CMP_PAD - 1 + k, ns, stride=4), :]
        return acc

    imp = jnp.concatenate([strided_sum(c) for c in range(QT // V7X_LANES)], axis=1)
    jrow = lax.broadcasted_iota(jnp.int32, (ns, QT), 0)
    cur = t_lane // SEL_BLOCK
    forced = (jrow == 0) | (jrow == cur) | (jrow == cur - 1)
    live = jrow * SEL_BLOCK <= t_lane
    sel = forced & live
    score = jnp.where(live, jnp.where(forced, -jnp.inf, imp), NEG_INF)
    jrow_f = jrow.astype(F32)

    def col_reduce(x, pair, reduce):
        parts = [x[i * V7X_SUBLANES:(i + 1) * V7X_SUBLANES, :] for i in range(x.shape[0] // V7X_SUBLANES)]
        while len(parts) > 1:
            parts = [pair(parts[i], parts[i + 1]) for i in range(0, len(parts) - 1, 2)] + parts[len(parts) & ~1:]
        return reduce(parts[0], axis=0, keepdims=True)

    for _ in range(min(SEL_TOPN, ns) - 3):
        mx = col_reduce(score, jnp.maximum, jnp.max)
        idx = col_reduce(jnp.where(score == mx, jrow_f, float(ns)), jnp.minimum, jnp.min)
        hit = jrow_f == idx
        sel = sel | hit
        score = jnp.where(hit, -jnp.inf, score)
    negmask = jnp.where(sel, 0.0, NEG_INF).astype(BF16)
    negfar = jnp.where(sel & (jrow < (qt - 1) * (QT // SEL_BLOCK)), 0.0, NEG_INF).astype(BF16)
    qfar_ref[0:2 * HEAD_DIM, :] = qaug_ref[0:2 * HEAD_DIM, :]
    for r in range(NSA_HPG):
        qaug_ref[2 * HEAD_DIM:, r * QT:(r + 1) * QT] = negmask
        qfar_ref[2 * HEAD_DIM:, r * QT:(r + 1) * QT] = negfar

    w0 = pl.multiple_of(qt * QT, QT)
    sw = jnp.dot(kwin_ref[0, 0, pl.ds(w0, WIN_KEYS), :], qk, preferred_element_type=F32) + wtab_ref[0]
    pw = jnp.exp2(sw - jnp.max(sw, axis=0, keepdims=True))
    vw = jnp.concatenate([vwinT_ref[0, 0, qt + j] for j in range(WIN_KEYS // QT)], axis=1)
    ow_aug = jnp.dot(vw, pw.astype(BF16), preferred_element_type=F32)
    ow_ref[...] = ow_aug[0:HEAD_DIM] * (1.0 / ow_aug[HEAD_DIM:HEAD_DIM + 1, :])

    n0 = pl.multiple_of((ATT_PAD // QT - 1 + qt) * QT, QT)
    sn = jnp.dot(ksel_ref[0, 0, pl.ds(n0, NEAR_KEYS), :], qaug_ref[...],
                 preferred_element_type=F32) + stab_ref[0]
    mn = jnp.max(sn, axis=0, keepdims=True)
    pn = jnp.exp2(sn - mn)
    vn = jnp.concatenate([vselT_ref[0, 0, ATT_PAD // QT - 1 + qt + j] for j in range(NEAR_KEYS // QT)], axis=1)
    ms_ref[...] = mn
    accs_ref[...] = jnp.dot(vn, pn.astype(BF16), preferred_element_type=F32)

    tiles_per_far = FAR_KEYS // QT
    n_far = (qt - 1 + tiles_per_far - 1) // tiles_per_far
    last_far = (ksel_ref.shape[2] - ATT_PAD) // FAR_KEYS - 1

    def far_logits(g, dst_ref, dmax_ref):
        r0 = pl.multiple_of(ATT_PAD + jnp.minimum(g, last_far) * FAR_KEYS, FAR_KEYS)
        s = jnp.dot(ksel_ref[0, 0, pl.ds(r0, FAR_KEYS), :], qfar_ref[...], preferred_element_type=F32)
        dst_ref[...] = s
        dmax_ref[...] = jnp.max(s, axis=0, keepdims=True)

    def far_values(g):
        t0 = (ATT_PAD + g * FAR_KEYS) // QT
        return jnp.concatenate([vselT_ref[0, 0, t0 + j] for j in range(tiles_per_far)], axis=1)

    far_logits(0, sa_ref, ma_ref)

    def far_body(j, carry):
        bufs = ((sa_ref, ma_ref), (sb_ref, mb_ref))
        for u in range(FAR_UNROLL):
            g = FAR_UNROLL * j + u
            far_logits(g + 1, *bufs[(u + 1) % 2])
            _flash_update(*bufs[u % 2], far_values(jnp.minimum(g, last_far)), ms_ref, accs_ref)
        return carry

    lax.fori_loop(0, (n_far + FAR_UNROLL - 1) // FAR_UNROLL, far_body, 0)

    o_s = accs_ref[0:HEAD_DIM, :] * (1.0 / accs_ref[HEAD_DIM:HEAD_DIM + 1, :])
    for r in range(NSA_HPG):
        cols = slice(r * QT, (r + 1) * QT)
        o = (gT_ref[0, 3 * r:3 * r + 1, :] * oc_ref[:, cols] + gT_ref[0, 3 * r + 1:3 * r + 2, :] * o_s[:, cols]
             + gT_ref[0, 3 * r + 2:3 * r + 3, :] * ow_ref[:, cols])
        o_ref[0, :, r * HEAD_DIM:(r + 1) * HEAD_DIM] = o.T


def _attn(qT, kcmp, vcmpT, bc, ksel, vT5, kwin, stab, wtab, gT):
    B, _, T = qT.shape
    ncp = kcmp.shape[2]
    ns = T // SEL_BLOCK
    nq = NSA_HPG * QT
    tp = T + ATT_PAD
    nt = tp // QT
    grid = (B, NSA_GROUPS, T // QT)

    def held(shape, index_map):
        return pl.BlockSpec(shape, index_map, pipeline_mode=pl.Buffered(1))

    in_specs = [
        pl.BlockSpec((1, NSA_HPG * HEAD_DIM, QT), lambda b, g, q: (b, g, q)),
        held((1, 1, ncp, 2 * HEAD_DIM), lambda b, g, q: (b, g, 0, 0)),
        held((1, 1, V_ROWS, ncp), lambda b, g, q: (b, g, 0, 0)),
        held((1, CMP_BAND, nq), lambda b, g, q: (g, 0, 0)),
        pl.BlockSpec((1, 1, tp, 2 * HEAD_DIM + ns), lambda b, g, q: (b, g, 0, 0)),
        pl.BlockSpec((1, 1, nt, V_ROWS, QT), lambda b, g, q: (b, g, 0, 0, 0)),
        pl.BlockSpec((1, 1, tp, 2 * HEAD_DIM), lambda b, g, q: (b, g, 0, 0)),
        pl.BlockSpec((1, 1, nt, V_ROWS, QT), lambda b, g, q: (b, NSA_GROUPS + g, 0, 0, 0)),
        held((1, NEAR_KEYS, nq), lambda b, g, q: (g, 0, 0)),
        held((1, WIN_KEYS, nq), lambda b, g, q: (g, 0, 0)),
        pl.BlockSpec((1, GATE_ROWS, QT), lambda b, g, q: (b, g, q)),
    ]
    scratch = [
        pltpu.VMEM((ncp, nq), F32),
        pltpu.VMEM((QT // V7X_LANES, ncp, V7X_LANES), F32),
        pltpu.VMEM((2 * HEAD_DIM + ns, nq), BF16),
        pltpu.VMEM((2 * HEAD_DIM + ns, nq), BF16),
        pltpu.VMEM((HEAD_DIM, nq), F32),
        pltpu.VMEM((HEAD_DIM, nq), F32),
        pltpu.VMEM((1, nq), F32),
        pltpu.VMEM((V_ROWS, nq), F32),
        pltpu.VMEM((FAR_KEYS, nq), F32),
        pltpu.VMEM((FAR_KEYS, nq), F32),
        pltpu.VMEM((1, nq), F32), pltpu.VMEM((1, nq), F32),
    ]
    return pl.pallas_call(
        _attn_kernel, grid=grid, in_specs=in_specs,
        out_specs=pl.BlockSpec((1, QT, NSA_HPG * HEAD_DIM), lambda b, g, q: (b, q, g)),
        out_shape=jax.ShapeDtypeStruct((B, T, D_NSA), F32),
        scratch_shapes=scratch,
        compiler_params=_cparams(("parallel", "parallel", "arbitrary")), name="attn",
    )(qT, kcmp, vcmpT, bc, ksel, vT5, kwin, vT5, stab, wtab, gT)


def _rwkv_tokens(rw_ref, halo_ref, mu_ref, w0_ref, w2_ref, a0_ref, a2_ref, g2_ref, kk_ref, ka_ref,
                 rk_ref, bd_ref, ext_ref):
    tm = rw_ref.shape[1]
    first_tile = pl.program_id(1) == 0
    ext_ref[0:V7X_SUBLANES, :] = jnp.where(first_tile, 0.0, halo_ref[0])
    ext_ref[V7X_SUBLANES:, :] = rw_ref[0]
    cur = rw_ref[0]
    prev = ext_ref[pl.ds(V7X_SUBLANES - 1, tm), :]
    mixed = cur + (prev - cur) * mu_ref[...]
    c = D_RWKV
    r = mixed[:, 0:c]
    k = mixed[:, c:2 * c]
    v = mixed[:, 2 * c:3 * c]
    xw = mixed[:, 3 * c:3 * c + LORA_W]
    xa = mixed[:, 3 * c + LORA_W:3 * c + LORA_W + LORA_A]
    xg = mixed[:, 3 * c + LORA_W + LORA_A:]

    z = -(w0_ref[...] + _bdot(jnp.tanh(xw), w2_ref[...]))
    softplus = jnp.maximum(z, 0.0) + jnp.log1p(jnp.exp(-jnp.abs(z)))
    w = -softplus - 0.5
    a = jax.nn.sigmoid(a0_ref[...] + _bdot(xa, a2_ref[...]))
    g = _bdot(jax.nn.sigmoid(xg), g2_ref[...])

    kk = k * kk_ref[...]
    n2 = _head_sum(kk * kk, bd_ref[...])
    kkn = kk / jnp.maximum(jnp.sqrt(n2), 1e-12)
    k2 = k * (1.0 + (a - 1.0) * ka_ref[...])
    bonus = _head_sum(r * k2 * rk_ref[...], bd_ref[...]) * v

    lw = -jnp.exp(w)
    return dict(r=r, lw=lw, k=k2, v=v, kk=kkn, b=kkn * a, g=g, bonus=bonus)


WKV_BATCH = 8
WKV_GROUP = 4
WKV_GW = WKV_GROUP * HEAD_DIM


def _wkv_kernel(rw_ref, halo_ref, mu_ref, w0_ref, w2_ref, a0_ref, a2_ref, g2_ref, kk_ref, ka_ref, rk_ref, bd_ref,
                y_ref, g_ref, bonus_ref, s_ref, ext_ref, *, chunks):
    L = WKV_CHUNK
    tok = _rwkv_tokens(rw_ref, halo_ref, mu_ref, w0_ref, w2_ref, a0_ref, a2_ref, g2_ref, kk_ref, ka_ref,
                       rk_ref, bd_ref, ext_ref)
    g_ref[0] = tok["g"]
    bonus_ref[0] = tok["bonus"]
    assert L == HEAD_DIM
    gw = WKV_GW
    n_groups = RWKV_HEADS // WKV_GROUP

    @pl.when(pl.program_id(1) == 0)
    def _():
        s_ref[...] = jnp.zeros(s_ref.shape, F32)

    ti = lax.broadcasted_iota(jnp.int32, (L, L), 0)
    tj = lax.broadcasted_iota(jnp.int32, (L, L), 1)
    tri = jnp.where(ti >= tj, 1.0, 0.0).astype(BF16)
    row = lax.broadcasted_iota(jnp.int32, (L, gw), 0)
    col = lax.broadcasted_iota(jnp.int32, (L, gw), 1) % L
    low_strict = col < row
    low_incl = col <= row
    eye_sbs = jnp.where(col == row, 1.0, 0.0)
    brow = lax.broadcasted_iota(jnp.int32, (gw, gw), 0) // L
    bcol = lax.broadcasted_iota(jnp.int32, (gw, gw), 1) // HEAD_DIM
    same_head = brow == bcol

    def bd_rows(x):
        xb = x.astype(BF16)
        return jnp.where(same_head, jnp.concatenate([xb] * WKV_GROUP, axis=0), jnp.zeros((), BF16))

    def mm(a, b_bf16):
        return jnp.dot(a.astype(BF16), b_bf16, preferred_element_type=F32)

    def tn(a, b):
        return lax.dot_general(a.astype(BF16), b.astype(BF16), (((0,), (0,)), ((), ())),
                               preferred_element_type=F32)

    pre, w2, c2, m_lr, d_sbs = {}, {}, {}, {}, {}

    def grp(c, g, name):
        return pre[c][name][:, g * gw:(g + 1) * gw]

    for c0 in range(0, chunks, WKV_BATCH):
        batch = range(c0, min(c0 + WKV_BATCH, chunks))
        inst = [(c, g) for c in batch for g in range(n_groups)]
        for c in batch:
            rows = slice(c * L, (c + 1) * L)
            lw = tok["lw"][rows]
            cs = _dot_exact_rhs_left(tri, lw)
            c_last = cs[L - 1:L, :]
            e_nc = jnp.exp(-cs)
            e_lc = jnp.exp(c_last - cs)
            kk = tok["kk"][rows]
            b = tok["b"][rows]
            k = tok["k"][rows]
            pre[c] = dict(a=-kk * jnp.exp(cs - lw), r=tok["r"][rows] * jnp.exp(cs), bh=b * e_nc, kh=k * e_nc,
                          be=b * e_lc, ke=k * e_lc, v=tok["v"][rows], e_last=jnp.exp(c_last))

        t_all = {}
        for (c, g) in inst:
            lhs = jnp.concatenate([grp(c, g, "a"), grp(c, g, "r")], axis=0).astype(BF16)
            rhs = jnp.concatenate([bd_rows(grp(c, g, "bh")), bd_rows(grp(c, g, "kh"))], axis=0)
            t_all[c, g] = lax.dot_general(lhs, rhs, (((1,), (1,)), ((), ())), preferred_element_type=F32)
        n_m = {i: jnp.where(low_strict, t_all[i][0:L, 0:gw], 0.0) for i in inst}
        tak = {i: jnp.where(low_strict, t_all[i][0:L, gw:], 0.0) for i in inst}
        trb = {i: jnp.where(low_incl, t_all[i][L:, 0:gw], 0.0) for i in inst}
        trk = {i: jnp.where(low_incl, t_all[i][L:, gw:], 0.0) for i in inst}
        z = {i: eye_sbs + n_m[i] for i in inst}
        pw = dict(n_m)
        for _ in range(int(math.log2(L)) - 1):
            pw = {i: mm(pw[i], bd_rows(pw[i])) for i in inst}
            z = {i: z[i] + mm(z[i], bd_rows(pw[i])) for i in inst}
        vbd = {(c, g): bd_rows(grp(c, g, "v")) for (c, g) in inst}
        w1 = {(c, g): mm(z[c, g], bd_rows(grp(c, g, "a"))) for (c, g) in inst}
        tv = {i: mm(tak[i], vbd[i]) for i in inst}
        c1 = {i: mm(z[i], bd_rows(tv[i])) for i in inst}
        w2.update({(c, g): grp(c, g, "r") + mm(trb[c, g], bd_rows(w1[c, g])) for (c, g) in inst})
        c2.update({i: mm(trb[i], bd_rows(c1[i])) + mm(trk[i], vbd[i]) for i in inst})
        m_lr.update({(c, g): jnp.where(same_head, tn(w1[c, g], grp(c, g, "be")), 0.0).astype(BF16)
                     for (c, g) in inst})
        for (c, g) in inst:
            full = jnp.where(same_head,
                             tn(jnp.concatenate([c1[c, g], grp(c, g, "v")], axis=0),
                                jnp.concatenate([grp(c, g, "be"), grp(c, g, "ke")], axis=0)), 0.0)
            d_sbs[c, g] = ((full[0:L] + full[L:2 * L]) + (full[2 * L:3 * L] + full[3 * L:4 * L]))

    for g in range(n_groups):
        s = s_ref[:, g * gw:(g + 1) * gw]
        for c in range(chunks):
            g_bd = jnp.where(same_head, jnp.concatenate([s.T.astype(BF16)] * WKV_GROUP, axis=1),
                             jnp.zeros((), BF16))
            y_ref[0, c * L:(c + 1) * L, g * gw:(g + 1) * gw] = mm(w2[c, g], g_bd) + c2[c, g]
            s = s * grp(c, g, "e_last") + mm(s, m_lr[c, g]) + d_sbs[c, g]
        s_ref[:, g * gw:(g + 1) * gw] = s


def _dot_exact_rhs_left(m_bf16, x):
    hi, mid, lo = _split3(x)
    return (jnp.dot(m_bf16, hi, preferred_element_type=F32)
            + jnp.dot(m_bf16, mid, preferred_element_type=F32)
            + jnp.dot(m_bf16, lo, preferred_element_type=F32))


def _wkv(rw, mu, w0, w2, a0, a2, g2, k_k, k_a, r_k, bd, *, chunks):
    B, T, C = rw.shape
    rows = chunks * WKV_CHUNK
    hb = rows // V7X_SUBLANES
    full = lambda a: pl.BlockSpec(a.shape, lambda bb, c: (0,) * a.ndim)
    tok = pl.BlockSpec((1, rows, D_RWKV), lambda bb, c: (bb, c, 0))
    return pl.pallas_call(
        functools.partial(_wkv_kernel, chunks=chunks), grid=(B, T // rows),
        in_specs=[pl.BlockSpec((1, rows, C), lambda bb, c: (bb, c, 0)),
                  pl.BlockSpec((1, V7X_SUBLANES, C), lambda bb, c: (bb, jnp.maximum(c * hb - 1, 0), 0)),
                  full(mu), full(w0), full(w2), full(a0), full(a2), full(g2), full(k_k), full(k_a),
                  full(r_k), full(bd)],
        out_specs=(tok,) * 3,
        out_shape=(jax.ShapeDtypeStruct((B, T, D_RWKV), F32),) * 3,
        scratch_shapes=[pltpu.VMEM((HEAD_DIM, D_RWKV), F32), pltpu.VMEM((rows + V7X_SUBLANES, C), F32)],
        compiler_params=_cparams(("parallel", "arbitrary")), name="wkv",
    )(rw, rw, mu, w0, w2, a0, a2, g2, k_k, k_a, r_k, bd)


def _mix_rows(x, on, y, bonus, g, bd_ref, lnw_ref, lnb_ref, wo_ref, g2_ref):
    mu = _dot_exact_rhs(y, bd_ref[...]) * (1.0 / HEAD_DIM)
    yc = y - mu
    var = _head_sum(yc * yc, bd_ref[...]) * (1.0 / HEAD_DIM)
    yn = yc * lax.rsqrt(var + GN_EPS) * lnw_ref[...] + lnb_ref[...]
    orw = (yn + bonus) * g
    x1 = x + _bdot(on, wo_ref[0:D_NSA, :]) + _bdot(orw, wo_ref[D_NSA:, :])
    ms = jnp.mean(x1 * x1, axis=-1, keepdims=True)
    return x1, (x1 * lax.rsqrt(ms + NORM_EPS) * g2_ref[...]).astype(BF16)


FFN_HALO = 16


def _ffn_kernel(x_ref, on_ref, y_ref, bonus_ref, g_ref, xh_ref, onh_ref, yh_ref, bonush_ref, gh_ref,
                bd_ref, lnw_ref, lnb_ref, wo_ref, g2_ref,
                wv_ref, wg_ref, cwv_ref, cwg_ref, cbv_ref, cbg_ref, wd_ref,
                o_ref, hext_ref, extv_ref, extg_ref, act_ref, x1_ref, *, tiles_per_seq, ft):
    tm = x_ref.shape[0]
    dff = wd_ref.shape[0]
    seq_start = (pl.program_id(0) % tiles_per_seq) == 0
    ext = lambda h_ref, m_ref: jnp.concatenate([h_ref[...], m_ref[...]], axis=0)
    x1, h2 = _mix_rows(ext(xh_ref, x_ref), ext(onh_ref, on_ref), ext(yh_ref, y_ref), ext(bonush_ref, bonus_ref),
                       ext(gh_ref, g_ref), bd_ref, lnw_ref, lnb_ref, wo_ref, g2_ref)
    x1_ref[...] = x1[FFN_HALO:]
    row = lax.broadcasted_iota(jnp.int32, h2.shape, 0)
    hext_ref[...] = jnp.where(seq_start & (row < FFN_HALO), jnp.zeros((), BF16), h2)

    for j in range(dff // ft):
        cols = slice(j * ft, (j + 1) * ft)

        def conv_branch(w_ref, cw_ref, cb_ref, ext_ref):
            ext_ref[j % 2] = jnp.dot(hext_ref[...], w_ref[:, cols], preferred_element_type=F32)
            out = cb_ref[:, cols] + ext_ref[j % 2, pl.ds(FFN_HALO, tm), :] * cw_ref[CONV_W - 1:CONV_W, cols]
            for i in range(CONV_W - 1):
                back = CONV_W - 1 - i
                out = out + ext_ref[j % 2, pl.ds(FFN_HALO - back, tm), :] * cw_ref[i:i + 1, cols]
            return out

        u_val = conv_branch(wv_ref, cwv_ref, cbv_ref, extv_ref)
        u_gate = conv_branch(wg_ref, cwg_ref, cbg_ref, extg_ref)
        act_ref[:, cols] = ((u_gate * jax.nn.sigmoid(u_gate)) * u_val).astype(BF16)
    o_ref[...] = x1_ref[...] + jnp.dot(act_ref[...], wd_ref[...], preferred_element_type=F32)


def _ffn(x, on, y, bonus, g, bd, lnw, lnb, wo, g2, wv, wg, cwv, cwg, cbv, cbg, wd, *, tm, ft, seq_len):
    n, d = x.shape
    dff = wv.shape[1]
    hb = tm // FFN_HALO
    kern = functools.partial(_ffn_kernel, tiles_per_seq=seq_len // tm, ft=ft)
    resident = lambda a: pl.BlockSpec(a.shape, lambda i: (0,) * a.ndim, pipeline_mode=pl.Buffered(1))
    tile = lambda a: pl.BlockSpec((tm, a.shape[1]), lambda i: (i, 0))
    halo = lambda a: pl.BlockSpec((FFN_HALO, a.shape[1]), lambda i: (jnp.maximum(i * hb - 1, 0), 0))
    tokens = (x, on, y, bonus, g)
    params = (bd, lnw, lnb, wo, g2, wv, wg, cwv, cwg, cbv, cbg, wd)
    return pl.pallas_call(
        kern, grid=(n // tm,),
        in_specs=[tile(a) for a in tokens] + [halo(a) for a in tokens] + [resident(a) for a in params],
        out_specs=pl.BlockSpec((tm, d), lambda i: (i, 0)),
        out_shape=jax.ShapeDtypeStruct((n, d), F32),
        scratch_shapes=[pltpu.VMEM((tm + FFN_HALO, d), BF16),
                        pltpu.VMEM((2, tm + FFN_HALO, ft), F32), pltpu.VMEM((2, tm + FFN_HALO, ft), F32),
                        pltpu.VMEM((tm, dff), BF16), pltpu.VMEM((tm, d), F32)],
        compiler_params=_cparams(("parallel",)), name="ffn",
    )(*tokens, *tokens, *params)


def _t5_bucket(dist):
    n = np.maximum(dist, 0)
    max_exact = N_BUCKETS // 2
    nf = np.maximum(n, 1).astype(np.float32)
    large = max_exact + (np.log(nf / np.float32(max_exact)) / np.float32(math.log(MAX_DISTANCE / max_exact))
                         * np.float32(N_BUCKETS - max_exact)).astype(np.int32)
    large = np.minimum(large, N_BUCKETS - 1)
    return np.where(n < max_exact, n, large)


def _toeplitz_kernel(v_ref, o_ref, *, step):
    rows, width = o_ref.shape[1], o_ref.shape[2]
    x = jnp.broadcast_to(v_ref[0], (rows, v_ref.shape[2]))
    o_ref[0] = pltpu.roll(x, 0, 1, stride=step, stride_axis=0)[:, :width]


def _toeplitz(v, rows, step):
    period = v.shape[2]
    return pl.pallas_call(
        functools.partial(_toeplitz_kernel, step=step), grid=(NSA_GROUPS, NSA_HPG),
        in_specs=[pl.BlockSpec((1, 1, period), lambda g, r: (g * NSA_HPG + r, 0, 0))],
        out_specs=pl.BlockSpec((1, rows, QT), lambda g, r: (g, 0, r)),
        out_shape=jax.ShapeDtypeStruct((NSA_GROUPS, rows, NSA_HPG * QT), F32),
        compiler_params=_cparams(("parallel", "parallel")), name="toeplitz",
    )(v)


def _bias_tables(rel_bias):
    rel = (rel_bias - rel_bias[N_BUCKETS - 1][None, :]) * LOG2E

    def table(rows, step, d0, d_hi):
        period = step * rows + QT
        i = np.arange(period)
        d = d0 + np.where(i < QT, i, i - period)
        onehot = (_t5_bucket(d)[:, None] == np.arange(N_BUCKETS)[None, :]).astype(np.float32)
        v = jnp.dot(jnp.asarray(onehot), rel, precision=lax.Precision.HIGHEST)
        v = jnp.where(jnp.asarray((d >= 0) & (d < d_hi))[:, None], v, NEG_INF).T
        return _toeplitz(v.reshape(NSA_HEADS, 1, period), rows, step)

    no_limit = 1 << 30
    stab = table(NEAR_KEYS, 1, QT, no_limit)
    wtab = table(WIN_KEYS, 1, WINDOW, WINDOW)
    bc = table(CMP_BAND, CMP_STRIDE, CMP_STRIDE * CMP_PAD - CMP_BLOCK + 1, no_limit)
    return stab, wtab, bc


def _compress_weights(w1):
    half = CMP_BLOCK // 2
    w1r = w1.reshape(2, half, HEAD_DIM, CMP_HIDDEN).transpose(1, 2, 0, 3)
    big = jnp.zeros((half, NSA_GROUPS, HEAD_DIM, NSA_GROUPS, 2, CMP_HIDDEN), w1.dtype)
    for g in range(NSA_GROUPS):
        big = big.at[:, g, :, g, :, :].set(w1r)
    return big.reshape(half * NSA_GROUPS * HEAD_DIM, NSA_GROUPS * 2 * CMP_HIDDEN)


def kernel(x, norm1_g, w_in, q_norm_g, k_norm_g, cmp_pos, cmp_w1, cmp_b1, cmp_w2, cmp_b2, rel_bias, rwkv_mu,
           w0, w2, a0, a2, g2, k_k, k_a, r_k, ln_x_w, ln_x_b, w_out, norm2_g, ffn_up, conv_w, conv_b, ffn_down):
    B, T, D = x.shape
    depth = w_in.shape[0]
    d_ff = ffn_down.shape[1]
    assert T % 2048 == 0 and D_NSA + 6 * D_KV + 3 * NSA_HEADS + D_RWKV_IN == w_in.shape[2]
    ncp = T // CMP_STRIDE + V7X_LANES
    stab, wtab, bc = _bias_tables(rel_bias)
    ii = jnp.arange(D_RWKV)
    bd = (ii[:, None] // HEAD_DIM == ii[None, :] // HEAD_DIM).astype(BF16)

    for l in range(depth):
        wi = w_in[l]
        o = D_NSA
        q_w, kc_w, vc_w, ksl_w, vsl_w, kwn_w, vwn_w = (
            wi[:, 0:o], wi[:, o:o + D_KV], wi[:, o + D_KV:o + 2 * D_KV], wi[:, o + 2 * D_KV:o + 3 * D_KV],
            wi[:, o + 3 * D_KV:o + 4 * D_KV], wi[:, o + 4 * D_KV:o + 5 * D_KV], wi[:, o + 5 * D_KV:o + 6 * D_KV])
        gl_w = wi[:, o + 6 * D_KV:o + 6 * D_KV + 3 * NSA_HEADS]
        rw_w = wi[:, o + 6 * D_KV + 3 * NSA_HEADS:]
        wstd = jnp.concatenate([kc_w, vc_w, ksl_w, kwn_w, rw_w], axis=1).astype(BF16)
        gl_rows = gl_w.T.reshape(NSA_GROUPS, 3 * NSA_HPG, D)
        gl_rows = jnp.pad(gl_rows, ((0, 0), (0, GATE_ROWS - 3 * NSA_HPG), (0, 0))).reshape(-1, D)
        wt = jnp.concatenate([q_w.T, vsl_w.T, vwn_w.T, gl_rows], axis=0).astype(BF16)
        qg = q_norm_g[l].reshape(HEAD_DIM, 1)
        kg = jnp.stack([jnp.tile(k_norm_g[l, 1], NSA_GROUPS), jnp.tile(k_norm_g[l, 2], NSA_GROUPS)])

        qT, kc, vc, ksel, kwin, vT5, gT, rw = _proj(x, norm1_g[l].reshape(1, D), wstd, wt, qg, kg, tm=512)

        wbig = jnp.stack([_compress_weights(cmp_w1[l, 0]), _compress_weights(cmp_w1[l, 1])]).astype(BF16)
        pos8 = jnp.pad(cmp_pos[l].reshape(2, 1, CMP_BLOCK * HEAD_DIM),
                       ((0, 0), (0, V7X_SUBLANES - 1), (0, 0))).astype(BF16)
        rows16 = CMP_STRIDE * D_KV
        kcmp, vcmpT = _compress(
            kc.reshape(B, T // CMP_STRIDE, rows16), vc.reshape(B, T // CMP_STRIDE, rows16), wbig, pos8,
            cmp_w1[l].astype(BF16), cmp_b1[l].reshape(2, 1, CMP_HIDDEN), cmp_w2[l, 0].astype(BF16),
            cmp_w2[l, 1].T.astype(BF16), cmp_b2[l, 0].reshape(1, HEAD_DIM), cmp_b2[l, 1].reshape(HEAD_DIM, 1),
            k_norm_g[l, 0].reshape(1, HEAD_DIM), ncp=ncp)

        def front_pad(k):
            flag = (jnp.arange(k.shape[-1]) == HEAD_DIM).astype(BF16)
            return jnp.concatenate([jnp.broadcast_to(flag, k.shape[:2] + (ATT_PAD, k.shape[-1])), k], axis=2)

        vT5p = jnp.pad(vT5, ((0, 0), (0, 0), (ATT_PAD // QT, 0), (0, 0), (0, 0)))
        o_nsa = _attn(qT, kcmp, vcmpT, bc, front_pad(ksel), vT5p, front_pad(kwin), stab, wtab, gT)

        row = lambda a: a.reshape(1, -1)
        y, g, bonus = _wkv(
            rw, row(rwkv_mu[l]), row(w0[l]), w2[l].astype(BF16), row(a0[l]), a2[l].astype(BF16),
            g2[l].astype(BF16), row(k_k[l]), row(k_a[l]), row(r_k[l]), bd, chunks=8)

        n = B * T
        up = ffn_up[l].astype(BF16)
        x = _ffn(x.reshape(n, D), o_nsa.reshape(n, D_NSA), y.reshape(n, D_RWKV), bonus.reshape(n, D_RWKV),
                 g.reshape(n, D_RWKV), bd, row(ln_x_w[l]), row(ln_x_b[l]), w_out[l].astype(BF16),
                 norm2_g[l].reshape(1, D), up[:, :d_ff], up[:, d_ff:], conv_w[l][:, :d_ff], conv_w[l][:, d_ff:],
                 conv_b[l][:d_ff].reshape(1, -1), conv_b[l][d_ff:].reshape(1, -1),
                 ffn_down[l].astype(BF16), tm=512, ft=256, seq_len=T).reshape(B, T, D)
    return x
```

```python
import functools
import math

import jax
import jax.numpy as jnp
import numpy as np
from jax import lax
from jax.experimental import pallas as pl
from jax.experimental.pallas import tpu as pltpu

F32 = jnp.float32
BF16 = jnp.bfloat16

V7X_LANES = 128
V7X_SUBLANES = 8
V7X_VMEM_LIMIT_BYTES = 56 * 1024 * 1024

HEAD_DIM = 64
NSA_HEADS = 8
NSA_GROUPS = 2
NSA_HPG = NSA_HEADS // NSA_GROUPS
RWKV_HEADS = 8
D_NSA = NSA_HEADS * HEAD_DIM
D_RWKV = RWKV_HEADS * HEAD_DIM
D_KV = NSA_GROUPS * HEAD_DIM
CMP_BLOCK = 32
CMP_STRIDE = 16
CMP_HIDDEN = 128
SEL_BLOCK = 64
SEL_TOPN = 16
WINDOW = 512
N_BUCKETS = 32
MAX_DISTANCE = 128
LORA_W = 64
LORA_A = 64
LORA_G = 128
D_RWKV_IN = 3 * D_RWKV + LORA_W + LORA_A + LORA_G
CONV_W = 3
NORM_EPS = 1e-6
GN_EPS = 64e-5
NEG_INF = -1e30
FORCE_SCORE = 1e9

QT = 256
CMP_PAD = 8
CMP_BAND = QT // CMP_STRIDE + 8
LOG2E = math.log2(math.e)
Q_SCALE = HEAD_DIM ** -0.5 * LOG2E
V_ROWS = HEAD_DIM + 16
ATT_PAD = WINDOW
NEAR_KEYS = 2 * QT
FAR_KEYS = 2 * QT
FAR_UNROLL = 2
WIN_KEYS = WINDOW + QT
WKV_CHUNK = 64
GATE_ROWS = 16
N_T_ROWS = D_NSA + 2 * D_KV + NSA_GROUPS * GATE_ROWS
N_STD_COLS = 4 * D_KV + D_RWKV_IN


def _cparams(sem):
    return pltpu.CompilerParams(dimension_semantics=sem, vmem_limit_bytes=V7X_VMEM_LIMIT_BYTES)


def _bdot(a, b):
    return jnp.dot(a.astype(BF16), b.astype(BF16), preferred_element_type=F32)


def _bdot_nt(a, b):
    return lax.dot_general(a.astype(BF16), b.astype(BF16), (((1,), (1,)), ((), ())),
                           preferred_element_type=F32)


def _split3(x):
    hi = x.astype(BF16)
    r1 = x - hi.astype(F32)
    mid = r1.astype(BF16)
    lo = (r1 - mid.astype(F32)).astype(BF16)
    return hi, mid, lo


def _dot_exact_rhs(x, m_bf16):
    hi = x.astype(BF16)
    lo = (x - hi.astype(F32)).astype(BF16)
    return jnp.dot(hi, m_bf16, preferred_element_type=F32) + jnp.dot(lo, m_bf16, preferred_element_type=F32)


def _head_sum(x, m_bf16):
    return jnp.dot(x.astype(BF16), m_bf16, preferred_element_type=F32)


def _proj_kernel(x_ref, g1_ref, wstd_ref, wt_ref, qg_ref, kg_ref,
                 qT_ref, kc_ref, vc_ref, ksel_ref, kwin_ref, vT_ref, gT_ref, rw_ref):
    tm = x_ref.shape[1]
    ns = ksel_ref.shape[3] - 2 * HEAD_DIM
    x = x_ref[0]
    ms = jnp.mean(x * x, axis=-1, keepdims=True)
    h = (x * lax.rsqrt(ms + NORM_EPS) * g1_ref[...]).astype(BF16)

    std = jnp.dot(h, wstd_ref[...], preferred_element_type=F32)
    kc_ref[0] = std[:, 0:D_KV].astype(BF16)
    vc_ref[0] = std[:, D_KV:2 * D_KV].astype(BF16)
    rw_ref[0] = std[:, 4 * D_KV:]

    lane = lax.broadcasted_iota(jnp.int32, (tm, D_KV), 1)
    first = lane < HEAD_DIM

    def group_rmsnorm(k, gain):
        k2 = k * k
        s0 = jnp.sum(jnp.where(first, k2, 0.0), axis=-1, keepdims=True)
        s1 = jnp.sum(jnp.where(first, 0.0, k2), axis=-1, keepdims=True)
        ms_g = jnp.where(first, s0, s1) * (1.0 / HEAD_DIM)
        return k * lax.rsqrt(ms_g + NORM_EPS) * gain

    ksl = group_rmsnorm(std[:, 2 * D_KV:3 * D_KV], kg_ref[0:1, :]).astype(BF16)
    kwn = group_rmsnorm(std[:, 3 * D_KV:4 * D_KV], kg_ref[1:2, :]).astype(BF16)

    tok = pl.program_id(1) * tm + lax.broadcasted_iota(jnp.int32, (tm, ns), 0)
    blk = lax.broadcasted_iota(jnp.int32, (tm, ns), 1)
    onehot = jnp.where((tok // SEL_BLOCK) == blk, 1.0, 0.0).astype(BF16)
    zeros = jnp.zeros((tm, HEAD_DIM), BF16)
    for g in range(NSA_GROUPS):
        ksel_ref[0, g, :, 0:HEAD_DIM] = ksl[:, g * HEAD_DIM:(g + 1) * HEAD_DIM]
        ksel_ref[0, g, :, HEAD_DIM:2 * HEAD_DIM] = zeros
        ksel_ref[0, g, :, 2 * HEAD_DIM:] = onehot
        kwin_ref[0, g, :, 0:HEAD_DIM] = kwn[:, g * HEAD_DIM:(g + 1) * HEAD_DIM]
        kwin_ref[0, g, :, HEAD_DIM:] = zeros

    tr = lax.dot_general(wt_ref[...], h, (((1,), (1,)), ((), ())),
                         preferred_element_type=F32)
    q = tr[0:D_NSA].reshape(NSA_HEADS, HEAD_DIM, tm)
    qms = jnp.mean(q * q, axis=1, keepdims=True)
    qn = q * lax.rsqrt(qms + NORM_EPS) * qg_ref[...][None] * Q_SCALE
    qT_ref[0] = qn.reshape(D_NSA, tm).astype(BF16)
    vt = tr[D_NSA:D_NSA + 2 * D_KV].astype(BF16)
    ones_rows = jnp.where(lax.broadcasted_iota(jnp.int32, (V_ROWS - HEAD_DIM, QT), 0) == 0, 1.0, 0.0).astype(BF16)
    for a in range(2 * NSA_GROUPS):
        for j in range(tm // QT):
            vT_ref[0, a, j, 0:HEAD_DIM, :] = vt[a * HEAD_DIM:(a + 1) * HEAD_DIM, j * QT:(j + 1) * QT]
            vT_ref[0, a, j, HEAD_DIM:, :] = ones_rows
    gT_ref[0] = jax.nn.sigmoid(tr[D_NSA + 2 * D_KV:])


def _proj(x, g1, wstd, wt, qg, kg, *, tm):
    B, T, D = x.shape
    ns = T // SEL_BLOCK
    grid = (B, T // tm)
    const2 = lambda b, i: (0, 0)
    out_shape = (
        jax.ShapeDtypeStruct((B, D_NSA, T), BF16),
        jax.ShapeDtypeStruct((B, T, D_KV), BF16),
        jax.ShapeDtypeStruct((B, T, D_KV), BF16),
        jax.ShapeDtypeStruct((B, NSA_GROUPS, T, 2 * HEAD_DIM + ns), BF16),
        jax.ShapeDtypeStruct((B, NSA_GROUPS, T, 2 * HEAD_DIM), BF16),
        jax.ShapeDtypeStruct((B, 2 * NSA_GROUPS, T // QT, V_ROWS, QT), BF16),
        jax.ShapeDtypeStruct((B, NSA_GROUPS * GATE_ROWS, T), F32),
        jax.ShapeDtypeStruct((B, T, D_RWKV_IN), F32),
    )
    out_specs = (
        pl.BlockSpec((1, D_NSA, tm), lambda b, i: (b, 0, i)),
        pl.BlockSpec((1, tm, D_KV), lambda b, i: (b, i, 0)),
        pl.BlockSpec((1, tm, D_KV), lambda b, i: (b, i, 0)),
        pl.BlockSpec((1, NSA_GROUPS, tm, 2 * HEAD_DIM + ns), lambda b, i: (b, 0, i, 0)),
        pl.BlockSpec((1, NSA_GROUPS, tm, 2 * HEAD_DIM), lambda b, i: (b, 0, i, 0)),
        pl.BlockSpec((1, 2 * NSA_GROUPS, tm // QT, V_ROWS, QT), lambda b, i: (b, 0, i, 0, 0)),
        pl.BlockSpec((1, NSA_GROUPS * GATE_ROWS, tm), lambda b, i: (b, 0, i)),
        pl.BlockSpec((1, tm, D_RWKV_IN), lambda b, i: (b, i, 0)),
    )
    in_specs = [
        pl.BlockSpec((1, tm, D), lambda b, i: (b, i, 0)),
        pl.BlockSpec(g1.shape, const2),
        pl.BlockSpec(wstd.shape, const2),
        pl.BlockSpec(wt.shape, const2),
        pl.BlockSpec(qg.shape, const2),
        pl.BlockSpec(kg.shape, const2),
    ]
    return pl.pallas_call(
        _proj_kernel, grid=grid, in_specs=in_specs, out_specs=out_specs, out_shape=out_shape,
        compiler_params=_cparams(("parallel", "parallel")), name="proj",
    )(x, g1, wstd, wt, qg, kg)


def _gelu_tanh(x):
    c = math.sqrt(2.0 / math.pi)
    return x * (0.5 * (1.0 + jnp.tanh(c * (x + 0.044715 * (x * x * x)))))


def _compress_kernel(kc_ref, vc_ref, wbig_ref, pos_ref, w1_ref, b1_ref, w2k_ref, w2vT_ref,
                     b2k_ref, b2v_ref, kg_ref, kcmp_ref, vcmpT_ref, act_ref):
    m = kc_ref.shape[1]
    nc = m - 1
    ncp = act_ref.shape[0]
    row_m = lax.broadcasted_iota(jnp.int32, (m, CMP_HIDDEN), 0)
    row_p = lax.broadcasted_iota(jnp.int32, (ncp, HEAD_DIM), 0)
    col_p = lax.broadcasted_iota(jnp.int32, (HEAD_DIM, ncp), 1)
    col_f = lax.broadcasted_iota(jnp.int32, (ncp, HEAD_DIM), 1)
    row_o = lax.broadcasted_iota(jnp.int32, (V_ROWS - HEAD_DIM, ncp), 0)
    act_ref[...] = jnp.zeros(act_ref.shape, F32)
    for which, src_ref in enumerate((kc_ref, vc_ref)):
        p = jnp.dot(src_ref[0], wbig_ref[which], preferred_element_type=F32)
        posc = jnp.dot(pos_ref[which], w1_ref[which], preferred_element_type=F32)[0:1]
        for g in range(NSA_GROUPS):
            top = p[:, (2 * g) * CMP_HIDDEN:(2 * g + 1) * CMP_HIDDEN]
            bot = p[:, (2 * g + 1) * CMP_HIDDEN:(2 * g + 2) * CMP_HIDDEN]
            hid = top + pltpu.roll(bot, m - 1, 0) + (b1_ref[which] + posc)
            act = jnp.where(row_m < nc, _gelu_tanh(hid), 0.0)
            act_ref[CMP_PAD:CMP_PAD + m, :] = act
            ap = act_ref[...].astype(BF16)
            if which == 0:
                kc = jnp.dot(ap, w2k_ref[...], preferred_element_type=F32) + b2k_ref[...]
                ms = jnp.mean(kc * kc, axis=-1, keepdims=True)
                kc = kc * lax.rsqrt(ms + NORM_EPS) * kg_ref[...]
                valid = (row_p >= CMP_PAD) & (row_p < CMP_PAD + nc)
                kcmp_ref[0, g, :, 0:HEAD_DIM] = jnp.where(valid, kc, 0.0).astype(BF16)
                flag = (col_f == 0) & jnp.logical_not(valid)
                kcmp_ref[0, g, :, HEAD_DIM:] = jnp.where(flag, 1.0, 0.0).astype(BF16)
            else:
                vt = lax.dot_general(w2vT_ref[...], ap, (((1,), (1,)), ((), ())),
                                     preferred_element_type=F32) + b2v_ref[...]
                valid = (col_p >= CMP_PAD) & (col_p < CMP_PAD + nc)
                vcmpT_ref[0, g, 0:HEAD_DIM, :] = jnp.where(valid, vt, 0.0).astype(BF16)
                vcmpT_ref[0, g, HEAD_DIM:, :] = jnp.where(row_o == 0, 1.0, 0.0).astype(BF16)


def _compress(kc16, vc16, wbig, pos8, w1, b1, w2k, w2vT, b2k, b2v, kg, *, ncp):
    B, m, _ = kc16.shape
    full = lambda a: pl.BlockSpec(a.shape, lambda b: (0,) * a.ndim)
    return pl.pallas_call(
        _compress_kernel, grid=(B,),
        in_specs=[pl.BlockSpec((1, m, kc16.shape[2]), lambda b: (b, 0, 0)),
                  pl.BlockSpec((1, m, vc16.shape[2]), lambda b: (b, 0, 0)),
                  full(wbig), full(pos8), full(w1), full(b1), full(w2k), full(w2vT),
                  full(b2k), full(b2v), full(kg)],
        out_specs=(pl.BlockSpec((1, NSA_GROUPS, ncp, 2 * HEAD_DIM), lambda b: (b, 0, 0, 0)),
                   pl.BlockSpec((1, NSA_GROUPS, V_ROWS, ncp), lambda b: (b, 0, 0, 0))),
        out_shape=(jax.ShapeDtypeStruct((B, NSA_GROUPS, ncp, 2 * HEAD_DIM), BF16),
                   jax.ShapeDtypeStruct((B, NSA_GROUPS, V_ROWS, ncp), BF16)),
        scratch_shapes=[pltpu.VMEM((ncp, CMP_HIDDEN), F32)],
        compiler_params=_cparams(("parallel",)), name="compress",
    )(kc16, vc16, wbig, pos8, w1, b1, w2k, w2vT, b2k, b2v, kg)


def _flash_update(s_ref, smax_ref, vT, m_ref, acc_ref):
    m_prev = m_ref[...]
    m_new = jnp.maximum(m_prev, smax_ref[...])
    p = jnp.exp2(s_ref[...] - m_new)
    acc_ref[...] = (jnp.exp2(m_prev - m_new) * acc_ref[...]
                    + jnp.dot(vT, p.astype(BF16), preferred_element_type=F32))
    m_ref[...] = m_new


def _attn_kernel(qT_ref, kcmp_ref, vcmpT_ref, bc_ref, ksel_ref, vselT_ref, kwin_ref, vwinT_ref,
                 stab_ref, wtab_ref, gT_ref, o_ref,
                 lc_ref, psum_ref, qaug_ref, qfar_ref, oc_ref, ow_ref, ms_ref, accs_ref, sa_ref, sb_ref, ma_ref, mb_ref):
    qt = pl.program_id(2)
    ncp = kcmp_ref.shape[2]
    ns = qaug_ref.shape[0] - 2 * HEAD_DIM
    nq = NSA_HPG * QT
    t_lane = qt * QT + lax.broadcasted_iota(jnp.int32, (1, QT), 1)

    for r in range(NSA_HPG):
        qaug_ref[0:HEAD_DIM, r * QT:(r + 1) * QT] = qT_ref[0, r * HEAD_DIM:(r + 1) * HEAD_DIM, :]
    flag_row = lax.broadcasted_iota(jnp.int32, (HEAD_DIM, nq), 0) == 0
    qaug_ref[HEAD_DIM:2 * HEAD_DIM, :] = jnp.where(flag_row, NEG_INF, 0.0).astype(BF16)
    qk = qaug_ref[0:2 * HEAD_DIM, :]

    rho = lax.broadcasted_iota(jnp.int32, (ncp, nq), 0)
    band0 = pl.multiple_of(qt * (QT // CMP_STRIDE), V7X_SUBLANES)
    lc_ref[...] = jnp.dot(kcmp_ref[0, 0], qk, preferred_element_type=F32)
    lc_ref[pl.ds(band0, CMP_BAND), :] += bc_ref[0]
    lc = jnp.where(rho < band0 + CMP_BAND, lc_ref[...], NEG_INF)
    e = jnp.exp2(lc - jnp.max(lc, axis=0, keepdims=True))
    oc_aug = jnp.dot(vcmpT_ref[0, 0], e.astype(BF16), preferred_element_type=F32)
    t_q = qt * QT + lax.broadcasted_iota(jnp.int32, (1, nq), 1) % QT
    inv_c = jnp.where(t_q >= CMP_BLOCK - 1, 1.0 / oc_aug[HEAD_DIM:HEAD_DIM + 1, :], 0.0)
    oc_ref[...] = oc_aug[0:HEAD_DIM] * inv_c
    p = e * inv_c
    psum = (p[:, 0:QT] + p[:, QT:2 * QT]) + (p[:, 2 * QT:3 * QT] + p[:, 3 * QT:4 * QT])
    for c in range(QT // V7X_LANES):
        psum_ref[c] = psum[:, c * V7X_LANES:(c + 1) * V7X_LANES]

    def strided_sum(c):
        acc = psum_ref[c, pl.ds(CMP_PAD - 1, ns, stride=4), :]
        for k in range(1, 5):
            acc = acc + psum_ref[c, pl.ds(CMP_PAD - 1 + k, ns, stride=4), :]
        return acc

    imp = jnp.concatenate([strided_sum(c) for c in range(QT // V7X_LANES)], axis=1)
    jrow = lax.broadcasted_iota(jnp.int32, (ns, QT), 0)
    cur = t_lane // SEL_BLOCK
    forced = (jrow == 0) | (jrow == cur) | (jrow == cur - 1)
    live = jrow * SEL_BLOCK <= t_lane
    sel = forced & live
    score = jnp.where(live, jnp.where(forced, -jnp.inf, imp), NEG_INF)
    jrow_f = jrow.astype(F32)

    def col_reduce(x, pair, reduce):
        parts = [x[i * V7X_SUBLANES:(i + 1) * V7X_SUBLANES, :] for i in range(x.shape[0] // V7X_SUBLANES)]
        while len(parts) > 1:
            parts = [pair(parts[i], parts[i + 1]) for i in range(0, len(parts) - 1, 2)] + parts[len(parts) & ~1:]
        return reduce(parts[0], axis=0, keepdims=True)

    for _ in range(min(SEL_TOPN, ns) - 3):
        mx = col_reduce(score, jnp.maximum, jnp.max)
        idx = col_reduce(jnp.where(score == mx, jrow_f, float(ns)), jnp.minimum, jnp.min)
        hit = jrow_f == idx
        sel = sel | hit
        score = jnp.where(hit, -jnp.inf, score)
    negmask = jnp.where(sel, 0.0, NEG_INF).astype(BF16)
    negfar = jnp.where(sel & (jrow < (qt - 1) * (QT // SEL_BLOCK)), 0.0, NEG_INF).astype(BF16)
    qfar_ref[0:2 * HEAD_DIM, :] = qaug_ref[0:2 * HEAD_DIM, :]
    for r in range(NSA_HPG):
        qaug_ref[2 * HEAD_DIM:, r * QT:(r + 1) * QT] = negmask
        qfar_ref[2 * HEAD_DIM:, r * QT:(r + 1) * QT] = negfar

    w0 = pl.multiple_of(qt * QT, QT)
    sw = jnp.dot(kwin_ref[0, 0, pl.ds(w0, WIN_KEYS), :], qk, preferred_element_type=F32) + wtab_ref[0]
    pw = jnp.exp2(sw - jnp.max(sw, axis=0, keepdims=True))
    vw = jnp.concatenate([vwinT_ref[0, 0, qt + j] for j in range(WIN_KEYS // QT)], axis=1)
    ow_aug = jnp.dot(vw, pw.astype(BF16), preferred_element_type=F32)
    ow_ref[...] = ow_aug[0:HEAD_DIM] * (1.0 / ow_aug[HEAD_DIM:HEAD_DIM + 1, :])

    n0 = pl.multiple_of((ATT_PAD // QT - 1 + qt) * QT, QT)
    sn = jnp.dot(ksel_ref[0, 0, pl.ds(n0, NEAR_KEYS), :], qaug_ref[...],
                 preferred_element_type=F32) + stab_ref[0]
    mn = jnp.max(sn, axis=0, keepdims=True)
    pn = jnp.exp2(sn - mn)
    vn = jnp.concatenate([vselT_ref[0, 0, ATT_PAD // QT - 1 + qt + j] for j in range(NEAR_KEYS // QT)], axis=1)
    ms_ref[...] = mn
    accs_ref[...] = jnp.dot(vn, pn.astype(BF16), preferred_element_type=F32)

    tiles_per_far = FAR_KEYS // QT
    n_far = (qt - 1 + tiles_per_far - 1) // tiles_per_far
    last_far = (ksel_ref.shape[2] - ATT_PAD) // FAR_KEYS - 1

    def far_logits(g, dst_ref, dmax_ref):
        r0 = pl.multiple_of(ATT_PAD + jnp.minimum(g, last_far) * FAR_KEYS, FAR_KEYS)
        s = jnp.dot(ksel_ref[0, 0, pl.ds(r0, FAR_KEYS), :], qfar_ref[...], preferred_element_type=F32)
        dst_ref[...] = s
        dmax_ref[...] = jnp.max(s, axis=0, keepdims=True)

    def far_values(g):
        t0 = (ATT_PAD + g * FAR_KEYS) // QT
        return jnp.concatenate([vselT_ref[0, 0, t0 + j] for j in range(tiles_per_far)], axis=1)

    far_logits(0, sa_ref, ma_ref)

    def far_body(j, carry):
        bufs = ((sa_ref, ma_ref), (sb_ref, mb_ref))
        for u in range(FAR_UNROLL):
            g = FAR_UNROLL * j + u
            far_logits(g + 1, *bufs[(u + 1) % 2])
            _flash_update(*bufs[u % 2], far_values(jnp.minimum(g, last_far)), ms_ref, accs_ref)
        return carry

    lax.fori_loop(0, (n_far + FAR_UNROLL - 1) // FAR_UNROLL, far_body, 0)

    o_s = accs_ref[0:HEAD_DIM, :] * (1.0 / accs_ref[HEAD_DIM:HEAD_DIM + 1, :])
    for r in range(NSA_HPG):
        cols = slice(r * QT, (r + 1) * QT)
        o = (gT_ref[0, 3 * r:3 * r + 1, :] * oc_ref[:, cols] + gT_ref[0, 3 * r + 1:3 * r + 2, :] * o_s[:, cols]
             + gT_ref[0, 3 * r + 2:3 * r + 3, :] * ow_ref[:, cols])
        o_ref[0, :, r * HEAD_DIM:(r + 1) * HEAD_DIM] = o.T


def _attn(qT, kcmp, vcmpT, bc, ksel, vT5, kwin, stab, wtab, gT):
    B, _, T = qT.shape
    ncp = kcmp.shape[2]
    ns = T // SEL_BLOCK
    nq = NSA_HPG * QT
    tp = T + ATT_PAD
    nt = tp // QT
    grid = (B, NSA_GROUPS, T // QT)

    def held(shape, index_map):
        return pl.BlockSpec(shape, index_map, pipeline_mode=pl.Buffered(1))

    in_specs = [
        pl.BlockSpec((1, NSA_HPG * HEAD_DIM, QT), lambda b, g, q: (b, g, q)),
        held((1, 1, ncp, 2 * HEAD_DIM), lambda b, g, q: (b, g, 0, 0)),
        held((1, 1, V_ROWS, ncp), lambda b, g, q: (b, g, 0, 0)),
        held((1, CMP_BAND, nq), lambda b, g, q: (g, 0, 0)),
        pl.BlockSpec((1, 1, tp, 2 * HEAD_DIM + ns), lambda b, g, q: (b, g, 0, 0)),
        pl.BlockSpec((1, 1, nt, V_ROWS, QT), lambda b, g, q: (b, g, 0, 0, 0)),
        pl.BlockSpec((1, 1, tp, 2 * HEAD_DIM), lambda b, g, q: (b, g, 0, 0)),
        pl.BlockSpec((1, 1, nt, V_ROWS, QT), lambda b, g, q: (b, NSA_GROUPS + g, 0, 0, 0)),
        held((1, NEAR_KEYS, nq), lambda b, g, q: (g, 0, 0)),
        held((1, WIN_KEYS, nq), lambda b, g, q: (g, 0, 0)),
        pl.BlockSpec((1, GATE_ROWS, QT), lambda b, g, q: (b, g, q)),
    ]
    scratch = [
        pltpu.VMEM((ncp, nq), F32),
        pltpu.VMEM((QT // V7X_LANES, ncp, V7X_LANES), F32),
        pltpu.VMEM((2 * HEAD_DIM + ns, nq), BF16),
        pltpu.VMEM((2 * HEAD_DIM + ns, nq), BF16),
        pltpu.VMEM((HEAD_DIM, nq), F32),
        pltpu.VMEM((HEAD_DIM, nq), F32),
        pltpu.VMEM((1, nq), F32),
        pltpu.VMEM((V_ROWS, nq), F32),
        pltpu.VMEM((FAR_KEYS, nq), F32),
        pltpu.VMEM((FAR_KEYS, nq), F32),
        pltpu.VMEM((1, nq), F32), pltpu.VMEM((1, nq), F32),
    ]
    return pl.pallas_call(
        _attn_kernel, grid=grid, in_specs=in_specs,
        out_specs=pl.BlockSpec((1, QT, NSA_HPG * HEAD_DIM), lambda b, g, q: (b, q, g)),
        out_shape=jax.ShapeDtypeStruct((B, T, D_NSA), F32),
        scratch_shapes=scratch,
        compiler_params=_cparams(("parallel", "parallel", "arbitrary")), name="attn",
    )(qT, kcmp, vcmpT, bc, ksel, vT5, kwin, vT5, stab, wtab, gT)


def _rwkv_tokens(rw_ref, halo_ref, mu_ref, w0_ref, w2_ref, a0_ref, a2_ref, g2_ref, kk_ref, ka_ref,
                 rk_ref, bd_ref, ext_ref):
    tm = rw_ref.shape[1]
    first_tile = pl.program_id(1) == 0
    ext_ref[0:V7X_SUBLANES, :] = jnp.where(first_tile, 0.0, halo_ref[0])
    ext_ref[V7X_SUBLANES:, :] = rw_ref[0]
    cur = rw_ref[0]
    prev = ext_ref[pl.ds(V7X_SUBLANES - 1, tm), :]
    mixed = cur + (prev - cur) * mu_ref[...]
    c = D_RWKV
    r = mixed[:, 0:c]
    k = mixed[:, c:2 * c]
    v = mixed[:, 2 * c:3 * c]
    xw = mixed[:, 3 * c:3 * c + LORA_W]
    xa = mixed[:, 3 * c + LORA_W:3 * c + LORA_W + LORA_A]
    xg = mixed[:, 3 * c + LORA_W + LORA_A:]

    z = -(w0_ref[...] + _bdot(jnp.tanh(xw), w2_ref[...]))
    softplus = jnp.maximum(z, 0.0) + jnp.log1p(jnp.exp(-jnp.abs(z)))
    w = -softplus - 0.5
    a = jax.nn.sigmoid(a0_ref[...] + _bdot(xa, a2_ref[...]))
    g = _bdot(jax.nn.sigmoid(xg), g2_ref[...])

    kk = k * kk_ref[...]
    n2 = _head_sum(kk * kk, bd_ref[...])
    kkn = kk / jnp.maximum(jnp.sqrt(n2), 1e-12)
    k2 = k * (1.0 + (a - 1.0) * ka_ref[...])
    bonus = _head_sum(r * k2 * rk_ref[...], bd_ref[...]) * v

    lw = -jnp.exp(w)
    return dict(r=r, lw=lw, k=k2, v=v, kk=kkn, b=kkn * a, g=g, bonus=bonus)


WKV_BATCH = 8
WKV_GROUP = 4
WKV_GW = WKV_GROUP * HEAD_DIM


def _wkv_kernel(rw_ref, halo_ref, mu_ref, w0_ref, w2_ref, a0_ref, a2_ref, g2_ref, kk_ref, ka_ref, rk_ref, bd_ref,
                y_ref, g_ref, bonus_ref, s_ref, ext_ref, *, chunks):
    L = WKV_CHUNK
    tok = _rwkv_tokens(rw_ref, halo_ref, mu_ref, w0_ref, w2_ref, a0_ref, a2_ref, g2_ref, kk_ref, ka_ref,
                       rk_ref, bd_ref, ext_ref)
    g_ref[0] = tok["g"]
    bonus_ref[0] = tok["bonus"]
    assert L == HEAD_DIM
    gw = WKV_GW
    n_groups = RWKV_HEADS // WKV_GROUP

    @pl.when(pl.program_id(1) == 0)
    def _():
        s_ref[...] = jnp.zeros(s_ref.shape, F32)

    ti = lax.broadcasted_iota(jnp.int32, (L, L), 0)
    tj = lax.broadcasted_iota(jnp.int32, (L, L), 1)
    tri = jnp.where(ti >= tj, 1.0, 0.0).astype(BF16)
    row = lax.broadcasted_iota(jnp.int32, (L, gw), 0)
    col = lax.broadcasted_iota(jnp.int32, (L, gw), 1) % L
    low_strict = col < row
    low_incl = col <= row
    eye_sbs = jnp.where(col == row, 1.0, 0.0)
    brow = lax.broadcasted_iota(jnp.int32, (gw, gw), 0) // L
    bcol = lax.broadcasted_iota(jnp.int32, (gw, gw), 1) // HEAD_DIM
    same_head = brow == bcol

    def bd_rows(x):
        xb = x.astype(BF16)
        return jnp.where(same_head, jnp.concatenate([xb] * WKV_GROUP, axis=0), jnp.zeros((), BF16))

    def mm(a, b_bf16):
        return jnp.dot(a.astype(BF16), b_bf16, preferred_element_type=F32)

    def tn(a, b):
        return lax.dot_general(a.astype(BF16), b.astype(BF16), (((0,), (0,)), ((), ())),
                               preferred_element_type=F32)

    pre, w2, c2, m_lr, d_sbs = {}, {}, {}, {}, {}

    def grp(c, g, name):
        return pre[c][name][:, g * gw:(g + 1) * gw]

    for c0 in range(0, chunks, WKV_BATCH):
        batch = range(c0, min(c0 + WKV_BATCH, chunks))
        inst = [(c, g) for c in batch for g in range(n_groups)]
        for c in batch:
            rows = slice(c * L, (c + 1) * L)
            lw = tok["lw"][rows]
            cs = _dot_exact_rhs_left(tri, lw)
            c_last = cs[L - 1:L, :]
            e_nc = jnp.exp(-cs)
            e_lc = jnp.exp(c_last - cs)
            kk = tok["kk"][rows]
            b = tok["b"][rows]
            k = tok["k"][rows]
            pre[c] = dict(a=-kk * jnp.exp(cs - lw), r=tok["r"][rows] * jnp.exp(cs), bh=b * e_nc, kh=k * e_nc,
                          be=b * e_lc, ke=k * e_lc, v=tok["v"][rows], e_last=jnp.exp(c_last))

        t_all = {}
        for (c, g) in inst:
            lhs = jnp.concatenate([grp(c, g, "a"), grp(c, g, "r")], axis=0).astype(BF16)
            rhs = jnp.concatenate([bd_rows(grp(c, g, "bh")), bd_rows(grp(c, g, "kh"))], axis=0)
            t_all[c, g] = lax.dot_general(lhs, rhs, (((1,), (1,)), ((), ())), preferred_element_type=F32)
        n_m = {i: jnp.where(low_strict, t_all[i][0:L, 0:gw], 0.0) for i in inst}
        tak = {i: jnp.where(low_strict, t_all[i][0:L, gw:], 0.0) for i in inst}
        trb = {i: jnp.where(low_incl, t_all[i][L:, 0:gw], 0.0) for i in inst}
        trk = {i: jnp.where(low_incl, t_all[i][L:, gw:], 0.0) for i in inst}
        z = {i: eye_sbs + n_m[i] for i in inst}
        pw = dict(n_m)
        for _ in range(int(math.log2(L)) - 1):
            pw = {i: mm(pw[i], bd_rows(pw[i])) for i in inst}
            z = {i: z[i] + mm(z[i], bd_rows(pw[i])) for i in inst}
        vbd = {(c, g): bd_rows(grp(c, g, "v")) for (c, g) in inst}
        w1 = {(c, g): mm(z[c, g], bd_rows(grp(c, g, "a"))) for (c, g) in inst}
        tv = {i: mm(tak[i], vbd[i]) for i in inst}
        c1 = {i: mm(z[i], bd_rows(tv[i])) for i in inst}
        w2.update({(c, g): grp(c, g, "r") + mm(trb[c, g], bd_rows(w1[c, g])) for (c, g) in inst})
        c2.update({i: mm(trb[i], bd_rows(c1[i])) + mm(trk[i], vbd[i]) for i in inst})
        m_lr.update({(c, g): jnp.where(same_head, tn(w1[c, g], grp(c, g, "be")), 0.0).astype(BF16)
                     for (c, g) in inst})
        for (c, g) in inst:
            full = jnp.where(same_head,
                             tn(jnp.concatenate([c1[c, g], grp(c, g, "v")], axis=0),
                                jnp.concatenate([grp(c, g, "be"), grp(c, g, "ke")], axis=0)), 0.0)
            d_sbs[c, g] = ((full[0:L] + full[L:2 * L]) + (full[2 * L:3 * L] + full[3 * L:4 * L]))

    for g in range(n_groups):
        s = s_ref[:, g * gw:(g + 1) * gw]
        for c in range(chunks):
            g_bd = jnp.where(same_head, jnp.concatenate([s.T.astype(BF16)] * WKV_GROUP, axis=1),
                             jnp.zeros((), BF16))
            y_ref[0, c * L:(c + 1) * L, g * gw:(g + 1) * gw] = mm(w2[c, g], g_bd) + c2[c, g]
            s = s * grp(c, g, "e_last") + mm(s, m_lr[c, g]) + d_sbs[c, g]
        s_ref[:, g * gw:(g + 1) * gw] = s


def _dot_exact_rhs_left(m_bf16, x):
    hi, mid, lo = _split3(x)
    return (jnp.dot(m_bf16, hi, preferred_element_type=F32)
            + jnp.dot(m_bf16, mid, preferred_element_type=F32)
            + jnp.dot(m_bf16, lo, preferred_element_type=F32))


def _wkv(rw, mu, w0, w2, a0, a2, g2, k_k, k_a, r_k, bd, *, chunks):
    B, T, C = rw.shape
    rows = chunks * WKV_CHUNK
    hb = rows // V7X_SUBLANES
    full = lambda a: pl.BlockSpec(a.shape, lambda bb, c: (0,) * a.ndim)
    tok = pl.BlockSpec((1, rows, D_RWKV), lambda bb, c: (bb, c, 0))
    return pl.pallas_call(
        functools.partial(_wkv_kernel, chunks=chunks), grid=(B, T // rows),
        in_specs=[pl.BlockSpec((1, rows, C), lambda bb, c: (bb, c, 0)),
                  pl.BlockSpec((1, V7X_SUBLANES, C), lambda bb, c: (bb, jnp.maximum(c * hb - 1, 0), 0)),
                  full(mu), full(w0), full(w2), full(a0), full(a2), full(g2), full(k_k), full(k_a),
                  full(r_k), full(bd)],
        out_specs=(tok,) * 3,
        out_shape=(jax.ShapeDtypeStruct((B, T, D_RWKV), F32),) * 3,
        scratch_shapes=[pltpu.VMEM((HEAD_DIM, D_RWKV), F32), pltpu.VMEM((rows + V7X_SUBLANES, C), F32)],
        compiler_params=_cparams(("parallel", "arbitrary")), name="wkv",
    )(rw, rw, mu, w0, w2, a0, a2, g2, k_k, k_a, r_k, bd)


def _mix_rows(x, on, y, bonus, g, bd_ref, lnw_ref, lnb_ref, wo_ref, g2_ref):
    mu = _dot_exact_rhs(y, bd_ref[...]) * (1.0 / HEAD_DIM)
    yc = y - mu
    var = _head_sum(yc * yc, bd_ref[...]) * (1.0 / HEAD_DIM)
    yn = yc * lax.rsqrt(var + GN_EPS) * lnw_ref[...] + lnb_ref[...]
    orw = (yn + bonus) * g
    x1 = x + _bdot(on, wo_ref[0:D_NSA, :]) + _bdot(orw, wo_ref[D_NSA:, :])
    ms = jnp.mean(x1 * x1, axis=-1, keepdims=True)
    return x1, (x1 * lax.rsqrt(ms + NORM_EPS) * g2_ref[...]).astype(BF16)


FFN_HALO = 16


def _ffn_kernel(x_ref, on_ref, y_ref, bonus_ref, g_ref, xh_ref, onh_ref, yh_ref, bonush_ref, gh_ref,
                bd_ref, lnw_ref, lnb_ref, wo_ref, g2_ref,
                wv_ref, wg_ref, cwv_ref, cwg_ref, cbv_ref, cbg_ref, wd_ref,
                o_ref, hext_ref, extv_ref, extg_ref, act_ref, x1_ref, *, tiles_per_seq, ft):
    tm = x_ref.shape[0]
    dff = wd_ref.shape[0]
    seq_start = (pl.program_id(0) % tiles_per_seq) == 0
    ext = lambda h_ref, m_ref: jnp.concatenate([h_ref[...], m_ref[...]], axis=0)
    x1, h2 = _mix_rows(ext(xh_ref, x_ref), ext(onh_ref, on_ref), ext(yh_ref, y_ref), ext(bonush_ref, bonus_ref),
                       ext(gh_ref, g_ref), bd_ref, lnw_ref, lnb_ref, wo_ref, g2_ref)
    x1_ref[...] = x1[FFN_HALO:]
    row = lax.broadcasted_iota(jnp.int32, h2.shape, 0)
    hext_ref[...] = jnp.where(seq_start & (row < FFN_HALO), jnp.zeros((), BF16), h2)

    for j in range(dff // ft):
        cols = slice(j * ft, (j + 1) * ft)

        def conv_branch(w_ref, cw_ref, cb_ref, ext_ref):
            ext_ref[j % 2] = jnp.dot(hext_ref[...], w_ref[:, cols], preferred_element_type=F32)
            out = cb_ref[:, cols] + ext_ref[j % 2, pl.ds(FFN_HALO, tm), :] * cw_ref[CONV_W - 1:CONV_W, cols]
            for i in range(CONV_W - 1):
                back = CONV_W - 1 - i
                out = out + ext_ref[j % 2, pl.ds(FFN_HALO - back, tm), :] * cw_ref[i:i + 1, cols]
            return out

        u_val = conv_branch(wv_ref, cwv_ref, cbv_ref, extv_ref)
        u_gate = conv_branch(wg_ref, cwg_ref, cbg_ref, extg_ref)
        act_ref[:, cols] = ((u_gate * jax.nn.sigmoid(u_gate)) * u_val).astype(BF16)
    o_ref[...] = x1_ref[...] + jnp.dot(act_ref[...], wd_ref[...], preferred_element_type=F32)


def _ffn(x, on, y, bonus, g, bd, lnw, lnb, wo, g2, wv, wg, cwv, cwg, cbv, cbg, wd, *, tm, ft, seq_len):
    n, d = x.shape
    dff = wv.shape[1]
    hb = tm // FFN_HALO
    kern = functools.partial(_ffn_kernel, tiles_per_seq=seq_len // tm, ft=ft)
    resident = lambda a: pl.BlockSpec(a.shape, lambda i: (0,) * a.ndim, pipeline_mode=pl.Buffered(1))
    tile = lambda a: pl.BlockSpec((tm, a.shape[1]), lambda i: (i, 0))
    halo = lambda a: pl.BlockSpec((FFN_HALO, a.shape[1]), lambda i: (jnp.maximum(i * hb - 1, 0), 0))
    tokens = (x, on, y, bonus, g)
    params = (bd, lnw, lnb, wo, g2, wv, wg, cwv, cwg, cbv, cbg, wd)
    return pl.pallas_call(
        kern, grid=(n // tm,),
        in_specs=[tile(a) for a in tokens] + [halo(a) for a in tokens] + [resident(a) for a in params],
        out_specs=pl.BlockSpec((tm, d), lambda i: (i, 0)),
        out_shape=jax.ShapeDtypeStruct((n, d), F32),
        scratch_shapes=[pltpu.VMEM((tm + FFN_HALO, d), BF16),
                        pltpu.VMEM((2, tm + FFN_HALO, ft), F32), pltpu.VMEM((2, tm + FFN_HALO, ft), F32),
                        pltpu.VMEM((tm, dff), BF16), pltpu.VMEM((tm, d), F32)],
        compiler_params=_cparams(("parallel",)), name="ffn",
    )(*tokens, *tokens, *params)


def _t5_bucket(dist):
    n = np.maximum(dist, 0)
    max_exact = N_BUCKETS // 2
    nf = np.maximum(n, 1).astype(np.float32)
    large = max_exact + (np.log(nf / np.float32(max_exact)) / np.float32(math.log(MAX_DISTANCE / max_exact))
                         * np.float32(N_BUCKETS - max_exact)).astype(np.int32)
    large = np.minimum(large, N_BUCKETS - 1)
    return np.where(n < max_exact, n, large)


def _toeplitz_kernel(v_ref, o_ref, *, step):
    rows, width = o_ref.shape[1], o_ref.shape[2]
    x = jnp.broadcast_to(v_ref[0], (rows, v_ref.shape[2]))
    o_ref[0] = pltpu.roll(x, 0, 1, stride=step, stride_axis=0)[:, :width]


def _toeplitz(v, rows, step):
    period = v.shape[2]
    return pl.pallas_call(
        functools.partial(_toeplitz_kernel, step=step), grid=(NSA_GROUPS, NSA_HPG),
        in_specs=[pl.BlockSpec((1, 1, period), lambda g, r: (g * NSA_HPG + r, 0, 0))],
        out_specs=pl.BlockSpec((1, rows, QT), lambda g, r: (g, 0, r)),
        out_shape=jax.ShapeDtypeStruct((NSA_GROUPS, rows, NSA_HPG * QT), F32),
        compiler_params=_cparams(("parallel", "parallel")), name="toeplitz",
    )(v)


def _bias_tables(rel_bias):
    rel = (rel_bias - rel_bias[N_BUCKETS - 1][None, :]) * LOG2E

    def table(rows, step, d0, d_hi):
        period = step * rows + QT
        i = np.arange(period)
        d = d0 + np.where(i < QT, i, i - period)
        onehot = (_t5_bucket(d)[:, None] == np.arange(N_BUCKETS)[None, :]).astype(np.float32)
        v = jnp.dot(jnp.asarray(onehot), rel, precision=lax.Precision.HIGHEST)
        v = jnp.where(jnp.asarray((d >= 0) & (d < d_hi))[:, None], v, NEG_INF).T
        return _toeplitz(v.reshape(NSA_HEADS, 1, period), rows, step)

    no_limit = 1 << 30
    stab = table(NEAR_KEYS, 1, QT, no_limit)
    wtab = table(WIN_KEYS, 1, WINDOW, WINDOW)
    bc = table(CMP_BAND, CMP_STRIDE, CMP_STRIDE * CMP_PAD - CMP_BLOCK + 1, no_limit)
    return stab, wtab, bc


def _compress_weights(w1):
    half = CMP_BLOCK // 2
    rows = half * HEAD_DIM
    w3 = jnp.concatenate([w1[:rows], w1[rows:]], axis=1).reshape(half, HEAD_DIM, 2 * CMP_HIDDEN)
    z = jnp.zeros_like(w3)
    assert NSA_GROUPS == 2
    big = jnp.concatenate([jnp.concatenate([w3, z], axis=2), jnp.concatenate([z, w3], axis=2)], axis=1)
    return big.reshape(half * NSA_GROUPS * HEAD_DIM, NSA_GROUPS * 2 * CMP_HIDDEN)


def kernel(x, norm1_g, w_in, q_norm_g, k_norm_g, cmp_pos, cmp_w1, cmp_b1, cmp_w2, cmp_b2, rel_bias, rwkv_mu,
           w0, w2, a0, a2, g2, k_k, k_a, r_k, ln_x_w, ln_x_b, w_out, norm2_g, ffn_up, conv_w, conv_b, ffn_down):
    B, T, D = x.shape
    depth = w_in.shape[0]
    d_ff = ffn_down.shape[1]
    assert T % 2048 == 0 and D_NSA + 6 * D_KV + 3 * NSA_HEADS + D_RWKV_IN == w_in.shape[2]
    ncp = T // CMP_STRIDE + V7X_LANES
    stab, wtab, bc = _bias_tables(rel_bias)
    ii = jnp.arange(D_RWKV)
    bd = (ii[:, None] // HEAD_DIM == ii[None, :] // HEAD_DIM).astype(BF16)

    for l in range(depth):
        wi = w_in[l]
        o = D_NSA
        q_w, kc_w, vc_w, ksl_w, vsl_w, kwn_w, vwn_w = (
            wi[:, 0:o], wi[:, o:o + D_KV], wi[:, o + D_KV:o + 2 * D_KV], wi[:, o + 2 * D_KV:o + 3 * D_KV],
            wi[:, o + 3 * D_KV:o + 4 * D_KV], wi[:, o + 4 * D_KV:o + 5 * D_KV], wi[:, o + 5 * D_KV:o + 6 * D_KV])
        gl_w = wi[:, o + 6 * D_KV:o + 6 * D_KV + 3 * NSA_HEADS]
        rw_w = wi[:, o + 6 * D_KV + 3 * NSA_HEADS:]
        wstd = jnp.concatenate([kc_w, vc_w, ksl_w, kwn_w, rw_w], axis=1).astype(BF16)
        gl_rows = gl_w.T.reshape(NSA_GROUPS, 3 * NSA_HPG, D)
        gl_rows = jnp.pad(gl_rows, ((0, 0), (0, GATE_ROWS - 3 * NSA_HPG), (0, 0))).reshape(-1, D)
        wt = jnp.concatenate([q_w.T, vsl_w.T, vwn_w.T, gl_rows], axis=0).astype(BF16)
        qg = q_norm_g[l].reshape(HEAD_DIM, 1)
        kg = jnp.stack([jnp.tile(k_norm_g[l, 1], NSA_GROUPS), jnp.tile(k_norm_g[l, 2], NSA_GROUPS)])

        qT, kc, vc, ksel, kwin, vT5, gT, rw = _proj(x, norm1_g[l].reshape(1, D), wstd, wt, qg, kg, tm=512)

        wbig = jnp.stack([_compress_weights(cmp_w1[l, 0]), _compress_weights(cmp_w1[l, 1])]).astype(BF16)
        pos8 = jnp.pad(cmp_pos[l].reshape(2, 1, CMP_BLOCK * HEAD_DIM),
                       ((0, 0), (0, V7X_SUBLANES - 1), (0, 0))).astype(BF16)
        rows16 = CMP_STRIDE * D_KV
        kcmp, vcmpT = _compress(
            kc.reshape(B, T // CMP_STRIDE, rows16), vc.reshape(B, T // CMP_STRIDE, rows16), wbig, pos8,
            cmp_w1[l].astype(BF16), cmp_b1[l].reshape(2, 1, CMP_HIDDEN), cmp_w2[l, 0].astype(BF16),
            cmp_w2[l, 1].T.astype(BF16), cmp_b2[l, 0].reshape(1, HEAD_DIM), cmp_b2[l, 1].reshape(HEAD_DIM, 1),
            k_norm_g[l, 0].reshape(1, HEAD_DIM), ncp=ncp)

        def front_pad(k):
            flag = (jnp.arange(k.shape[-1]) == HEAD_DIM).astype(BF16)
            return jnp.concatenate([jnp.broadcast_to(flag, k.shape[:2] + (ATT_PAD, k.shape[-1])), k], axis=2)

        vT5p = jnp.pad(vT5, ((0, 0), (0, 0), (ATT_PAD // QT, 0), (0, 0), (0, 0)))
        o_nsa = _attn(qT, kcmp, vcmpT, bc, front_pad(ksel), vT5p, front_pad(kwin), stab, wtab, gT)

        row = lambda a: a.reshape(1, -1)
        y, g, bonus = _wkv(
            rw, row(rwkv_mu[l]), row(w0[l]), w2[l].astype(BF16), row(a0[l]), a2[l].astype(BF16),
            g2[l].astype(BF16), row(k_k[l]), row(k_a[l]), row(r_k[l]), bd, chunks=8)

        n = B * T
        up = ffn_up[l].astype(BF16)
        x = _ffn(x.reshape(n, D), o_nsa.reshape(n, D_NSA), y.reshape(n, D_RWKV), bonus.reshape(n, D_RWKV),
                 g.reshape(n, D_RWKV), bd, row(ln_x_w[l]), row(ln_x_b[l]), w_out[l].astype(BF16),
                 norm2_g[l].reshape(1, D), up[:, :d_ff], up[:, d_ff:], conv_w[l][:, :d_ff], conv_w[l][:, d_ff:],
                 conv_b[l][:d_ff].reshape(1, -1), conv_b[l][d_ff:].reshape(1, -1),
                 ffn_down[l].astype(BF16), tm=512, ft=256, seq_len=T).reshape(B, T, D)
    return x
```

```python
import functools
import math

import jax
import jax.numpy as jnp
import numpy as np
from jax import lax
from jax.experimental import pallas as pl
from jax.experimental.pallas import tpu as pltpu

F32 = jnp.float32
BF16 = jnp.bfloat16

V7X_LANES = 128
V7X_SUBLANES = 8
V7X_VMEM_LIMIT_BYTES = 56 * 1024 * 1024

HEAD_DIM = 64
NSA_HEADS = 8
NSA_GROUPS = 2
NSA_HPG = NSA_HEADS // NSA_GROUPS
RWKV_HEADS = 8
D_NSA = NSA_HEADS * HEAD_DIM
D_RWKV = RWKV_HEADS * HEAD_DIM
D_KV = NSA_GROUPS * HEAD_DIM
CMP_BLOCK = 32
CMP_STRIDE = 16
CMP_HIDDEN = 128
SEL_BLOCK = 64
SEL_TOPN = 16
WINDOW = 512
N_BUCKETS = 32
MAX_DISTANCE = 128
LORA_W = 64
LORA_A = 64
LORA_G = 128
D_RWKV_IN = 3 * D_RWKV + LORA_W + LORA_A + LORA_G
CONV_W = 3
NORM_EPS = 1e-6
GN_EPS = 64e-5
NEG_INF = -1e30
FORCE_SCORE = 1e9

QT = 256
CMP_PAD = 8
CMP_BAND = QT // CMP_STRIDE + 8
LOG2E = math.log2(math.e)
Q_SCALE = HEAD_DIM ** -0.5 * LOG2E
V_ROWS = HEAD_DIM + 16
ATT_PAD = WINDOW
NEAR_KEYS = 2 * QT
FAR_KEYS = 2 * QT
FAR_UNROLL = 2
WIN_KEYS = WINDOW + QT
WKV_CHUNK = 64
GATE_ROWS = 16
N_T_ROWS = D_NSA + 2 * D_KV + NSA_GROUPS * GATE_ROWS
N_STD_COLS = 4 * D_KV + D_RWKV_IN


def _cparams(sem):
    return pltpu.CompilerParams(dimension_semantics=sem, vmem_limit_bytes=V7X_VMEM_LIMIT_BYTES)


def _bdot(a, b):
    return jnp.dot(a.astype(BF16), b.astype(BF16), preferred_element_type=F32)


def _bdot_nt(a, b):
    return lax.dot_general(a.astype(BF16), b.astype(BF16), (((1,), (1,)), ((), ())),
                           preferred_element_type=F32)


def _split3(x):
    hi = x.astype(BF16)
    r1 = x - hi.astype(F32)
    mid = r1.astype(BF16)
    lo = (r1 - mid.astype(F32)).astype(BF16)
    return hi, mid, lo


def _dot_exact_rhs(x, m_bf16):
    hi = x.astype(BF16)
    lo = (x - hi.astype(F32)).astype(BF16)
    return jnp.dot(hi, m_bf16, preferred_element_type=F32) + jnp.dot(lo, m_bf16, preferred_element_type=F32)


def _head_sum(x, m_bf16):
    return jnp.dot(x.astype(BF16), m_bf16, preferred_element_type=F32)


def _proj_kernel(x_ref, g1_ref, wstd_ref, wt_ref, qg_ref, kg_ref,
                 qT_ref, kc_ref, vc_ref, ksel_ref, kwin_ref, vT_ref, gT_ref, rw_ref):
    step = pl.program_id(1)

    @pl.when(step == 0)
    def _():
        for k_ref in (ksel_ref, kwin_ref):
            lanes = lax.broadcasted_iota(jnp.int32, k_ref.shape[2:], 1)
            flagged = jnp.where(lanes == HEAD_DIM, 1.0, 0.0).astype(BF16)
            for g in range(NSA_GROUPS):
                k_ref[0, g] = flagged
        vT_ref[...] = jnp.zeros(vT_ref.shape, BF16)

    @pl.when(step > 0)
    def _():
        _proj_tile(x_ref, g1_ref, wstd_ref, wt_ref, qg_ref, kg_ref,
                   qT_ref, kc_ref, vc_ref, ksel_ref, kwin_ref, vT_ref, gT_ref, rw_ref, step - 1)


def _proj_tile(x_ref, g1_ref, wstd_ref, wt_ref, qg_ref, kg_ref,
               qT_ref, kc_ref, vc_ref, ksel_ref, kwin_ref, vT_ref, gT_ref, rw_ref, tile):
    tm = x_ref.shape[1]
    ns = ksel_ref.shape[3] - 2 * HEAD_DIM
    x = x_ref[0]
    ms = jnp.mean(x * x, axis=-1, keepdims=True)
    h = (x * lax.rsqrt(ms + NORM_EPS) * g1_ref[...]).astype(BF16)

    std = jnp.dot(h, wstd_ref[...], preferred_element_type=F32)
    kc_ref[0] = std[:, 0:D_KV]
    vc_ref[0] = std[:, D_KV:2 * D_KV]
    rw_ref[0] = std[:, 4 * D_KV:]

    lane = lax.broadcasted_iota(jnp.int32, (tm, D_KV), 1)
    first = lane < HEAD_DIM

    def group_rmsnorm(k, gain):
        k2 = k * k
        s0 = jnp.sum(jnp.where(first, k2, 0.0), axis=-1, keepdims=True)
        s1 = jnp.sum(jnp.where(first, 0.0, k2), axis=-1, keepdims=True)
        ms_g = jnp.where(first, s0, s1) * (1.0 / HEAD_DIM)
        return k * lax.rsqrt(ms_g + NORM_EPS) * gain

    ksl = group_rmsnorm(std[:, 2 * D_KV:3 * D_KV], kg_ref[0:1, :]).astype(BF16)
    kwn = group_rmsnorm(std[:, 3 * D_KV:4 * D_KV], kg_ref[1:2, :]).astype(BF16)

    tok = tile * tm + lax.broadcasted_iota(jnp.int32, (tm, ns), 0)
    blk = lax.broadcasted_iota(jnp.int32, (tm, ns), 1)
    onehot = jnp.where((tok // SEL_BLOCK) == blk, 1.0, 0.0).astype(BF16)
    zeros = jnp.zeros((tm, HEAD_DIM), BF16)
    for g in range(NSA_GROUPS):
        ksel_ref[0, g, :, 0:HEAD_DIM] = ksl[:, g * HEAD_DIM:(g + 1) * HEAD_DIM]
        ksel_ref[0, g, :, HEAD_DIM:2 * HEAD_DIM] = zeros
        ksel_ref[0, g, :, 2 * HEAD_DIM:] = onehot
        kwin_ref[0, g, :, 0:HEAD_DIM] = kwn[:, g * HEAD_DIM:(g + 1) * HEAD_DIM]
        kwin_ref[0, g, :, HEAD_DIM:] = zeros

    tr = lax.dot_general(wt_ref[...], h, (((1,), (1,)), ((), ())),
                         preferred_element_type=F32)
    q = tr[0:D_NSA].reshape(NSA_HEADS, HEAD_DIM, tm)
    qms = jnp.mean(q * q, axis=1, keepdims=True)
    qn = q * lax.rsqrt(qms + NORM_EPS) * qg_ref[...][None] * Q_SCALE
    qT_ref[0] = qn.reshape(D_NSA, tm).astype(BF16)
    vt = tr[D_NSA:D_NSA + 2 * D_KV].astype(BF16)
    ones_rows = jnp.where(lax.broadcasted_iota(jnp.int32, (V_ROWS - HEAD_DIM, QT), 0) == 0, 1.0, 0.0).astype(BF16)
    for a in range(2 * NSA_GROUPS):
        for j in range(tm // QT):
            vT_ref[0, a, j, 0:HEAD_DIM, :] = vt[a * HEAD_DIM:(a + 1) * HEAD_DIM, j * QT:(j + 1) * QT]
            vT_ref[0, a, j, HEAD_DIM:, :] = ones_rows
    gT_ref[0] = jax.nn.sigmoid(tr[D_NSA + 2 * D_KV:])


def _proj(x, g1, wstd, wt, qg, kg, *, tm):
    B, T, D = x.shape
    ns = T // SEL_BLOCK
    assert tm == ATT_PAD
    tp = T + ATT_PAD
    grid = (B, T // tm + 1)
    const2 = lambda b, i: (0, 0)
    tile = lambda i: jnp.maximum(i - 1, 0)
    out_shape = (
        jax.ShapeDtypeStruct((B, D_NSA, T), BF16),
        jax.ShapeDtypeStruct((B, T, D_KV), F32),
        jax.ShapeDtypeStruct((B, T, D_KV), F32),
        jax.ShapeDtypeStruct((B, NSA_GROUPS, tp, 2 * HEAD_DIM + ns), BF16),
        jax.ShapeDtypeStruct((B, NSA_GROUPS, tp, 2 * HEAD_DIM), BF16),
        jax.ShapeDtypeStruct((B, 2 * NSA_GROUPS, tp // QT, V_ROWS, QT), BF16),
        jax.ShapeDtypeStruct((B, NSA_GROUPS * GATE_ROWS, T), F32),
        jax.ShapeDtypeStruct((B, T, D_RWKV_IN), F32),
    )
    out_specs = (
        pl.BlockSpec((1, D_NSA, tm), lambda b, i: (b, 0, tile(i))),
        pl.BlockSpec((1, tm, D_KV), lambda b, i: (b, tile(i), 0)),
        pl.BlockSpec((1, tm, D_KV), lambda b, i: (b, tile(i), 0)),
        pl.BlockSpec((1, NSA_GROUPS, tm, 2 * HEAD_DIM + ns), lambda b, i: (b, 0, i, 0)),
        pl.BlockSpec((1, NSA_GROUPS, tm, 2 * HEAD_DIM), lambda b, i: (b, 0, i, 0)),
        pl.BlockSpec((1, 2 * NSA_GROUPS, tm // QT, V_ROWS, QT), lambda b, i: (b, 0, i, 0, 0)),
        pl.BlockSpec((1, NSA_GROUPS * GATE_ROWS, tm), lambda b, i: (b, 0, tile(i))),
        pl.BlockSpec((1, tm, D_RWKV_IN), lambda b, i: (b, tile(i), 0)),
    )
    in_specs = [
        pl.BlockSpec((1, tm, D), lambda b, i: (b, tile(i), 0)),
        pl.BlockSpec(g1.shape, const2),
        pl.BlockSpec(wstd.shape, const2),
        pl.BlockSpec(wt.shape, const2),
        pl.BlockSpec(qg.shape, const2),
        pl.BlockSpec(kg.shape, const2),
    ]
    return pl.pallas_call(
        _proj_kernel, grid=grid, in_specs=in_specs, out_specs=out_specs, out_shape=out_shape,
        compiler_params=_cparams(("parallel", "arbitrary")), name="proj",
    )(x, g1, wstd, wt, qg, kg)


def _gelu_tanh(x):
    c = math.sqrt(2.0 / math.pi)
    return x * (0.5 * (1.0 + jnp.tanh(c * (x + 0.044715 * (x * x * x)))))


def _compress_kernel(kc_ref, vc_ref, wbig_ref, pos_ref, w1_ref, b1_ref, w2k_ref, w2vT_ref,
                     b2k_ref, b2v_ref, kg_ref, kcmp_ref, vcmpT_ref, act_ref):
    m = kc_ref.shape[1] // CMP_STRIDE
    nc = m - 1
    ncp = act_ref.shape[0]
    row_m = lax.broadcasted_iota(jnp.int32, (m, CMP_HIDDEN), 0)
    row_p = lax.broadcasted_iota(jnp.int32, (ncp, HEAD_DIM), 0)
    col_p = lax.broadcasted_iota(jnp.int32, (HEAD_DIM, ncp), 1)
    col_f = lax.broadcasted_iota(jnp.int32, (ncp, HEAD_DIM), 1)
    row_o = lax.broadcasted_iota(jnp.int32, (V_ROWS - HEAD_DIM, ncp), 0)
    act_ref[...] = jnp.zeros(act_ref.shape, F32)
    for which, src_ref in enumerate((kc_ref, vc_ref)):
        p = jnp.zeros((m, 2 * NSA_GROUPS * CMP_HIDDEN), F32)
        for tl in range(CMP_STRIDE):
            p = p + jnp.dot(src_ref[0, pl.ds(tl, m, stride=CMP_STRIDE), :].astype(BF16),
                            wbig_ref[which, tl * D_KV:(tl + 1) * D_KV, :], preferred_element_type=F32)
        posc = jnp.dot(pos_ref[which], w1_ref[which], preferred_element_type=F32)[0:1]
        for g in range(NSA_GROUPS):
            top = p[:, (2 * g) * CMP_HIDDEN:(2 * g + 1) * CMP_HIDDEN]
            bot = p[:, (2 * g + 1) * CMP_HIDDEN:(2 * g + 2) * CMP_HIDDEN]
            hid = top + pltpu.roll(bot, m - 1, 0) + (b1_ref[which] + posc)
            act = jnp.where(row_m < nc, _gelu_tanh(hid), 0.0)
            act_ref[CMP_PAD:CMP_PAD + m, :] = act
            ap = act_ref[...].astype(BF16)
            if which == 0:
                kc = jnp.dot(ap, w2k_ref[...], preferred_element_type=F32) + b2k_ref[...]
                ms = jnp.mean(kc * kc, axis=-1, keepdims=True)
                kc = kc * lax.rsqrt(ms + NORM_EPS) * kg_ref[...]
                valid = (row_p >= CMP_PAD) & (row_p < CMP_PAD + nc)
                kcmp_ref[0, g, :, 0:HEAD_DIM] = jnp.where(valid, kc, 0.0).astype(BF16)
                flag = (col_f == 0) & jnp.logical_not(valid)
                kcmp_ref[0, g, :, HEAD_DIM:] = jnp.where(flag, 1.0, 0.0).astype(BF16)
            else:
                vt = lax.dot_general(w2vT_ref[...], ap, (((1,), (1,)), ((), ())),
                                     preferred_element_type=F32) + b2v_ref[...]
                valid = (col_p >= CMP_PAD) & (col_p < CMP_PAD + nc)
                vcmpT_ref[0, g, 0:HEAD_DIM, :] = jnp.where(valid, vt, 0.0).astype(BF16)
                vcmpT_ref[0, g, HEAD_DIM:, :] = jnp.where(row_o == 0, 1.0, 0.0).astype(BF16)


def _compress(kc, vc, wbig, pos8, w1, b1, w2k, w2vT, b2k, b2v, kg, *, ncp):
    B = kc.shape[0]
    full = lambda a: pl.BlockSpec(a.shape, lambda b: (0,) * a.ndim)
    return pl.pallas_call(
        _compress_kernel, grid=(B,),
        in_specs=[pl.BlockSpec((1,) + kc.shape[1:], lambda b: (b, 0, 0)),
                  pl.BlockSpec((1,) + vc.shape[1:], lambda b: (b, 0, 0)),
                  full(wbig), full(pos8), full(w1), full(b1), full(w2k), full(w2vT),
                  full(b2k), full(b2v), full(kg)],
        out_specs=(pl.BlockSpec((1, NSA_GROUPS, ncp, 2 * HEAD_DIM), lambda b: (b, 0, 0, 0)),
                   pl.BlockSpec((1, NSA_GROUPS, V_ROWS, ncp), lambda b: (b, 0, 0, 0))),
        out_shape=(jax.ShapeDtypeStruct((B, NSA_GROUPS, ncp, 2 * HEAD_DIM), BF16),
                   jax.ShapeDtypeStruct((B, NSA_GROUPS, V_ROWS, ncp), BF16)),
        scratch_shapes=[pltpu.VMEM((ncp, CMP_HIDDEN), F32)],
        compiler_params=_cparams(("parallel",)), name="compress",
    )(kc, vc, wbig, pos8, w1, b1, w2k, w2vT, b2k, b2v, kg)


def _flash_update(s_ref, smax_ref, vT, m_ref, acc_ref):
    m_prev = m_ref[...]
    m_new = jnp.maximum(m_prev, smax_ref[...])
    p = jnp.exp2(s_ref[...] - m_new)
    acc_ref[...] = (jnp.exp2(m_prev - m_new) * acc_ref[...]
                    + jnp.dot(vT, p.astype(BF16), preferred_element_type=F32))
    m_ref[...] = m_new


def _attn_kernel(qT_ref, kcmp_ref, vcmpT_ref, bc_ref, ksel_ref, vselT_ref, kwin_ref, vwinT_ref,
                 stab_ref, wtab_ref, gT_ref, o_ref,
                 lc_ref, psum_ref, qaug_ref, qfar_ref, oc_ref, ow_ref, ms_ref, accs_ref, sa_ref, sb_ref, ma_ref, mb_ref):
    qt = pl.program_id(2)
    ncp = kcmp_ref.shape[2]
    ns = qaug_ref.shape[0] - 2 * HEAD_DIM
    nq = NSA_HPG * QT
    t_lane = qt * QT + lax.broadcasted_iota(jnp.int32, (1, QT), 1)

    for r in range(NSA_HPG):
        qaug_ref[0:HEAD_DIM, r * QT:(r + 1) * QT] = qT_ref[0, r * HEAD_DIM:(r + 1) * HEAD_DIM, :]
    flag_row = lax.broadcasted_iota(jnp.int32, (HEAD_DIM, nq), 0) == 0
    qaug_ref[HEAD_DIM:2 * HEAD_DIM, :] = jnp.where(flag_row, NEG_INF, 0.0).astype(BF16)
    qk = qaug_ref[0:2 * HEAD_DIM, :]

    rho = lax.broadcasted_iota(jnp.int32, (ncp, nq), 0)
    band0 = pl.multiple_of(qt * (QT // CMP_STRIDE), V7X_SUBLANES)
    lc_ref[...] = jnp.dot(kcmp_ref[0, 0], qk, preferred_element_type=F32)
    lc_ref[pl.ds(band0, CMP_BAND), :] += bc_ref[0]
    lc = jnp.where(rho < band0 + CMP_BAND, lc_ref[...], NEG_INF)
    e = jnp.exp2(lc - jnp.max(lc, axis=0, keepdims=True))
    oc_aug = jnp.dot(vcmpT_ref[0, 0], e.astype(BF16), preferred_element_type=F32)
    t_q = qt * QT + lax.broadcasted_iota(jnp.int32, (1, nq), 1) % QT
    inv_c = jnp.where(t_q >= CMP_BLOCK - 1, 1.0 / oc_aug[HEAD_DIM:HEAD_DIM + 1, :], 0.0)
    oc_ref[...] = oc_aug[0:HEAD_DIM] * inv_c
    p = e * inv_c
    psum = (p[:, 0:QT] + p[:, QT:2 * QT]) + (p[:, 2 * QT:3 * QT] + p[:, 3 * QT:4 * QT])
    for c in range(QT // V7X_LANES):
        psum_ref[c] = psum[:, c * V7X_LANES:(c + 1) * V7X_LANES]

    def strided_sum(c):
        acc = psum_ref[c, pl.ds(CMP_PAD - 1, ns, stride=4), :]
        for k in range(1, 5):
            acc = acc + psum_ref[c, pl.ds(CMP_PAD - 1 + k, ns, stride=4), :]
        return acc

    imp = jnp.concatenate([strided_sum(c) for c in range(QT // V7X_LANES)], axis=1)
    jrow = lax.broadcasted_iota(jnp.int32, (ns, QT), 0)
    cur = t_lane // SEL_BLOCK
    forced = (jrow == 0) | (jrow == cur) | (jrow == cur - 1)
    live = jrow * SEL_BLOCK <= t_lane
    sel = forced & live
    score = jnp.where(live, jnp.where(forced, -jnp.inf, imp), NEG_INF)
    jrow_f = jrow.astype(F32)

    def col_reduce(x, pair, reduce):
        parts = [x[i * V7X_SUBLANES:(i + 1) * V7X_SUBLANES, :] for i in range(x.shape[0] // V7X_SUBLANES)]
        while len(parts) > 1:
            parts = [pair(parts[i], parts[i + 1]) for i in range(0, len(parts) - 1, 2)] + parts[len(parts) & ~1:]
        return reduce(parts[0], axis=0, keepdims=True)

    for _ in range(min(SEL_TOPN, ns) - 3):
        mx = col_reduce(score, jnp.maximum, jnp.max)
        idx = col_reduce(jnp.where(score == mx, jrow_f, float(ns)), jnp.minimum, jnp.min)
        hit = jrow_f == idx
        sel = sel | hit
        score = jnp.where(hit, -jnp.inf, score)
    negmask = jnp.where(sel, 0.0, NEG_INF).astype(BF16)
    negfar = jnp.where(sel & (jrow < (qt - 1) * (QT // SEL_BLOCK)), 0.0, NEG_INF).astype(BF16)
    qfar_ref[0:2 * HEAD_DIM, :] = qaug_ref[0:2 * HEAD_DIM, :]
    for r in range(NSA_HPG):
        qaug_ref[2 * HEAD_DIM:, r * QT:(r + 1) * QT] = negmask
        qfar_ref[2 * HEAD_DIM:, r * QT:(r + 1) * QT] = negfar

    w0 = pl.multiple_of(qt * QT, QT)
    sw = jnp.dot(kwin_ref[0, 0, pl.ds(w0, WIN_KEYS), :], qk, preferred_element_type=F32) + wtab_ref[0]
    pw = jnp.exp2(sw - jnp.max(sw, axis=0, keepdims=True))
    vw = jnp.concatenate([vwinT_ref[0, 0, qt + j] for j in range(WIN_KEYS // QT)], axis=1)
    ow_aug = jnp.dot(vw, pw.astype(BF16), preferred_element_type=F32)
    ow_ref[...] = ow_aug[0:HEAD_DIM] * (1.0 / ow_aug[HEAD_DIM:HEAD_DIM + 1, :])

    n0 = pl.multiple_of((ATT_PAD // QT - 1 + qt) * QT, QT)
    sn = jnp.dot(ksel_ref[0, 0, pl.ds(n0, NEAR_KEYS), :], qaug_ref[...],
                 preferred_element_type=F32) + stab_ref[0]
    mn = jnp.max(sn, axis=0, keepdims=True)
    pn = jnp.exp2(sn - mn)
    vn = jnp.concatenate([vselT_ref[0, 0, ATT_PAD // QT - 1 + qt + j] for j in range(NEAR_KEYS // QT)], axis=1)
    ms_ref[...] = mn
    accs_ref[...] = jnp.dot(vn, pn.astype(BF16), preferred_element_type=F32)

    tiles_per_far = FAR_KEYS // QT
    n_far = (qt - 1 + tiles_per_far - 1) // tiles_per_far
    last_far = (ksel_ref.shape[2] - ATT_PAD) // FAR_KEYS - 1

    def far_logits(g, dst_ref, dmax_ref):
        r0 = pl.multiple_of(ATT_PAD + jnp.minimum(g, last_far) * FAR_KEYS, FAR_KEYS)
        s = jnp.dot(ksel_ref[0, 0, pl.ds(r0, FAR_KEYS), :], qfar_ref[...], preferred_element_type=F32)
        dst_ref[...] = s
        dmax_ref[...] = jnp.max(s, axis=0, keepdims=True)

    def far_values(g):
        t0 = (ATT_PAD + g * FAR_KEYS) // QT
        return jnp.concatenate([vselT_ref[0, 0, t0 + j] for j in range(tiles_per_far)], axis=1)

    far_logits(0, sa_ref, ma_ref)

    def far_body(j, carry):
        bufs = ((sa_ref, ma_ref), (sb_ref, mb_ref))
        for u in range(FAR_UNROLL):
            g = FAR_UNROLL * j + u
            far_logits(g + 1, *bufs[(u + 1) % 2])
            _flash_update(*bufs[u % 2], far_values(jnp.minimum(g, last_far)), ms_ref, accs_ref)
        return carry

    lax.fori_loop(0, (n_far + FAR_UNROLL - 1) // FAR_UNROLL, far_body, 0)

    o_s = accs_ref[0:HEAD_DIM, :] * (1.0 / accs_ref[HEAD_DIM:HEAD_DIM + 1, :])
    for r in range(NSA_HPG):
        cols = slice(r * QT, (r + 1) * QT)
        o = (gT_ref[0, 3 * r:3 * r + 1, :] * oc_ref[:, cols] + gT_ref[0, 3 * r + 1:3 * r + 2, :] * o_s[:, cols]
             + gT_ref[0, 3 * r + 2:3 * r + 3, :] * ow_ref[:, cols])
        o_ref[0, :, r * HEAD_DIM:(r + 1) * HEAD_DIM] = o.T


def _attn(qT, kcmp, vcmpT, bc, ksel, vT5, kwin, stab, wtab, gT):
    B, _, T = qT.shape
    ncp = kcmp.shape[2]
    ns = T // SEL_BLOCK
    nq = NSA_HPG * QT
    tp = T + ATT_PAD
    nt = tp // QT
    grid = (B, NSA_GROUPS, T // QT)

    def held(shape, index_map):
        return pl.BlockSpec(shape, index_map, pipeline_mode=pl.Buffered(1))

    in_specs = [
        pl.BlockSpec((1, NSA_HPG * HEAD_DIM, QT), lambda b, g, q: (b, g, q)),
        held((1, 1, ncp, 2 * HEAD_DIM), lambda b, g, q: (b, g, 0, 0)),
        held((1, 1, V_ROWS, ncp), lambda b, g, q: (b, g, 0, 0)),
        held((1, CMP_BAND, nq), lambda b, g, q: (g, 0, 0)),
        pl.BlockSpec((1, 1, tp, 2 * HEAD_DIM + ns), lambda b, g, q: (b, g, 0, 0)),
        pl.BlockSpec((1, 1, nt, V_ROWS, QT), lambda b, g, q: (b, g, 0, 0, 0)),
        pl.BlockSpec((1, 1, tp, 2 * HEAD_DIM), lambda b, g, q: (b, g, 0, 0)),
        pl.BlockSpec((1, 1, nt, V_ROWS, QT), lambda b, g, q: (b, NSA_GROUPS + g, 0, 0, 0)),
        held((1, NEAR_KEYS, nq), lambda b, g, q: (g, 0, 0)),
        held((1, WIN_KEYS, nq), lambda b, g, q: (g, 0, 0)),
        pl.BlockSpec((1, GATE_ROWS, QT), lambda b, g, q: (b, g, q)),
    ]
    scratch = [
        pltpu.VMEM((ncp, nq), F32),
        pltpu.VMEM((QT // V7X_LANES, ncp, V7X_LANES), F32),
        pltpu.VMEM((2 * HEAD_DIM + ns, nq), BF16),
        pltpu.VMEM((2 * HEAD_DIM + ns, nq), BF16),
        pltpu.VMEM((HEAD_DIM, nq), F32),
        pltpu.VMEM((HEAD_DIM, nq), F32),
        pltpu.VMEM((1, nq), F32),
        pltpu.VMEM((V_ROWS, nq), F32),
        pltpu.VMEM((FAR_KEYS, nq), F32),
        pltpu.VMEM((FAR_KEYS, nq), F32),
        pltpu.VMEM((1, nq), F32), pltpu.VMEM((1, nq), F32),
    ]
    return pl.pallas_call(
        _attn_kernel, grid=grid, in_specs=in_specs,
        out_specs=pl.BlockSpec((1, QT, NSA_HPG * HEAD_DIM), lambda b, g, q: (b, q, g)),
        out_shape=jax.ShapeDtypeStruct((B, T, D_NSA), F32),
        scratch_shapes=scratch,
        compiler_params=_cparams(("parallel", "parallel", "arbitrary")), name="attn",
    )(qT, kcmp, vcmpT, bc, ksel, vT5, kwin, vT5, stab, wtab, gT)


def _rwkv_tokens(rw_ref, halo_ref, mu_ref, w0_ref, w2_ref, a0_ref, a2_ref, g2_ref, kk_ref, ka_ref,
                 rk_ref, bd_ref, ext_ref):
    tm = rw_ref.shape[1]
    first_tile = pl.program_id(1) == 0
    ext_ref[0:V7X_SUBLANES, :] = jnp.where(first_tile, 0.0, halo_ref[0])
    ext_ref[V7X_SUBLANES:, :] = rw_ref[0]
    cur = rw_ref[0]
    prev = ext_ref[pl.ds(V7X_SUBLANES - 1, tm), :]
    mixed = cur + (prev - cur) * mu_ref[...]
    c = D_RWKV
    r = mixed[:, 0:c]
    k = mixed[:, c:2 * c]
    v = mixed[:, 2 * c:3 * c]
    xw = mixed[:, 3 * c:3 * c + LORA_W]
    xa = mixed[:, 3 * c + LORA_W:3 * c + LORA_W + LORA_A]
    xg = mixed[:, 3 * c + LORA_W + LORA_A:]

    z = -(w0_ref[...] + _bdot(jnp.tanh(xw), w2_ref[...]))
    softplus = jnp.maximum(z, 0.0) + jnp.log1p(jnp.exp(-jnp.abs(z)))
    w = -softplus - 0.5
    a = jax.nn.sigmoid(a0_ref[...] + _bdot(xa, a2_ref[...]))
    g = _bdot(jax.nn.sigmoid(xg), g2_ref[...])

    kk = k * kk_ref[...]
    n2 = _head_sum(kk * kk, bd_ref[...])
    kkn = kk / jnp.maximum(jnp.sqrt(n2), 1e-12)
    k2 = k * (1.0 + (a - 1.0) * ka_ref[...])
    bonus = _head_sum(r * k2 * rk_ref[...], bd_ref[...]) * v

    lw = -jnp.exp(w)
    return dict(r=r, lw=lw, k=k2, v=v, kk=kkn, b=kkn * a, g=g, bonus=bonus)


WKV_BATCH = 8
WKV_GROUP = 4
WKV_GW = WKV_GROUP * HEAD_DIM


def _wkv_kernel(rw_ref, halo_ref, mu_ref, w0_ref, w2_ref, a0_ref, a2_ref, g2_ref, kk_ref, ka_ref, rk_ref, bd_ref,
                y_ref, g_ref, bonus_ref, s_ref, ext_ref, *, chunks):
    L = WKV_CHUNK
    tok = _rwkv_tokens(rw_ref, halo_ref, mu_ref, w0_ref, w2_ref, a0_ref, a2_ref, g2_ref, kk_ref, ka_ref,
                       rk_ref, bd_ref, ext_ref)
    g_ref[0] = tok["g"]
    bonus_ref[0] = tok["bonus"]
    assert L == HEAD_DIM
    gw = WKV_GW
    n_groups = RWKV_HEADS // WKV_GROUP

    @pl.when(pl.program_id(1) == 0)
    def _():
        s_ref[...] = jnp.zeros(s_ref.shape, F32)

    ti = lax.broadcasted_iota(jnp.int32, (L, L), 0)
    tj = lax.broadcasted_iota(jnp.int32, (L, L), 1)
    tri = jnp.where(ti >= tj, 1.0, 0.0).astype(BF16)
    row = lax.broadcasted_iota(jnp.int32, (L, gw), 0)
    col = lax.broadcasted_iota(jnp.int32, (L, gw), 1) % L
    low_strict = col < row
    low_incl = col <= row
    eye_sbs = jnp.where(col == row, 1.0, 0.0)
    brow = lax.broadcasted_iota(jnp.int32, (gw, gw), 0) // L
    bcol = lax.broadcasted_iota(jnp.int32, (gw, gw), 1) // HEAD_DIM
    same_head = brow == bcol

    def bd_rows(x):
        xb = x.astype(BF16)
        return jnp.where(same_head, jnp.concatenate([xb] * WKV_GROUP, axis=0), jnp.zeros((), BF16))

    def mm(a, b_bf16):
        return jnp.dot(a.astype(BF16), b_bf16, preferred_element_type=F32)

    def tn(a, b):
        return lax.dot_general(a.astype(BF16), b.astype(BF16), (((0,), (0,)), ((), ())),
                               preferred_element_type=F32)

    pre, w2, c2, m_lr, d_sbs = {}, {}, {}, {}, {}

    def grp(c, g, name):
        return pre[c][name][:, g * gw:(g + 1) * gw]

    for c0 in range(0, chunks, WKV_BATCH):
        batch = range(c0, min(c0 + WKV_BATCH, chunks))
        inst = [(c, g) for c in batch for g in range(n_groups)]
        for c in batch:
            rows = slice(c * L, (c + 1) * L)
            lw = tok["lw"][rows]
            cs = _dot_exact_rhs_left(tri, lw)
            c_last = cs[L - 1:L, :]
            e_nc = jnp.exp(-cs)
            e_lc = jnp.exp(c_last - cs)
            kk = tok["kk"][rows]
            b = tok["b"][rows]
            k = tok["k"][rows]
            pre[c] = dict(a=-kk * jnp.exp(cs - lw), r=tok["r"][rows] * jnp.exp(cs), bh=b * e_nc, kh=k * e_nc,
                          be=b * e_lc, ke=k * e_lc, v=tok["v"][rows], e_last=jnp.exp(c_last))

        t_all = {}
        for (c, g) in inst:
            lhs = jnp.concatenate([grp(c, g, "a"), grp(c, g, "r")], axis=0).astype(BF16)
            rhs = jnp.concatenate([bd_rows(grp(c, g, "bh")), bd_rows(grp(c, g, "kh"))], axis=0)
            t_all[c, g] = lax.dot_general(lhs, rhs, (((1,), (1,)), ((), ())), preferred_element_type=F32)
        n_m = {i: jnp.where(low_strict, t_all[i][0:L, 0:gw], 0.0) for i in inst}
        tak = {i: jnp.where(low_strict, t_all[i][0:L, gw:], 0.0) for i in inst}
        trb = {i: jnp.where(low_incl, t_all[i][L:, 0:gw], 0.0) for i in inst}
        trk = {i: jnp.where(low_incl, t_all[i][L:, gw:], 0.0) for i in inst}
        z = {i: eye_sbs + n_m[i] for i in inst}
        pw = dict(n_m)
        for _ in range(int(math.log2(L)) - 1):
            pw = {i: mm(pw[i], bd_rows(pw[i])) for i in inst}
            z = {i: z[i] + mm(z[i], bd_rows(pw[i])) for i in inst}
        vbd = {(c, g): bd_rows(grp(c, g, "v")) for (c, g) in inst}
        w1 = {(c, g): mm(z[c, g], bd_rows(grp(c, g, "a"))) for (c, g) in inst}
        tv = {i: mm(tak[i], vbd[i]) for i in inst}
        c1 = {i: mm(z[i], bd_rows(tv[i])) for i in inst}
        w2.update({(c, g): grp(c, g, "r") + mm(trb[c, g], bd_rows(w1[c, g])) for (c, g) in inst})
        c2.update({i: mm(trb[i], bd_rows(c1[i])) + mm(trk[i], vbd[i]) for i in inst})
        m_lr.update({(c, g): jnp.where(same_head, tn(w1[c, g], grp(c, g, "be")), 0.0).astype(BF16)
                     for (c, g) in inst})
        for (c, g) in inst:
            full = jnp.where(same_head,
                             tn(jnp.concatenate([c1[c, g], grp(c, g, "v")], axis=0),
                                jnp.concatenate([grp(c, g, "be"), grp(c, g, "ke")], axis=0)), 0.0)
            d_sbs[c, g] = ((full[0:L] + full[L:2 * L]) + (full[2 * L:3 * L] + full[3 * L:4 * L]))

    for g in range(n_groups):
        s = s_ref[:, g * gw:(g + 1) * gw]
        for c in range(chunks):
            g_bd = jnp.where(same_head, jnp.concatenate([s.T.astype(BF16)] * WKV_GROUP, axis=1),
                             jnp.zeros((), BF16))
            y_ref[0, c * L:(c + 1) * L, g * gw:(g + 1) * gw] = mm(w2[c, g], g_bd) + c2[c, g]
            s = s * grp(c, g, "e_last") + mm(s, m_lr[c, g]) + d_sbs[c, g]
        s_ref[:, g * gw:(g + 1) * gw] = s


def _dot_exact_rhs_left(m_bf16, x):
    hi, mid, lo = _split3(x)
    return (jnp.dot(m_bf16, hi, preferred_element_type=F32)
            + jnp.dot(m_bf16, mid, preferred_element_type=F32)
            + jnp.dot(m_bf16, lo, preferred_element_type=F32))


def _wkv(rw, mu, w0, w2, a0, a2, g2, k_k, k_a, r_k, bd, *, chunks):
    B, T, C = rw.shape
    rows = chunks * WKV_CHUNK
    hb = rows // V7X_SUBLANES
    full = lambda a: pl.BlockSpec(a.shape, lambda bb, c: (0,) * a.ndim)
    tok = pl.BlockSpec((1, rows, D_RWKV), lambda bb, c: (bb, c, 0))
    return pl.pallas_call(
        functools.partial(_wkv_kernel, chunks=chunks), grid=(B, T // rows),
        in_specs=[pl.BlockSpec((1, rows, C), lambda bb, c: (bb, c, 0)),
                  pl.BlockSpec((1, V7X_SUBLANES, C), lambda bb, c: (bb, jnp.maximum(c * hb - 1, 0), 0)),
                  full(mu), full(w0), full(w2), full(a0), full(a2), full(g2), full(k_k), full(k_a),
                  full(r_k), full(bd)],
        out_specs=(tok,) * 3,
        out_shape=(jax.ShapeDtypeStruct((B, T, D_RWKV), F32),) * 3,
        scratch_shapes=[pltpu.VMEM((HEAD_DIM, D_RWKV), F32), pltpu.VMEM((rows + V7X_SUBLANES, C), F32)],
        compiler_params=_cparams(("parallel", "arbitrary")), name="wkv",
    )(rw, rw, mu, w0, w2, a0, a2, g2, k_k, k_a, r_k, bd)


def _mix_rows(x, on, y, bonus, g, bd_ref, lnw_ref, lnb_ref, wo_ref, g2_ref):
    mu = _dot_exact_rhs(y, bd_ref[...]) * (1.0 / HEAD_DIM)
    yc = y - mu
    var = _head_sum(yc * yc, bd_ref[...]) * (1.0 / HEAD_DIM)
    yn = yc * lax.rsqrt(var + GN_EPS) * lnw_ref[...] + lnb_ref[...]
    orw = (yn + bonus) * g
    x1 = x + _bdot(on, wo_ref[0:D_NSA, :]) + _bdot(orw, wo_ref[D_NSA:, :])
    ms = jnp.mean(x1 * x1, axis=-1, keepdims=True)
    return x1, (x1 * lax.rsqrt(ms + NORM_EPS) * g2_ref[...]).astype(BF16)


FFN_HALO = 16


def _ffn_kernel(x_ref, on_ref, y_ref, bonus_ref, g_ref, xh_ref, onh_ref, yh_ref, bonush_ref, gh_ref,
                bd_ref, lnw_ref, lnb_ref, wo_ref, g2_ref,
                wv_ref, wg_ref, cwv_ref, cwg_ref, cbv_ref, cbg_ref, wd_ref,
                o_ref, hext_ref, extv_ref, extg_ref, act_ref, x1_ref, *, tiles_per_seq, ft):
    tm = x_ref.shape[0]
    dff = wd_ref.shape[0]
    seq_start = (pl.program_id(0) % tiles_per_seq) == 0
    ext = lambda h_ref, m_ref: jnp.concatenate([h_ref[...], m_ref[...]], axis=0)
    x1, h2 = _mix_rows(ext(xh_ref, x_ref), ext(onh_ref, on_ref), ext(yh_ref, y_ref), ext(bonush_ref, bonus_ref),
                       ext(gh_ref, g_ref), bd_ref, lnw_ref, lnb_ref, wo_ref, g2_ref)
    x1_ref[...] = x1[FFN_HALO:]
    row = lax.broadcasted_iota(jnp.int32, h2.shape, 0)
    hext_ref[...] = jnp.where(seq_start & (row < FFN_HALO), jnp.zeros((), BF16), h2)

    for j in range(dff // ft):
        cols = slice(j * ft, (j + 1) * ft)

        def conv_branch(w_ref, cw_ref, cb_ref, ext_ref):
            ext_ref[j % 2] = jnp.dot(hext_ref[...], w_ref[:, cols], preferred_element_type=F32)
            out = cb_ref[:, cols] + ext_ref[j % 2, pl.ds(FFN_HALO, tm), :] * cw_ref[CONV_W - 1:CONV_W, cols]
            for i in range(CONV_W - 1):
                back = CONV_W - 1 - i
                out = out + ext_ref[j % 2, pl.ds(FFN_HALO - back, tm), :] * cw_ref[i:i + 1, cols]
            return out

        u_val = conv_branch(wv_ref, cwv_ref, cbv_ref, extv_ref)
        u_gate = conv_branch(wg_ref, cwg_ref, cbg_ref, extg_ref)
        act_ref[:, cols] = ((u_gate * jax.nn.sigmoid(u_gate)) * u_val).astype(BF16)
    o_ref[...] = x1_ref[...] + jnp.dot(act_ref[...], wd_ref[...], preferred_element_type=F32)


def _ffn(x, on, y, bonus, g, bd, lnw, lnb, wo, g2, wv, wg, cwv, cwg, cbv, cbg, wd, *, tm, ft, seq_len):
    n, d = x.shape
    dff = wv.shape[1]
    hb = tm // FFN_HALO
    kern = functools.partial(_ffn_kernel, tiles_per_seq=seq_len // tm, ft=ft)
    resident = lambda a: pl.BlockSpec(a.shape, lambda i: (0,) * a.ndim, pipeline_mode=pl.Buffered(1))
    tile = lambda a: pl.BlockSpec((tm, a.shape[1]), lambda i: (i, 0))
    halo = lambda a: pl.BlockSpec((FFN_HALO, a.shape[1]), lambda i: (jnp.maximum(i * hb - 1, 0), 0))
    tokens = (x, on, y, bonus, g)
    params = (bd, lnw, lnb, wo, g2, wv, wg, cwv, cwg, cbv, cbg, wd)
    return pl.pallas_call(
        kern, grid=(n // tm,),
        in_specs=[tile(a) for a in tokens] + [halo(a) for a in tokens] + [resident(a) for a in params],
        out_specs=pl.BlockSpec((tm, d), lambda i: (i, 0)),
        out_shape=jax.ShapeDtypeStruct((n, d), F32),
        scratch_shapes=[pltpu.VMEM((tm + FFN_HALO, d), BF16),
                        pltpu.VMEM((2, tm + FFN_HALO, ft), F32), pltpu.VMEM((2, tm + FFN_HALO, ft), F32),
                        pltpu.VMEM((tm, dff), BF16), pltpu.VMEM((tm, d), F32)],
        compiler_params=_cparams(("parallel",)), name="ffn",
    )(*tokens, *tokens, *params)


def _t5_bucket(dist):
    n = np.maximum(dist, 0)
    max_exact = N_BUCKETS // 2
    nf = np.maximum(n, 1).astype(np.float32)
    large = max_exact + (np.log(nf / np.float32(max_exact)) / np.float32(math.log(MAX_DISTANCE / max_exact))
                         * np.float32(N_BUCKETS - max_exact)).astype(np.int32)
    large = np.minimum(large, N_BUCKETS - 1)
    return np.where(n < max_exact, n, large)


def _toeplitz_kernel(*refs, steps):
    n = len(steps)
    for v_ref, o_ref, step in zip(refs[:n], refs[n:], steps):
        rows, width = o_ref.shape[1], o_ref.shape[2]
        x = jnp.broadcast_to(v_ref[0], (rows, v_ref.shape[2]))
        o_ref[0] = pltpu.roll(x, 0, 1, stride=step, stride_axis=0)[:, :width]


def _toeplitz(specs):
    return pl.pallas_call(
        functools.partial(_toeplitz_kernel, steps=tuple(step for _, _, step in specs)), grid=(NSA_GROUPS, NSA_HPG),
        in_specs=[pl.BlockSpec((1, 1, v.shape[2]), lambda g, r: (g * NSA_HPG + r, 0, 0)) for v, _, _ in specs],
        out_specs=tuple(pl.BlockSpec((1, rows, QT), lambda g, r: (g, 0, r)) for _, rows, _ in specs),
        out_shape=tuple(jax.ShapeDtypeStruct((NSA_GROUPS, rows, NSA_HPG * QT), F32) for _, rows, _ in specs),
        compiler_params=_cparams(("parallel", "parallel")), name="toeplitz",
    )(*[v for v, _, _ in specs])


def _bias_tables(rel_bias):
    rel = (rel_bias - rel_bias[N_BUCKETS - 1][None, :]) * LOG2E

    def table(rows, step, d0, d_hi):
        period = step * rows + QT
        i = np.arange(period)
        d = d0 + np.where(i < QT, i, i - period)
        onehot = (_t5_bucket(d)[:, None] == np.arange(N_BUCKETS)[None, :]).astype(np.float32)
        v = jnp.dot(jnp.asarray(onehot), rel, precision=lax.Precision.HIGHEST)
        v = jnp.where(jnp.asarray((d >= 0) & (d < d_hi))[:, None], v, NEG_INF).T
        return v.reshape(NSA_HEADS, 1, period), rows, step

    no_limit = 1 << 30
    stab, wtab, bc = _toeplitz([table(NEAR_KEYS, 1, QT, no_limit), table(WIN_KEYS, 1, WINDOW, WINDOW),
                                table(CMP_BAND, CMP_STRIDE, CMP_STRIDE * CMP_PAD - CMP_BLOCK + 1, no_limit)])
    return stab, wtab, bc


def _compress_weights(w1):
    half = CMP_BLOCK // 2
    rows = half * HEAD_DIM
    w3 = jnp.concatenate([w1[:rows], w1[rows:]], axis=1).reshape(half, HEAD_DIM, 2 * CMP_HIDDEN)
    z = jnp.zeros_like(w3)
    assert NSA_GROUPS == 2
    big = jnp.concatenate([jnp.concatenate([w3, z], axis=2), jnp.concatenate([z, w3], axis=2)], axis=1)
    return big.reshape(half * NSA_GROUPS * HEAD_DIM, NSA_GROUPS * 2 * CMP_HIDDEN)


def kernel(x, norm1_g, w_in, q_norm_g, k_norm_g, cmp_pos, cmp_w1, cmp_b1, cmp_w2, cmp_b2, rel_bias, rwkv_mu,
           w0, w2, a0, a2, g2, k_k, k_a, r_k, ln_x_w, ln_x_b, w_out, norm2_g, ffn_up, conv_w, conv_b, ffn_down):
    B, T, D = x.shape
    depth = w_in.shape[0]
    d_ff = ffn_down.shape[1]
    assert T % 2048 == 0 and D_NSA + 6 * D_KV + 3 * NSA_HEADS + D_RWKV_IN == w_in.shape[2]
    ncp = T // CMP_STRIDE + V7X_LANES
    stab, wtab, bc = _bias_tables(rel_bias)
    ii = jnp.arange(D_RWKV)
    bd = (ii[:, None] // HEAD_DIM == ii[None, :] // HEAD_DIM).astype(BF16)

    for l in range(depth):
        wi = w_in[l]
        o = D_NSA
        q_w, kc_w, vc_w, ksl_w, vsl_w, kwn_w, vwn_w = (
            wi[:, 0:o], wi[:, o:o + D_KV], wi[:, o + D_KV:o + 2 * D_KV], wi[:, o + 2 * D_KV:o + 3 * D_KV],
            wi[:, o + 3 * D_KV:o + 4 * D_KV], wi[:, o + 4 * D_KV:o + 5 * D_KV], wi[:, o + 5 * D_KV:o + 6 * D_KV])
        gl_w = wi[:, o + 6 * D_KV:o + 6 * D_KV + 3 * NSA_HEADS]
        rw_w = wi[:, o + 6 * D_KV + 3 * NSA_HEADS:]
        wstd = jnp.concatenate([kc_w, vc_w, ksl_w, kwn_w, rw_w], axis=1).astype(BF16)
        gl_rows = gl_w.T.reshape(NSA_GROUPS, 3 * NSA_HPG, D)
        gl_rows = jnp.pad(gl_rows, ((0, 0), (0, GATE_ROWS - 3 * NSA_HPG), (0, 0))).reshape(-1, D)
        wt = jnp.concatenate([q_w.T, vsl_w.T, vwn_w.T, gl_rows], axis=0).astype(BF16)
        qg = q_norm_g[l].reshape(HEAD_DIM, 1)
        kg = jnp.stack([jnp.tile(k_norm_g[l, 1], NSA_GROUPS), jnp.tile(k_norm_g[l, 2], NSA_GROUPS)])

        qT, kc, vc, ksel, kwin, vT5, gT, rw = _proj(x, norm1_g[l].reshape(1, D), wstd, wt, qg, kg, tm=512)

        wbig = jnp.stack([_compress_weights(cmp_w1[l, 0]), _compress_weights(cmp_w1[l, 1])]).astype(BF16)
        pos8 = jnp.pad(cmp_pos[l].reshape(2, 1, CMP_BLOCK * HEAD_DIM),
                       ((0, 0), (0, V7X_SUBLANES - 1), (0, 0))).astype(BF16)
        kcmp, vcmpT = _compress(
            kc, vc, wbig, pos8,
            cmp_w1[l].astype(BF16), cmp_b1[l].reshape(2, 1, CMP_HIDDEN), cmp_w2[l, 0].astype(BF16),
            cmp_w2[l, 1].T.astype(BF16), cmp_b2[l, 0].reshape(1, HEAD_DIM), cmp_b2[l, 1].reshape(HEAD_DIM, 1),
            k_norm_g[l, 0].reshape(1, HEAD_DIM), ncp=ncp)

        o_nsa = _attn(qT, kcmp, vcmpT, bc, ksel, vT5, kwin, stab, wtab, gT)

        row = lambda a: a.reshape(1, -1)
        y, g, bonus = _wkv(
            rw, row(rwkv_mu[l]), row(w0[l]), w2[l].astype(BF16), row(a0[l]), a2[l].astype(BF16),
            g2[l].astype(BF16), row(k_k[l]), row(k_a[l]), row(r_k[l]), bd, chunks=8)

        n = B * T
        up = ffn_up[l].astype(BF16)
        x = _ffn(x.reshape(n, D), o_nsa.reshape(n, D_NSA), y.reshape(n, D_RWKV), bonus.reshape(n, D_RWKV),
                 g.reshape(n, D_RWKV), bd, row(ln_x_w[l]), row(ln_x_b[l]), w_out[l].astype(BF16),
                 norm2_g[l].reshape(1, D), up[:, :d_ff], up[:, d_ff:], conv_w[l][:, :d_ff], conv_w[l][:, d_ff:],
                 conv_b[l][:d_ff].reshape(1, -1), conv_b[l][d_ff:].reshape(1, -1),
                 ffn_down[l].astype(BF16), tm=512, ft=256, seq_len=T).reshape(B, T, D)
    return x
```

```python
import functools
import math

import jax
import jax.numpy as jnp
import numpy as np
from jax import lax
from jax.experimental import pallas as pl
from jax.experimental.pallas import tpu as pltpu

F32 = jnp.float32
BF16 = jnp.bfloat16

V7X_LANES = 128
V7X_SUBLANES = 8
V7X_VMEM_LIMIT_BYTES = 56 * 1024 * 1024

HEAD_DIM = 64
NSA_HEADS = 8
NSA_GROUPS = 2
NSA_HPG = NSA_HEADS // NSA_GROUPS
RWKV_HEADS = 8
D_NSA = NSA_HEADS * HEAD_DIM
D_RWKV = RWKV_HEADS * HEAD_DIM
D_KV = NSA_GROUPS * HEAD_DIM
CMP_BLOCK = 32
CMP_STRIDE = 16
CMP_HIDDEN = 128
SEL_BLOCK = 64
SEL_TOPN = 16
WINDOW = 512
N_BUCKETS = 32
MAX_DISTANCE = 128
LORA_W = 64
LORA_A = 64
LORA_G = 128
D_RWKV_IN = 3 * D_RWKV + LORA_W + LORA_A + LORA_G
CONV_W = 3
NORM_EPS = 1e-6
GN_EPS = 64e-5
NEG_INF = -1e30
FORCE_SCORE = 1e9

QT = 256
CMP_PAD = 8
CMP_BAND = QT // CMP_STRIDE + 8
LOG2E = math.log2(math.e)
Q_SCALE = HEAD_DIM ** -0.5 * LOG2E
V_ROWS = HEAD_DIM + 16
ATT_PAD = WINDOW
NEAR_KEYS = 2 * QT
FAR_KEYS = 2 * QT
FAR_UNROLL = 2
WIN_KEYS = WINDOW + QT
WKV_CHUNK = 64
GATE_ROWS = 16
N_T_ROWS = D_NSA + 2 * D_KV + NSA_GROUPS * GATE_ROWS
N_STD_COLS = 4 * D_KV + D_RWKV_IN


def _cparams(sem):
    return pltpu.CompilerParams(dimension_semantics=sem, vmem_limit_bytes=V7X_VMEM_LIMIT_BYTES)


def _bdot(a, b):
    return jnp.dot(a.astype(BF16), b.astype(BF16), preferred_element_type=F32)


def _split3(x):
    hi = x.astype(BF16)
    r1 = x - hi.astype(F32)
    mid = r1.astype(BF16)
    lo = (r1 - mid.astype(F32)).astype(BF16)
    return hi, mid, lo


def _dot_exact_rhs(x, m_bf16):
    hi = x.astype(BF16)
    lo = (x - hi.astype(F32)).astype(BF16)
    return jnp.dot(hi, m_bf16, preferred_element_type=F32) + jnp.dot(lo, m_bf16, preferred_element_type=F32)


def _head_sum(x, m_bf16):
    return jnp.dot(x.astype(BF16), m_bf16, preferred_element_type=F32)


def _proj_kernel(x_ref, g1_ref, wstd_ref, wt_ref, qg_ref, kg_ref,
                 qT_ref, kc_ref, vc_ref, ksel_ref, kwin_ref, vT_ref, gT_ref, rw_ref):
    step = pl.program_id(1)

    @pl.when(step == 0)
    def _():
        for k_ref in (ksel_ref, kwin_ref):
            lanes = lax.broadcasted_iota(jnp.int32, k_ref.shape[2:], 1)
            flagged = jnp.where(lanes == HEAD_DIM, 1.0, 0.0).astype(BF16)
            for g in range(NSA_GROUPS):
                k_ref[0, g] = flagged
        vT_ref[...] = jnp.zeros(vT_ref.shape, BF16)

    @pl.when(step > 0)
    def _():
        _proj_tile(x_ref, g1_ref, wstd_ref, wt_ref, qg_ref, kg_ref,
                   qT_ref, kc_ref, vc_ref, ksel_ref, kwin_ref, vT_ref, gT_ref, rw_ref, step - 1)


def _proj_tile(x_ref, g1_ref, wstd_ref, wt_ref, qg_ref, kg_ref,
               qT_ref, kc_ref, vc_ref, ksel_ref, kwin_ref, vT_ref, gT_ref, rw_ref, tile):
    tm = x_ref.shape[1]
    ns = ksel_ref.shape[3] - 2 * HEAD_DIM
    x = x_ref[0]
    ms = jnp.mean(x * x, axis=-1, keepdims=True)
    h = (x * lax.rsqrt(ms + NORM_EPS) * g1_ref[...]).astype(BF16)

    std = jnp.dot(h, wstd_ref[...], preferred_element_type=F32)
    kc_ref[0] = std[:, 0:D_KV]
    vc_ref[0] = std[:, D_KV:2 * D_KV]
    rw_ref[0] = std[:, 4 * D_KV:]

    lane = lax.broadcasted_iota(jnp.int32, (tm, D_KV), 1)
    first = lane < HEAD_DIM

    def group_rmsnorm(k, gain):
        k2 = k * k
        s0 = jnp.sum(jnp.where(first, k2, 0.0), axis=-1, keepdims=True)
        s1 = jnp.sum(jnp.where(first, 0.0, k2), axis=-1, keepdims=True)
        ms_g = jnp.where(first, s0, s1) * (1.0 / HEAD_DIM)
        return k * lax.rsqrt(ms_g + NORM_EPS) * gain

    ksl = group_rmsnorm(std[:, 2 * D_KV:3 * D_KV], kg_ref[0:1, :]).astype(BF16)
    kwn = group_rmsnorm(std[:, 3 * D_KV:4 * D_KV], kg_ref[1:2, :]).astype(BF16)

    tok = tile * tm + lax.broadcasted_iota(jnp.int32, (tm, ns), 0)
    blk = lax.broadcasted_iota(jnp.int32, (tm, ns), 1)
    onehot = jnp.where((tok // SEL_BLOCK) == blk, 1.0, 0.0).astype(BF16)
    zeros = jnp.zeros((tm, HEAD_DIM), BF16)
    for g in range(NSA_GROUPS):
        ksel_ref[0, g, :, 0:HEAD_DIM] = ksl[:, g * HEAD_DIM:(g + 1) * HEAD_DIM]
        ksel_ref[0, g, :, HEAD_DIM:2 * HEAD_DIM] = zeros
        ksel_ref[0, g, :, 2 * HEAD_DIM:] = onehot
        kwin_ref[0, g, :, 0:HEAD_DIM] = kwn[:, g * HEAD_DIM:(g + 1) * HEAD_DIM]
        kwin_ref[0, g, :, HEAD_DIM:] = zeros

    tr = lax.dot_general(wt_ref[...], h, (((1,), (1,)), ((), ())),
                         preferred_element_type=F32)
    q = tr[0:D_NSA].reshape(NSA_HEADS, HEAD_DIM, tm)
    qms = jnp.mean(q * q, axis=1, keepdims=True)
    qn = q * lax.rsqrt(qms + NORM_EPS) * qg_ref[...][None] * Q_SCALE
    qT_ref[0] = qn.reshape(D_NSA, tm).astype(BF16)
    vt = tr[D_NSA:D_NSA + 2 * D_KV].astype(BF16)
    ones_rows = jnp.where(lax.broadcasted_iota(jnp.int32, (V_ROWS - HEAD_DIM, QT), 0) == 0, 1.0, 0.0).astype(BF16)
    for a in range(2 * NSA_GROUPS):
        for j in range(tm // QT):
            vT_ref[0, a, j, 0:HEAD_DIM, :] = vt[a * HEAD_DIM:(a + 1) * HEAD_DIM, j * QT:(j + 1) * QT]
            vT_ref[0, a, j, HEAD_DIM:, :] = ones_rows
    gT_ref[0] = jax.nn.sigmoid(tr[D_NSA + 2 * D_KV:])


def _proj(x, g1, wstd, wt, qg, kg, *, tm):
    B, T, D = x.shape
    ns = T // SEL_BLOCK
    assert tm == ATT_PAD
    tp = T + ATT_PAD
    grid = (B, T // tm + 1)
    const2 = lambda b, i: (0, 0)
    tile = lambda i: jnp.maximum(i - 1, 0)
    out_shape = (
        jax.ShapeDtypeStruct((B, D_NSA, T), BF16),
        jax.ShapeDtypeStruct((B, T, D_KV), F32),
        jax.ShapeDtypeStruct((B, T, D_KV), F32),
        jax.ShapeDtypeStruct((B, NSA_GROUPS, tp, 2 * HEAD_DIM + ns), BF16),
        jax.ShapeDtypeStruct((B, NSA_GROUPS, tp, 2 * HEAD_DIM), BF16),
        jax.ShapeDtypeStruct((B, 2 * NSA_GROUPS, tp // QT, V_ROWS, QT), BF16),
        jax.ShapeDtypeStruct((B, NSA_GROUPS * GATE_ROWS, T), F32),
        jax.ShapeDtypeStruct((B, T, D_RWKV_IN), F32),
    )
    out_specs = (
        pl.BlockSpec((1, D_NSA, tm), lambda b, i: (b, 0, tile(i))),
        pl.BlockSpec((1, tm, D_KV), lambda b, i: (b, tile(i), 0)),
        pl.BlockSpec((1, tm, D_KV), lambda b, i: (b, tile(i), 0)),
        pl.BlockSpec((1, NSA_GROUPS, tm, 2 * HEAD_DIM + ns), lambda b, i: (b, 0, i, 0)),
        pl.BlockSpec((1, NSA_GROUPS, tm, 2 * HEAD_DIM), lambda b, i: (b, 0, i, 0)),
        pl.BlockSpec((1, 2 * NSA_GROUPS, tm // QT, V_ROWS, QT), lambda b, i: (b, 0, i, 0, 0)),
        pl.BlockSpec((1, NSA_GROUPS * GATE_ROWS, tm), lambda b, i: (b, 0, tile(i))),
        pl.BlockSpec((1, tm, D_RWKV_IN), lambda b, i: (b, tile(i), 0)),
    )
    in_specs = [
        pl.BlockSpec((1, tm, D), lambda b, i: (b, tile(i), 0)),
        pl.BlockSpec(g1.shape, const2),
        pl.BlockSpec(wstd.shape, const2),
        pl.BlockSpec(wt.shape, const2),
        pl.BlockSpec(qg.shape, const2),
        pl.BlockSpec(kg.shape, const2),
    ]
    return pl.pallas_call(
        _proj_kernel, grid=grid, in_specs=in_specs, out_specs=out_specs, out_shape=out_shape,
        compiler_params=_cparams(("parallel", "arbitrary")), name="proj",
    )(x, g1, wstd, wt, qg, kg)


def _gelu_tanh(x):
    c = math.sqrt(2.0 / math.pi)
    return x * (0.5 * (1.0 + jnp.tanh(c * (x + 0.044715 * (x * x * x)))))


def _compress_kernel(kc_ref, vc_ref, wbig_ref, pos_ref, w1_ref, b1_ref, w2k_ref, w2vT_ref,
                     b2k_ref, b2v_ref, kg_ref, kcmp_ref, vcmpT_ref, act_ref):
    m = kc_ref.shape[1] // CMP_STRIDE
    nc = m - 1
    ncp = act_ref.shape[0]
    row_m = lax.broadcasted_iota(jnp.int32, (m, CMP_HIDDEN), 0)
    row_p = lax.broadcasted_iota(jnp.int32, (ncp, HEAD_DIM), 0)
    col_p = lax.broadcasted_iota(jnp.int32, (HEAD_DIM, ncp), 1)
    col_f = lax.broadcasted_iota(jnp.int32, (ncp, HEAD_DIM), 1)
    row_o = lax.broadcasted_iota(jnp.int32, (V_ROWS - HEAD_DIM, ncp), 0)
    act_ref[...] = jnp.zeros(act_ref.shape, F32)
    for which, src_ref in enumerate((kc_ref, vc_ref)):
        p = jnp.zeros((m, 2 * NSA_GROUPS * CMP_HIDDEN), F32)
        for tl in range(CMP_STRIDE):
            p = p + jnp.dot(src_ref[0, pl.ds(tl, m, stride=CMP_STRIDE), :].astype(BF16),
                            wbig_ref[which, tl * D_KV:(tl + 1) * D_KV, :], preferred_element_type=F32)
        posc = jnp.dot(pos_ref[which], w1_ref[which], preferred_element_type=F32)[0:1]
        for g in range(NSA_GROUPS):
            top = p[:, (2 * g) * CMP_HIDDEN:(2 * g + 1) * CMP_HIDDEN]
            bot = p[:, (2 * g + 1) * CMP_HIDDEN:(2 * g + 2) * CMP_HIDDEN]
            hid = top + pltpu.roll(bot, m - 1, 0) + (b1_ref[which] + posc)
            act = jnp.where(row_m < nc, _gelu_tanh(hid), 0.0)
            act_ref[CMP_PAD:CMP_PAD + m, :] = act
            ap = act_ref[...].astype(BF16)
            if which == 0:
                kc = jnp.dot(ap, w2k_ref[...], preferred_element_type=F32) + b2k_ref[...]
                ms = jnp.mean(kc * kc, axis=-1, keepdims=True)
                kc = kc * lax.rsqrt(ms + NORM_EPS) * kg_ref[...]
                valid = (row_p >= CMP_PAD) & (row_p < CMP_PAD + nc)
                kcmp_ref[0, g, :, 0:HEAD_DIM] = jnp.where(valid, kc, 0.0).astype(BF16)
                flag = (col_f == 0) & jnp.logical_not(valid)
                kcmp_ref[0, g, :, HEAD_DIM:] = jnp.where(flag, 1.0, 0.0).astype(BF16)
            else:
                vt = lax.dot_general(w2vT_ref[...], ap, (((1,), (1,)), ((), ())),
                                     preferred_element_type=F32) + b2v_ref[...]
                valid = (col_p >= CMP_PAD) & (col_p < CMP_PAD + nc)
                vcmpT_ref[0, g, 0:HEAD_DIM, :] = jnp.where(valid, vt, 0.0).astype(BF16)
                vcmpT_ref[0, g, HEAD_DIM:, :] = jnp.where(row_o == 0, 1.0, 0.0).astype(BF16)


def _compress(kc, vc, wbig, pos8, w1, b1, w2k, w2vT, b2k, b2v, kg, *, ncp):
    B = kc.shape[0]
    full = lambda a: pl.BlockSpec(a.shape, lambda b: (0,) * a.ndim)
    return pl.pallas_call(
        _compress_kernel, grid=(B,),
        in_specs=[pl.BlockSpec((1,) + kc.shape[1:], lambda b: (b, 0, 0)),
                  pl.BlockSpec((1,) + vc.shape[1:], lambda b: (b, 0, 0)),
                  full(wbig), full(pos8), full(w1), full(b1), full(w2k), full(w2vT),
                  full(b2k), full(b2v), full(kg)],
        out_specs=(pl.BlockSpec((1, NSA_GROUPS, ncp, 2 * HEAD_DIM), lambda b: (b, 0, 0, 0)),
                   pl.BlockSpec((1, NSA_GROUPS, V_ROWS, ncp), lambda b: (b, 0, 0, 0))),
        out_shape=(jax.ShapeDtypeStruct((B, NSA_GROUPS, ncp, 2 * HEAD_DIM), BF16),
                   jax.ShapeDtypeStruct((B, NSA_GROUPS, V_ROWS, ncp), BF16)),
        scratch_shapes=[pltpu.VMEM((ncp, CMP_HIDDEN), F32)],
        compiler_params=_cparams(("parallel",)), name="compress",
    )(kc, vc, wbig, pos8, w1, b1, w2k, w2vT, b2k, b2v, kg)


def _flash_update(s_ref, smax_ref, vT, m_ref, acc_ref):
    m_prev = m_ref[...]
    m_new = jnp.maximum(m_prev, smax_ref[...])
    p = jnp.exp2(s_ref[...] - m_new)
    acc_ref[...] = (jnp.exp2(m_prev - m_new) * acc_ref[...]
                    + jnp.dot(vT, p.astype(BF16), preferred_element_type=F32))
    m_ref[...] = m_new


def _attn_kernel(qT_ref, kcmp_ref, vcmpT_ref, bc_ref, ksel_ref, vselT_ref, kwin_ref, vwinT_ref,
                 stab_ref, wtab_ref, gT_ref, o_ref,
                 lc_ref, psum_ref, qaug_ref, qfar_ref, oc_ref, ow_ref, ms_ref, accs_ref, sa_ref, sb_ref, ma_ref, mb_ref):
    qt = pl.program_id(2)
    ncp = kcmp_ref.shape[2]
    ns = qaug_ref.shape[0] - 2 * HEAD_DIM
    nq = NSA_HPG * QT
    t_lane = qt * QT + lax.broadcasted_iota(jnp.int32, (1, QT), 1)

    for r in range(NSA_HPG):
        qaug_ref[0:HEAD_DIM, r * QT:(r + 1) * QT] = qT_ref[0, r * HEAD_DIM:(r + 1) * HEAD_DIM, :]
    flag_row = lax.broadcasted_iota(jnp.int32, (HEAD_DIM, nq), 0) == 0
    qaug_ref[HEAD_DIM:2 * HEAD_DIM, :] = jnp.where(flag_row, NEG_INF, 0.0).astype(BF16)
    qk = qaug_ref[0:2 * HEAD_DIM, :]

    rho = lax.broadcasted_iota(jnp.int32, (ncp, nq), 0)
    band0 = pl.multiple_of(qt * (QT // CMP_STRIDE), V7X_SUBLANES)
    lc_ref[...] = jnp.dot(kcmp_ref[0, 0], qk, preferred_element_type=F32)
    lc_ref[pl.ds(band0, CMP_BAND), :] += bc_ref[0]
    lc = jnp.where(rho < band0 + CMP_BAND, lc_ref[...], NEG_INF)
    e = jnp.exp2(lc - jnp.max(lc, axis=0, keepdims=True))
    oc_aug = jnp.dot(vcmpT_ref[0, 0], e.astype(BF16), preferred_element_type=F32)
    t_q = qt * QT + lax.broadcasted_iota(jnp.int32, (1, nq), 1) % QT
    inv_c = jnp.where(t_q >= CMP_BLOCK - 1, 1.0 / oc_aug[HEAD_DIM:HEAD_DIM + 1, :], 0.0)
    oc_ref[...] = oc_aug[0:HEAD_DIM] * inv_c
    p = e * inv_c
    psum = (p[:, 0:QT] + p[:, QT:2 * QT]) + (p[:, 2 * QT:3 * QT] + p[:, 3 * QT:4 * QT])
    for c in range(QT // V7X_LANES):
        psum_ref[c] = psum[:, c * V7X_LANES:(c + 1) * V7X_LANES]

    def strided_sum(c):
        acc = psum_ref[c, pl.ds(CMP_PAD - 1, ns, stride=4), :]
        for k in range(1, 5):
            acc = acc + psum_ref[c, pl.ds(CMP_PAD - 1 + k, ns, stride=4), :]
        return acc

    imp = jnp.concatenate([strided_sum(c) for c in range(QT // V7X_LANES)], axis=1)
    jrow = lax.broadcasted_iota(jnp.int32, (ns, QT), 0)
    cur = t_lane // SEL_BLOCK
    forced = (jrow == 0) | (jrow == cur) | (jrow == cur - 1)
    live = jrow * SEL_BLOCK <= t_lane
    sel = forced & live
    score = jnp.where(live, jnp.where(forced, -jnp.inf, imp), NEG_INF)
    jrow_f = jrow.astype(F32)

    def col_reduce(x, pair, reduce):
        parts = [x[i * V7X_SUBLANES:(i + 1) * V7X_SUBLANES, :] for i in range(x.shape[0] // V7X_SUBLANES)]
        while len(parts) > 1:
            parts = [pair(parts[i], parts[i + 1]) for i in range(0, len(parts) - 1, 2)] + parts[len(parts) & ~1:]
        return reduce(parts[0], axis=0, keepdims=True)

    for _ in range(min(SEL_TOPN, ns) - 3):
        mx = col_reduce(score, jnp.maximum, jnp.max)
        idx = col_reduce(jnp.where(score == mx, jrow_f, float(ns)), jnp.minimum, jnp.min)
        hit = jrow_f == idx
        sel = sel | hit
        score = jnp.where(hit, -jnp.inf, score)
    negmask = jnp.where(sel, 0.0, NEG_INF).astype(BF16)
    negfar = jnp.where(sel & (jrow < (qt - 1) * (QT // SEL_BLOCK)), 0.0, NEG_INF).astype(BF16)
    qfar_ref[0:2 * HEAD_DIM, :] = qaug_ref[0:2 * HEAD_DIM, :]
    for r in range(NSA_HPG):
        qaug_ref[2 * HEAD_DIM:, r * QT:(r + 1) * QT] = negmask
        qfar_ref[2 * HEAD_DIM:, r * QT:(r + 1) * QT] = negfar

    w0 = pl.multiple_of(qt * QT, QT)
    sw = jnp.dot(kwin_ref[0, 0, pl.ds(w0, WIN_KEYS), :], qk, preferred_element_type=F32) + wtab_ref[0]
    pw = jnp.exp2(sw - jnp.max(sw, axis=0, keepdims=True))
    vw = jnp.concatenate([vwinT_ref[0, 0, qt + j] for j in range(WIN_KEYS // QT)], axis=1)
    ow_aug = jnp.dot(vw, pw.astype(BF16), preferred_element_type=F32)
    ow_ref[...] = ow_aug[0:HEAD_DIM] * (1.0 / ow_aug[HEAD_DIM:HEAD_DIM + 1, :])

    n0 = pl.multiple_of((ATT_PAD // QT - 1 + qt) * QT, QT)
    sn = jnp.dot(ksel_ref[0, 0, pl.ds(n0, NEAR_KEYS), :], qaug_ref[...],
                 preferred_element_type=F32) + stab_ref[0]
    mn = jnp.max(sn, axis=0, keepdims=True)
    pn = jnp.exp2(sn - mn)
    vn = jnp.concatenate([vselT_ref[0, 0, ATT_PAD // QT - 1 + qt + j] for j in range(NEAR_KEYS // QT)], axis=1)
    ms_ref[...] = mn
    accs_ref[...] = jnp.dot(vn, pn.astype(BF16), preferred_element_type=F32)

    tiles_per_far = FAR_KEYS // QT
    n_far = (qt - 1 + tiles_per_far - 1) // tiles_per_far
    last_far = (ksel_ref.shape[2] - ATT_PAD) // FAR_KEYS - 1

    def far_logits(g, dst_ref, dmax_ref):
        r0 = pl.multiple_of(ATT_PAD + jnp.minimum(g, last_far) * FAR_KEYS, FAR_KEYS)
        s = jnp.dot(ksel_ref[0, 0, pl.ds(r0, FAR_KEYS), :], qfar_ref[...], preferred_element_type=F32)
        dst_ref[...] = s
        dmax_ref[...] = jnp.max(s, axis=0, keepdims=True)

    def far_values(g):
        t0 = (ATT_PAD + g * FAR_KEYS) // QT
        return jnp.concatenate([vselT_ref[0, 0, t0 + j] for j in range(tiles_per_far)], axis=1)

    far_logits(0, sa_ref, ma_ref)

    def far_body(j, carry):
        bufs = ((sa_ref, ma_ref), (sb_ref, mb_ref))
        for u in range(FAR_UNROLL):
            g = FAR_UNROLL * j + u
            far_logits(g + 1, *bufs[(u + 1) % 2])
            _flash_update(*bufs[u % 2], far_values(jnp.minimum(g, last_far)), ms_ref, accs_ref)
        return carry

    lax.fori_loop(0, (n_far + FAR_UNROLL - 1) // FAR_UNROLL, far_body, 0)

    o_s = accs_ref[0:HEAD_DIM, :] * (1.0 / accs_ref[HEAD_DIM:HEAD_DIM + 1, :])
    for r in range(NSA_HPG):
        cols = slice(r * QT, (r + 1) * QT)
        o = (gT_ref[0, 3 * r:3 * r + 1, :] * oc_ref[:, cols] + gT_ref[0, 3 * r + 1:3 * r + 2, :] * o_s[:, cols]
             + gT_ref[0, 3 * r + 2:3 * r + 3, :] * ow_ref[:, cols])
        o_ref[0, :, r * HEAD_DIM:(r + 1) * HEAD_DIM] = o.T


def _attn(qT, kcmp, vcmpT, bc, ksel, vT5, kwin, stab, wtab, gT):
    B, _, T = qT.shape
    ncp = kcmp.shape[2]
    ns = T // SEL_BLOCK
    nq = NSA_HPG * QT
    tp = T + ATT_PAD
    nt = tp // QT
    grid = (B, NSA_GROUPS, T // QT)

    def held(shape, index_map):
        return pl.BlockSpec(shape, index_map, pipeline_mode=pl.Buffered(1))

    in_specs = [
        pl.BlockSpec((1, NSA_HPG * HEAD_DIM, QT), lambda b, g, q: (b, g, q)),
        held((1, 1, ncp, 2 * HEAD_DIM), lambda b, g, q: (b, g, 0, 0)),
        held((1, 1, V_ROWS, ncp), lambda b, g, q: (b, g, 0, 0)),
        held((1, CMP_BAND, nq), lambda b, g, q: (g, 0, 0)),
        pl.BlockSpec((1, 1, tp, 2 * HEAD_DIM + ns), lambda b, g, q: (b, g, 0, 0)),
        pl.BlockSpec((1, 1, nt, V_ROWS, QT), lambda b, g, q: (b, g, 0, 0, 0)),
        pl.BlockSpec((1, 1, tp, 2 * HEAD_DIM), lambda b, g, q: (b, g, 0, 0)),
        pl.BlockSpec((1, 1, nt, V_ROWS, QT), lambda b, g, q: (b, NSA_GROUPS + g, 0, 0, 0)),
        held((1, NEAR_KEYS, nq), lambda b, g, q: (g, 0, 0)),
        held((1, WIN_KEYS, nq), lambda b, g, q: (g, 0, 0)),
        pl.BlockSpec((1, GATE_ROWS, QT), lambda b, g, q: (b, g, q)),
    ]
    scratch = [
        pltpu.VMEM((ncp, nq), F32),
        pltpu.VMEM((QT // V7X_LANES, ncp, V7X_LANES), F32),
        pltpu.VMEM((2 * HEAD_DIM + ns, nq), BF16),
        pltpu.VMEM((2 * HEAD_DIM + ns, nq), BF16),
        pltpu.VMEM((HEAD_DIM, nq), F32),
        pltpu.VMEM((HEAD_DIM, nq), F32),
        pltpu.VMEM((1, nq), F32),
        pltpu.VMEM((V_ROWS, nq), F32),
        pltpu.VMEM((FAR_KEYS, nq), F32),
        pltpu.VMEM((FAR_KEYS, nq), F32),
        pltpu.VMEM((1, nq), F32), pltpu.VMEM((1, nq), F32),
    ]
    return pl.pallas_call(
        _attn_kernel, grid=grid, in_specs=in_specs,
        out_specs=pl.BlockSpec((1, QT, NSA_HPG * HEAD_DIM), lambda b, g, q: (b, q, g)),
        out_shape=jax.ShapeDtypeStruct((B, T, D_NSA), F32),
        scratch_shapes=scratch,
        compiler_params=_cparams(("parallel", "parallel", "arbitrary")), name="attn",
    )(qT, kcmp, vcmpT, bc, ksel, vT5, kwin, vT5, stab, wtab, gT)


def _rwkv_tokens(rw_ref, halo_ref, mu_ref, w0_ref, w2_ref, a0_ref, a2_ref, g2_ref, kk_ref, ka_ref,
                 rk_ref, bd_ref, ext_ref):
    tm = rw_ref.shape[1]
    first_tile = pl.program_id(1) == 0
    ext_ref[0:V7X_SUBLANES, :] = jnp.where(first_tile, 0.0, halo_ref[0])
    ext_ref[V7X_SUBLANES:, :] = rw_ref[0]
    cur = rw_ref[0]
    prev = ext_ref[pl.ds(V7X_SUBLANES - 1, tm), :]
    mixed = cur + (prev - cur) * mu_ref[...]
    c = D_RWKV
    r = mixed[:, 0:c]
    k = mixed[:, c:2 * c]
    v = mixed[:, 2 * c:3 * c]
    xw = mixed[:, 3 * c:3 * c + LORA_W]
    xa = mixed[:, 3 * c + LORA_W:3 * c + LORA_W + LORA_A]
    xg = mixed[:, 3 * c + LORA_W + LORA_A:]

    z = -(w0_ref[...] + _bdot(jnp.tanh(xw), w2_ref[...]))
    softplus = jnp.maximum(z, 0.0) + jnp.log1p(jnp.exp(-jnp.abs(z)))
    w = -softplus - 0.5
    a = jax.nn.sigmoid(a0_ref[...] + _bdot(xa, a2_ref[...]))
    g = _bdot(jax.nn.sigmoid(xg), g2_ref[...])

    kk = k * kk_ref[...]
    n2 = _head_sum(kk * kk, bd_ref[...])
    kkn = kk / jnp.maximum(jnp.sqrt(n2), 1e-12)
    k2 = k * (1.0 + (a - 1.0) * ka_ref[...])
    bonus = _head_sum(r * k2 * rk_ref[...], bd_ref[...]) * v

    lw = -jnp.exp(w)
    return dict(r=r, lw=lw, k=k2, v=v, kk=kkn, b=kkn * a, g=g, bonus=bonus)


WKV_BATCH = 8
WKV_GROUP = 4
WKV_GW = WKV_GROUP * HEAD_DIM


def _wkv_kernel(rw_ref, halo_ref, mu_ref, w0_ref, w2_ref, a0_ref, a2_ref, g2_ref, kk_ref, ka_ref, rk_ref, bd_ref,
                y_ref, g_ref, bonus_ref, s_ref, ext_ref, *, chunks):
    L = WKV_CHUNK
    tok = _rwkv_tokens(rw_ref, halo_ref, mu_ref, w0_ref, w2_ref, a0_ref, a2_ref, g2_ref, kk_ref, ka_ref,
                       rk_ref, bd_ref, ext_ref)
    g_ref[0] = tok["g"]
    bonus_ref[0] = tok["bonus"]
    assert L == HEAD_DIM
    gw = WKV_GW
    n_groups = RWKV_HEADS // WKV_GROUP

    @pl.when(pl.program_id(1) == 0)
    def _():
        s_ref[...] = jnp.zeros(s_ref.shape, F32)

    ti = lax.broadcasted_iota(jnp.int32, (L, L), 0)
    tj = lax.broadcasted_iota(jnp.int32, (L, L), 1)
    tri = jnp.where(ti >= tj, 1.0, 0.0).astype(BF16)
    row = lax.broadcasted_iota(jnp.int32, (L, gw), 0)
    col = lax.broadcasted_iota(jnp.int32, (L, gw), 1) % L
    low_strict = col < row
    low_incl = col <= row
    eye_sbs = jnp.where(col == row, 1.0, 0.0)
    brow = lax.broadcasted_iota(jnp.int32, (gw, gw), 0) // L
    bcol = lax.broadcasted_iota(jnp.int32, (gw, gw), 1) // HEAD_DIM
    same_head = brow == bcol

    def bd_rows(x):
        xb = x.astype(BF16)
        return jnp.where(same_head, jnp.concatenate([xb] * WKV_GROUP, axis=0), jnp.zeros((), BF16))

    def mm(a, b_bf16):
        return jnp.dot(a.astype(BF16), b_bf16, preferred_element_type=F32)

    def tn(a, b):
        return lax.dot_general(a.astype(BF16), b.astype(BF16), (((0,), (0,)), ((), ())),
                               preferred_element_type=F32)

    pre, w2, c2, m_lr, d_sbs = {}, {}, {}, {}, {}

    def grp(c, g, name):
        return pre[c][name][:, g * gw:(g + 1) * gw]

    for c0 in range(0, chunks, WKV_BATCH):
        batch = range(c0, min(c0 + WKV_BATCH, chunks))
        inst = [(c, g) for c in batch for g in range(n_groups)]
        for c in batch:
            rows = slice(c * L, (c + 1) * L)
            lw = tok["lw"][rows]
            cs = _dot_exact_rhs_left(tri, lw)
            c_last = cs[L - 1:L, :]
            e_nc = jnp.exp(-cs)
            e_lc = jnp.exp(c_last - cs)
            kk = tok["kk"][rows]
            b = tok["b"][rows]
            k = tok["k"][rows]
            pre[c] = dict(a=-kk * jnp.exp(cs - lw), r=tok["r"][rows] * jnp.exp(cs), bh=b * e_nc, kh=k * e_nc,
                          be=b * e_lc, ke=k * e_lc, v=tok["v"][rows], e_last=jnp.exp(c_last))

        t_all = {}
        for (c, g) in inst:
            lhs = jnp.concatenate([grp(c, g, "a"), grp(c, g, "r")], axis=0).astype(BF16)
            rhs = jnp.concatenate([bd_rows(grp(c, g, "bh")), bd_rows(grp(c, g, "kh"))], axis=0)
            t_all[c, g] = lax.dot_general(lhs, rhs, (((1,), (1,)), ((), ())), preferred_element_type=F32)
        n_m = {i: jnp.where(low_strict, t_all[i][0:L, 0:gw], 0.0) for i in inst}
        tak = {i: jnp.where(low_strict, t_all[i][0:L, gw:], 0.0) for i in inst}
        trb = {i: jnp.where(low_incl, t_all[i][L:, 0:gw], 0.0) for i in inst}
        trk = {i: jnp.where(low_incl, t_all[i][L:, gw:], 0.0) for i in inst}
        z = {i: eye_sbs + n_m[i] for i in inst}
        pw = dict(n_m)
        for _ in range(int(math.log2(L)) - 1):
            pw = {i: mm(pw[i], bd_rows(pw[i])) for i in inst}
            z = {i: z[i] + mm(z[i], bd_rows(pw[i])) for i in inst}
        vbd = {(c, g): bd_rows(grp(c, g, "v")) for (c, g) in inst}
        w1 = {(c, g): mm(z[c, g], bd_rows(grp(c, g, "a"))) for (c, g) in inst}
        tv = {i: mm(tak[i], vbd[i]) for i in inst}
        c1 = {i: mm(z[i], bd_rows(tv[i])) for i in inst}
        w2.update({(c, g): grp(c, g, "r") + mm(trb[c, g], bd_rows(w1[c, g])) for (c, g) in inst})
        c2.update({i: mm(trb[i], bd_rows(c1[i])) + mm(trk[i], vbd[i]) for i in inst})
        m_lr.update({(c, g): jnp.where(same_head, tn(w1[c, g], grp(c, g, "be")), 0.0).astype(BF16)
                     for (c, g) in inst})
        for (c, g) in inst:
            full = jnp.where(same_head,
                             tn(jnp.concatenate([c1[c, g], grp(c, g, "v")], axis=0),
                                jnp.concatenate([grp(c, g, "be"), grp(c, g, "ke")], axis=0)), 0.0)
            d_sbs[c, g] = ((full[0:L] + full[L:2 * L]) + (full[2 * L:3 * L] + full[3 * L:4 * L]))

    for g in range(n_groups):
        s = s_ref[:, g * gw:(g + 1) * gw]
        for c in range(chunks):
            g_bd = jnp.where(same_head, jnp.concatenate([s.T.astype(BF16)] * WKV_GROUP, axis=1),
                             jnp.zeros((), BF16))
            y_ref[0, c * L:(c + 1) * L, g * gw:(g + 1) * gw] = mm(w2[c, g], g_bd) + c2[c, g]
            s = s * grp(c, g, "e_last") + mm(s, m_lr[c, g]) + d_sbs[c, g]
        s_ref[:, g * gw:(g + 1) * gw] = s


def _dot_exact_rhs_left(m_bf16, x):
    hi, mid, lo = _split3(x)
    return (jnp.dot(m_bf16, hi, preferred_element_type=F32)
            + jnp.dot(m_bf16, mid, preferred_element_type=F32)
            + jnp.dot(m_bf16, lo, preferred_element_type=F32))


def _wkv(rw, mu, w0, w2, a0, a2, g2, k_k, k_a, r_k, bd, *, chunks):
    B, T, C = rw.shape
    rows = chunks * WKV_CHUNK
    hb = rows // V7X_SUBLANES
    full = lambda a: pl.BlockSpec(a.shape, lambda bb, c: (0,) * a.ndim)
    tok = pl.BlockSpec((1, rows, D_RWKV), lambda bb, c: (bb, c, 0))
    return pl.pallas_call(
        functools.partial(_wkv_kernel, chunks=chunks), grid=(B, T // rows),
        in_specs=[pl.BlockSpec((1, rows, C), lambda bb, c: (bb, c, 0)),
                  pl.BlockSpec((1, V7X_SUBLANES, C), lambda bb, c: (bb, jnp.maximum(c * hb - 1, 0), 0)),
                  full(mu), full(w0), full(w2), full(a0), full(a2), full(g2), full(k_k), full(k_a),
                  full(r_k), full(bd)],
        out_specs=(tok,) * 3,
        out_shape=(jax.ShapeDtypeStruct((B, T, D_RWKV), F32),) * 3,
        scratch_shapes=[pltpu.VMEM((HEAD_DIM, D_RWKV), F32), pltpu.VMEM((rows + V7X_SUBLANES, C), F32)],
        compiler_params=_cparams(("parallel", "arbitrary")), name="wkv",
    )(rw, rw, mu, w0, w2, a0, a2, g2, k_k, k_a, r_k, bd)


def _mix_rows(x, on, y, bonus, g, bd_ref, lnw_ref, lnb_ref, wo_ref, g2_ref):
    mu = _dot_exact_rhs(y, bd_ref[...]) * (1.0 / HEAD_DIM)
    yc = y - mu
    var = _head_sum(yc * yc, bd_ref[...]) * (1.0 / HEAD_DIM)
    yn = yc * lax.rsqrt(var + GN_EPS) * lnw_ref[...] + lnb_ref[...]
    orw = (yn + bonus) * g
    x1 = x + _bdot(on, wo_ref[0:D_NSA, :]) + _bdot(orw, wo_ref[D_NSA:, :])
    ms = jnp.mean(x1 * x1, axis=-1, keepdims=True)
    return x1, (x1 * lax.rsqrt(ms + NORM_EPS) * g2_ref[...]).astype(BF16)


FFN_HALO = 16


def _ffn_kernel(x_ref, on_ref, y_ref, bonus_ref, g_ref, xh_ref, onh_ref, yh_ref, bonush_ref, gh_ref,
                bd_ref, lnw_ref, lnb_ref, wo_ref, g2_ref,
                wv_ref, wg_ref, cwv_ref, cwg_ref, cbv_ref, cbg_ref, wd_ref,
                o_ref, hext_ref, extv_ref, extg_ref, act_ref, x1_ref, *, tiles_per_seq, ft):
    tm = x_ref.shape[0]
    dff = wd_ref.shape[0]
    seq_start = (pl.program_id(0) % tiles_per_seq) == 0
    ext = lambda h_ref, m_ref: jnp.concatenate([h_ref[...], m_ref[...]], axis=0)
    x1, h2 = _mix_rows(ext(xh_ref, x_ref), ext(onh_ref, on_ref), ext(yh_ref, y_ref), ext(bonush_ref, bonus_ref),
                       ext(gh_ref, g_ref), bd_ref, lnw_ref, lnb_ref, wo_ref, g2_ref)
    x1_ref[...] = x1[FFN_HALO:]
    row = lax.broadcasted_iota(jnp.int32, h2.shape, 0)
    hext_ref[...] = jnp.where(seq_start & (row < FFN_HALO), jnp.zeros((), BF16), h2)

    for j in range(dff // ft):
        cols = slice(j * ft, (j + 1) * ft)

        def conv_branch(w_ref, cw_ref, cb_ref, ext_ref):
            ext_ref[j % 2] = jnp.dot(hext_ref[...], w_ref[:, cols], preferred_element_type=F32)
            out = cb_ref[:, cols] + ext_ref[j % 2, pl.ds(FFN_HALO, tm), :] * cw_ref[CONV_W - 1:CONV_W, cols]
            for i in range(CONV_W - 1):
                back = CONV_W - 1 - i
                out = out + ext_ref[j % 2, pl.ds(FFN_HALO - back, tm), :] * cw_ref[i:i + 1, cols]
            return out

        u_val = conv_branch(wv_ref, cwv_ref, cbv_ref, extv_ref)
        u_gate = conv_branch(wg_ref, cwg_ref, cbg_ref, extg_ref)
        act_ref[:, cols] = ((u_gate * jax.nn.sigmoid(u_gate)) * u_val).astype(BF16)
    o_ref[...] = x1_ref[...] + jnp.dot(act_ref[...], wd_ref[...], preferred_element_type=F32)


def _ffn(x, on, y, bonus, g, bd, lnw, lnb, wo, g2, wv, wg, cwv, cwg, cbv, cbg, wd, *, tm, ft, seq_len):
    n, d = x.shape
    dff = wv.shape[1]
    hb = tm // FFN_HALO
    kern = functools.partial(_ffn_kernel, tiles_per_seq=seq_len // tm, ft=ft)
    resident = lambda a: pl.BlockSpec(a.shape, lambda i: (0,) * a.ndim, pipeline_mode=pl.Buffered(1))
    tile = lambda a: pl.BlockSpec((tm, a.shape[1]), lambda i: (i, 0))
    halo = lambda a: pl.BlockSpec((FFN_HALO, a.shape[1]), lambda i: (jnp.maximum(i * hb - 1, 0), 0))
    tokens = (x, on, y, bonus, g)
    params = (bd, lnw, lnb, wo, g2, wv, wg, cwv, cwg, cbv, cbg, wd)
    return pl.pallas_call(
        kern, grid=(n // tm,),
        in_specs=[tile(a) for a in tokens] + [halo(a) for a in tokens] + [resident(a) for a in params],
        out_specs=pl.BlockSpec((tm, d), lambda i: (i, 0)),
        out_shape=jax.ShapeDtypeStruct((n, d), F32),
        scratch_shapes=[pltpu.VMEM((tm + FFN_HALO, d), BF16),
                        pltpu.VMEM((2, tm + FFN_HALO, ft), F32), pltpu.VMEM((2, tm + FFN_HALO, ft), F32),
                        pltpu.VMEM((tm, dff), BF16), pltpu.VMEM((tm, d), F32)],
        compiler_params=_cparams(("parallel",)), name="ffn",
    )(*tokens, *tokens, *params)


def _t5_bucket(dist):
    n = np.maximum(dist, 0)
    max_exact = N_BUCKETS // 2
    nf = np.maximum(n, 1).astype(np.float32)
    large = max_exact + (np.log(nf / np.float32(max_exact)) / np.float32(math.log(MAX_DISTANCE / max_exact))
                         * np.float32(N_BUCKETS - max_exact)).astype(np.int32)
    large = np.minimum(large, N_BUCKETS - 1)
    return np.where(n < max_exact, n, large)


def _toeplitz_kernel(*refs, steps):
    n = len(steps)
    for v_ref, o_ref, step in zip(refs[:n], refs[n:], steps):
        rows, width = o_ref.shape[1], o_ref.shape[2]
        x = jnp.broadcast_to(v_ref[0], (rows, v_ref.shape[2]))
        o_ref[0] = pltpu.roll(x, 0, 1, stride=step, stride_axis=0)[:, :width]


def _toeplitz(specs):
    return pl.pallas_call(
        functools.partial(_toeplitz_kernel, steps=tuple(step for _, _, step in specs)), grid=(NSA_GROUPS, NSA_HPG),
        in_specs=[pl.BlockSpec((1, 1, v.shape[2]), lambda g, r: (g * NSA_HPG + r, 0, 0)) for v, _, _ in specs],
        out_specs=tuple(pl.BlockSpec((1, rows, QT), lambda g, r: (g, 0, r)) for _, rows, _ in specs),
        out_shape=tuple(jax.ShapeDtypeStruct((NSA_GROUPS, rows, NSA_HPG * QT), F32) for _, rows, _ in specs),
        compiler_params=_cparams(("parallel", "parallel")), name="toeplitz",
    )(*[v for v, _, _ in specs])


def _bias_tables(rel_bias):
    rel = (rel_bias - rel_bias[N_BUCKETS - 1][None, :]) * LOG2E

    def table(rows, step, d0, d_hi):
        period = step * rows + QT
        i = np.arange(period)
        d = d0 + np.where(i < QT, i, i - period)
        onehot = (_t5_bucket(d)[:, None] == np.arange(N_BUCKETS)[None, :]).astype(np.float32)
        v = jnp.dot(jnp.asarray(onehot), rel, precision=lax.Precision.HIGHEST)
        v = jnp.where(jnp.asarray((d >= 0) & (d < d_hi))[:, None], v, NEG_INF).T
        return v.reshape(NSA_HEADS, 1, period), rows, step

    no_limit = 1 << 30
    stab, wtab, bc = _toeplitz([table(NEAR_KEYS, 1, QT, no_limit), table(WIN_KEYS, 1, WINDOW, WINDOW),
                                table(CMP_BAND, CMP_STRIDE, CMP_STRIDE * CMP_PAD - CMP_BLOCK + 1, no_limit)])
    return stab, wtab, bc


def _compress_weights(w1):
    half = CMP_BLOCK // 2
    rows = half * HEAD_DIM
    w3 = jnp.concatenate([w1[:rows], w1[rows:]], axis=1).reshape(half, HEAD_DIM, 2 * CMP_HIDDEN)
    z = jnp.zeros_like(w3)
    assert NSA_GROUPS == 2
    big = jnp.concatenate([jnp.concatenate([w3, z], axis=2), jnp.concatenate([z, w3], axis=2)], axis=1)
    return big.reshape(half * NSA_GROUPS * HEAD_DIM, NSA_GROUPS * 2 * CMP_HIDDEN)


def kernel(x, norm1_g, w_in, q_norm_g, k_norm_g, cmp_pos, cmp_w1, cmp_b1, cmp_w2, cmp_b2, rel_bias, rwkv_mu,
           w0, w2, a0, a2, g2, k_k, k_a, r_k, ln_x_w, ln_x_b, w_out, norm2_g, ffn_up, conv_w, conv_b, ffn_down):
    B, T, D = x.shape
    depth = w_in.shape[0]
    d_ff = ffn_down.shape[1]
    assert T % 2048 == 0 and D_NSA + 6 * D_KV + 3 * NSA_HEADS + D_RWKV_IN == w_in.shape[2]
    ncp = T // CMP_STRIDE + V7X_LANES
    stab, wtab, bc = _bias_tables(rel_bias)
    ii = jnp.arange(D_RWKV)
    bd = (ii[:, None] // HEAD_DIM == ii[None, :] // HEAD_DIM).astype(BF16)

    for l in range(depth):
        wi = w_in[l]
        o = D_NSA
        q_w, kc_w, vc_w, ksl_w, vsl_w, kwn_w, vwn_w = (
            wi[:, 0:o], wi[:, o:o + D_KV], wi[:, o + D_KV:o + 2 * D_KV], wi[:, o + 2 * D_KV:o + 3 * D_KV],
            wi[:, o + 3 * D_KV:o + 4 * D_KV], wi[:, o + 4 * D_KV:o + 5 * D_KV], wi[:, o + 5 * D_KV:o + 6 * D_KV])
        gl_w = wi[:, o + 6 * D_KV:o + 6 * D_KV + 3 * NSA_HEADS]
        rw_w = wi[:, o + 6 * D_KV + 3 * NSA_HEADS:]
        wstd = jnp.concatenate([kc_w, vc_w, ksl_w, kwn_w, rw_w], axis=1).astype(BF16)
        gl_rows = gl_w.T.reshape(NSA_GROUPS, 3 * NSA_HPG, D)
        gl_rows = jnp.pad(gl_rows, ((0, 0), (0, GATE_ROWS - 3 * NSA_HPG), (0, 0))).reshape(-1, D)
        wt = jnp.concatenate([q_w.T, vsl_w.T, vwn_w.T, gl_rows], axis=0).astype(BF16)
        qg = q_norm_g[l].reshape(HEAD_DIM, 1)
        kg = jnp.stack([jnp.tile(k_norm_g[l, 1], NSA_GROUPS), jnp.tile(k_norm_g[l, 2], NSA_GROUPS)])

        qT, kc, vc, ksel, kwin, vT5, gT, rw = _proj(x, norm1_g[l].reshape(1, D), wstd, wt, qg, kg, tm=512)

        wbig = jnp.stack([_compress_weights(cmp_w1[l, 0]), _compress_weights(cmp_w1[l, 1])]).astype(BF16)
        pos8 = jnp.pad(cmp_pos[l].reshape(2, 1, CMP_BLOCK * HEAD_DIM),
                       ((0, 0), (0, V7X_SUBLANES - 1), (0, 0))).astype(BF16)
        kcmp, vcmpT = _compress(
            kc, vc, wbig, pos8,
            cmp_w1[l].astype(BF16), cmp_b1[l].reshape(2, 1, CMP_HIDDEN), cmp_w2[l, 0].astype(BF16),
            cmp_w2[l, 1].T.astype(BF16), cmp_b2[l, 0].reshape(1, HEAD_DIM), cmp_b2[l, 1].reshape(HEAD_DIM, 1),
            k_norm_g[l, 0].reshape(1, HEAD_DIM), ncp=ncp)

        o_nsa = _attn(qT, kcmp, vcmpT, bc, ksel, vT5, kwin, stab, wtab, gT)

        row = lambda a: a.reshape(1, -1)
        y, g, bonus = _wkv(
            rw, row(rwkv_mu[l]), row(w0[l]), w2[l].astype(BF16), row(a0[l]), a2[l].astype(BF16),
            g2[l].astype(BF16), row(k_k[l]), row(k_a[l]), row(r_k[l]), bd, chunks=8)

        n = B * T
        up = ffn_up[l].astype(BF16)
        x = _ffn(x.reshape(n, D), o_nsa.reshape(n, D_NSA), y.reshape(n, D_RWKV), bonus.reshape(n, D_RWKV),
                 g.reshape(n, D_RWKV), bd, row(ln_x_w[l]), row(ln_x_b[l]), w_out[l].astype(BF16),
                 norm2_g[l].reshape(1, D), up[:, :d_ff], up[:, d_ff:], conv_w[l][:, :d_ff], conv_w[l][:, d_ff:],
                 conv_b[l][:d_ff].reshape(1, -1), conv_b[l][d_ff:].reshape(1, -1),
                 ffn_down[l].astype(BF16), tm=512, ft=256, seq_len=T).reshape(B, T, D)
    return x
```

```python
import functools
import math

import jax
import jax.numpy as jnp
import numpy as np
from jax import lax
from jax.experimental import pallas as pl
from jax.experimental.pallas import tpu as pltpu

F32 = jnp.float32
BF16 = jnp.bfloat16

V7X_LANES = 128
V7X_SUBLANES = 8
V7X_VMEM_LIMIT_BYTES = 56 * 1024 * 1024

HEAD_DIM = 64
NSA_HEADS = 8
NSA_GROUPS = 2
NSA_HPG = NSA_HEADS // NSA_GROUPS
RWKV_HEADS = 8
D_NSA = NSA_HEADS * HEAD_DIM
D_RWKV = RWKV_HEADS * HEAD_DIM
D_KV = NSA_GROUPS * HEAD_DIM
CMP_BLOCK = 32
CMP_STRIDE = 16
CMP_HIDDEN = 128
SEL_BLOCK = 64
SEL_TOPN = 16
WINDOW = 512
N_BUCKETS = 32
MAX_DISTANCE = 128
LORA_W = 64
LORA_A = 64
LORA_G = 128
D_RWKV_IN = 3 * D_RWKV + LORA_W + LORA_A + LORA_G
CONV_W = 3
NORM_EPS = 1e-6
GN_EPS = 64e-5
NEG_INF = -1e30
FORCE_SCORE = 1e9

QT = 256
CMP_PAD = 8
CMP_BAND = QT // CMP_STRIDE + 8
LOG2E = math.log2(math.e)
Q_SCALE = HEAD_DIM ** -0.5 * LOG2E
V_ROWS = HEAD_DIM + 16
ATT_PAD = WINDOW
NEAR_KEYS = 2 * QT
FAR_KEYS = 2 * QT
FAR_UNROLL = 2
WIN_KEYS = WINDOW + QT
WKV_CHUNK = 64
GATE_ROWS = 16
N_T_ROWS = D_NSA + 2 * D_KV + NSA_GROUPS * GATE_ROWS
N_STD_COLS = 4 * D_KV + D_RWKV_IN


def _cparams(sem):
    return pltpu.CompilerParams(dimension_semantics=sem, vmem_limit_bytes=V7X_VMEM_LIMIT_BYTES)


def _bdot(a, b):
    return jnp.dot(a.astype(BF16), b.astype(BF16), preferred_element_type=F32)


def _split3(x):
    hi = x.astype(BF16)
    r1 = x - hi.astype(F32)
    mid = r1.astype(BF16)
    lo = (r1 - mid.astype(F32)).astype(BF16)
    return hi, mid, lo


def _dot_exact_rhs(x, m_bf16):
    hi = x.astype(BF16)
    lo = (x - hi.astype(F32)).astype(BF16)
    return jnp.dot(hi, m_bf16, preferred_element_type=F32) + jnp.dot(lo, m_bf16, preferred_element_type=F32)


def _head_sum(x, m_bf16):
    return jnp.dot(x.astype(BF16), m_bf16, preferred_element_type=F32)


def _proj_kernel(x_ref, g1_ref, wstd_ref, wt_ref, qg_ref, kg_ref,
                 qT_ref, kc_ref, vc_ref, ksel_ref, kwin_ref, vT_ref, gT_ref, rw_ref):
    step = pl.program_id(1)

    @pl.when(step == 0)
    def _():
        for k_ref in (ksel_ref, kwin_ref):
            lanes = lax.broadcasted_iota(jnp.int32, k_ref.shape[2:], 1)
            flagged = jnp.where(lanes == HEAD_DIM, 1.0, 0.0).astype(BF16)
            for g in range(NSA_GROUPS):
                k_ref[0, g] = flagged
        vT_ref[...] = jnp.zeros(vT_ref.shape, BF16)

    @pl.when(step > 0)
    def _():
        _proj_tile(x_ref, g1_ref, wstd_ref, wt_ref, qg_ref, kg_ref,
                   qT_ref, kc_ref, vc_ref, ksel_ref, kwin_ref, vT_ref, gT_ref, rw_ref, step - 1)


def _proj_tile(x_ref, g1_ref, wstd_ref, wt_ref, qg_ref, kg_ref,
               qT_ref, kc_ref, vc_ref, ksel_ref, kwin_ref, vT_ref, gT_ref, rw_ref, tile):
    tm = x_ref.shape[1]
    ns = ksel_ref.shape[3] - 2 * HEAD_DIM
    x = x_ref[0]
    ms = jnp.mean(x * x, axis=-1, keepdims=True)
    h = (x * lax.rsqrt(ms + NORM_EPS) * g1_ref[...]).astype(BF16)

    std = jnp.dot(h, wstd_ref[...], preferred_element_type=F32)
    kc_ref[0] = std[:, 0:D_KV]
    vc_ref[0] = std[:, D_KV:2 * D_KV]
    rw_ref[0] = std[:, 4 * D_KV:]

    lane = lax.broadcasted_iota(jnp.int32, (tm, D_KV), 1)
    first = lane < HEAD_DIM

    def group_rmsnorm(k, gain):
        k2 = k * k
        s0 = jnp.sum(jnp.where(first, k2, 0.0), axis=-1, keepdims=True)
        s1 = jnp.sum(jnp.where(first, 0.0, k2), axis=-1, keepdims=True)
        ms_g = jnp.where(first, s0, s1) * (1.0 / HEAD_DIM)
        return k * lax.rsqrt(ms_g + NORM_EPS) * gain

    ksl = group_rmsnorm(std[:, 2 * D_KV:3 * D_KV], kg_ref[0:1, :]).astype(BF16)
    kwn = group_rmsnorm(std[:, 3 * D_KV:4 * D_KV], kg_ref[1:2, :]).astype(BF16)

    tok = tile * tm + lax.broadcasted_iota(jnp.int32, (tm, ns), 0)
    blk = lax.broadcasted_iota(jnp.int32, (tm, ns), 1)
    onehot = jnp.where((tok // SEL_BLOCK) == blk, 1.0, 0.0).astype(BF16)
    zeros = jnp.zeros((tm, HEAD_DIM), BF16)
    for g in range(NSA_GROUPS):
        ksel_ref[0, g, :, 0:HEAD_DIM] = ksl[:, g * HEAD_DIM:(g + 1) * HEAD_DIM]
        ksel_ref[0, g, :, HEAD_DIM:2 * HEAD_DIM] = zeros
        ksel_ref[0, g, :, 2 * HEAD_DIM:] = onehot
        kwin_ref[0, g, :, 0:HEAD_DIM] = kwn[:, g * HEAD_DIM:(g + 1) * HEAD_DIM]
        kwin_ref[0, g, :, HEAD_DIM:] = zeros

    tr = lax.dot_general(wt_ref[...], h, (((1,), (1,)), ((), ())),
                         preferred_element_type=F32)
    q = tr[0:D_NSA].reshape(NSA_HEADS, HEAD_DIM, tm)
    qms = jnp.mean(q * q, axis=1, keepdims=True)
    qn = q * lax.rsqrt(qms + NORM_EPS) * qg_ref[...][None] * Q_SCALE
    qT_ref[0] = qn.reshape(D_NSA, tm).astype(BF16)
    vt = tr[D_NSA:D_NSA + 2 * D_KV].astype(BF16)
    ones_rows = jnp.where(lax.broadcasted_iota(jnp.int32, (V_ROWS - HEAD_DIM, QT), 0) == 0, 1.0, 0.0).astype(BF16)
    for a in range(2 * NSA_GROUPS):
        for j in range(tm // QT):
            vT_ref[0, a, j, 0:HEAD_DIM, :] = vt[a * HEAD_DIM:(a + 1) * HEAD_DIM, j * QT:(j + 1) * QT]
            vT_ref[0, a, j, HEAD_DIM:, :] = ones_rows
    gT_ref[0] = jax.nn.sigmoid(tr[D_NSA + 2 * D_KV:])


def _proj(x, g1, wstd, wt, qg, kg, *, tm):
    B, T, D = x.shape
    ns = T // SEL_BLOCK
    assert tm == ATT_PAD
    tp = T + ATT_PAD
    grid = (B, T // tm + 1)
    const2 = lambda b, i: (0, 0)
    tile = lambda i: jnp.maximum(i - 1, 0)
    out_shape = (
        jax.ShapeDtypeStruct((B, D_NSA, T), BF16),
        jax.ShapeDtypeStruct((B, T, D_KV), F32),
        jax.ShapeDtypeStruct((B, T, D_KV), F32),
        jax.ShapeDtypeStruct((B, NSA_GROUPS, tp, 2 * HEAD_DIM + ns), BF16),
        jax.ShapeDtypeStruct((B, NSA_GROUPS, tp, 2 * HEAD_DIM), BF16),
        jax.ShapeDtypeStruct((B, 2 * NSA_GROUPS, tp // QT, V_ROWS, QT), BF16),
        jax.ShapeDtypeStruct((B, NSA_GROUPS * GATE_ROWS, T), F32),
        jax.ShapeDtypeStruct((B, T, D_RWKV_IN), F32),
    )
    out_specs = (
        pl.BlockSpec((1, D_NSA, tm), lambda b, i: (b, 0, tile(i))),
        pl.BlockSpec((1, tm, D_KV), lambda b, i: (b, tile(i), 0)),
        pl.BlockSpec((1, tm, D_KV), lambda b, i: (b, tile(i), 0)),
        pl.BlockSpec((1, NSA_GROUPS, tm, 2 * HEAD_DIM + ns), lambda b, i: (b, 0, i, 0)),
        pl.BlockSpec((1, NSA_GROUPS, tm, 2 * HEAD_DIM), lambda b, i: (b, 0, i, 0)),
        pl.BlockSpec((1, 2 * NSA_GROUPS, tm // QT, V_ROWS, QT), lambda b, i: (b, 0, i, 0, 0)),
        pl.BlockSpec((1, NSA_GROUPS * GATE_ROWS, tm), lambda b, i: (b, 0, tile(i))),
        pl.BlockSpec((1, tm, D_RWKV_IN), lambda b, i: (b, tile(i), 0)),
    )
    in_specs = [
        pl.BlockSpec((1, tm, D), lambda b, i: (b, tile(i), 0)),
        pl.BlockSpec(g1.shape, const2),
        pl.BlockSpec(wstd.shape, const2),
        pl.BlockSpec(wt.shape, const2),
        pl.BlockSpec(qg.shape, const2),
        pl.BlockSpec(kg.shape, const2),
    ]
    return pl.pallas_call(
        _proj_kernel, grid=grid, in_specs=in_specs, out_specs=out_specs, out_shape=out_shape,
        compiler_params=_cparams(("parallel", "arbitrary")), name="proj",
    )(x, g1, wstd, wt, qg, kg)


def _gelu_tanh(x):
    c = math.sqrt(2.0 / math.pi)
    return x * (0.5 * (1.0 + jnp.tanh(c * (x + 0.044715 * (x * x * x)))))


def _compress_kernel(kc_ref, vc_ref, wbig_ref, pos_ref, w1_ref, b1_ref, w2k_ref, w2vT_ref,
                     b2k_ref, b2v_ref, kg_ref, kcmp_ref, vcmpT_ref, act_ref):
    m = kc_ref.shape[1] // CMP_STRIDE
    nc = m - 1
    ncp = act_ref.shape[0]
    row_m = lax.broadcasted_iota(jnp.int32, (m, CMP_HIDDEN), 0)
    row_p = lax.broadcasted_iota(jnp.int32, (ncp, HEAD_DIM), 0)
    col_p = lax.broadcasted_iota(jnp.int32, (HEAD_DIM, ncp), 1)
    col_f = lax.broadcasted_iota(jnp.int32, (ncp, HEAD_DIM), 1)
    row_o = lax.broadcasted_iota(jnp.int32, (V_ROWS - HEAD_DIM, ncp), 0)
    act_ref[...] = jnp.zeros(act_ref.shape, F32)
    for which, src_ref in enumerate((kc_ref, vc_ref)):
        p = jnp.zeros((m, 2 * NSA_GROUPS * CMP_HIDDEN), F32)
        for tl in range(CMP_STRIDE):
            p = p + jnp.dot(src_ref[0, pl.ds(tl, m, stride=CMP_STRIDE), :].astype(BF16),
                            wbig_ref[which, tl * D_KV:(tl + 1) * D_KV, :], preferred_element_type=F32)
        posc = jnp.dot(pos_ref[which], w1_ref[which], preferred_element_type=F32)[0:1]
        for g in range(NSA_GROUPS):
            top = p[:, (2 * g) * CMP_HIDDEN:(2 * g + 1) * CMP_HIDDEN]
            bot = p[:, (2 * g + 1) * CMP_HIDDEN:(2 * g + 2) * CMP_HIDDEN]
            hid = top + pltpu.roll(bot, m - 1, 0) + (b1_ref[which] + posc)
            act = jnp.where(row_m < nc, _gelu_tanh(hid), 0.0)
            act_ref[CMP_PAD:CMP_PAD + m, :] = act
            ap = act_ref[...].astype(BF16)
            if which == 0:
                kc = jnp.dot(ap, w2k_ref[...], preferred_element_type=F32) + b2k_ref[...]
                ms = jnp.mean(kc * kc, axis=-1, keepdims=True)
                kc = kc * lax.rsqrt(ms + NORM_EPS) * kg_ref[...]
                valid = (row_p >= CMP_PAD) & (row_p < CMP_PAD + nc)
                kcmp_ref[0, g, :, 0:HEAD_DIM] = jnp.where(valid, kc, 0.0).astype(BF16)
                flag = (col_f == 0) & jnp.logical_not(valid)
                kcmp_ref[0, g, :, HEAD_DIM:] = jnp.where(flag, 1.0, 0.0).astype(BF16)
            else:
                vt = lax.dot_general(w2vT_ref[...], ap, (((1,), (1,)), ((), ())),
                                     preferred_element_type=F32) + b2v_ref[...]
                valid = (col_p >= CMP_PAD) & (col_p < CMP_PAD + nc)
                vcmpT_ref[0, g, 0:HEAD_DIM, :] = jnp.where(valid, vt, 0.0).astype(BF16)
                vcmpT_ref[0, g, HEAD_DIM:, :] = jnp.where(row_o == 0, 1.0, 0.0).astype(BF16)


def _compress(kc, vc, wbig, pos8, w1, b1, w2k, w2vT, b2k, b2v, kg, *, ncp):
    B = kc.shape[0]
    full = lambda a: pl.BlockSpec(a.shape, lambda b: (0,) * a.ndim)
    return pl.pallas_call(
        _compress_kernel, grid=(B,),
        in_specs=[pl.BlockSpec((1,) + kc.shape[1:], lambda b: (b, 0, 0)),
                  pl.BlockSpec((1,) + vc.shape[1:], lambda b: (b, 0, 0)),
                  full(wbig), full(pos8), full(w1), full(b1), full(w2k), full(w2vT),
                  full(b2k), full(b2v), full(kg)],
        out_specs=(pl.BlockSpec((1, NSA_GROUPS, ncp, 2 * HEAD_DIM), lambda b: (b, 0, 0, 0)),
                   pl.BlockSpec((1, NSA_GROUPS, V_ROWS, ncp), lambda b: (b, 0, 0, 0))),
        out_shape=(jax.ShapeDtypeStruct((B, NSA_GROUPS, ncp, 2 * HEAD_DIM), BF16),
                   jax.ShapeDtypeStruct((B, NSA_GROUPS, V_ROWS, ncp), BF16)),
        scratch_shapes=[pltpu.VMEM((ncp, CMP_HIDDEN), F32)],
        compiler_params=_cparams(("parallel",)), name="compress",
    )(kc, vc, wbig, pos8, w1, b1, w2k, w2vT, b2k, b2v, kg)


def _flash_update(s_ref, smax_ref, vT, m_ref, acc_ref):
    m_prev = m_ref[...]
    m_new = jnp.maximum(m_prev, smax_ref[...])
    p = jnp.exp2(s_ref[...] - m_new)
    acc_ref[...] = (jnp.exp2(m_prev - m_new) * acc_ref[...]
                    + jnp.dot(vT, p.astype(BF16), preferred_element_type=F32))
    m_ref[...] = m_new


def _attn_kernel(qT_ref, kcmp_ref, vcmpT_ref, bc_ref, ksel_ref, vselT_ref, kwin_ref, vwinT_ref,
                 stab_ref, wtab_ref, gT_ref, o_ref,
                 lc_ref, psum_ref, qaug_ref, qfar_ref, oc_ref, ow_ref, ms_ref, accs_ref, sa_ref, sb_ref, ma_ref, mb_ref):
    qt = pl.program_id(2)
    ncp = kcmp_ref.shape[2]
    ns = qaug_ref.shape[0] - 2 * HEAD_DIM
    nq = NSA_HPG * QT
    t_lane = qt * QT + lax.broadcasted_iota(jnp.int32, (1, QT), 1)

    for r in range(NSA_HPG):
        qaug_ref[0:HEAD_DIM, r * QT:(r + 1) * QT] = qT_ref[0, r * HEAD_DIM:(r + 1) * HEAD_DIM, :]
    flag_row = lax.broadcasted_iota(jnp.int32, (HEAD_DIM, nq), 0) == 0
    qaug_ref[HEAD_DIM:2 * HEAD_DIM, :] = jnp.where(flag_row, NEG_INF, 0.0).astype(BF16)
    qk = qaug_ref[0:2 * HEAD_DIM, :]

    rho = lax.broadcasted_iota(jnp.int32, (ncp, nq), 0)
    band0 = pl.multiple_of(qt * (QT // CMP_STRIDE), V7X_SUBLANES)
    lc_ref[...] = jnp.dot(kcmp_ref[0, 0], qk, preferred_element_type=F32)
    lc_ref[pl.ds(band0, CMP_BAND), :] += bc_ref[0]
    lc = jnp.where(rho < band0 + CMP_BAND, lc_ref[...], NEG_INF)
    e = jnp.exp2(lc - jnp.max(lc, axis=0, keepdims=True))
    oc_aug = jnp.dot(vcmpT_ref[0, 0], e.astype(BF16), preferred_element_type=F32)
    t_q = qt * QT + lax.broadcasted_iota(jnp.int32, (1, nq), 1) % QT
    inv_c = jnp.where(t_q >= CMP_BLOCK - 1, 1.0 / oc_aug[HEAD_DIM:HEAD_DIM + 1, :], 0.0)
    oc_ref[...] = oc_aug[0:HEAD_DIM] * inv_c
    p = e * inv_c
    psum = (p[:, 0:QT] + p[:, QT:2 * QT]) + (p[:, 2 * QT:3 * QT] + p[:, 3 * QT:4 * QT])
    for c in range(QT // V7X_LANES):
        psum_ref[c] = psum[:, c * V7X_LANES:(c + 1) * V7X_LANES]

    def strided_sum(c):
        acc = psum_ref[c, pl.ds(CMP_PAD - 1, ns, stride=4), :]
        for k in range(1, 5):
            acc = acc + psum_ref[c, pl.ds(CMP_PAD - 1 + k, ns, stride=4), :]
        return acc

    imp = jnp.concatenate([strided_sum(c) for c in range(QT // V7X_LANES)], axis=1)
    jrow = lax.broadcasted_iota(jnp.int32, (ns, QT), 0)
    cur = t_lane // SEL_BLOCK
    forced = (jrow == 0) | (jrow == cur) | (jrow == cur - 1)
    live = jrow * SEL_BLOCK <= t_lane
    score = jnp.where(live, jnp.where(forced, -jnp.inf, imp), NEG_INF)
    jrow_f = jrow.astype(F32)

    def col_reduce(x, pair, reduce):
        parts = [x[i * V7X_SUBLANES:(i + 1) * V7X_SUBLANES, :] for i in range(x.shape[0] // V7X_SUBLANES)]
        while len(parts) > 1:
            parts = [pair(parts[i], parts[i + 1]) for i in range(0, len(parts) - 1, 2)] + parts[len(parts) & ~1:]
        return reduce(parts[0], axis=0, keepdims=True)

    for _ in range(min(SEL_TOPN, ns) - 3):
        mx = col_reduce(score, jnp.maximum, jnp.max)
        idx = col_reduce(jnp.where(score == mx, jrow_f, float(ns)), jnp.minimum, jnp.min)
        score = jnp.where(jrow_f == idx, -jnp.inf, score)
    sel = live & (score == -jnp.inf)
    negmask = jnp.where(sel, 0.0, NEG_INF).astype(BF16)
    negfar = jnp.where(sel & (jrow < (qt - 1) * (QT // SEL_BLOCK)), 0.0, NEG_INF).astype(BF16)
    qfar_ref[0:2 * HEAD_DIM, :] = qaug_ref[0:2 * HEAD_DIM, :]
    for r in range(NSA_HPG):
        qaug_ref[2 * HEAD_DIM:, r * QT:(r + 1) * QT] = negmask
        qfar_ref[2 * HEAD_DIM:, r * QT:(r + 1) * QT] = negfar

    w0 = pl.multiple_of(qt * QT, QT)
    sw = jnp.dot(kwin_ref[0, 0, pl.ds(w0, WIN_KEYS), :], qk, preferred_element_type=F32) + wtab_ref[0]
    pw = jnp.exp2(sw - jnp.max(sw, axis=0, keepdims=True))
    vw = jnp.concatenate([vwinT_ref[0, 0, qt + j] for j in range(WIN_KEYS // QT)], axis=1)
    ow_aug = jnp.dot(vw, pw.astype(BF16), preferred_element_type=F32)
    ow_ref[...] = ow_aug[0:HEAD_DIM] * (1.0 / ow_aug[HEAD_DIM:HEAD_DIM + 1, :])

    n0 = pl.multiple_of((ATT_PAD // QT - 1 + qt) * QT, QT)
    sn = jnp.dot(ksel_ref[0, 0, pl.ds(n0, NEAR_KEYS), :], qaug_ref[...],
                 preferred_element_type=F32) + stab_ref[0]
    mn = jnp.max(sn, axis=0, keepdims=True)
    pn = jnp.exp2(sn - mn)
    vn = jnp.concatenate([vselT_ref[0, 0, ATT_PAD // QT - 1 + qt + j] for j in range(NEAR_KEYS // QT)], axis=1)
    ms_ref[...] = mn
    accs_ref[...] = jnp.dot(vn, pn.astype(BF16), preferred_element_type=F32)

    tiles_per_far = FAR_KEYS // QT
    n_far = (qt - 1 + tiles_per_far - 1) // tiles_per_far
    last_far = (ksel_ref.shape[2] - ATT_PAD) // FAR_KEYS - 1

    def far_logits(g, dst_ref, dmax_ref):
        r0 = pl.multiple_of(ATT_PAD + jnp.minimum(g, last_far) * FAR_KEYS, FAR_KEYS)
        s = jnp.dot(ksel_ref[0, 0, pl.ds(r0, FAR_KEYS), :], qfar_ref[...], preferred_element_type=F32)
        dst_ref[...] = s
        dmax_ref[...] = jnp.max(s, axis=0, keepdims=True)

    def far_values(g):
        t0 = (ATT_PAD + g * FAR_KEYS) // QT
        return jnp.concatenate([vselT_ref[0, 0, t0 + j] for j in range(tiles_per_far)], axis=1)

    far_logits(0, sa_ref, ma_ref)

    def far_body(j, carry):
        bufs = ((sa_ref, ma_ref), (sb_ref, mb_ref))
        for u in range(FAR_UNROLL):
            g = FAR_UNROLL * j + u
            far_logits(g + 1, *bufs[(u + 1) % 2])
            _flash_update(*bufs[u % 2], far_values(jnp.minimum(g, last_far)), ms_ref, accs_ref)
        return carry

    lax.fori_loop(0, (n_far + FAR_UNROLL - 1) // FAR_UNROLL, far_body, 0)

    o_s = accs_ref[0:HEAD_DIM, :] * (1.0 / accs_ref[HEAD_DIM:HEAD_DIM + 1, :])
    for r in range(NSA_HPG):
        cols = slice(r * QT, (r + 1) * QT)
        o = (gT_ref[0, 3 * r:3 * r + 1, :] * oc_ref[:, cols] + gT_ref[0, 3 * r + 1:3 * r + 2, :] * o_s[:, cols]
             + gT_ref[0, 3 * r + 2:3 * r + 3, :] * ow_ref[:, cols])
        o_ref[0, :, r * HEAD_DIM:(r + 1) * HEAD_DIM] = o.T


def _attn(qT, kcmp, vcmpT, bc, ksel, vT5, kwin, stab, wtab, gT):
    B, _, T = qT.shape
    ncp = kcmp.shape[2]
    ns = T // SEL_BLOCK
    nq = NSA_HPG * QT
    tp = T + ATT_PAD
    nt = tp // QT
    grid = (B, NSA_GROUPS, T // QT)

    def held(shape, index_map):
        return pl.BlockSpec(shape, index_map, pipeline_mode=pl.Buffered(1))

    in_specs = [
        pl.BlockSpec((1, NSA_HPG * HEAD_DIM, QT), lambda b, g, q: (b, g, q)),
        held((1, 1, ncp, 2 * HEAD_DIM), lambda b, g, q: (b, g, 0, 0)),
        held((1, 1, V_ROWS, ncp), lambda b, g, q: (b, g, 0, 0)),
        held((1, CMP_BAND, nq), lambda b, g, q: (g, 0, 0)),
        pl.BlockSpec((1, 1, tp, 2 * HEAD_DIM + ns), lambda b, g, q: (b, g, 0, 0)),
        pl.BlockSpec((1, 1, nt, V_ROWS, QT), lambda b, g, q: (b, g, 0, 0, 0)),
        pl.BlockSpec((1, 1, tp, 2 * HEAD_DIM), lambda b, g, q: (b, g, 0, 0)),
        pl.BlockSpec((1, 1, nt, V_ROWS, QT), lambda b, g, q: (b, NSA_GROUPS + g, 0, 0, 0)),
        held((1, NEAR_KEYS, nq), lambda b, g, q: (g, 0, 0)),
        held((1, WIN_KEYS, nq), lambda b, g, q: (g, 0, 0)),
        pl.BlockSpec((1, GATE_ROWS, QT), lambda b, g, q: (b, g, q)),
    ]
    scratch = [
        pltpu.VMEM((ncp, nq), F32),
        pltpu.VMEM((QT // V7X_LANES, ncp, V7X_LANES), F32),
        pltpu.VMEM((2 * HEAD_DIM + ns, nq), BF16),
        pltpu.VMEM((2 * HEAD_DIM + ns, nq), BF16),
        pltpu.VMEM((HEAD_DIM, nq), F32),
        pltpu.VMEM((HEAD_DIM, nq), F32),
        pltpu.VMEM((1, nq), F32),
        pltpu.VMEM((V_ROWS, nq), F32),
        pltpu.VMEM((FAR_KEYS, nq), F32),
        pltpu.VMEM((FAR_KEYS, nq), F32),
        pltpu.VMEM((1, nq), F32), pltpu.VMEM((1, nq), F32),
    ]
    return pl.pallas_call(
        _attn_kernel, grid=grid, in_specs=in_specs,
        out_specs=pl.BlockSpec((1, QT, NSA_HPG * HEAD_DIM), lambda b, g, q: (b, q, g)),
        out_shape=jax.ShapeDtypeStruct((B, T, D_NSA), F32),
        scratch_shapes=scratch,
        compiler_params=_cparams(("parallel", "parallel", "arbitrary")), name="attn",
    )(qT, kcmp, vcmpT, bc, ksel, vT5, kwin, vT5, stab, wtab, gT)


def _rwkv_tokens(rw_ref, halo_ref, mu_ref, w0_ref, w2_ref, a0_ref, a2_ref, g2_ref, kk_ref, ka_ref,
                 rk_ref, bd_ref, ext_ref):
    tm = rw_ref.shape[1]
    first_tile = pl.program_id(1) == 0
    ext_ref[0:V7X_SUBLANES, :] = jnp.where(first_tile, 0.0, halo_ref[0])
    ext_ref[V7X_SUBLANES:, :] = rw_ref[0]
    cur = rw_ref[0]
    prev = ext_ref[pl.ds(V7X_SUBLANES - 1, tm), :]
    mixed = cur + (prev - cur) * mu_ref[...]
    c = D_RWKV
    r = mixed[:, 0:c]
    k = mixed[:, c:2 * c]
    v = mixed[:, 2 * c:3 * c]
    xw = mixed[:, 3 * c:3 * c + LORA_W]
    xa = mixed[:, 3 * c + LORA_W:3 * c + LORA_W + LORA_A]
    xg = mixed[:, 3 * c + LORA_W + LORA_A:]

    z = -(w0_ref[...] + _bdot(jnp.tanh(xw), w2_ref[...]))
    softplus = jnp.maximum(z, 0.0) + jnp.log1p(jnp.exp(-jnp.abs(z)))
    w = -softplus - 0.5
    a = jax.nn.sigmoid(a0_ref[...] + _bdot(xa, a2_ref[...]))
    g = _bdot(jax.nn.sigmoid(xg), g2_ref[...])

    kk = k * kk_ref[...]
    n2 = _head_sum(kk * kk, bd_ref[...])
    kkn = kk / jnp.maximum(jnp.sqrt(n2), 1e-12)
    k2 = k * (1.0 + (a - 1.0) * ka_ref[...])
    bonus = _head_sum(r * k2 * rk_ref[...], bd_ref[...]) * v

    lw = -jnp.exp(w)
    return dict(r=r, lw=lw, k=k2, v=v, kk=kkn, b=kkn * a, g=g, bonus=bonus)


WKV_BATCH = 8
WKV_GROUP = 4
WKV_GW = WKV_GROUP * HEAD_DIM


def _wkv_kernel(rw_ref, halo_ref, mu_ref, w0_ref, w2_ref, a0_ref, a2_ref, g2_ref, kk_ref, ka_ref, rk_ref, bd_ref,
                y_ref, g_ref, bonus_ref, s_ref, ext_ref, *, chunks):
    L = WKV_CHUNK
    tok = _rwkv_tokens(rw_ref, halo_ref, mu_ref, w0_ref, w2_ref, a0_ref, a2_ref, g2_ref, kk_ref, ka_ref,
                       rk_ref, bd_ref, ext_ref)
    g_ref[0] = tok["g"]
    bonus_ref[0] = tok["bonus"]
    assert L == HEAD_DIM
    gw = WKV_GW
    n_groups = RWKV_HEADS // WKV_GROUP

    @pl.when(pl.program_id(1) == 0)
    def _():
        s_ref[...] = jnp.zeros(s_ref.shape, F32)

    ti = lax.broadcasted_iota(jnp.int32, (L, L), 0)
    tj = lax.broadcasted_iota(jnp.int32, (L, L), 1)
    tri = jnp.where(ti >= tj, 1.0, 0.0).astype(BF16)
    row = lax.broadcasted_iota(jnp.int32, (L, gw), 0)
    col = lax.broadcasted_iota(jnp.int32, (L, gw), 1) % L
    low_strict = col < row
    low_incl = col <= row
    eye_sbs = jnp.where(col == row, 1.0, 0.0)
    brow = lax.broadcasted_iota(jnp.int32, (gw, gw), 0) // L
    bcol = lax.broadcasted_iota(jnp.int32, (gw, gw), 1) // HEAD_DIM
    same_head = brow == bcol

    def bd_rows(x):
        xb = x.astype(BF16)
        return jnp.where(same_head, jnp.concatenate([xb] * WKV_GROUP, axis=0), jnp.zeros((), BF16))

    def mm(a, b_bf16):
        return jnp.dot(a.astype(BF16), b_bf16, preferred_element_type=F32)

    def tn(a, b):
        return lax.dot_general(a.astype(BF16), b.astype(BF16), (((0,), (0,)), ((), ())),
                               preferred_element_type=F32)

    pre, w2, c2, m_lr, d_sbs = {}, {}, {}, {}, {}

    def grp(c, g, name):
        return pre[c][name][:, g * gw:(g + 1) * gw]

    for c0 in range(0, chunks, WKV_BATCH):
        batch = range(c0, min(c0 + WKV_BATCH, chunks))
        inst = [(c, g) for c in batch for g in range(n_groups)]
        for c in batch:
            rows = slice(c * L, (c + 1) * L)
            lw = tok["lw"][rows]
            cs = _dot_exact_rhs_left(tri, lw)
            c_last = cs[L - 1:L, :]
            e_nc = jnp.exp(-cs)
            e_lc = jnp.exp(c_last - cs)
            kk = tok["kk"][rows]
            b = tok["b"][rows]
            k = tok["k"][rows]
            pre[c] = dict(a=-kk * jnp.exp(cs - lw), r=tok["r"][rows] * jnp.exp(cs), bh=b * e_nc, kh=k * e_nc,
                          be=b * e_lc, ke=k * e_lc, v=tok["v"][rows], e_last=jnp.exp(c_last))

        t_all = {}
        for (c, g) in inst:
            lhs = jnp.concatenate([grp(c, g, "a"), grp(c, g, "r")], axis=0).astype(BF16)
            rhs = jnp.concatenate([bd_rows(grp(c, g, "bh")), bd_rows(grp(c, g, "kh"))], axis=0)
            t_all[c, g] = lax.dot_general(lhs, rhs, (((1,), (1,)), ((), ())), preferred_element_type=F32)
        n_m = {i: jnp.where(low_strict, t_all[i][0:L, 0:gw], 0.0) for i in inst}
        tak = {i: jnp.where(low_strict, t_all[i][0:L, gw:], 0.0) for i in inst}
        trb = {i: jnp.where(low_incl, t_all[i][L:, 0:gw], 0.0) for i in inst}
        trk = {i: jnp.where(low_incl, t_all[i][L:, gw:], 0.0) for i in inst}
        z = {i: eye_sbs + n_m[i] for i in inst}
        pw = dict(n_m)
        for _ in range(int(math.log2(L)) - 1):
            pw = {i: mm(pw[i], bd_rows(pw[i])) for i in inst}
            z = {i: z[i] + mm(z[i], bd_rows(pw[i])) for i in inst}
        vbd = {(c, g): bd_rows(grp(c, g, "v")) for (c, g) in inst}
        w1 = {(c, g): mm(z[c, g], bd_rows(grp(c, g, "a"))) for (c, g) in inst}
        tv = {i: mm(tak[i], vbd[i]) for i in inst}
        c1 = {i: mm(z[i], bd_rows(tv[i])) for i in inst}
        w2.update({(c, g): grp(c, g, "r") + mm(trb[c, g], bd_rows(w1[c, g])) for (c, g) in inst})
        c2.update({i: mm(trb[i], bd_rows(c1[i])) + mm(trk[i], vbd[i]) for i in inst})
        m_lr.update({(c, g): jnp.where(same_head, tn(w1[c, g], grp(c, g, "be")), 0.0).astype(BF16)
                     for (c, g) in inst})
        for (c, g) in inst:
            full = jnp.where(same_head,
                             tn(jnp.concatenate([c1[c, g], grp(c, g, "v")], axis=0),
                                jnp.concatenate([grp(c, g, "be"), grp(c, g, "ke")], axis=0)), 0.0)
            d_sbs[c, g] = ((full[0:L] + full[L:2 * L]) + (full[2 * L:3 * L] + full[3 * L:4 * L]))

    for g in range(n_groups):
        s = s_ref[:, g * gw:(g + 1) * gw]
        for c in range(chunks):
            g_bd = jnp.where(same_head, jnp.concatenate([s.T.astype(BF16)] * WKV_GROUP, axis=1),
                             jnp.zeros((), BF16))
            y_ref[0, c * L:(c + 1) * L, g * gw:(g + 1) * gw] = mm(w2[c, g], g_bd) + c2[c, g]
            s = s * grp(c, g, "e_last") + mm(s, m_lr[c, g]) + d_sbs[c, g]
        s_ref[:, g * gw:(g + 1) * gw] = s


def _dot_exact_rhs_left(m_bf16, x):
    hi, mid, lo = _split3(x)
    return (jnp.dot(m_bf16, hi, preferred_element_type=F32)
            + jnp.dot(m_bf16, mid, preferred_element_type=F32)
            + jnp.dot(m_bf16, lo, preferred_element_type=F32))


def _wkv(rw, mu, w0, w2, a0, a2, g2, k_k, k_a, r_k, bd, *, chunks):
    B, T, C = rw.shape
    rows = chunks * WKV_CHUNK
    hb = rows // V7X_SUBLANES
    full = lambda a: pl.BlockSpec(a.shape, lambda bb, c: (0,) * a.ndim)
    tok = pl.BlockSpec((1, rows, D_RWKV), lambda bb, c: (bb, c, 0))
    return pl.pallas_call(
        functools.partial(_wkv_kernel, chunks=chunks), grid=(B, T // rows),
        in_specs=[pl.BlockSpec((1, rows, C), lambda bb, c: (bb, c, 0)),
                  pl.BlockSpec((1, V7X_SUBLANES, C), lambda bb, c: (bb, jnp.maximum(c * hb - 1, 0), 0)),
                  full(mu), full(w0), full(w2), full(a0), full(a2), full(g2), full(k_k), full(k_a),
                  full(r_k), full(bd)],
        out_specs=(tok,) * 3,
        out_shape=(jax.ShapeDtypeStruct((B, T, D_RWKV), F32),) * 3,
        scratch_shapes=[pltpu.VMEM((HEAD_DIM, D_RWKV), F32), pltpu.VMEM((rows + V7X_SUBLANES, C), F32)],
        compiler_params=_cparams(("parallel", "arbitrary")), name="wkv",
    )(rw, rw, mu, w0, w2, a0, a2, g2, k_k, k_a, r_k, bd)


def _mix_rows(x, on, y, bonus, g, bd_ref, lnw_ref, lnb_ref, wo_ref, g2_ref):
    mu = _dot_exact_rhs(y, bd_ref[...]) * (1.0 / HEAD_DIM)
    yc = y - mu
    var = _head_sum(yc * yc, bd_ref[...]) * (1.0 / HEAD_DIM)
    yn = yc * lax.rsqrt(var + GN_EPS) * lnw_ref[...] + lnb_ref[...]
    orw = (yn + bonus) * g
    x1 = x + _bdot(on, wo_ref[0:D_NSA, :]) + _bdot(orw, wo_ref[D_NSA:, :])
    ms = jnp.mean(x1 * x1, axis=-1, keepdims=True)
    return x1, (x1 * lax.rsqrt(ms + NORM_EPS) * g2_ref[...]).astype(BF16)


FFN_HALO = 16


def _ffn_kernel(x_ref, on_ref, y_ref, bonus_ref, g_ref, xh_ref, onh_ref, yh_ref, bonush_ref, gh_ref,
                bd_ref, lnw_ref, lnb_ref, wo_ref, g2_ref,
                wv_ref, wg_ref, cwv_ref, cwg_ref, cbv_ref, cbg_ref, wd_ref,
                o_ref, hext_ref, extv_ref, extg_ref, act_ref, x1_ref, *, tiles_per_seq, ft):
    tm = x_ref.shape[0]
    dff = wd_ref.shape[0]
    seq_start = (pl.program_id(0) % tiles_per_seq) == 0
    ext = lambda h_ref, m_ref: jnp.concatenate([h_ref[...], m_ref[...]], axis=0)
    x1, h2 = _mix_rows(ext(xh_ref, x_ref), ext(onh_ref, on_ref), ext(yh_ref, y_ref), ext(bonush_ref, bonus_ref),
                       ext(gh_ref, g_ref), bd_ref, lnw_ref, lnb_ref, wo_ref, g2_ref)
    x1_ref[...] = x1[FFN_HALO:]
    row = lax.broadcasted_iota(jnp.int32, h2.shape, 0)
    hext_ref[...] = jnp.where(seq_start & (row < FFN_HALO), jnp.zeros((), BF16), h2)

    for j in range(dff // ft):
        cols = slice(j * ft, (j + 1) * ft)

        def conv_branch(w_ref, cw_ref, cb_ref, ext_ref):
            ext_ref[j % 2] = jnp.dot(hext_ref[...], w_ref[:, cols], preferred_element_type=F32)
            out = cb_ref[:, cols] + ext_ref[j % 2, pl.ds(FFN_HALO, tm), :] * cw_ref[CONV_W - 1:CONV_W, cols]
            for i in range(CONV_W - 1):
                back = CONV_W - 1 - i
                out = out + ext_ref[j % 2, pl.ds(FFN_HALO - back, tm), :] * cw_ref[i:i + 1, cols]
            return out

        u_val = conv_branch(wv_ref, cwv_ref, cbv_ref, extv_ref)
        u_gate = conv_branch(wg_ref, cwg_ref, cbg_ref, extg_ref)
        act_ref[:, cols] = ((u_gate * jax.nn.sigmoid(u_gate)) * u_val).astype(BF16)
    o_ref[...] = x1_ref[...] + jnp.dot(act_ref[...], wd_ref[...], preferred_element_type=F32)


def _ffn(x, on, y, bonus, g, bd, lnw, lnb, wo, g2, wv, wg, cwv, cwg, cbv, cbg, wd, *, tm, ft, seq_len):
    n, d = x.shape
    dff = wv.shape[1]
    hb = tm // FFN_HALO
    kern = functools.partial(_ffn_kernel, tiles_per_seq=seq_len // tm, ft=ft)
    resident = lambda a: pl.BlockSpec(a.shape, lambda i: (0,) * a.ndim, pipeline_mode=pl.Buffered(1))
    tile = lambda a: pl.BlockSpec((tm, a.shape[1]), lambda i: (i, 0))
    halo = lambda a: pl.BlockSpec((FFN_HALO, a.shape[1]), lambda i: (jnp.maximum(i * hb - 1, 0), 0))
    tokens = (x, on, y, bonus, g)
    params = (bd, lnw, lnb, wo, g2, wv, wg, cwv, cwg, cbv, cbg, wd)
    return pl.pallas_call(
        kern, grid=(n // tm,),
        in_specs=[tile(a) for a in tokens] + [halo(a) for a in tokens] + [resident(a) for a in params],
        out_specs=pl.BlockSpec((tm, d), lambda i: (i, 0)),
        out_shape=jax.ShapeDtypeStruct((n, d), F32),
        scratch_shapes=[pltpu.VMEM((tm + FFN_HALO, d), BF16),
                        pltpu.VMEM((2, tm + FFN_HALO, ft), F32), pltpu.VMEM((2, tm + FFN_HALO, ft), F32),
                        pltpu.VMEM((tm, dff), BF16), pltpu.VMEM((tm, d), F32)],
        compiler_params=_cparams(("parallel",)), name="ffn",
    )(*tokens, *tokens, *params)


def _t5_bucket(dist):
    n = np.maximum(dist, 0)
    max_exact = N_BUCKETS // 2
    nf = np.maximum(n, 1).astype(np.float32)
    large = max_exact + (np.log(nf / np.float32(max_exact)) / np.float32(math.log(MAX_DISTANCE / max_exact))
                         * np.float32(N_BUCKETS - max_exact)).astype(np.int32)
    large = np.minimum(large, N_BUCKETS - 1)
    return np.where(n < max_exact, n, large)


def _toeplitz_kernel(*refs, steps):
    n = len(steps)
    for v_ref, o_ref, step in zip(refs[:n], refs[n:], steps):
        rows, width = o_ref.shape[1], o_ref.shape[2]
        x = jnp.broadcast_to(v_ref[0], (rows, v_ref.shape[2]))
        o_ref[0] = pltpu.roll(x, 0, 1, stride=step, stride_axis=0)[:, :width]


def _toeplitz(specs):
    return pl.pallas_call(
        functools.partial(_toeplitz_kernel, steps=tuple(step for _, _, step in specs)), grid=(NSA_GROUPS, NSA_HPG),
        in_specs=[pl.BlockSpec((1, 1, v.shape[2]), lambda g, r: (g * NSA_HPG + r, 0, 0)) for v, _, _ in specs],
        out_specs=tuple(pl.BlockSpec((1, rows, QT), lambda g, r: (g, 0, r)) for _, rows, _ in specs),
        out_shape=tuple(jax.ShapeDtypeStruct((NSA_GROUPS, rows, NSA_HPG * QT), F32) for _, rows, _ in specs),
        compiler_params=_cparams(("parallel", "parallel")), name="toeplitz",
    )(*[v for v, _, _ in specs])


def _bias_tables(rel_bias):
    rel = (rel_bias - rel_bias[N_BUCKETS - 1][None, :]) * LOG2E

    def table(rows, step, d0, d_hi):
        period = step * rows + QT
        i = np.arange(period)
        d = d0 + np.where(i < QT, i, i - period)
        onehot = (_t5_bucket(d)[:, None] == np.arange(N_BUCKETS)[None, :]).astype(np.float32)
        v = jnp.dot(jnp.asarray(onehot), rel, precision=lax.Precision.HIGHEST)
        v = jnp.where(jnp.asarray((d >= 0) & (d < d_hi))[:, None], v, NEG_INF).T
        return v.reshape(NSA_HEADS, 1, period), rows, step

    no_limit = 1 << 30
    stab, wtab, bc = _toeplitz([table(NEAR_KEYS, 1, QT, no_limit), table(WIN_KEYS, 1, WINDOW, WINDOW),
                                table(CMP_BAND, CMP_STRIDE, CMP_STRIDE * CMP_PAD - CMP_BLOCK + 1, no_limit)])
    return stab, wtab, bc


def _compress_weights(w1):
    half = CMP_BLOCK // 2
    rows = half * HEAD_DIM
    w3 = jnp.concatenate([w1[:rows], w1[rows:]], axis=1).reshape(half, HEAD_DIM, 2 * CMP_HIDDEN)
    z = jnp.zeros_like(w3)
    assert NSA_GROUPS == 2
    big = jnp.concatenate([jnp.concatenate([w3, z], axis=2), jnp.concatenate([z, w3], axis=2)], axis=1)
    return big.reshape(half * NSA_GROUPS * HEAD_DIM, NSA_GROUPS * 2 * CMP_HIDDEN)


def kernel(x, norm1_g, w_in, q_norm_g, k_norm_g, cmp_pos, cmp_w1, cmp_b1, cmp_w2, cmp_b2, rel_bias, rwkv_mu,
           w0, w2, a0, a2, g2, k_k, k_a, r_k, ln_x_w, ln_x_b, w_out, norm2_g, ffn_up, conv_w, conv_b, ffn_down):
    B, T, D = x.shape
    depth = w_in.shape[0]
    d_ff = ffn_down.shape[1]
    assert T % 2048 == 0 and D_NSA + 6 * D_KV + 3 * NSA_HEADS + D_RWKV_IN == w_in.shape[2]
    ncp = T // CMP_STRIDE + V7X_LANES
    stab, wtab, bc = _bias_tables(rel_bias)
    ii = jnp.arange(D_RWKV)
    bd = (ii[:, None] // HEAD_DIM == ii[None, :] // HEAD_DIM).astype(BF16)

    for l in range(depth):
        wi = w_in[l]
        o = D_NSA
        q_w, kc_w, vc_w, ksl_w, vsl_w, kwn_w, vwn_w = (
            wi[:, 0:o], wi[:, o:o + D_KV], wi[:, o + D_KV:o + 2 * D_KV], wi[:, o + 2 * D_KV:o + 3 * D_KV],
            wi[:, o + 3 * D_KV:o + 4 * D_KV], wi[:, o + 4 * D_KV:o + 5 * D_KV], wi[:, o + 5 * D_KV:o + 6 * D_KV])
        gl_w = wi[:, o + 6 * D_KV:o + 6 * D_KV + 3 * NSA_HEADS]
        rw_w = wi[:, o + 6 * D_KV + 3 * NSA_HEADS:]
        wstd = jnp.concatenate([kc_w, vc_w, ksl_w, kwn_w, rw_w], axis=1).astype(BF16)
        gl_rows = gl_w.T.reshape(NSA_GROUPS, 3 * NSA_HPG, D)
        gl_rows = jnp.pad(gl_rows, ((0, 0), (0, GATE_ROWS - 3 * NSA_HPG), (0, 0))).reshape(-1, D)
        wt = jnp.concatenate([q_w.T, vsl_w.T, vwn_w.T, gl_rows], axis=0).astype(BF16)
        qg = q_norm_g[l].reshape(HEAD_DIM, 1)
        kg = jnp.stack([jnp.tile(k_norm_g[l, 1], NSA_GROUPS), jnp.tile(k_norm_g[l, 2], NSA_GROUPS)])

        qT, kc, vc, ksel, kwin, vT5, gT, rw = _proj(x, norm1_g[l].reshape(1, D), wstd, wt, qg, kg, tm=512)

        wbig = jnp.stack([_compress_weights(cmp_w1[l, 0]), _compress_weights(cmp_w1[l, 1])]).astype(BF16)
        pos8 = jnp.pad(cmp_pos[l].reshape(2, 1, CMP_BLOCK * HEAD_DIM),
                       ((0, 0), (0, V7X_SUBLANES - 1), (0, 0))).astype(BF16)
        kcmp, vcmpT = _compress(
            kc, vc, wbig, pos8,
            cmp_w1[l].astype(BF16), cmp_b1[l].reshape(2, 1, CMP_HIDDEN), cmp_w2[l, 0].astype(BF16),
            cmp_w2[l, 1].T.astype(BF16), cmp_b2[l, 0].reshape(1, HEAD_DIM), cmp_b2[l, 1].reshape(HEAD_DIM, 1),
            k_norm_g[l, 0].reshape(1, HEAD_DIM), ncp=ncp)

        o_nsa = _attn(qT, kcmp, vcmpT, bc, ksel, vT5, kwin, stab, wtab, gT)

        row = lambda a: a.reshape(1, -1)
        y, g, bonus = _wkv(
            rw, row(rwkv_mu[l]), row(w0[l]), w2[l].astype(BF16), row(a0[l]), a2[l].astype(BF16),
            g2[l].astype(BF16), row(k_k[l]), row(k_a[l]), row(r_k[l]), bd, chunks=8)

        n = B * T
        up = ffn_up[l].astype(BF16)
        x = _ffn(x.reshape(n, D), o_nsa.reshape(n, D_NSA), y.reshape(n, D_RWKV), bonus.reshape(n, D_RWKV),
                 g.reshape(n, D_RWKV), bd, row(ln_x_w[l]), row(ln_x_b[l]), w_out[l].astype(BF16),
                 norm2_g[l].reshape(1, D), up[:, :d_ff], up[:, d_ff:], conv_w[l][:, :d_ff], conv_w[l][:, d_ff:],
                 conv_b[l][:d_ff].reshape(1, -1), conv_b[l][d_ff:].reshape(1, -1),
                 ffn_down[l].astype(BF16), tm=512, ft=256, seq_len=T).reshape(B, T, D)
    return x
```

```python
import functools
import math

import jax
import jax.numpy as jnp
import numpy as np
from jax import lax
from jax.experimental import pallas as pl
from jax.experimental.pallas import tpu as pltpu

F32 = jnp.float32
BF16 = jnp.bfloat16

V7X_LANES = 128
V7X_SUBLANES = 8
V7X_VMEM_LIMIT_BYTES = 56 * 1024 * 1024

HEAD_DIM = 64
NSA_HEADS = 8
NSA_GROUPS = 2
NSA_HPG = NSA_HEADS // NSA_GROUPS
RWKV_HEADS = 8
D_NSA = NSA_HEADS * HEAD_DIM
D_RWKV = RWKV_HEADS * HEAD_DIM
D_KV = NSA_GROUPS * HEAD_DIM
CMP_BLOCK = 32
CMP_STRIDE = 16
CMP_HIDDEN = 128
SEL_BLOCK = 64
SEL_TOPN = 16
WINDOW = 512
N_BUCKETS = 32
MAX_DISTANCE = 128
LORA_W = 64
LORA_A = 64
LORA_G = 128
D_RWKV_IN = 3 * D_RWKV + LORA_W + LORA_A + LORA_G
CONV_W = 3
NORM_EPS = 1e-6
GN_EPS = 64e-5
NEG_INF = -1e30
FORCE_SCORE = 1e9

QT = 256
CMP_PAD = 8
CMP_BAND = QT // CMP_STRIDE + 8
LOG2E = math.log2(math.e)
Q_SCALE = HEAD_DIM ** -0.5 * LOG2E
V_ROWS = HEAD_DIM + 16
ATT_PAD = WINDOW
NEAR_KEYS = 2 * QT
FAR_KEYS = 2 * QT
FAR_UNROLL = 2
WIN_KEYS = WINDOW + QT
WKV_CHUNK = 64
GATE_ROWS = 16
N_T_ROWS = D_NSA + 2 * D_KV + NSA_GROUPS * GATE_ROWS
N_STD_COLS = 4 * D_KV + D_RWKV_IN


def _cparams(sem):
    return pltpu.CompilerParams(dimension_semantics=sem, vmem_limit_bytes=V7X_VMEM_LIMIT_BYTES)


def _bdot(a, b):
    return jnp.dot(a.astype(BF16), b.astype(BF16), preferred_element_type=F32)


def _split3(x):
    hi = x.astype(BF16)
    r1 = x - hi.astype(F32)
    mid = r1.astype(BF16)
    lo = (r1 - mid.astype(F32)).astype(BF16)
    return hi, mid, lo


def _dot_exact_rhs(x, m_bf16):
    hi = x.astype(BF16)
    lo = (x - hi.astype(F32)).astype(BF16)
    return jnp.dot(hi, m_bf16, preferred_element_type=F32) + jnp.dot(lo, m_bf16, preferred_element_type=F32)


def _head_sum(x, m_bf16):
    return jnp.dot(x.astype(BF16), m_bf16, preferred_element_type=F32)


def _proj_kernel(x_ref, g1_ref, wstd_ref, wt_ref, qg_ref, kg_ref,
                 qT_ref, kc_ref, vc_ref, ksel_ref, kwin_ref, vT_ref, gT_ref, rw_ref):
    step = pl.program_id(1)

    @pl.when(step == 0)
    def _():
        for k_ref in (ksel_ref, kwin_ref):
            lanes = lax.broadcasted_iota(jnp.int32, k_ref.shape[2:], 1)
            flagged = jnp.where(lanes == HEAD_DIM, 1.0, 0.0).astype(BF16)
            for g in range(NSA_GROUPS):
                k_ref[0, g] = flagged
        vT_ref[...] = jnp.zeros(vT_ref.shape, BF16)

    @pl.when(step > 0)
    def _():
        _proj_tile(x_ref, g1_ref, wstd_ref, wt_ref, qg_ref, kg_ref,
                   qT_ref, kc_ref, vc_ref, ksel_ref, kwin_ref, vT_ref, gT_ref, rw_ref, step - 1)


def _proj_tile(x_ref, g1_ref, wstd_ref, wt_ref, qg_ref, kg_ref,
               qT_ref, kc_ref, vc_ref, ksel_ref, kwin_ref, vT_ref, gT_ref, rw_ref, tile):
    tm = x_ref.shape[1]
    ns = ksel_ref.shape[3] - 2 * HEAD_DIM
    x = x_ref[0]
    ms = jnp.mean(x * x, axis=-1, keepdims=True)
    h = (x * lax.rsqrt(ms + NORM_EPS) * g1_ref[...]).astype(BF16)

    std = jnp.dot(h, wstd_ref[...], preferred_element_type=F32)
    kc_ref[0] = std[:, 0:D_KV]
    vc_ref[0] = std[:, D_KV:2 * D_KV]
    rw_ref[0] = std[:, 4 * D_KV:]

    lane = lax.broadcasted_iota(jnp.int32, (tm, D_KV), 1)
    first = lane < HEAD_DIM

    def group_rmsnorm(k, gain):
        k2 = k * k
        s0 = jnp.sum(jnp.where(first, k2, 0.0), axis=-1, keepdims=True)
        s1 = jnp.sum(jnp.where(first, 0.0, k2), axis=-1, keepdims=True)
        ms_g = jnp.where(first, s0, s1) * (1.0 / HEAD_DIM)
        return k * lax.rsqrt(ms_g + NORM_EPS) * gain

    ksl = group_rmsnorm(std[:, 2 * D_KV:3 * D_KV], kg_ref[0:1, :]).astype(BF16)
    kwn = group_rmsnorm(std[:, 3 * D_KV:4 * D_KV], kg_ref[1:2, :]).astype(BF16)

    tok = tile * tm + lax.broadcasted_iota(jnp.int32, (tm, ns), 0)
    blk = lax.broadcasted_iota(jnp.int32, (tm, ns), 1)
    onehot = jnp.where((tok // SEL_BLOCK) == blk, 1.0, 0.0).astype(BF16)
    zeros = jnp.zeros((tm, HEAD_DIM), BF16)
    for g in range(NSA_GROUPS):
        ksel_ref[0, g, :, 0:HEAD_DIM] = ksl[:, g * HEAD_DIM:(g + 1) * HEAD_DIM]
        ksel_ref[0, g, :, HEAD_DIM:2 * HEAD_DIM] = zeros
        ksel_ref[0, g, :, 2 * HEAD_DIM:] = onehot
        kwin_ref[0, g, :, 0:HEAD_DIM] = kwn[:, g * HEAD_DIM:(g + 1) * HEAD_DIM]
        kwin_ref[0, g, :, HEAD_DIM:] = zeros

    tr = lax.dot_general(wt_ref[...], h, (((1,), (1,)), ((), ())),
                         preferred_element_type=F32)
    q = tr[0:D_NSA].reshape(NSA_HEADS, HEAD_DIM, tm)
    qms = jnp.mean(q * q, axis=1, keepdims=True)
    qn = q * lax.rsqrt(qms + NORM_EPS) * qg_ref[...][None] * Q_SCALE
    qT_ref[0] = qn.reshape(D_NSA, tm).astype(BF16)
    vt = tr[D_NSA:D_NSA + 2 * D_KV].astype(BF16)
    ones_rows = jnp.where(lax.broadcasted_iota(jnp.int32, (V_ROWS - HEAD_DIM, QT), 0) == 0, 1.0, 0.0).astype(BF16)
    for a in range(2 * NSA_GROUPS):
        for j in range(tm // QT):
            vT_ref[0, a, j, 0:HEAD_DIM, :] = vt[a * HEAD_DIM:(a + 1) * HEAD_DIM, j * QT:(j + 1) * QT]
            vT_ref[0, a, j, HEAD_DIM:, :] = ones_rows
    gT_ref[0] = jax.nn.sigmoid(tr[D_NSA + 2 * D_KV:])


def _proj(x, g1, wstd, wt, qg, kg, *, tm):
    B, T, D = x.shape
    ns = T // SEL_BLOCK
    assert tm == ATT_PAD
    tp = T + ATT_PAD
    grid = (B, T // tm + 1)
    const2 = lambda b, i: (0, 0)
    tile = lambda i: jnp.maximum(i - 1, 0)
    out_shape = (
        jax.ShapeDtypeStruct((B, D_NSA, T), BF16),
        jax.ShapeDtypeStruct((B, T, D_KV), F32),
        jax.ShapeDtypeStruct((B, T, D_KV), F32),
        jax.ShapeDtypeStruct((B, NSA_GROUPS, tp, 2 * HEAD_DIM + ns), BF16),
        jax.ShapeDtypeStruct((B, NSA_GROUPS, tp, 2 * HEAD_DIM), BF16),
        jax.ShapeDtypeStruct((B, 2 * NSA_GROUPS, tp // QT, V_ROWS, QT), BF16),
        jax.ShapeDtypeStruct((B, NSA_GROUPS * GATE_ROWS, T), F32),
        jax.ShapeDtypeStruct((B, T, D_RWKV_IN), F32),
    )
    out_specs = (
        pl.BlockSpec((1, D_NSA, tm), lambda b, i: (b, 0, tile(i))),
        pl.BlockSpec((1, tm, D_KV), lambda b, i: (b, tile(i), 0)),
        pl.BlockSpec((1, tm, D_KV), lambda b, i: (b, tile(i), 0)),
        pl.BlockSpec((1, NSA_GROUPS, tm, 2 * HEAD_DIM + ns), lambda b, i: (b, 0, i, 0)),
        pl.BlockSpec((1, NSA_GROUPS, tm, 2 * HEAD_DIM), lambda b, i: (b, 0, i, 0)),
        pl.BlockSpec((1, 2 * NSA_GROUPS, tm // QT, V_ROWS, QT), lambda b, i: (b, 0, i, 0, 0)),
        pl.BlockSpec((1, NSA_GROUPS * GATE_ROWS, tm), lambda b, i: (b, 0, tile(i))),
        pl.BlockSpec((1, tm, D_RWKV_IN), lambda b, i: (b, tile(i), 0)),
    )
    in_specs = [
        pl.BlockSpec((1, tm, D), lambda b, i: (b, tile(i), 0)),
        pl.BlockSpec(g1.shape, const2),
        pl.BlockSpec(wstd.shape, const2),
        pl.BlockSpec(wt.shape, const2),
        pl.BlockSpec(qg.shape, const2),
        pl.BlockSpec(kg.shape, const2),
    ]
    return pl.pallas_call(
        _proj_kernel, grid=grid, in_specs=in_specs, out_specs=out_specs, out_shape=out_shape,
        compiler_params=_cparams(("parallel", "arbitrary")), name="proj",
    )(x, g1, wstd, wt, qg, kg)


def _gelu_tanh(x):
    c = math.sqrt(2.0 / math.pi)
    return x * (0.5 * (1.0 + jnp.tanh(c * (x + 0.044715 * (x * x * x)))))


def _compress_kernel(kc_ref, vc_ref, wbig_ref, pos_ref, w1_ref, b1_ref, w2k_ref, w2vT_ref,
                     b2k_ref, b2v_ref, kg_ref, kcmp_ref, vcmpT_ref, act_ref):
    m = kc_ref.shape[1] // CMP_STRIDE
    nc = m - 1
    ncp = act_ref.shape[0]
    row_m = lax.broadcasted_iota(jnp.int32, (m, CMP_HIDDEN), 0)
    row_p = lax.broadcasted_iota(jnp.int32, (ncp, HEAD_DIM), 0)
    col_p = lax.broadcasted_iota(jnp.int32, (HEAD_DIM, ncp), 1)
    col_f = lax.broadcasted_iota(jnp.int32, (ncp, HEAD_DIM), 1)
    row_o = lax.broadcasted_iota(jnp.int32, (V_ROWS - HEAD_DIM, ncp), 0)
    act_ref[...] = jnp.zeros(act_ref.shape, F32)
    for which, src_ref in enumerate((kc_ref, vc_ref)):
        p = jnp.zeros((m, 2 * NSA_GROUPS * CMP_HIDDEN), F32)
        for tl in range(CMP_STRIDE):
            p = p + jnp.dot(src_ref[0, pl.ds(tl, m, stride=CMP_STRIDE), :].astype(BF16),
                            wbig_ref[which, tl * D_KV:(tl + 1) * D_KV, :], preferred_element_type=F32)
        posc = jnp.dot(pos_ref[which], w1_ref[which], preferred_element_type=F32)[0:1]
        for g in range(NSA_GROUPS):
            top = p[:, (2 * g) * CMP_HIDDEN:(2 * g + 1) * CMP_HIDDEN]
            bot = p[:, (2 * g + 1) * CMP_HIDDEN:(2 * g + 2) * CMP_HIDDEN]
            hid = top + pltpu.roll(bot, m - 1, 0) + (b1_ref[which] + posc)
            act = jnp.where(row_m < nc, _gelu_tanh(hid), 0.0)
            act_ref[CMP_PAD:CMP_PAD + m, :] = act
            ap = act_ref[...].astype(BF16)
            if which == 0:
                kc = jnp.dot(ap, w2k_ref[...], preferred_element_type=F32) + b2k_ref[...]
                ms = jnp.mean(kc * kc, axis=-1, keepdims=True)
                kc = kc * lax.rsqrt(ms + NORM_EPS) * kg_ref[...]
                valid = (row_p >= CMP_PAD) & (row_p < CMP_PAD + nc)
                kcmp_ref[0, g, :, 0:HEAD_DIM] = jnp.where(valid, kc, 0.0).astype(BF16)
                flag = (col_f == 0) & jnp.logical_not(valid)
                kcmp_ref[0, g, :, HEAD_DIM:] = jnp.where(flag, 1.0, 0.0).astype(BF16)
            else:
                vt = lax.dot_general(w2vT_ref[...], ap, (((1,), (1,)), ((), ())),
                                     preferred_element_type=F32) + b2v_ref[...]
                valid = (col_p >= CMP_PAD) & (col_p < CMP_PAD + nc)
                vcmpT_ref[0, g, 0:HEAD_DIM, :] = jnp.where(valid, vt, 0.0).astype(BF16)
                vcmpT_ref[0, g, HEAD_DIM:, :] = jnp.where(row_o == 0, 1.0, 0.0).astype(BF16)


def _compress(kc, vc, wbig, pos8, w1, b1, w2k, w2vT, b2k, b2v, kg, *, ncp):
    B = kc.shape[0]
    full = lambda a: pl.BlockSpec(a.shape, lambda b: (0,) * a.ndim)
    return pl.pallas_call(
        _compress_kernel, grid=(B,),
        in_specs=[pl.BlockSpec((1,) + kc.shape[1:], lambda b: (b, 0, 0)),
                  pl.BlockSpec((1,) + vc.shape[1:], lambda b: (b, 0, 0)),
                  full(wbig), full(pos8), full(w1), full(b1), full(w2k), full(w2vT),
                  full(b2k), full(b2v), full(kg)],
        out_specs=(pl.BlockSpec((1, NSA_GROUPS, ncp, 2 * HEAD_DIM), lambda b: (b, 0, 0, 0)),
                   pl.BlockSpec((1, NSA_GROUPS, V_ROWS, ncp), lambda b: (b, 0, 0, 0))),
        out_shape=(jax.ShapeDtypeStruct((B, NSA_GROUPS, ncp, 2 * HEAD_DIM), BF16),
                   jax.ShapeDtypeStruct((B, NSA_GROUPS, V_ROWS, ncp), BF16)),
        scratch_shapes=[pltpu.VMEM((ncp, CMP_HIDDEN), F32)],
        compiler_params=_cparams(("parallel",)), name="compress",
    )(kc, vc, wbig, pos8, w1, b1, w2k, w2vT, b2k, b2v, kg)


def _flash_update(s_ref, smax_ref, vT, m_ref, acc_ref):
    m_prev = m_ref[...]
    m_new = jnp.maximum(m_prev, smax_ref[...])
    p = jnp.exp2(s_ref[...] - m_new)
    acc_ref[...] = (jnp.exp2(m_prev - m_new) * acc_ref[...]
                    + jnp.dot(vT, p.astype(BF16), preferred_element_type=F32))
    m_ref[...] = m_new


def _attn_kernel(qT_ref, kcmp_ref, vcmpT_ref, bc_ref, ksel_ref, vselT_ref, kwin_ref, vwinT_ref,
                 stab_ref, wtab_ref, gT_ref, o_ref,
                 lc_ref, psum_ref, qaug_ref, qfar_ref, oc_ref, ow_ref, ms_ref, accs_ref, sa_ref, sb_ref, ma_ref, mb_ref):
    qt = pl.program_id(2)
    ncp = kcmp_ref.shape[2]
    ns = qaug_ref.shape[0] - 2 * HEAD_DIM
    nq = NSA_HPG * QT
    t_lane = qt * QT + lax.broadcasted_iota(jnp.int32, (1, QT), 1)

    for r in range(NSA_HPG):
        qaug_ref[0:HEAD_DIM, r * QT:(r + 1) * QT] = qT_ref[0, r * HEAD_DIM:(r + 1) * HEAD_DIM, :]
    flag_row = lax.broadcasted_iota(jnp.int32, (HEAD_DIM, nq), 0) == 0
    qaug_ref[HEAD_DIM:2 * HEAD_DIM, :] = jnp.where(flag_row, NEG_INF, 0.0).astype(BF16)
    qk = qaug_ref[0:2 * HEAD_DIM, :]

    rho = lax.broadcasted_iota(jnp.int32, (ncp, 1), 0)
    band0 = pl.multiple_of(qt * (QT // CMP_STRIDE), V7X_SUBLANES)
    lc_ref[...] = jnp.dot(kcmp_ref[0, 0], qk, preferred_element_type=F32)
    lc_ref[pl.ds(band0, CMP_BAND), :] += bc_ref[0]
    lc = lc_ref[...] + jnp.where(rho < band0 + CMP_BAND, 0.0, NEG_INF)
    e = jnp.exp2(lc - jnp.max(lc, axis=0, keepdims=True))
    oc_aug = jnp.dot(vcmpT_ref[0, 0], e.astype(BF16), preferred_element_type=F32)
    t_q = qt * QT + lax.broadcasted_iota(jnp.int32, (1, nq), 1) % QT
    inv_c = jnp.where(t_q >= CMP_BLOCK - 1, 1.0 / oc_aug[HEAD_DIM:HEAD_DIM + 1, :], 0.0)
    oc_ref[...] = oc_aug[0:HEAD_DIM] * inv_c
    p = e * inv_c
    psum = (p[:, 0:QT] + p[:, QT:2 * QT]) + (p[:, 2 * QT:3 * QT] + p[:, 3 * QT:4 * QT])
    for c in range(QT // V7X_LANES):
        psum_ref[c] = psum[:, c * V7X_LANES:(c + 1) * V7X_LANES]

    def strided_sum(c):
        acc = psum_ref[c, pl.ds(CMP_PAD - 1, ns, stride=4), :]
        for k in range(1, 5):
            acc = acc + psum_ref[c, pl.ds(CMP_PAD - 1 + k, ns, stride=4), :]
        return acc

    imp = jnp.concatenate([strided_sum(c) for c in range(QT // V7X_LANES)], axis=1)
    jrow = lax.broadcasted_iota(jnp.int32, (ns, QT), 0)
    cur = t_lane // SEL_BLOCK
    forced = (jrow == 0) | (jrow == cur) | (jrow == cur - 1)
    live = jrow * SEL_BLOCK <= t_lane
    score = jnp.where(live, jnp.where(forced, -jnp.inf, imp), NEG_INF)
    jrow_f = jrow.astype(F32)

    def col_reduce(x, pair, reduce):
        parts = [x[i * V7X_SUBLANES:(i + 1) * V7X_SUBLANES, :] for i in range(x.shape[0] // V7X_SUBLANES)]
        while len(parts) > 1:
            parts = [pair(parts[i], parts[i + 1]) for i in range(0, len(parts) - 1, 2)] + parts[len(parts) & ~1:]
        return reduce(parts[0], axis=0, keepdims=True)

    for _ in range(min(SEL_TOPN, ns) - 3):
        mx = col_reduce(score, jnp.maximum, jnp.max)
        idx = col_reduce(jnp.where(score == mx, jrow_f, float(ns)), jnp.minimum, jnp.min)
        score = jnp.where(jrow_f == idx, -jnp.inf, score)
    sel = live & (score == -jnp.inf)
    negmask = jnp.where(sel, 0.0, NEG_INF).astype(BF16)
    negfar = jnp.where(sel & (jrow < (qt - 1) * (QT // SEL_BLOCK)), 0.0, NEG_INF).astype(BF16)
    qfar_ref[0:2 * HEAD_DIM, :] = qaug_ref[0:2 * HEAD_DIM, :]
    for r in range(NSA_HPG):
        qaug_ref[2 * HEAD_DIM:, r * QT:(r + 1) * QT] = negmask
        qfar_ref[2 * HEAD_DIM:, r * QT:(r + 1) * QT] = negfar

    w0 = pl.multiple_of(qt * QT, QT)
    sw = jnp.dot(kwin_ref[0, 0, pl.ds(w0, WIN_KEYS), :], qk, preferred_element_type=F32) + wtab_ref[0]
    pw = jnp.exp2(sw - jnp.max(sw, axis=0, keepdims=True))
    vw = jnp.concatenate([vwinT_ref[0, 0, qt + j] for j in range(WIN_KEYS // QT)], axis=1)
    ow_aug = jnp.dot(vw, pw.astype(BF16), preferred_element_type=F32)
    ow_ref[...] = ow_aug[0:HEAD_DIM] * (1.0 / ow_aug[HEAD_DIM:HEAD_DIM + 1, :])

    n0 = pl.multiple_of((ATT_PAD // QT - 1 + qt) * QT, QT)
    sn = jnp.dot(ksel_ref[0, 0, pl.ds(n0, NEAR_KEYS), :], qaug_ref[...],
                 preferred_element_type=F32) + stab_ref[0]
    mn = jnp.max(sn, axis=0, keepdims=True)
    pn = jnp.exp2(sn - mn)
    vn = jnp.concatenate([vselT_ref[0, 0, ATT_PAD // QT - 1 + qt + j] for j in range(NEAR_KEYS // QT)], axis=1)
    ms_ref[...] = mn
    accs_ref[...] = jnp.dot(vn, pn.astype(BF16), preferred_element_type=F32)

    tiles_per_far = FAR_KEYS // QT
    n_far = (qt - 1 + tiles_per_far - 1) // tiles_per_far
    last_far = (ksel_ref.shape[2] - ATT_PAD) // FAR_KEYS - 1

    def far_logits(g, dst_ref, dmax_ref):
        r0 = pl.multiple_of(ATT_PAD + jnp.minimum(g, last_far) * FAR_KEYS, FAR_KEYS)
        s = jnp.dot(ksel_ref[0, 0, pl.ds(r0, FAR_KEYS), :], qfar_ref[...], preferred_element_type=F32)
        dst_ref[...] = s
        dmax_ref[...] = jnp.max(s, axis=0, keepdims=True)

    def far_values(g):
        t0 = (ATT_PAD + g * FAR_KEYS) // QT
        return jnp.concatenate([vselT_ref[0, 0, t0 + j] for j in range(tiles_per_far)], axis=1)

    far_logits(0, sa_ref, ma_ref)

    def far_body(j, carry):
        bufs = ((sa_ref, ma_ref), (sb_ref, mb_ref))
        for u in range(FAR_UNROLL):
            g = FAR_UNROLL * j + u
            far_logits(g + 1, *bufs[(u + 1) % 2])
            _flash_update(*bufs[u % 2], far_values(jnp.minimum(g, last_far)), ms_ref, accs_ref)
        return carry

    lax.fori_loop(0, (n_far + FAR_UNROLL - 1) // FAR_UNROLL, far_body, 0)

    o_s = accs_ref[0:HEAD_DIM, :] * (1.0 / accs_ref[HEAD_DIM:HEAD_DIM + 1, :])
    for r in range(NSA_HPG):
        cols = slice(r * QT, (r + 1) * QT)
        o = (gT_ref[0, 3 * r:3 * r + 1, :] * oc_ref[:, cols] + gT_ref[0, 3 * r + 1:3 * r + 2, :] * o_s[:, cols]
             + gT_ref[0, 3 * r + 2:3 * r + 3, :] * ow_ref[:, cols])
        o_ref[0, :, r * HEAD_DIM:(r + 1) * HEAD_DIM] = o.T


def _attn(qT, kcmp, vcmpT, bc, ksel, vT5, kwin, stab, wtab, gT):
    B, _, T = qT.shape
    ncp = kcmp.shape[2]
    ns = T // SEL_BLOCK
    nq = NSA_HPG * QT
    tp = T + ATT_PAD
    nt = tp // QT
    grid = (B, NSA_GROUPS, T // QT)

    def held(shape, index_map):
        return pl.BlockSpec(shape, index_map, pipeline_mode=pl.Buffered(1))

    in_specs = [
        pl.BlockSpec((1, NSA_HPG * HEAD_DIM, QT), lambda b, g, q: (b, g, q)),
        held((1, 1, ncp, 2 * HEAD_DIM), lambda b, g, q: (b, g, 0, 0)),
        held((1, 1, V_ROWS, ncp), lambda b, g, q: (b, g, 0, 0)),
        held((1, CMP_BAND, nq), lambda b, g, q: (g, 0, 0)),
        pl.BlockSpec((1, 1, tp, 2 * HEAD_DIM + ns), lambda b, g, q: (b, g, 0, 0)),
        pl.BlockSpec((1, 1, nt, V_ROWS, QT), lambda b, g, q: (b, g, 0, 0, 0)),
        pl.BlockSpec((1, 1, tp, 2 * HEAD_DIM), lambda b, g, q: (b, g, 0, 0)),
        pl.BlockSpec((1, 1, nt, V_ROWS, QT), lambda b, g, q: (b, NSA_GROUPS + g, 0, 0, 0)),
        held((1, NEAR_KEYS, nq), lambda b, g, q: (g, 0, 0)),
        held((1, WIN_KEYS, nq), lambda b, g, q: (g, 0, 0)),
        pl.BlockSpec((1, GATE_ROWS, QT), lambda b, g, q: (b, g, q)),
    ]
    scratch = [
        pltpu.VMEM((ncp, nq), F32),
        pltpu.VMEM((QT // V7X_LANES, ncp, V7X_LANES), F32),
        pltpu.VMEM((2 * HEAD_DIM + ns, nq), BF16),
        pltpu.VMEM((2 * HEAD_DIM + ns, nq), BF16),
        pltpu.VMEM((HEAD_DIM, nq), F32),
        pltpu.VMEM((HEAD_DIM, nq), F32),
        pltpu.VMEM((1, nq), F32),
        pltpu.VMEM((V_ROWS, nq), F32),
        pltpu.VMEM((FAR_KEYS, nq), F32),
        pltpu.VMEM((FAR_KEYS, nq), F32),
        pltpu.VMEM((1, nq), F32), pltpu.VMEM((1, nq), F32),
    ]
    return pl.pallas_call(
        _attn_kernel, grid=grid, in_specs=in_specs,
        out_specs=pl.BlockSpec((1, QT, NSA_HPG * HEAD_DIM), lambda b, g, q: (b, q, g)),
        out_shape=jax.ShapeDtypeStruct((B, T, D_NSA), F32),
        scratch_shapes=scratch,
        compiler_params=_cparams(("parallel", "parallel", "arbitrary")), name="attn",
    )(qT, kcmp, vcmpT, bc, ksel, vT5, kwin, vT5, stab, wtab, gT)


def _rwkv_tokens(rw_ref, halo_ref, mu_ref, w0_ref, w2_ref, a0_ref, a2_ref, g2_ref, kk_ref, ka_ref,
                 rk_ref, bd_ref, ext_ref):
    tm = rw_ref.shape[1]
    first_tile = pl.program_id(1) == 0
    ext_ref[0:V7X_SUBLANES, :] = jnp.where(first_tile, 0.0, halo_ref[0])
    ext_ref[V7X_SUBLANES:, :] = rw_ref[0]
    cur = rw_ref[0]
    prev = ext_ref[pl.ds(V7X_SUBLANES - 1, tm), :]
    mixed = cur + (prev - cur) * mu_ref[...]
    c = D_RWKV
    r = mixed[:, 0:c]
    k = mixed[:, c:2 * c]
    v = mixed[:, 2 * c:3 * c]
    xw = mixed[:, 3 * c:3 * c + LORA_W]
    xa = mixed[:, 3 * c + LORA_W:3 * c + LORA_W + LORA_A]
    xg = mixed[:, 3 * c + LORA_W + LORA_A:]

    z = -(w0_ref[...] + _bdot(jnp.tanh(xw), w2_ref[...]))
    softplus = jnp.maximum(z, 0.0) + jnp.log1p(jnp.exp(-jnp.abs(z)))
    w = -softplus - 0.5
    a = jax.nn.sigmoid(a0_ref[...] + _bdot(xa, a2_ref[...]))
    g = _bdot(jax.nn.sigmoid(xg), g2_ref[...])

    kk = k * kk_ref[...]
    n2 = _head_sum(kk * kk, bd_ref[...])
    kkn = kk / jnp.maximum(jnp.sqrt(n2), 1e-12)
    k2 = k * (1.0 + (a - 1.0) * ka_ref[...])
    bonus = _head_sum(r * k2 * rk_ref[...], bd_ref[...]) * v

    lw = -jnp.exp(w)
    return dict(r=r, lw=lw, k=k2, v=v, kk=kkn, b=kkn * a, g=g, bonus=bonus)


WKV_BATCH = 8
WKV_GROUP = 4
WKV_GW = WKV_GROUP * HEAD_DIM


def _wkv_kernel(rw_ref, halo_ref, mu_ref, w0_ref, w2_ref, a0_ref, a2_ref, g2_ref, kk_ref, ka_ref, rk_ref, bd_ref,
                y_ref, g_ref, bonus_ref, s_ref, ext_ref, *, chunks):
    L = WKV_CHUNK
    tok = _rwkv_tokens(rw_ref, halo_ref, mu_ref, w0_ref, w2_ref, a0_ref, a2_ref, g2_ref, kk_ref, ka_ref,
                       rk_ref, bd_ref, ext_ref)
    g_ref[0] = tok["g"]
    bonus_ref[0] = tok["bonus"]
    assert L == HEAD_DIM
    gw = WKV_GW
    n_groups = RWKV_HEADS // WKV_GROUP

    @pl.when(pl.program_id(1) == 0)
    def _():
        s_ref[...] = jnp.zeros(s_ref.shape, F32)

    ti = lax.broadcasted_iota(jnp.int32, (L, L), 0)
    tj = lax.broadcasted_iota(jnp.int32, (L, L), 1)
    tri = jnp.where(ti >= tj, 1.0, 0.0).astype(BF16)
    row = lax.broadcasted_iota(jnp.int32, (L, gw), 0)
    col = lax.broadcasted_iota(jnp.int32, (L, gw), 1) % L
    low_strict = col < row
    low_incl = col <= row
    eye_sbs = jnp.where(col == row, 1.0, 0.0)
    brow = lax.broadcasted_iota(jnp.int32, (gw, gw), 0) // L
    bcol = lax.broadcasted_iota(jnp.int32, (gw, gw), 1) // HEAD_DIM
    same_head = brow == bcol

    def bd_rows(x):
        xb = x.astype(BF16)
        return jnp.where(same_head, jnp.concatenate([xb] * WKV_GROUP, axis=0), jnp.zeros((), BF16))

    def mm(a, b_bf16):
        return jnp.dot(a.astype(BF16), b_bf16, preferred_element_type=F32)

    def tn(a, b):
        return lax.dot_general(a.astype(BF16), b.astype(BF16), (((0,), (0,)), ((), ())),
                               preferred_element_type=F32)

    pre, w2, c2, m_lr, d_sbs = {}, {}, {}, {}, {}

    def grp(c, g, name):
        return pre[c][name][:, g * gw:(g + 1) * gw]

    for c0 in range(0, chunks, WKV_BATCH):
        batch = range(c0, min(c0 + WKV_BATCH, chunks))
        inst = [(c, g) for c in batch for g in range(n_groups)]
        for c in batch:
            rows = slice(c * L, (c + 1) * L)
            lw = tok["lw"][rows]
            cs = _dot_exact_rhs_left(tri, lw)
            c_last = cs[L - 1:L, :]
            e_nc = jnp.exp(-cs)
            e_lc = jnp.exp(c_last - cs)
            kk = tok["kk"][rows]
            b = tok["b"][rows]
            k = tok["k"][rows]
            pre[c] = dict(a=-kk * jnp.exp(cs - lw), r=tok["r"][rows] * jnp.exp(cs), bh=b * e_nc, kh=k * e_nc,
                          be=b * e_lc, ke=k * e_lc, v=tok["v"][rows], e_last=jnp.exp(c_last))

        t_all = {}
        for (c, g) in inst:
            lhs = jnp.concatenate([grp(c, g, "a"), grp(c, g, "r")], axis=0).astype(BF16)
            rhs = jnp.concatenate([bd_rows(grp(c, g, "bh")), bd_rows(grp(c, g, "kh"))], axis=0)
            t_all[c, g] = lax.dot_general(lhs, rhs, (((1,), (1,)), ((), ())), preferred_element_type=F32)
        n_m = {i: jnp.where(low_strict, t_all[i][0:L, 0:gw], 0.0) for i in inst}
        tak = {i: jnp.where(low_strict, t_all[i][0:L, gw:], 0.0) for i in inst}
        trb = {i: jnp.where(low_incl, t_all[i][L:, 0:gw], 0.0) for i in inst}
        trk = {i: jnp.where(low_incl, t_all[i][L:, gw:], 0.0) for i in inst}
        z = {i: eye_sbs + n_m[i] for i in inst}
        pw = dict(n_m)
        for _ in range(int(math.log2(L)) - 1):
            pw = {i: mm(pw[i], bd_rows(pw[i])) for i in inst}
            z = {i: z[i] + mm(z[i], bd_rows(pw[i])) for i in inst}
        vbd = {(c, g): bd_rows(grp(c, g, "v")) for (c, g) in inst}
        w1 = {(c, g): mm(z[c, g], bd_rows(grp(c, g, "a"))) for (c, g) in inst}
        tv = {i: mm(tak[i], vbd[i]) for i in inst}
        c1 = {i: mm(z[i], bd_rows(tv[i])) for i in inst}
        w2.update({(c, g): grp(c, g, "r") + mm(trb[c, g], bd_rows(w1[c, g])) for (c, g) in inst})
        c2.update({i: mm(trb[i], bd_rows(c1[i])) + mm(trk[i], vbd[i]) for i in inst})
        m_lr.update({(c, g): jnp.where(same_head, tn(w1[c, g], grp(c, g, "be")), 0.0).astype(BF16)
                     for (c, g) in inst})
        for (c, g) in inst:
            full = jnp.where(same_head,
                             tn(jnp.concatenate([c1[c, g], grp(c, g, "v")], axis=0),
                                jnp.concatenate([grp(c, g, "be"), grp(c, g, "ke")], axis=0)), 0.0)
            d_sbs[c, g] = ((full[0:L] + full[L:2 * L]) + (full[2 * L:3 * L] + full[3 * L:4 * L]))

    for g in range(n_groups):
        s = s_ref[:, g * gw:(g + 1) * gw]
        for c in range(chunks):
            g_bd = jnp.where(same_head, jnp.concatenate([s.T.astype(BF16)] * WKV_GROUP, axis=1),
                             jnp.zeros((), BF16))
            y_ref[0, c * L:(c + 1) * L, g * gw:(g + 1) * gw] = mm(w2[c, g], g_bd) + c2[c, g]
            s = s * grp(c, g, "e_last") + mm(s, m_lr[c, g]) + d_sbs[c, g]
        s_ref[:, g * gw:(g + 1) * gw] = s


def _dot_exact_rhs_left(m_bf16, x):
    hi, mid, lo = _split3(x)
    return (jnp.dot(m_bf16, hi, preferred_element_type=F32)
            + jnp.dot(m_bf16, mid, preferred_element_type=F32)
            + jnp.dot(m_bf16, lo, preferred_element_type=F32))


def _wkv(rw, mu, w0, w2, a0, a2, g2, k_k, k_a, r_k, bd, *, chunks):
    B, T, C = rw.shape
    rows = chunks * WKV_CHUNK
    hb = rows // V7X_SUBLANES
    full = lambda a: pl.BlockSpec(a.shape, lambda bb, c: (0,) * a.ndim)
    tok = pl.BlockSpec((1, rows, D_RWKV), lambda bb, c: (bb, c, 0))
    return pl.pallas_call(
        functools.partial(_wkv_kernel, chunks=chunks), grid=(B, T // rows),
        in_specs=[pl.BlockSpec((1, rows, C), lambda bb, c: (bb, c, 0)),
                  pl.BlockSpec((1, V7X_SUBLANES, C), lambda bb, c: (bb, jnp.maximum(c * hb - 1, 0), 0)),
                  full(mu), full(w0), full(w2), full(a0), full(a2), full(g2), full(k_k), full(k_a),
                  full(r_k), full(bd)],
        out_specs=(tok,) * 3,
        out_shape=(jax.ShapeDtypeStruct((B, T, D_RWKV), F32),) * 3,
        scratch_shapes=[pltpu.VMEM((HEAD_DIM, D_RWKV), F32), pltpu.VMEM((rows + V7X_SUBLANES, C), F32)],
        compiler_params=_cparams(("parallel", "arbitrary")), name="wkv",
    )(rw, rw, mu, w0, w2, a0, a2, g2, k_k, k_a, r_k, bd)


def _mix_rows(x, on, y, bonus, g, bd_ref, lnw_ref, lnb_ref, wo_ref, g2_ref):
    mu = _dot_exact_rhs(y, bd_ref[...]) * (1.0 / HEAD_DIM)
    yc = y - mu
    var = _head_sum(yc * yc, bd_ref[...]) * (1.0 / HEAD_DIM)
    yn = yc * lax.rsqrt(var + GN_EPS) * lnw_ref[...] + lnb_ref[...]
    orw = (yn + bonus) * g
    x1 = x + _bdot(on, wo_ref[0:D_NSA, :]) + _bdot(orw, wo_ref[D_NSA:, :])
    ms = jnp.mean(x1 * x1, axis=-1, keepdims=True)
    return x1, (x1 * lax.rsqrt(ms + NORM_EPS) * g2_ref[...]).astype(BF16)


FFN_HALO = 16


def _ffn_kernel(x_ref, on_ref, y_ref, bonus_ref, g_ref, xh_ref, onh_ref, yh_ref, bonush_ref, gh_ref,
                bd_ref, lnw_ref, lnb_ref, wo_ref, g2_ref,
                wv_ref, wg_ref, cwv_ref, cwg_ref, cbv_ref, cbg_ref, wd_ref,
                o_ref, hext_ref, extv_ref, extg_ref, act_ref, x1_ref, *, tiles_per_seq, ft):
    tm = x_ref.shape[0]
    dff = wd_ref.shape[0]
    seq_start = (pl.program_id(0) % tiles_per_seq) == 0
    ext = lambda h_ref, m_ref: jnp.concatenate([h_ref[...], m_ref[...]], axis=0)
    x1, h2 = _mix_rows(ext(xh_ref, x_ref), ext(onh_ref, on_ref), ext(yh_ref, y_ref), ext(bonush_ref, bonus_ref),
                       ext(gh_ref, g_ref), bd_ref, lnw_ref, lnb_ref, wo_ref, g2_ref)
    x1_ref[...] = x1[FFN_HALO:]
    row = lax.broadcasted_iota(jnp.int32, h2.shape, 0)
    hext_ref[...] = jnp.where(seq_start & (row < FFN_HALO), jnp.zeros((), BF16), h2)

    for j in range(dff // ft):
        cols = slice(j * ft, (j + 1) * ft)

        def conv_branch(w_ref, cw_ref, cb_ref, ext_ref):
            ext_ref[j % 2] = jnp.dot(hext_ref[...], w_ref[:, cols], preferred_element_type=F32)
            out = cb_ref[:, cols] + ext_ref[j % 2, pl.ds(FFN_HALO, tm), :] * cw_ref[CONV_W - 1:CONV_W, cols]
            for i in range(CONV_W - 1):
                back = CONV_W - 1 - i
                out = out + ext_ref[j % 2, pl.ds(FFN_HALO - back, tm), :] * cw_ref[i:i + 1, cols]
            return out

        u_val = conv_branch(wv_ref, cwv_ref, cbv_ref, extv_ref)
        u_gate = conv_branch(wg_ref, cwg_ref, cbg_ref, extg_ref)
        act_ref[:, cols] = ((u_gate * jax.nn.sigmoid(u_gate)) * u_val).astype(BF16)
    o_ref[...] = x1_ref[...] + jnp.dot(act_ref[...], wd_ref[...], preferred_element_type=F32)


def _ffn(x, on, y, bonus, g, bd, lnw, lnb, wo, g2, wv, wg, cwv, cwg, cbv, cbg, wd, *, tm, ft, seq_len):
    n, d = x.shape
    dff = wv.shape[1]
    hb = tm // FFN_HALO
    kern = functools.partial(_ffn_kernel, tiles_per_seq=seq_len // tm, ft=ft)
    resident = lambda a: pl.BlockSpec(a.shape, lambda i: (0,) * a.ndim, pipeline_mode=pl.Buffered(1))
    tile = lambda a: pl.BlockSpec((tm, a.shape[1]), lambda i: (i, 0))
    halo = lambda a: pl.BlockSpec((FFN_HALO, a.shape[1]), lambda i: (jnp.maximum(i * hb - 1, 0), 0))
    tokens = (x, on, y, bonus, g)
    params = (bd, lnw, lnb, wo, g2, wv, wg, cwv, cwg, cbv, cbg, wd)
    return pl.pallas_call(
        kern, grid=(n // tm,),
        in_specs=[tile(a) for a in tokens] + [halo(a) for a in tokens] + [resident(a) for a in params],
        out_specs=pl.BlockSpec((tm, d), lambda i: (i, 0)),
        out_shape=jax.ShapeDtypeStruct((n, d), F32),
        scratch_shapes=[pltpu.VMEM((tm + FFN_HALO, d), BF16),
                        pltpu.VMEM((2, tm + FFN_HALO, ft), F32), pltpu.VMEM((2, tm + FFN_HALO, ft), F32),
                        pltpu.VMEM((tm, dff), BF16), pltpu.VMEM((tm, d), F32)],
        compiler_params=_cparams(("parallel",)), name="ffn",
    )(*tokens, *tokens, *params)


def _t5_bucket(dist):
    n = np.maximum(dist, 0)
    max_exact = N_BUCKETS // 2
    nf = np.maximum(n, 1).astype(np.float32)
    large = max_exact + (np.log(nf / np.float32(max_exact)) / np.float32(math.log(MAX_DISTANCE / max_exact))
                         * np.float32(N_BUCKETS - max_exact)).astype(np.int32)
    large = np.minimum(large, N_BUCKETS - 1)
    return np.where(n < max_exact, n, large)


def _toeplitz_kernel(*refs, steps):
    n = len(steps)
    for v_ref, o_ref, step in zip(refs[:n], refs[n:], steps):
        rows, width = o_ref.shape[1], o_ref.shape[2]
        x = jnp.broadcast_to(v_ref[0], (rows, v_ref.shape[2]))
        o_ref[0] = pltpu.roll(x, 0, 1, stride=step, stride_axis=0)[:, :width]


def _toeplitz(specs):
    return pl.pallas_call(
        functools.partial(_toeplitz_kernel, steps=tuple(step for _, _, step in specs)), grid=(NSA_GROUPS, NSA_HPG),
        in_specs=[pl.BlockSpec((1, 1, v.shape[2]), lambda g, r: (g * NSA_HPG + r, 0, 0)) for v, _, _ in specs],
        out_specs=tuple(pl.BlockSpec((1, rows, QT), lambda g, r: (g, 0, r)) for _, rows, _ in specs),
        out_shape=tuple(jax.ShapeDtypeStruct((NSA_GROUPS, rows, NSA_HPG * QT), F32) for _, rows, _ in specs),
        compiler_params=_cparams(("parallel", "parallel")), name="toeplitz",
    )(*[v for v, _, _ in specs])


def _bias_tables(rel_bias):
    rel = (rel_bias - rel_bias[N_BUCKETS - 1][None, :]) * LOG2E

    def table(rows, step, d0, d_hi):
        period = step * rows + QT
        i = np.arange(period)
        d = d0 + np.where(i < QT, i, i - period)
        onehot = (_t5_bucket(d)[:, None] == np.arange(N_BUCKETS)[None, :]).astype(np.float32)
        v = jnp.dot(jnp.asarray(onehot), rel, precision=lax.Precision.HIGHEST)
        v = jnp.where(jnp.asarray((d >= 0) & (d < d_hi))[:, None], v, NEG_INF).T
        return v.reshape(NSA_HEADS, 1, period), rows, step

    no_limit = 1 << 30
    stab, wtab, bc = _toeplitz([table(NEAR_KEYS, 1, QT, no_limit), table(WIN_KEYS, 1, WINDOW, WINDOW),
                                table(CMP_BAND, CMP_STRIDE, CMP_STRIDE * CMP_PAD - CMP_BLOCK + 1, no_limit)])
    return stab, wtab, bc


def _compress_weights(w1):
    half = CMP_BLOCK // 2
    rows = half * HEAD_DIM
    w3 = jnp.concatenate([w1[:rows], w1[rows:]], axis=1).reshape(half, HEAD_DIM, 2 * CMP_HIDDEN)
    z = jnp.zeros_like(w3)
    assert NSA_GROUPS == 2
    big = jnp.concatenate([jnp.concatenate([w3, z], axis=2), jnp.concatenate([z, w3], axis=2)], axis=1)
    return big.reshape(half * NSA_GROUPS * HEAD_DIM, NSA_GROUPS * 2 * CMP_HIDDEN)


def kernel(x, norm1_g, w_in, q_norm_g, k_norm_g, cmp_pos, cmp_w1, cmp_b1, cmp_w2, cmp_b2, rel_bias, rwkv_mu,
           w0, w2, a0, a2, g2, k_k, k_a, r_k, ln_x_w, ln_x_b, w_out, norm2_g, ffn_up, conv_w, conv_b, ffn_down):
    B, T, D = x.shape
    depth = w_in.shape[0]
    d_ff = ffn_down.shape[1]
    assert T % 2048 == 0 and D_NSA + 6 * D_KV + 3 * NSA_HEADS + D_RWKV_IN == w_in.shape[2]
    ncp = T // CMP_STRIDE + V7X_LANES
    stab, wtab, bc = _bias_tables(rel_bias)
    ii = jnp.arange(D_RWKV)
    bd = (ii[:, None] // HEAD_DIM == ii[None, :] // HEAD_DIM).astype(BF16)

    for l in range(depth):
        wi = w_in[l]
        o = D_NSA
        q_w, kc_w, vc_w, ksl_w, vsl_w, kwn_w, vwn_w = (
            wi[:, 0:o], wi[:, o:o + D_KV], wi[:, o + D_KV:o + 2 * D_KV], wi[:, o + 2 * D_KV:o + 3 * D_KV],
            wi[:, o + 3 * D_KV:o + 4 * D_KV], wi[:, o + 4 * D_KV:o + 5 * D_KV], wi[:, o + 5 * D_KV:o + 6 * D_KV])
        gl_w = wi[:, o + 6 * D_KV:o + 6 * D_KV + 3 * NSA_HEADS]
        rw_w = wi[:, o + 6 * D_KV + 3 * NSA_HEADS:]
        wstd = jnp.concatenate([kc_w, vc_w, ksl_w, kwn_w, rw_w], axis=1).astype(BF16)
        gl_rows = gl_w.T.reshape(NSA_GROUPS, 3 * NSA_HPG, D)
        gl_rows = jnp.pad(gl_rows, ((0, 0), (0, GATE_ROWS - 3 * NSA_HPG), (0, 0))).reshape(-1, D)
        wt = jnp.concatenate([q_w.T, vsl_w.T, vwn_w.T, gl_rows], axis=0).astype(BF16)
        qg = q_norm_g[l].reshape(HEAD_DIM, 1)
        kg = jnp.stack([jnp.tile(k_norm_g[l, 1], NSA_GROUPS), jnp.tile(k_norm_g[l, 2], NSA_GROUPS)])

        qT, kc, vc, ksel, kwin, vT5, gT, rw = _proj(x, norm1_g[l].reshape(1, D), wstd, wt, qg, kg, tm=512)

        wbig = jnp.stack([_compress_weights(cmp_w1[l, 0]), _compress_weights(cmp_w1[l, 1])]).astype(BF16)
        pos8 = jnp.pad(cmp_pos[l].reshape(2, 1, CMP_BLOCK * HEAD_DIM),
                       ((0, 0), (0, V7X_SUBLANES - 1), (0, 0))).astype(BF16)
        kcmp, vcmpT = _compress(
            kc, vc, wbig, pos8,
            cmp_w1[l].astype(BF16), cmp_b1[l].reshape(2, 1, CMP_HIDDEN), cmp_w2[l, 0].astype(BF16),
            cmp_w2[l, 1].T.astype(BF16), cmp_b2[l, 0].reshape(1, HEAD_DIM), cmp_b2[l, 1].reshape(HEAD_DIM, 1),
            k_norm_g[l, 0].reshape(1, HEAD_DIM), ncp=ncp)

        o_nsa = _attn(qT, kcmp, vcmpT, bc, ksel, vT5, kwin, stab, wtab, gT)

        row = lambda a: a.reshape(1, -1)
        y, g, bonus = _wkv(
            rw, row(rwkv_mu[l]), row(w0[l]), w2[l].astype(BF16), row(a0[l]), a2[l].astype(BF16),
            g2[l].astype(BF16), row(k_k[l]), row(k_a[l]), row(r_k[l]), bd, chunks=8)

        n = B * T
        up = ffn_up[l].astype(BF16)
        x = _ffn(x.reshape(n, D), o_nsa.reshape(n, D_NSA), y.reshape(n, D_RWKV), bonus.reshape(n, D_RWKV),
                 g.reshape(n, D_RWKV), bd, row(ln_x_w[l]), row(ln_x_b[l]), w_out[l].astype(BF16),
                 norm2_g[l].reshape(1, D), up[:, :d_ff], up[:, d_ff:], conv_w[l][:, :d_ff], conv_w[l][:, d_ff:],
                 conv_b[l][:d_ff].reshape(1, -1), conv_b[l][d_ff:].reshape(1, -1),
                 ffn_down[l].astype(BF16), tm=512, ft=256, seq_len=T).reshape(B, T, D)
    return x
```

```python
import functools
import math

import jax
import jax.numpy as jnp
import numpy as np
from jax import lax
from jax.experimental import pallas as pl
from jax.experimental.pallas import tpu as pltpu

F32 = jnp.float32
BF16 = jnp.bfloat16

V7X_LANES = 128
V7X_SUBLANES = 8
V7X_VMEM_LIMIT_BYTES = 56 * 1024 * 1024

HEAD_DIM = 64
NSA_HEADS = 8
NSA_GROUPS = 2
NSA_HPG = NSA_HEADS // NSA_GROUPS
RWKV_HEADS = 8
D_NSA = NSA_HEADS * HEAD_DIM
D_RWKV = RWKV_HEADS * HEAD_DIM
D_KV = NSA_GROUPS * HEAD_DIM
CMP_BLOCK = 32
CMP_STRIDE = 16
CMP_HIDDEN = 128
SEL_BLOCK = 64
SEL_TOPN = 16
WINDOW = 512
N_BUCKETS = 32
MAX_DISTANCE = 128
LORA_W = 64
LORA_A = 64
LORA_G = 128
D_RWKV_IN = 3 * D_RWKV + LORA_W + LORA_A + LORA_G
CONV_W = 3
NORM_EPS = 1e-6
GN_EPS = 64e-5
NEG_INF = -1e30
FORCE_SCORE = 1e9

QT = 256
CMP_PAD = 8
CMP_BAND = QT // CMP_STRIDE + 8
LOG2E = math.log2(math.e)
Q_SCALE = HEAD_DIM ** -0.5 * LOG2E
V_ROWS = HEAD_DIM + 16
ATT_PAD = WINDOW
NEAR_KEYS = 2 * QT
FAR_KEYS = 2 * QT
FAR_UNROLL = 2
WIN_KEYS = WINDOW + QT
WKV_CHUNK = 64
GATE_ROWS = 16
N_T_ROWS = D_NSA + 2 * D_KV + NSA_GROUPS * GATE_ROWS
N_STD_COLS = 4 * D_KV + D_RWKV_IN


def _cparams(sem, fuse_inputs=None):
    return pltpu.CompilerParams(dimension_semantics=sem, vmem_limit_bytes=V7X_VMEM_LIMIT_BYTES,
                                allow_input_fusion=fuse_inputs)


def _bdot(a, b):
    return jnp.dot(a.astype(BF16), b.astype(BF16), preferred_element_type=F32)


def _split3(x):
    hi = x.astype(BF16)
    r1 = x - hi.astype(F32)
    mid = r1.astype(BF16)
    lo = (r1 - mid.astype(F32)).astype(BF16)
    return hi, mid, lo


def _dot_exact_rhs(x, m_bf16):
    hi = x.astype(BF16)
    lo = (x - hi.astype(F32)).astype(BF16)
    return jnp.dot(hi, m_bf16, preferred_element_type=F32) + jnp.dot(lo, m_bf16, preferred_element_type=F32)


def _head_sum(x, m_bf16):
    return jnp.dot(x.astype(BF16), m_bf16, preferred_element_type=F32)


def _proj_kernel(x_ref, g1_ref, wstd_ref, wt_ref, qg_ref, kg_ref,
                 qT_ref, kc_ref, vc_ref, ksel_ref, kwin_ref, vT_ref, gT_ref, rw_ref):
    step = pl.program_id(1)

    @pl.when(step == 0)
    def _():
        for k_ref in (ksel_ref, kwin_ref):
            lanes = lax.broadcasted_iota(jnp.int32, k_ref.shape[2:], 1)
            flagged = jnp.where(lanes == HEAD_DIM, 1.0, 0.0).astype(BF16)
            for g in range(NSA_GROUPS):
                k_ref[0, g] = flagged
        vT_ref[...] = jnp.zeros(vT_ref.shape, BF16)

    @pl.when(step > 0)
    def _():
        _proj_tile(x_ref, g1_ref, wstd_ref, wt_ref, qg_ref, kg_ref,
                   qT_ref, kc_ref, vc_ref, ksel_ref, kwin_ref, vT_ref, gT_ref, rw_ref, step - 1)


def _proj_tile(x_ref, g1_ref, wstd_ref, wt_ref, qg_ref, kg_ref,
               qT_ref, kc_ref, vc_ref, ksel_ref, kwin_ref, vT_ref, gT_ref, rw_ref, tile):
    tm = x_ref.shape[1]
    ns = ksel_ref.shape[3] - 2 * HEAD_DIM
    x = x_ref[0]
    ms = jnp.mean(x * x, axis=-1, keepdims=True)
    h = (x * lax.rsqrt(ms + NORM_EPS) * g1_ref[...]).astype(BF16)

    std = jnp.dot(h, wstd_ref[...], preferred_element_type=F32)
    kc_ref[0] = std[:, 0:D_KV]
    vc_ref[0] = std[:, D_KV:2 * D_KV]
    rw_ref[0] = std[:, 4 * D_KV:]

    lane = lax.broadcasted_iota(jnp.int32, (tm, D_KV), 1)
    first = lane < HEAD_DIM

    def group_rmsnorm(k, gain):
        k2 = k * k
        s0 = jnp.sum(jnp.where(first, k2, 0.0), axis=-1, keepdims=True)
        s1 = jnp.sum(jnp.where(first, 0.0, k2), axis=-1, keepdims=True)
        ms_g = jnp.where(first, s0, s1) * (1.0 / HEAD_DIM)
        return k * lax.rsqrt(ms_g + NORM_EPS) * gain

    ksl = group_rmsnorm(std[:, 2 * D_KV:3 * D_KV], kg_ref[0:1, :]).astype(BF16)
    kwn = group_rmsnorm(std[:, 3 * D_KV:4 * D_KV], kg_ref[1:2, :]).astype(BF16)

    tok = tile * tm + lax.broadcasted_iota(jnp.int32, (tm, ns), 0)
    blk = lax.broadcasted_iota(jnp.int32, (tm, ns), 1)
    onehot = jnp.where((tok // SEL_BLOCK) == blk, 1.0, 0.0).astype(BF16)
    zeros = jnp.zeros((tm, HEAD_DIM), BF16)
    for g in range(NSA_GROUPS):
        ksel_ref[0, g, :, 0:HEAD_DIM] = ksl[:, g * HEAD_DIM:(g + 1) * HEAD_DIM]
        ksel_ref[0, g, :, HEAD_DIM:2 * HEAD_DIM] = zeros
        ksel_ref[0, g, :, 2 * HEAD_DIM:] = onehot
        kwin_ref[0, g, :, 0:HEAD_DIM] = kwn[:, g * HEAD_DIM:(g + 1) * HEAD_DIM]
        kwin_ref[0, g, :, HEAD_DIM:] = zeros

    tr = lax.dot_general(wt_ref[...], h, (((1,), (1,)), ((), ())),
                         preferred_element_type=F32)
    q = tr[0:D_NSA].reshape(NSA_HEADS, HEAD_DIM, tm)
    qms = jnp.mean(q * q, axis=1, keepdims=True)
    qn = q * lax.rsqrt(qms + NORM_EPS) * qg_ref[...][None] * Q_SCALE
    qT_ref[0] = qn.reshape(D_NSA, tm).astype(BF16)
    vt = tr[D_NSA:D_NSA + 2 * D_KV].astype(BF16)
    ones_rows = jnp.where(lax.broadcasted_iota(jnp.int32, (V_ROWS - HEAD_DIM, QT), 0) == 0, 1.0, 0.0).astype(BF16)
    for a in range(2 * NSA_GROUPS):
        for j in range(tm // QT):
            vT_ref[0, a, j, 0:HEAD_DIM, :] = vt[a * HEAD_DIM:(a + 1) * HEAD_DIM, j * QT:(j + 1) * QT]
            vT_ref[0, a, j, HEAD_DIM:, :] = ones_rows
    gT_ref[0] = jax.nn.sigmoid(tr[D_NSA + 2 * D_KV:])


def _proj(x, g1, wstd, wt, qg, kg, *, tm):
    B, T, D = x.shape
    ns = T // SEL_BLOCK
    assert tm == ATT_PAD
    tp = T + ATT_PAD
    grid = (B, T // tm + 1)
    const2 = lambda b, i: (0, 0)
    tile = lambda i: jnp.maximum(i - 1, 0)
    out_shape = (
        jax.ShapeDtypeStruct((B, D_NSA, T), BF16),
        jax.ShapeDtypeStruct((B, T, D_KV), F32),
        jax.ShapeDtypeStruct((B, T, D_KV), F32),
        jax.ShapeDtypeStruct((B, NSA_GROUPS, tp, 2 * HEAD_DIM + ns), BF16),
        jax.ShapeDtypeStruct((B, NSA_GROUPS, tp, 2 * HEAD_DIM), BF16),
        jax.ShapeDtypeStruct((B, 2 * NSA_GROUPS, tp // QT, V_ROWS, QT), BF16),
        jax.ShapeDtypeStruct((B, NSA_GROUPS * GATE_ROWS, T), F32),
        jax.ShapeDtypeStruct((B, T, D_RWKV_IN), F32),
    )
    out_specs = (
        pl.BlockSpec((1, D_NSA, tm), lambda b, i: (b, 0, tile(i))),
        pl.BlockSpec((1, tm, D_KV), lambda b, i: (b, tile(i), 0)),
        pl.BlockSpec((1, tm, D_KV), lambda b, i: (b, tile(i), 0)),
        pl.BlockSpec((1, NSA_GROUPS, tm, 2 * HEAD_DIM + ns), lambda b, i: (b, 0, i, 0)),
        pl.BlockSpec((1, NSA_GROUPS, tm, 2 * HEAD_DIM), lambda b, i: (b, 0, i, 0)),
        pl.BlockSpec((1, 2 * NSA_GROUPS, tm // QT, V_ROWS, QT), lambda b, i: (b, 0, i, 0, 0)),
        pl.BlockSpec((1, NSA_GROUPS * GATE_ROWS, tm), lambda b, i: (b, 0, tile(i))),
        pl.BlockSpec((1, tm, D_RWKV_IN), lambda b, i: (b, tile(i), 0)),
    )
    in_specs = [
        pl.BlockSpec((1, tm, D), lambda b, i: (b, tile(i), 0)),
        pl.BlockSpec(g1.shape, const2),
        pl.BlockSpec(wstd.shape, const2),
        pl.BlockSpec(wt.shape, const2),
        pl.BlockSpec(qg.shape, const2),
        pl.BlockSpec(kg.shape, const2),
    ]
    return pl.pallas_call(
        _proj_kernel, grid=grid, in_specs=in_specs, out_specs=out_specs, out_shape=out_shape,
        compiler_params=_cparams(("parallel", "arbitrary"), [False, False, True, True, False, False]), name="proj",
    )(x, g1, wstd, wt, qg, kg)


def _gelu_tanh(x):
    c = math.sqrt(2.0 / math.pi)
    return x * (0.5 * (1.0 + jnp.tanh(c * (x + 0.044715 * (x * x * x)))))


def _compress_kernel(kc_ref, vc_ref, wbig_ref, pos_ref, w1_ref, b1_ref, w2k_ref, w2vT_ref,
                     b2k_ref, b2v_ref, kg_ref, kcmp_ref, vcmpT_ref, act_ref):
    m = kc_ref.shape[1] // CMP_STRIDE
    nc = m - 1
    ncp = act_ref.shape[0]
    row_m = lax.broadcasted_iota(jnp.int32, (m, CMP_HIDDEN), 0)
    row_p = lax.broadcasted_iota(jnp.int32, (ncp, HEAD_DIM), 0)
    col_p = lax.broadcasted_iota(jnp.int32, (HEAD_DIM, ncp), 1)
    col_f = lax.broadcasted_iota(jnp.int32, (ncp, HEAD_DIM), 1)
    row_o = lax.broadcasted_iota(jnp.int32, (V_ROWS - HEAD_DIM, ncp), 0)
    act_ref[...] = jnp.zeros(act_ref.shape, F32)
    for which, src_ref in enumerate((kc_ref, vc_ref)):
        p = jnp.zeros((m, 2 * NSA_GROUPS * CMP_HIDDEN), F32)
        for tl in range(CMP_STRIDE):
            p = p + jnp.dot(src_ref[0, pl.ds(tl, m, stride=CMP_STRIDE), :].astype(BF16),
                            wbig_ref[which, tl * D_KV:(tl + 1) * D_KV, :], preferred_element_type=F32)
        posc = jnp.dot(pos_ref[which], w1_ref[which], preferred_element_type=F32)[0:1]
        for g in range(NSA_GROUPS):
            top = p[:, (2 * g) * CMP_HIDDEN:(2 * g + 1) * CMP_HIDDEN]
            bot = p[:, (2 * g + 1) * CMP_HIDDEN:(2 * g + 2) * CMP_HIDDEN]
            hid = top + pltpu.roll(bot, m - 1, 0) + (b1_ref[which] + posc)
            act = jnp.where(row_m < nc, _gelu_tanh(hid), 0.0)
            act_ref[CMP_PAD:CMP_PAD + m, :] = act
            ap = act_ref[...].astype(BF16)
            if which == 0:
                kc = jnp.dot(ap, w2k_ref[...], preferred_element_type=F32) + b2k_ref[...]
                ms = jnp.mean(kc * kc, axis=-1, keepdims=True)
                kc = kc * lax.rsqrt(ms + NORM_EPS) * kg_ref[...]
                valid = (row_p >= CMP_PAD) & (row_p < CMP_PAD + nc)
                kcmp_ref[0, g, :, 0:HEAD_DIM] = jnp.where(valid, kc, 0.0).astype(BF16)
                flag = (col_f == 0) & jnp.logical_not(valid)
                kcmp_ref[0, g, :, HEAD_DIM:] = jnp.where(flag, 1.0, 0.0).astype(BF16)
            else:
                vt = lax.dot_general(w2vT_ref[...], ap, (((1,), (1,)), ((), ())),
                                     preferred_element_type=F32) + b2v_ref[...]
                valid = (col_p >= CMP_PAD) & (col_p < CMP_PAD + nc)
                vcmpT_ref[0, g, 0:HEAD_DIM, :] = jnp.where(valid, vt, 0.0).astype(BF16)
                vcmpT_ref[0, g, HEAD_DIM:, :] = jnp.where(row_o == 0, 1.0, 0.0).astype(BF16)


def _compress(kc, vc, wbig, pos8, w1, b1, w2k, w2vT, b2k, b2v, kg, *, ncp):
    B = kc.shape[0]
    full = lambda a: pl.BlockSpec(a.shape, lambda b: (0,) * a.ndim)
    return pl.pallas_call(
        _compress_kernel, grid=(B,),
        in_specs=[pl.BlockSpec((1,) + kc.shape[1:], lambda b: (b, 0, 0)),
                  pl.BlockSpec((1,) + vc.shape[1:], lambda b: (b, 0, 0)),
                  full(wbig), full(pos8), full(w1), full(b1), full(w2k), full(w2vT),
                  full(b2k), full(b2v), full(kg)],
        out_specs=(pl.BlockSpec((1, NSA_GROUPS, ncp, 2 * HEAD_DIM), lambda b: (b, 0, 0, 0)),
                   pl.BlockSpec((1, NSA_GROUPS, V_ROWS, ncp), lambda b: (b, 0, 0, 0))),
        out_shape=(jax.ShapeDtypeStruct((B, NSA_GROUPS, ncp, 2 * HEAD_DIM), BF16),
                   jax.ShapeDtypeStruct((B, NSA_GROUPS, V_ROWS, ncp), BF16)),
        scratch_shapes=[pltpu.VMEM((ncp, CMP_HIDDEN), F32)],
        compiler_params=_cparams(("parallel",)), name="compress",
    )(kc, vc, wbig, pos8, w1, b1, w2k, w2vT, b2k, b2v, kg)


def _flash_update(s_ref, smax_ref, vT, m_ref, acc_ref):
    m_prev = m_ref[...]
    m_new = jnp.maximum(m_prev, smax_ref[...])
    p = jnp.exp2(s_ref[...] - m_new)
    acc_ref[...] = (jnp.exp2(m_prev - m_new) * acc_ref[...]
                    + jnp.dot(vT, p.astype(BF16), preferred_element_type=F32))
    m_ref[...] = m_new


def _attn_kernel(qT_ref, kcmp_ref, vcmpT_ref, bc_ref, ksel_ref, vselT_ref, kwin_ref, vwinT_ref,
                 stab_ref, wtab_ref, gT_ref, o_ref,
                 lc_ref, psum_ref, qaug_ref, qfar_ref, oc_ref, ow_ref, ms_ref, accs_ref, sa_ref, sb_ref, ma_ref, mb_ref):
    qt = pl.program_id(2)
    ncp = kcmp_ref.shape[2]
    ns = qaug_ref.shape[0] - 2 * HEAD_DIM
    nq = NSA_HPG * QT
    t_lane = qt * QT + lax.broadcasted_iota(jnp.int32, (1, QT), 1)

    for r in range(NSA_HPG):
        qaug_ref[0:HEAD_DIM, r * QT:(r + 1) * QT] = qT_ref[0, r * HEAD_DIM:(r + 1) * HEAD_DIM, :]
    flag_row = lax.broadcasted_iota(jnp.int32, (HEAD_DIM, nq), 0) == 0
    qaug_ref[HEAD_DIM:2 * HEAD_DIM, :] = jnp.where(flag_row, NEG_INF, 0.0).astype(BF16)
    qk = qaug_ref[0:2 * HEAD_DIM, :]

    rho = lax.broadcasted_iota(jnp.int32, (ncp, 1), 0)
    band0 = pl.multiple_of(qt * (QT // CMP_STRIDE), V7X_SUBLANES)
    lc_ref[...] = jnp.dot(kcmp_ref[0, 0], qk, preferred_element_type=F32)
    lc_ref[pl.ds(band0, CMP_BAND), :] += bc_ref[0]
    lc = lc_ref[...] + jnp.where(rho < band0 + CMP_BAND, 0.0, NEG_INF)
    e = jnp.exp2(lc - jnp.max(lc, axis=0, keepdims=True))
    oc_aug = jnp.dot(vcmpT_ref[0, 0], e.astype(BF16), preferred_element_type=F32)
    t_q = qt * QT + lax.broadcasted_iota(jnp.int32, (1, nq), 1) % QT
    inv_c = jnp.where(t_q >= CMP_BLOCK - 1, 1.0 / oc_aug[HEAD_DIM:HEAD_DIM + 1, :], 0.0)
    oc_ref[...] = oc_aug[0:HEAD_DIM] * inv_c
    p = e * inv_c
    psum = (p[:, 0:QT] + p[:, QT:2 * QT]) + (p[:, 2 * QT:3 * QT] + p[:, 3 * QT:4 * QT])
    for c in range(QT // V7X_LANES):
        psum_ref[c] = psum[:, c * V7X_LANES:(c + 1) * V7X_LANES]

    def strided_sum(c):
        acc = psum_ref[c, pl.ds(CMP_PAD - 1, ns, stride=4), :]
        for k in range(1, 5):
            acc = acc + psum_ref[c, pl.ds(CMP_PAD - 1 + k, ns, stride=4), :]
        return acc

    imp = jnp.concatenate([strided_sum(c) for c in range(QT // V7X_LANES)], axis=1)
    jrow = lax.broadcasted_iota(jnp.int32, (ns, QT), 0)
    cur = t_lane // SEL_BLOCK
    forced = (jrow == 0) | (jrow == cur) | (jrow == cur - 1)
    live = jrow * SEL_BLOCK <= t_lane
    score = jnp.where(live, jnp.where(forced, -jnp.inf, imp), NEG_INF)
    jrow_f = jrow.astype(F32)

    def col_reduce(x, pair, reduce):
        parts = [x[i * V7X_SUBLANES:(i + 1) * V7X_SUBLANES, :] for i in range(x.shape[0] // V7X_SUBLANES)]
        while len(parts) > 1:
            parts = [pair(parts[i], parts[i + 1]) for i in range(0, len(parts) - 1, 2)] + parts[len(parts) & ~1:]
        return reduce(parts[0], axis=0, keepdims=True)

    for _ in range(min(SEL_TOPN, ns) - 3):
        mx = col_reduce(score, jnp.maximum, jnp.max)
        idx = col_reduce(jnp.where(score == mx, jrow_f, float(ns)), jnp.minimum, jnp.min)
        score = jnp.where(jrow_f == idx, -jnp.inf, score)
    sel = live & (score == -jnp.inf)
    negmask = jnp.where(sel, 0.0, NEG_INF).astype(BF16)
    negfar = jnp.where(sel & (jrow < (qt - 1) * (QT // SEL_BLOCK)), 0.0, NEG_INF).astype(BF16)
    qfar_ref[0:2 * HEAD_DIM, :] = qaug_ref[0:2 * HEAD_DIM, :]
    for r in range(NSA_HPG):
        qaug_ref[2 * HEAD_DIM:, r * QT:(r + 1) * QT] = negmask
        qfar_ref[2 * HEAD_DIM:, r * QT:(r + 1) * QT] = negfar

    w0 = pl.multiple_of(qt * QT, QT)
    sw = jnp.dot(kwin_ref[0, 0, pl.ds(w0, WIN_KEYS), :], qk, preferred_element_type=F32) + wtab_ref[0]
    pw = jnp.exp2(sw - jnp.max(sw, axis=0, keepdims=True))
    vw = jnp.concatenate([vwinT_ref[0, 0, qt + j] for j in range(WIN_KEYS // QT)], axis=1)
    ow_aug = jnp.dot(vw, pw.astype(BF16), preferred_element_type=F32)
    ow_ref[...] = ow_aug[0:HEAD_DIM] * (1.0 / ow_aug[HEAD_DIM:HEAD_DIM + 1, :])

    n0 = pl.multiple_of((ATT_PAD // QT - 1 + qt) * QT, QT)
    sn = jnp.dot(ksel_ref[0, 0, pl.ds(n0, NEAR_KEYS), :], qaug_ref[...],
                 preferred_element_type=F32) + stab_ref[0]
    mn = jnp.max(sn, axis=0, keepdims=True)
    pn = jnp.exp2(sn - mn)
    vn = jnp.concatenate([vselT_ref[0, 0, ATT_PAD // QT - 1 + qt + j] for j in range(NEAR_KEYS // QT)], axis=1)
    ms_ref[...] = mn
    accs_ref[...] = jnp.dot(vn, pn.astype(BF16), preferred_element_type=F32)

    tiles_per_far = FAR_KEYS // QT
    n_far = (qt - 1 + tiles_per_far - 1) // tiles_per_far
    last_far = (ksel_ref.shape[2] - ATT_PAD) // FAR_KEYS - 1

    def far_logits(g, dst_ref, dmax_ref):
        r0 = pl.multiple_of(ATT_PAD + jnp.minimum(g, last_far) * FAR_KEYS, FAR_KEYS)
        s = jnp.dot(ksel_ref[0, 0, pl.ds(r0, FAR_KEYS), :], qfar_ref[...], preferred_element_type=F32)
        dst_ref[...] = s
        dmax_ref[...] = jnp.max(s, axis=0, keepdims=True)

    def far_values(g):
        t0 = (ATT_PAD + g * FAR_KEYS) // QT
        return jnp.concatenate([vselT_ref[0, 0, t0 + j] for j in range(tiles_per_far)], axis=1)

    far_logits(0, sa_ref, ma_ref)

    def far_body(j, carry):
        bufs = ((sa_ref, ma_ref), (sb_ref, mb_ref))
        for u in range(FAR_UNROLL):
            g = FAR_UNROLL * j + u
            far_logits(g + 1, *bufs[(u + 1) % 2])
            _flash_update(*bufs[u % 2], far_values(jnp.minimum(g, last_far)), ms_ref, accs_ref)
        return carry

    lax.fori_loop(0, (n_far + FAR_UNROLL - 1) // FAR_UNROLL, far_body, 0)

    o_s = accs_ref[0:HEAD_DIM, :] * (1.0 / accs_ref[HEAD_DIM:HEAD_DIM + 1, :])
    for r in range(NSA_HPG):
        cols = slice(r * QT, (r + 1) * QT)
        o = (gT_ref[0, 3 * r:3 * r + 1, :] * oc_ref[:, cols] + gT_ref[0, 3 * r + 1:3 * r + 2, :] * o_s[:, cols]
             + gT_ref[0, 3 * r + 2:3 * r + 3, :] * ow_ref[:, cols])
        o_ref[0, :, r * HEAD_DIM:(r + 1) * HEAD_DIM] = o.T


def _attn(qT, kcmp, vcmpT, bc, ksel, vT5, kwin, stab, wtab, gT):
    B, _, T = qT.shape
    ncp = kcmp.shape[2]
    ns = T // SEL_BLOCK
    nq = NSA_HPG * QT
    tp = T + ATT_PAD
    nt = tp // QT
    grid = (B, NSA_GROUPS, T // QT)

    def held(shape, index_map):
        return pl.BlockSpec(shape, index_map, pipeline_mode=pl.Buffered(1))

    in_specs = [
        pl.BlockSpec((1, NSA_HPG * HEAD_DIM, QT), lambda b, g, q: (b, g, q)),
        held((1, 1, ncp, 2 * HEAD_DIM), lambda b, g, q: (b, g, 0, 0)),
        held((1, 1, V_ROWS, ncp), lambda b, g, q: (b, g, 0, 0)),
        held((1, CMP_BAND, nq), lambda b, g, q: (g, 0, 0)),
        pl.BlockSpec((1, 1, tp, 2 * HEAD_DIM + ns), lambda b, g, q: (b, g, 0, 0)),
        pl.BlockSpec((1, 1, nt, V_ROWS, QT), lambda b, g, q: (b, g, 0, 0, 0)),
        pl.BlockSpec((1, 1, tp, 2 * HEAD_DIM), lambda b, g, q: (b, g, 0, 0)),
        pl.BlockSpec((1, 1, nt, V_ROWS, QT), lambda b, g, q: (b, NSA_GROUPS + g, 0, 0, 0)),
        held((1, NEAR_KEYS, nq), lambda b, g, q: (g, 0, 0)),
        held((1, WIN_KEYS, nq), lambda b, g, q: (g, 0, 0)),
        pl.BlockSpec((1, GATE_ROWS, QT), lambda b, g, q: (b, g, q)),
    ]
    scratch = [
        pltpu.VMEM((ncp, nq), F32),
        pltpu.VMEM((QT // V7X_LANES, ncp, V7X_LANES), F32),
        pltpu.VMEM((2 * HEAD_DIM + ns, nq), BF16),
        pltpu.VMEM((2 * HEAD_DIM + ns, nq), BF16),
        pltpu.VMEM((HEAD_DIM, nq), F32),
        pltpu.VMEM((HEAD_DIM, nq), F32),
        pltpu.VMEM((1, nq), F32),
        pltpu.VMEM((V_ROWS, nq), F32),
        pltpu.VMEM((FAR_KEYS, nq), F32),
        pltpu.VMEM((FAR_KEYS, nq), F32),
        pltpu.VMEM((1, nq), F32), pltpu.VMEM((1, nq), F32),
    ]
    return pl.pallas_call(
        _attn_kernel, grid=grid, in_specs=in_specs,
        out_specs=pl.BlockSpec((1, QT, NSA_HPG * HEAD_DIM), lambda b, g, q: (b, q, g)),
        out_shape=jax.ShapeDtypeStruct((B, T, D_NSA), F32),
        scratch_shapes=scratch,
        compiler_params=_cparams(("parallel", "parallel", "arbitrary")), name="attn",
    )(qT, kcmp, vcmpT, bc, ksel, vT5, kwin, vT5, stab, wtab, gT)


def _rwkv_tokens(rw_ref, halo_ref, mu_ref, w0_ref, w2_ref, a0_ref, a2_ref, g2_ref, kk_ref, ka_ref,
                 rk_ref, bd_ref, ext_ref):
    tm = rw_ref.shape[1]
    first_tile = pl.program_id(1) == 0
    ext_ref[0:V7X_SUBLANES, :] = jnp.where(first_tile, 0.0, halo_ref[0])
    ext_ref[V7X_SUBLANES:, :] = rw_ref[0]
    cur = rw_ref[0]
    prev = ext_ref[pl.ds(V7X_SUBLANES - 1, tm), :]
    mixed = cur + (prev - cur) * mu_ref[...]
    c = D_RWKV
    r = mixed[:, 0:c]
    k = mixed[:, c:2 * c]
    v = mixed[:, 2 * c:3 * c]
    xw = mixed[:, 3 * c:3 * c + LORA_W]
    xa = mixed[:, 3 * c + LORA_W:3 * c + LORA_W + LORA_A]
    xg = mixed[:, 3 * c + LORA_W + LORA_A:]

    z = -(w0_ref[...] + _bdot(jnp.tanh(xw), w2_ref[...]))
    softplus = jnp.maximum(z, 0.0) + jnp.log1p(jnp.exp(-jnp.abs(z)))
    w = -softplus - 0.5
    a = jax.nn.sigmoid(a0_ref[...] + _bdot(xa, a2_ref[...]))
    g = _bdot(jax.nn.sigmoid(xg), g2_ref[...])

    kk = k * kk_ref[...]
    n2 = _head_sum(kk * kk, bd_ref[...])
    kkn = kk / jnp.maximum(jnp.sqrt(n2), 1e-12)
    k2 = k * (1.0 + (a - 1.0) * ka_ref[...])
    bonus = _head_sum(r * k2 * rk_ref[...], bd_ref[...]) * v

    lw = -jnp.exp(w)
    return dict(r=r, lw=lw, k=k2, v=v, kk=kkn, b=kkn * a, g=g, bonus=bonus)


WKV_BATCH = 8
WKV_GROUP = 4
WKV_GW = WKV_GROUP * HEAD_DIM


def _wkv_kernel(rw_ref, halo_ref, mu_ref, w0_ref, w2_ref, a0_ref, a2_ref, g2_ref, kk_ref, ka_ref, rk_ref, bd_ref,
                y_ref, g_ref, bonus_ref, s_ref, ext_ref, *, chunks):
    L = WKV_CHUNK
    tok = _rwkv_tokens(rw_ref, halo_ref, mu_ref, w0_ref, w2_ref, a0_ref, a2_ref, g2_ref, kk_ref, ka_ref,
                       rk_ref, bd_ref, ext_ref)
    g_ref[0] = tok["g"]
    bonus_ref[0] = tok["bonus"]
    assert L == HEAD_DIM
    gw = WKV_GW
    n_groups = RWKV_HEADS // WKV_GROUP

    @pl.when(pl.program_id(1) == 0)
    def _():
        s_ref[...] = jnp.zeros(s_ref.shape, F32)

    ti = lax.broadcasted_iota(jnp.int32, (L, L), 0)
    tj = lax.broadcasted_iota(jnp.int32, (L, L), 1)
    tri = jnp.where(ti >= tj, 1.0, 0.0).astype(BF16)
    row = lax.broadcasted_iota(jnp.int32, (L, gw), 0)
    col = lax.broadcasted_iota(jnp.int32, (L, gw), 1) % L
    low_strict = col < row
    low_incl = col <= row
    eye_sbs = jnp.where(col == row, 1.0, 0.0)
    brow = lax.broadcasted_iota(jnp.int32, (gw, gw), 0) // L
    bcol = lax.broadcasted_iota(jnp.int32, (gw, gw), 1) // HEAD_DIM
    same_head = brow == bcol

    def bd_rows(x):
        xb = x.astype(BF16)
        return jnp.where(same_head, jnp.concatenate([xb] * WKV_GROUP, axis=0), jnp.zeros((), BF16))

    def mm(a, b_bf16):
        return jnp.dot(a.astype(BF16), b_bf16, preferred_element_type=F32)

    def tn(a, b):
        return lax.dot_general(a.astype(BF16), b.astype(BF16), (((0,), (0,)), ((), ())),
                               preferred_element_type=F32)

    pre, w2, c2, m_lr, d_sbs = {}, {}, {}, {}, {}

    def grp(c, g, name):
        return pre[c][name][:, g * gw:(g + 1) * gw]

    for c0 in range(0, chunks, WKV_BATCH):
        batch = range(c0, min(c0 + WKV_BATCH, chunks))
        inst = [(c, g) for c in batch for g in range(n_groups)]
        for c in batch:
            rows = slice(c * L, (c + 1) * L)
            lw = tok["lw"][rows]
            cs = _dot_exact_rhs_left(tri, lw)
            c_last = cs[L - 1:L, :]
            e_nc = jnp.exp(-cs)
            e_lc = jnp.exp(c_last - cs)
            kk = tok["kk"][rows]
            b = tok["b"][rows]
            k = tok["k"][rows]
            pre[c] = dict(a=-kk * jnp.exp(cs - lw), r=tok["r"][rows] * jnp.exp(cs), bh=b * e_nc, kh=k * e_nc,
                          be=b * e_lc, ke=k * e_lc, v=tok["v"][rows], e_last=jnp.exp(c_last))

        t_all = {}
        for (c, g) in inst:
            lhs = jnp.concatenate([grp(c, g, "a"), grp(c, g, "r")], axis=0).astype(BF16)
            rhs = jnp.concatenate([bd_rows(grp(c, g, "bh")), bd_rows(grp(c, g, "kh"))], axis=0)
            t_all[c, g] = lax.dot_general(lhs, rhs, (((1,), (1,)), ((), ())), preferred_element_type=F32)
        n_m = {i: jnp.where(low_strict, t_all[i][0:L, 0:gw], 0.0) for i in inst}
        tak = {i: jnp.where(low_strict, t_all[i][0:L, gw:], 0.0) for i in inst}
        trb = {i: jnp.where(low_incl, t_all[i][L:, 0:gw], 0.0) for i in inst}
        trk = {i: jnp.where(low_incl, t_all[i][L:, gw:], 0.0) for i in inst}
        z = {i: eye_sbs + n_m[i] for i in inst}
        pw = dict(n_m)
        for _ in range(int(math.log2(L)) - 1):
            pw = {i: mm(pw[i], bd_rows(pw[i])) for i in inst}
            z = {i: z[i] + mm(z[i], bd_rows(pw[i])) for i in inst}
        vbd = {(c, g): bd_rows(grp(c, g, "v")) for (c, g) in inst}
        w1 = {(c, g): mm(z[c, g], bd_rows(grp(c, g, "a"))) for (c, g) in inst}
        tv = {i: mm(tak[i], vbd[i]) for i in inst}
        c1 = {i: mm(z[i], bd_rows(tv[i])) for i in inst}
        w2.update({(c, g): grp(c, g, "r") + mm(trb[c, g], bd_rows(w1[c, g])) for (c, g) in inst})
        c2.update({i: mm(trb[i], bd_rows(c1[i])) + mm(trk[i], vbd[i]) for i in inst})
        m_lr.update({(c, g): jnp.where(same_head, tn(w1[c, g], grp(c, g, "be")), 0.0).astype(BF16)
                     for (c, g) in inst})
        for (c, g) in inst:
            full = jnp.where(same_head,
                             tn(jnp.concatenate([c1[c, g], grp(c, g, "v")], axis=0),
                                jnp.concatenate([grp(c, g, "be"), grp(c, g, "ke")], axis=0)), 0.0)
            d_sbs[c, g] = ((full[0:L] + full[L:2 * L]) + (full[2 * L:3 * L] + full[3 * L:4 * L]))

    for g in range(n_groups):
        s = s_ref[:, g * gw:(g + 1) * gw]
        for c in range(chunks):
            g_bd = jnp.where(same_head, jnp.concatenate([s.T.astype(BF16)] * WKV_GROUP, axis=1),
                             jnp.zeros((), BF16))
            y_ref[0, c * L:(c + 1) * L, g * gw:(g + 1) * gw] = mm(w2[c, g], g_bd) + c2[c, g]
            s = s * grp(c, g, "e_last") + mm(s, m_lr[c, g]) + d_sbs[c, g]
        s_ref[:, g * gw:(g + 1) * gw] = s


def _dot_exact_rhs_left(m_bf16, x):
    hi, mid, lo = _split3(x)
    return (jnp.dot(m_bf16, hi, preferred_element_type=F32)
            + jnp.dot(m_bf16, mid, preferred_element_type=F32)
            + jnp.dot(m_bf16, lo, preferred_element_type=F32))


def _wkv(rw, mu, w0, w2, a0, a2, g2, k_k, k_a, r_k, bd, *, chunks):
    B, T, C = rw.shape
    rows = chunks * WKV_CHUNK
    hb = rows // V7X_SUBLANES
    full = lambda a: pl.BlockSpec(a.shape, lambda bb, c: (0,) * a.ndim)
    tok = pl.BlockSpec((1, rows, D_RWKV), lambda bb, c: (bb, c, 0))
    return pl.pallas_call(
        functools.partial(_wkv_kernel, chunks=chunks), grid=(B, T // rows),
        in_specs=[pl.BlockSpec((1, rows, C), lambda bb, c: (bb, c, 0)),
                  pl.BlockSpec((1, V7X_SUBLANES, C), lambda bb, c: (bb, jnp.maximum(c * hb - 1, 0), 0)),
                  full(mu), full(w0), full(w2), full(a0), full(a2), full(g2), full(k_k), full(k_a),
                  full(r_k), full(bd)],
        out_specs=(tok,) * 3,
        out_shape=(jax.ShapeDtypeStruct((B, T, D_RWKV), F32),) * 3,
        scratch_shapes=[pltpu.VMEM((HEAD_DIM, D_RWKV), F32), pltpu.VMEM((rows + V7X_SUBLANES, C), F32)],
        compiler_params=_cparams(("parallel", "arbitrary")), name="wkv",
    )(rw, rw, mu, w0, w2, a0, a2, g2, k_k, k_a, r_k, bd)


def _mix_rows(x, on, y, bonus, g, bd_ref, lnw_ref, lnb_ref, wo_ref, g2_ref):
    mu = _dot_exact_rhs(y, bd_ref[...]) * (1.0 / HEAD_DIM)
    yc = y - mu
    var = _head_sum(yc * yc, bd_ref[...]) * (1.0 / HEAD_DIM)
    yn = yc * lax.rsqrt(var + GN_EPS) * lnw_ref[...] + lnb_ref[...]
    orw = (yn + bonus) * g
    x1 = x + _bdot(on, wo_ref[0:D_NSA, :]) + _bdot(orw, wo_ref[D_NSA:, :])
    ms = jnp.mean(x1 * x1, axis=-1, keepdims=True)
    return x1, (x1 * lax.rsqrt(ms + NORM_EPS) * g2_ref[...]).astype(BF16)


FFN_HALO = 16


def _ffn_kernel(x_ref, on_ref, y_ref, bonus_ref, g_ref, xh_ref, onh_ref, yh_ref, bonush_ref, gh_ref,
                bd_ref, lnw_ref, lnb_ref, wo_ref, g2_ref,
                wv_ref, wg_ref, cwv_ref, cwg_ref, cbv_ref, cbg_ref, wd_ref,
                o_ref, hext_ref, extv_ref, extg_ref, act_ref, x1_ref, *, tiles_per_seq, ft):
    tm = x_ref.shape[0]
    dff = wd_ref.shape[0]
    seq_start = (pl.program_id(0) % tiles_per_seq) == 0
    ext = lambda h_ref, m_ref: jnp.concatenate([h_ref[...], m_ref[...]], axis=0)
    x1, h2 = _mix_rows(ext(xh_ref, x_ref), ext(onh_ref, on_ref), ext(yh_ref, y_ref), ext(bonush_ref, bonus_ref),
                       ext(gh_ref, g_ref), bd_ref, lnw_ref, lnb_ref, wo_ref, g2_ref)
    x1_ref[...] = x1[FFN_HALO:]
    row = lax.broadcasted_iota(jnp.int32, h2.shape, 0)
    hext_ref[...] = jnp.where(seq_start & (row < FFN_HALO), jnp.zeros((), BF16), h2)

    for j in range(dff // ft):
        cols = slice(j * ft, (j + 1) * ft)

        def conv_branch(w_ref, cw_ref, cb_ref, ext_ref):
            ext_ref[j % 2] = jnp.dot(hext_ref[...], w_ref[:, cols], preferred_element_type=F32)
            out = cb_ref[:, cols] + ext_ref[j % 2, pl.ds(FFN_HALO, tm), :] * cw_ref[CONV_W - 1:CONV_W, cols]
            for i in range(CONV_W - 1):
                back = CONV_W - 1 - i
                out = out + ext_ref[j % 2, pl.ds(FFN_HALO - back, tm), :] * cw_ref[i:i + 1, cols]
            return out

        u_val = conv_branch(wv_ref, cwv_ref, cbv_ref, extv_ref)
        u_gate = conv_branch(wg_ref, cwg_ref, cbg_ref, extg_ref)
        act_ref[:, cols] = ((u_gate * jax.nn.sigmoid(u_gate)) * u_val).astype(BF16)
    o_ref[...] = x1_ref[...] + jnp.dot(act_ref[...], wd_ref[...], preferred_element_type=F32)


def _ffn(x, on, y, bonus, g, bd, lnw, lnb, wo, g2, wv, wg, cwv, cwg, cbv, cbg, wd, *, tm, ft, seq_len):
    n, d = x.shape
    dff = wv.shape[1]
    hb = tm // FFN_HALO
    kern = functools.partial(_ffn_kernel, tiles_per_seq=seq_len // tm, ft=ft)
    resident = lambda a: pl.BlockSpec(a.shape, lambda i: (0,) * a.ndim, pipeline_mode=pl.Buffered(1))
    tile = lambda a: pl.BlockSpec((tm, a.shape[1]), lambda i: (i, 0))
    halo = lambda a: pl.BlockSpec((FFN_HALO, a.shape[1]), lambda i: (jnp.maximum(i * hb - 1, 0), 0))
    tokens = (x, on, y, bonus, g)
    params = (bd, lnw, lnb, wo, g2, wv, wg, cwv, cwg, cbv, cbg, wd)
    return pl.pallas_call(
        kern, grid=(n // tm,),
        in_specs=[tile(a) for a in tokens] + [halo(a) for a in tokens] + [resident(a) for a in params],
        out_specs=pl.BlockSpec((tm, d), lambda i: (i, 0)),
        out_shape=jax.ShapeDtypeStruct((n, d), F32),
        scratch_shapes=[pltpu.VMEM((tm + FFN_HALO, d), BF16),
                        pltpu.VMEM((2, tm + FFN_HALO, ft), F32), pltpu.VMEM((2, tm + FFN_HALO, ft), F32),
                        pltpu.VMEM((tm, dff), BF16), pltpu.VMEM((tm, d), F32)],
        compiler_params=_cparams(("parallel",), [False] * (2 * len(tokens)) + [a.dtype == BF16 and a.ndim == 2
                                                                   and a.shape[0] > 8 for a in params]), name="ffn",
    )(*tokens, *tokens, *params)


def _t5_bucket(dist):
    n = np.maximum(dist, 0)
    max_exact = N_BUCKETS // 2
    nf = np.maximum(n, 1).astype(np.float32)
    large = max_exact + (np.log(nf / np.float32(max_exact)) / np.float32(math.log(MAX_DISTANCE / max_exact))
                         * np.float32(N_BUCKETS - max_exact)).astype(np.int32)
    large = np.minimum(large, N_BUCKETS - 1)
    return np.where(n < max_exact, n, large)


def _toeplitz_kernel(*refs, steps):
    n = len(steps)
    for v_ref, o_ref, step in zip(refs[:n], refs[n:], steps):
        rows, width = o_ref.shape[1], o_ref.shape[2]
        x = jnp.broadcast_to(v_ref[0], (rows, v_ref.shape[2]))
        o_ref[0] = pltpu.roll(x, 0, 1, stride=step, stride_axis=0)[:, :width]


def _toeplitz(specs):
    return pl.pallas_call(
        functools.partial(_toeplitz_kernel, steps=tuple(step for _, _, step in specs)), grid=(NSA_GROUPS, NSA_HPG),
        in_specs=[pl.BlockSpec((1, 1, v.shape[2]), lambda g, r: (g * NSA_HPG + r, 0, 0)) for v, _, _ in specs],
        out_specs=tuple(pl.BlockSpec((1, rows, QT), lambda g, r: (g, 0, r)) for _, rows, _ in specs),
        out_shape=tuple(jax.ShapeDtypeStruct((NSA_GROUPS, rows, NSA_HPG * QT), F32) for _, rows, _ in specs),
        compiler_params=_cparams(("parallel", "parallel")), name="toeplitz",
    )(*[v for v, _, _ in specs])


def _bias_tables(rel_bias):
    rel = (rel_bias - rel_bias[N_BUCKETS - 1][None, :]) * LOG2E

    def table(rows, step, d0, d_hi):
        period = step * rows + QT
        i = np.arange(period)
        d = d0 + np.where(i < QT, i, i - period)
        onehot = (_t5_bucket(d)[:, None] == np.arange(N_BUCKETS)[None, :]).astype(np.float32)
        v = jnp.dot(jnp.asarray(onehot), rel, precision=lax.Precision.HIGHEST)
        v = jnp.where(jnp.asarray((d >= 0) & (d < d_hi))[:, None], v, NEG_INF).T
        return v.reshape(NSA_HEADS, 1, period), rows, step

    no_limit = 1 << 30
    stab, wtab, bc = _toeplitz([table(NEAR_KEYS, 1, QT, no_limit), table(WIN_KEYS, 1, WINDOW, WINDOW),
                                table(CMP_BAND, CMP_STRIDE, CMP_STRIDE * CMP_PAD - CMP_BLOCK + 1, no_limit)])
    return stab, wtab, bc


def _compress_weights(w1):
    half = CMP_BLOCK // 2
    rows = half * HEAD_DIM
    w3 = jnp.concatenate([w1[:rows], w1[rows:]], axis=1).reshape(half, HEAD_DIM, 2 * CMP_HIDDEN)
    z = jnp.zeros_like(w3)
    assert NSA_GROUPS == 2
    big = jnp.concatenate([jnp.concatenate([w3, z], axis=2), jnp.concatenate([z, w3], axis=2)], axis=1)
    return big.reshape(half * NSA_GROUPS * HEAD_DIM, NSA_GROUPS * 2 * CMP_HIDDEN)


def kernel(x, norm1_g, w_in, q_norm_g, k_norm_g, cmp_pos, cmp_w1, cmp_b1, cmp_w2, cmp_b2, rel_bias, rwkv_mu,
           w0, w2, a0, a2, g2, k_k, k_a, r_k, ln_x_w, ln_x_b, w_out, norm2_g, ffn_up, conv_w, conv_b, ffn_down):
    B, T, D = x.shape
    depth = w_in.shape[0]
    d_ff = ffn_down.shape[1]
    assert T % 2048 == 0 and D_NSA + 6 * D_KV + 3 * NSA_HEADS + D_RWKV_IN == w_in.shape[2]
    ncp = T // CMP_STRIDE + V7X_LANES
    stab, wtab, bc = _bias_tables(rel_bias)
    ii = jnp.arange(D_RWKV)
    bd = (ii[:, None] // HEAD_DIM == ii[None, :] // HEAD_DIM).astype(BF16)

    for l in range(depth):
        wi = w_in[l]
        o = D_NSA
        q_w, kc_w, vc_w, ksl_w, vsl_w, kwn_w, vwn_w = (
            wi[:, 0:o], wi[:, o:o + D_KV], wi[:, o + D_KV:o + 2 * D_KV], wi[:, o + 2 * D_KV:o + 3 * D_KV],
            wi[:, o + 3 * D_KV:o + 4 * D_KV], wi[:, o + 4 * D_KV:o + 5 * D_KV], wi[:, o + 5 * D_KV:o + 6 * D_KV])
        gl_w = wi[:, o + 6 * D_KV:o + 6 * D_KV + 3 * NSA_HEADS]
        rw_w = wi[:, o + 6 * D_KV + 3 * NSA_HEADS:]
        wstd = jnp.concatenate([kc_w, vc_w, ksl_w, kwn_w, rw_w], axis=1).astype(BF16)
        gl_rows = gl_w.T.reshape(NSA_GROUPS, 3 * NSA_HPG, D)
        gl_rows = jnp.pad(gl_rows, ((0, 0), (0, GATE_ROWS - 3 * NSA_HPG), (0, 0))).reshape(-1, D)
        wt = jnp.concatenate([q_w.T, vsl_w.T, vwn_w.T, gl_rows], axis=0).astype(BF16)
        qg = q_norm_g[l].reshape(HEAD_DIM, 1)
        kg = jnp.stack([jnp.tile(k_norm_g[l, 1], NSA_GROUPS), jnp.tile(k_norm_g[l, 2], NSA_GROUPS)])

        qT, kc, vc, ksel, kwin, vT5, gT, rw = _proj(x, norm1_g[l].reshape(1, D), wstd, wt, qg, kg, tm=512)

        wbig = jnp.stack([_compress_weights(cmp_w1[l, 0]), _compress_weights(cmp_w1[l, 1])]).astype(BF16)
        pos8 = jnp.pad(cmp_pos[l].reshape(2, 1, CMP_BLOCK * HEAD_DIM),
                       ((0, 0), (0, V7X_SUBLANES - 1), (0, 0))).astype(BF16)
        kcmp, vcmpT = _compress(
            kc, vc, wbig, pos8,
            cmp_w1[l].astype(BF16), cmp_b1[l].reshape(2, 1, CMP_HIDDEN), cmp_w2[l, 0].astype(BF16),
            cmp_w2[l, 1].T.astype(BF16), cmp_b2[l, 0].reshape(1, HEAD_DIM), cmp_b2[l, 1].reshape(HEAD_DIM, 1),
            k_norm_g[l, 0].reshape(1, HEAD_DIM), ncp=ncp)

        o_nsa = _attn(qT, kcmp, vcmpT, bc, ksel, vT5, kwin, stab, wtab, gT)

        row = lambda a: a.reshape(1, -1)
        y, g, bonus = _wkv(
            rw, row(rwkv_mu[l]), row(w0[l]), w2[l].astype(BF16), row(a0[l]), a2[l].astype(BF16),
            g2[l].astype(BF16), row(k_k[l]), row(k_a[l]), row(r_k[l]), bd, chunks=8)

        n = B * T
        up = ffn_up[l].astype(BF16)
        x = _ffn(x.reshape(n, D), o_nsa.reshape(n, D_NSA), y.reshape(n, D_RWKV), bonus.reshape(n, D_RWKV),
                 g.reshape(n, D_RWKV), bd, row(ln_x_w[l]), row(ln_x_b[l]), w_out[l].astype(BF16),
                 norm2_g[l].reshape(1, D), up[:, :d_ff], up[:, d_ff:], conv_w[l][:, :d_ff], conv_w[l][:, d_ff:],
                 conv_b[l][:d_ff].reshape(1, -1), conv_b[l][d_ff:].reshape(1, -1),
                 ffn_down[l].astype(BF16), tm=512, ft=256, seq_len=T).reshape(B, T, D)
    return x
```
